```python
import math
import jax
import jax.numpy as jnp
from jax import lax
import numpy as np

D_MODEL = 1024
BATCH = 8
SEQ = 4096
DEPTH = 1

N_META = 16
CHUNK = 64
D_MIX = 2 * D_MODEL
GLA_HEADS = 4
GLA_DV = D_MODEL
GLA_DK = D_MODEL // 2
GLA_HEAD_K = GLA_DK // GLA_HEADS
GLA_HEAD_V = GLA_DV // GLA_HEADS
GLA_RANK = 16
GLA_GATE_NORM = 16.0
SSD_INNER = D_MIX - GLA_DV
SSD_HEAD_DIM = 64
SSD_HEADS = SSD_INNER // SSD_HEAD_DIM
SSD_GROUPS = 2
SSD_STATE = 128
SSD_CONV = 5
SSD_CONV_CH = SSD_INNER + 2 * SSD_GROUPS * SSD_STATE
N_EXPERTS = 32
TOP_K = 4
D_FF = D_MODEL
SWIGLU_LIMIT = 7.0
SWIGLU_ALPHA = 1.702
MOE_BLOCK = 256
EPS = 1e-6

kernel_name = "hymba_gla_ssd_moe_encoder"


def _proj_widths():
    return [GLA_DK, GLA_DK, GLA_DV, GLA_DV, GLA_RANK, GLA_RANK,
            SSD_INNER, SSD_CONV_CH, SSD_HEADS, SSD_HEADS]


def rms_norm(x, w):
    xf = x.astype(jnp.float32)
    y = xf * lax.rsqrt(jnp.mean(xf * xf, axis=-1, keepdims=True) + EPS)
    return (y * w.astype(jnp.float32)).astype(x.dtype)


def pad_front(t, n):
    return jnp.pad(t, [(0, 0), (n, 0)] + [(0, 0)] * (t.ndim - 2))


def flip_seq(t):
    return jnp.flip(t, axis=1)


def centred_dwconv(x, w, b):
    c = x.shape[-1]
    half = (SSD_CONV - 1) // 2
    y = lax.conv_general_dilated(x, w[:, None, :].astype(x.dtype), (1,), [(half, half)],
                                 dimension_numbers=("NWC", "WIO", "NWC"),
                                 feature_group_count=c)
    return y + b


def gla_chunked(q, k, v, log_g, exclusive):
    bsz, lp, h, dk = q.shape
    dv = v.shape[-1]
    nc = lp // CHUNK
    shp = lambda t: t.astype(jnp.float32).reshape(bsz, nc, CHUNK, h, t.shape[-1])
    q, k, v, log_g = shp(q), shp(k), shp(v), shp(log_g)
    b = jnp.cumsum(log_g, axis=2)
    b_last = b[:, :, -1:]
    q_d = q * jnp.exp(b)
    k_d = k * jnp.exp(-b)
    k_end = k * jnp.exp(b_last - b)
    mask = jnp.tril(jnp.ones((CHUNK, CHUNK), bool), -1 if exclusive else 0)
    att = jnp.where(mask, jnp.einsum("bcthd,bcshd->bchts", q_d, k_d), 0.0)
    o = jnp.einsum("bchts,bcshv->bcthv", att, v)
    chunk_kv = jnp.einsum("bcshd,bcshv->bchdv", k_end, v)
    decay = jnp.exp(b_last[:, :, 0])

    def step(state, inp):
        dec, kv = inp
        return state * dec[..., None] + kv, state

    init = jnp.zeros((bsz, h, dk, dv), jnp.float32)
    _, s_in = lax.scan(step, init, (jnp.moveaxis(decay, 1, 0), jnp.moveaxis(chunk_kv, 1, 0)))
    s_in = jnp.moveaxis(s_in, 0, 1)
    o = o + jnp.einsum("bcthd,bchdv->bcthv", q_d, s_in)
    return o.reshape(bsz, lp, h, dv)


def ssd_chunked(x, dt, a, bm, cm, exclusive):
    bsz, lp, h, p = x.shape
    g, n = bm.shape[-2:]
    hg = h // g
    nc = lp // CHUNK
    f32 = jnp.float32
    x = x.astype(f32).reshape(bsz, nc, CHUNK, g, hg, p)
    dt = dt.astype(f32).reshape(bsz, nc, CHUNK, g, hg)
    bm = bm.astype(f32).reshape(bsz, nc, CHUNK, g, n)
    cm = cm.astype(f32).reshape(bsz, nc, CHUNK, g, n)
    cs = jnp.cumsum(dt * a.reshape(g, hg), axis=2)
    cs_last = cs[:, :, -1]
    mask = jnp.tril(jnp.ones((CHUNK, CHUNK), bool), -1 if exclusive else 0)
    seg = cs[:, :, :, None] - cs[:, :, None]
    lmat = jnp.exp(jnp.where(mask[:, :, None, None], seg, -jnp.inf))
    xdt = x * dt[..., None]
    cb = jnp.einsum("bctgn,bcsgn->bctsg", cm, bm)
    y = jnp.einsum("bctsgh,bcsghp->bctghp", cb[..., None] * lmat, xdt)
    decay_end = jnp.exp(cs_last[:, :, None] - cs)
    states = jnp.einsum("bcsgn,bcsghp->bcghpn", bm, xdt * decay_end[..., None])

    def step(state, inp):
        dec, st = inp
        return state * dec[..., None, None] + st, state

    init = jnp.zeros((bsz, g, hg, p, n), f32)
    _, s_in = lax.scan(step, init, (jnp.moveaxis(jnp.exp(cs_last), 1, 0), jnp.moveaxis(states, 1, 0)))
    s_in = jnp.moveaxis(s_in, 0, 1)
    y = y + jnp.einsum("bctgn,bcghpn->bctghp", cm, s_in) * jnp.exp(cs)[..., None]
    return y.reshape(bsz, lp, h, p)


def hybrid_mixer(h, w_in, gla_wa2_f, gla_ba2_f, gla_wa2_b, gla_ba2_b, gla_norm_w,
                 conv_w, conv_b, dt_bias_f, dt_bias_b, a_log_f, a_log_b, ssd_d,
                 ssd_norm_w, w_out):
    bsz, seq_len, _ = h.shape
    npad = CHUNK - N_META
    f32 = jnp.float32
    splits = np.cumsum(_proj_widths())[:-1].tolist()
    q, k, v, g, a_f, a_b, z, xbc, dt_f, dt_b = jnp.split(h @ w_in, splits, axis=-1)

    heads_k = lambda t: t.reshape(bsz, seq_len, GLA_HEADS, GLA_HEAD_K)
    heads_v = lambda t: t.reshape(bsz, seq_len, GLA_HEADS, GLA_HEAD_V)
    qh = pad_front(heads_k(q) * GLA_HEAD_K ** -0.5, npad)
    kh = pad_front(heads_k(k), npad)
    vh = pad_front(heads_v(v), npad)
    lg_f = pad_front(heads_k(jax.nn.log_sigmoid((a_f @ gla_wa2_f + gla_ba2_f).astype(f32)) / GLA_GATE_NORM), npad)
    lg_b = pad_front(heads_k(jax.nn.log_sigmoid((a_b @ gla_wa2_b + gla_ba2_b).astype(f32)) / GLA_GATE_NORM), npad)
    o = gla_chunked(qh, kh, vh, lg_f, False) + flip_seq(
        gla_chunked(flip_seq(qh), flip_seq(kh), flip_seq(vh), flip_seq(lg_b), True))
    o = o[:, npad:].astype(h.dtype)
    o_gla = (rms_norm(o, gla_norm_w) * jax.nn.silu(heads_v(g))).reshape(bsz, seq_len, GLA_DV)

    xbc = jax.nn.silu(centred_dwconv(xbc, conv_w, conv_b))
    xs, b_in, c_in = jnp.split(xbc, [SSD_INNER, SSD_INNER + SSD_GROUPS * SSD_STATE], axis=-1)
    xs = xs.reshape(bsz, seq_len, SSD_HEADS, SSD_HEAD_DIM)
    b_p = pad_front(b_in.reshape(bsz, seq_len, SSD_GROUPS, SSD_STATE), npad)
    c_p = pad_front(c_in.reshape(bsz, seq_len, SSD_GROUPS, SSD_STATE), npad)
    xs_p = pad_front(xs, npad)
    dtp_f = pad_front(jax.nn.softplus(dt_f.astype(f32) + dt_bias_f.astype(f32)), npad)
    dtp_b = pad_front(jax.nn.softplus(dt_b.astype(f32) + dt_bias_b.astype(f32)), npad)
    a_f_ssm = -jnp.exp(a_log_f.astype(f32))
    a_b_ssm = -jnp.exp(a_log_b.astype(f32))
    y = ssd_chunked(xs_p, dtp_f, a_f_ssm, b_p, c_p, False) + flip_seq(
        ssd_chunked(flip_seq(xs_p), flip_seq(dtp_b), a_b_ssm, flip_seq(b_p), flip_seq(c_p), True))
    y = y[:, npad:].astype(h.dtype) + xs * ssd_d[:, None]
    y = y.reshape(bsz, seq_len, SSD_INNER) * jax.nn.silu(z)
    y_ssd = rms_norm(y.reshape(bsz, seq_len, SSD_GROUPS, -1),
                     ssd_norm_w.reshape(SSD_GROUPS, -1)).reshape(bsz, seq_len, SSD_INNER)

    return jnp.concatenate([o_gla, y_ssd], axis=-1) @ w_out


def moe_ffn(h, w_router, b_router, w_gu, b_gu, w_dn, b_dn):
    t = h.shape[0]
    d = h.shape[1]
    logits = (h @ w_router + b_router).astype(jnp.float32)
    top_vals, top_idx = lax.top_k(logits, TOP_K)
    gates = jax.nn.softmax(top_vals, axis=-1)
    n_assign = t * TOP_K
    e_flat = top_idx.reshape(-1)
    tok_flat = jnp.arange(n_assign, dtype=jnp.int32) // TOP_K
    g_flat = gates.reshape(-1)
    order = jnp.argsort(e_flat)
    e_sorted = e_flat[order]
    counts = jnp.bincount(e_flat, length=N_EXPERTS)
    starts = jnp.cumsum(counts) - counts
    padded = (counts + MOE_BLOCK - 1) // MOE_BLOCK * MOE_BLOCK
    pends = jnp.cumsum(padded)
    pstarts = pends - padded
    dest = pstarts[e_sorted] + (jnp.arange(n_assign) - starts[e_sorted])
    n_blocks = -(-n_assign // MOE_BLOCK) + N_EXPERTS
    n_rows = n_blocks * MOE_BLOCK
    row_tok = jnp.full((n_rows,), t, jnp.int32).at[dest].set(tok_flat[order])
    row_gate = jnp.zeros((n_rows,), h.dtype).at[dest].set(g_flat[order].astype(h.dtype))
    block_expert = jnp.minimum(
        jnp.searchsorted(pends, jnp.arange(n_blocks) * MOE_BLOCK, side="right"), N_EXPERTS - 1)
    h_pad = jnp.concatenate([h, jnp.zeros((1, d), h.dtype)], axis=0)

    def one_block(args):
        tok, gate_w, e = args
        xb = h_pad[tok]
        gu = xb @ w_gu[e] + b_gu[e]
        gt, up = jnp.split(gu, 2, axis=-1)
        gt = jnp.minimum(gt, SWIGLU_LIMIT)
        up = jnp.clip(up, -SWIGLU_LIMIT, SWIGLU_LIMIT)
        act = (up + 1.0) * gt * jax.nn.sigmoid(gt * SWIGLU_ALPHA)
        return (act @ w_dn[e] + b_dn[e]) * gate_w[:, None]

    y_blocks = lax.map(one_block, (row_tok.reshape(n_blocks, MOE_BLOCK),
                                   row_gate.reshape(n_blocks, MOE_BLOCK), block_expert))
    y = jax.ops.segment_sum(y_blocks.reshape(n_rows, d), row_tok, num_segments=t + 1)
    return y[:t]


def setup_inputs(seed: int = 0) -> dict:
    key = jax.random.key(seed)
    ks = jax.random.split(key, 26)
    f32 = jnp.float32
    nrm = lambda k, shape, fan: jax.random.normal(k, shape, f32) * fan ** -0.5
    gain = lambda k, shape: 1.0 + 0.01 * jax.random.normal(k, shape, f32)
    small = lambda k, shape: 0.01 * jax.random.normal(k, shape, f32)
    d_in = sum(_proj_widths())
    dt0_f = jnp.exp(jax.random.uniform(ks[11], (DEPTH, SSD_HEADS), f32, math.log(1e-3), math.log(1e-1)))
    dt0_b = jnp.exp(jax.random.uniform(ks[12], (DEPTH, SSD_HEADS), f32, math.log(1e-3), math.log(1e-1)))
    return {
        "x": jax.random.normal(ks[0], (BATCH, SEQ, D_MODEL), f32),
        "meta": jax.random.normal(ks[1], (N_META, D_MODEL), f32),
        "norm1_w": gain(ks[2], (DEPTH, D_MODEL)),
        "w_in": nrm(ks[3], (DEPTH, D_MODEL, d_in), D_MODEL),
        "gla_wa2_f": nrm(ks[4], (DEPTH, GLA_RANK, GLA_DK), GLA_RANK),
        "gla_ba2_f": small(ks[5], (DEPTH, GLA_DK)),
        "gla_wa2_b": nrm(ks[6], (DEPTH, GLA_RANK, GLA_DK), GLA_RANK),
        "gla_ba2_b": small(ks[7], (DEPTH, GLA_DK)),
        "gla_norm_w": gain(ks[8], (DEPTH, GLA_HEAD_V)),
        "conv_w": nrm(ks[9], (DEPTH, SSD_CONV, SSD_CONV_CH), SSD_CONV),
        "conv_b": small(ks[10], (DEPTH, SSD_CONV_CH)),
        "dt_bias_f": dt0_f + jnp.log(-jnp.expm1(-dt0_f)),
        "dt_bias_b": dt0_b + jnp.log(-jnp.expm1(-dt0_b)),
        "a_log_f": jnp.log(jax.random.uniform(ks[13], (DEPTH, SSD_HEADS), f32, 1.0, 16.0)),
        "a_log_b": jnp.log(jax.random.uniform(ks[14], (DEPTH, SSD_HEADS), f32, 1.0, 16.0)),
        "ssd_d": gain(ks[15], (DEPTH, SSD_HEADS)),
        "ssd_norm_w": gain(ks[16], (DEPTH, SSD_INNER)),
        "w_out": nrm(ks[17], (DEPTH, D_MIX, D_MODEL), D_MIX),
        "norm2_w": gain(ks[18], (DEPTH, D_MODEL)),
        "w_router": nrm(ks[19], (DEPTH, D_MODEL, N_EXPERTS), D_MODEL),
        "b_router": small(ks[20], (DEPTH, N_EXPERTS)),
        "w_gu": nrm(ks[21], (DEPTH, N_EXPERTS, D_MODEL, 2 * D_FF), D_MODEL),
        "b_gu": small(ks[22], (DEPTH, N_EXPERTS, 2 * D_FF)),
        "w_dn": nrm(ks[23], (DEPTH, N_EXPERTS, D_FF, D_MODEL), D_FF),
        "b_dn": small(ks[24], (DEPTH, N_EXPERTS, D_MODEL)),
        "norm_f_w": gain(ks[25], (D_MODEL,)),
    }


def reference(x, meta, norm1_w, w_in, gla_wa2_f, gla_ba2_f, gla_wa2_b, gla_ba2_b,
              gla_norm_w, conv_w, conv_b, dt_bias_f, dt_bias_b, a_log_f, a_log_b,
              ssd_d, ssd_norm_w, w_out, norm2_w, w_router, b_router, w_gu, b_gu,
              w_dn, b_dn, norm_f_w):
    bsz = x.shape[0]
    h = jnp.concatenate([jnp.broadcast_to(meta[None].astype(x.dtype), (bsz, N_META, D_MODEL)), x], axis=1)
    for l in range(DEPTH):
        h = h + hybrid_mixer(rms_norm(h, norm1_w[l]), w_in[l], gla_wa2_f[l], gla_ba2_f[l],
                             gla_wa2_b[l], gla_ba2_b[l], gla_norm_w[l], conv_w[l], conv_b[l],
                             dt_bias_f[l], dt_bias_b[l], a_log_f[l], a_log_b[l], ssd_d[l],
                             ssd_norm_w[l], w_out[l])
        n2 = rms_norm(h, norm2_w[l]).reshape(-1, D_MODEL)
        h = h + moe_ffn(n2, w_router[l], b_router[l], w_gu[l], b_gu[l], w_dn[l], b_dn[l]).reshape(h.shape)
    out = rms_norm(h, norm_f_w)
    return out[:, N_META:]
```

```python
import functools

import jax
import jax.numpy as jnp
from jax import lax
from jax.experimental import pallas as pl
from jax.experimental.pallas import tpu as pltpu

F32 = jnp.float32
BF16 = jnp.bfloat16
I32 = jnp.int32
U32 = jnp.uint32

D_MODEL = 1024
N_META = 16
CHUNK = 64
NPAD = CHUNK - N_META
GLA_HEADS = 4
GLA_DK = 512
GLA_DV = 1024
GLA_HEAD_K = GLA_DK // GLA_HEADS
GLA_HEAD_V = GLA_DV // GLA_HEADS
GLA_RANK = 16
GLA_GATE_NORM = 16.0
SSD_INNER = 1024
SSD_HEAD_DIM = 64
SSD_HEADS = SSD_INNER // SSD_HEAD_DIM
SSD_GROUPS = 2
SSD_GROUP_HEADS = SSD_HEADS // SSD_GROUPS
SSD_GROUP_CH = SSD_INNER // SSD_GROUPS
SSD_STATE = 128
SSD_CONV = 5
N_EXPERTS = 32
TOP_K = 4
D_FF = 1024
SWIGLU_LIMIT = 7.0
SWIGLU_ALPHA = 1.702
EPS = 1e-6

LANES = 128
HALO = 16

COL_Q = 0
COL_K = 512
COL_V = 1024
COL_G = 2048
COL_Z = 3072
COL_X = 4096
COL_B = 5120
COL_C = 5376
COL_SMALL = 5632
N_PROJ = 5760
LANE_AF = 0
LANE_AB = 16
LANE_DTF = 32
LANE_DTB = 48

ROW_TILE = 512
EXPERT_TILE = 512
COMBINE_TILE = 256
VMEM_LIMIT = 56 * 1024 * 1024


def _dot(a, b):
    return jnp.dot(a, b, preferred_element_type=F32)


def _dot_nt(a, b):
    return lax.dot_general(a, b, (((1,), (1,)), ((), ())), preferred_element_type=F32)


def _dot_tn(a, b):
    return lax.dot_general(a, b, (((0,), (0,)), ((), ())), preferred_element_type=F32)


def _split(x):
    hi = x.astype(BF16)
    lo = (x - hi.astype(F32)).astype(BF16)
    return hi, lo


def _sel_dot_l(m01, x):
    hi, lo = _split(x)
    return _dot(m01, hi) + _dot(m01, lo)


def _sel_dot_r(x, m01):
    hi, lo = _split(x)
    return _dot(hi, m01) + _dot(lo, m01)


def _softplus(x):
    return jnp.maximum(x, 0.0) + jnp.log(1.0 + jnp.exp(-jnp.abs(x)))


def _log_sigmoid(x):
    return jnp.minimum(x, 0.0) - jnp.log(1.0 + jnp.exp(-jnp.abs(x)))


def _silu(x):
    return x * jax.nn.sigmoid(x)


def _rms(x, w):
    return x * lax.rsqrt(jnp.mean(x * x, axis=-1, keepdims=True) + EPS) * w


def _pack_bf16_pairs(x):
    w = x.shape[1] // 2
    bits = lax.bitcast_convert_type(x.astype(BF16).astype(F32), U32)
    return (bits[:, :w] >> 16) | (bits[:, w:] & jnp.uint32(0xFFFF0000))


def _unpack_bf16_pairs(p):
    lo = lax.bitcast_convert_type(p << 16, F32)
    hi = lax.bitcast_convert_type(p & jnp.uint32(0xFFFF0000), F32)
    return jnp.concatenate([lo, hi], axis=1)


def _tri(n, *, lower, inclusive):
    r = lax.broadcasted_iota(I32, (n, n), 0)
    c = lax.broadcasted_iota(I32, (n, n), 1)
    if lower:
        return (c <= r) if inclusive else (c < r)
    return (c >= r) if inclusive else (c > r)


def _inproj_body(x_ref, nw_ref, w_ref, o_ref, *, col_chunks):
    xn = _rms(x_ref[...], nw_ref[...]).astype(BF16)
    for lo, hi in col_chunks:
        o_ref[:, lo:hi] = _dot(xn, w_ref[:, lo:hi]).astype(BF16)


def _inproj(x2d, norm_w, w_perm, tile):
    rows = x2d.shape[0]
    col_chunks = tuple((c, min(c + 1024, N_PROJ)) for c in range(0, N_PROJ, 1024))
    return pl.pallas_call(
        functools.partial(_inproj_body, col_chunks=col_chunks),
        grid=(rows // tile,),
        in_specs=[
            pl.BlockSpec((tile, D_MODEL), lambda i: (i, 0)),
            pl.BlockSpec((1, D_MODEL), lambda i: (0, 0)),
            pl.BlockSpec((D_MODEL, N_PROJ), lambda i: (0, 0)),
        ],
        out_specs=pl.BlockSpec((tile, N_PROJ), lambda i: (i, 0)),
        out_shape=jax.ShapeDtypeStruct((rows, N_PROJ), BF16),
        compiler_params=pltpu.CompilerParams(
            dimension_semantics=("parallel",), vmem_limit_bytes=VMEM_LIMIT),
        name="inproj",
    )(x2d, norm_w, w_perm)


def _gla_chunk(q, k, v, sm, wa, ba, st_ref, tri, amask, *, fwd, rowmask=None, need_out=True):
    z = _dot(sm, wa) + ba
    lg = _log_sigmoid(z) * (1.0 / GLA_GATE_NORM)
    if rowmask is not None:
        lg = jnp.where(rowmask, lg, 0.0)
    b = _sel_dot_l(tri, lg)
    tot = b[CHUNK - 1:CHUNK] if fwd else b[0:1]
    kf = k.astype(F32)
    kend = (kf * jnp.exp(tot - b)).astype(BF16)
    st = st_ref[...]
    out = None
    if need_out:
        qd = (q.astype(F32) * (GLA_HEAD_K ** -0.5) * jnp.exp(b)).astype(BF16)
        kd = (kf * jnp.exp(-b)).astype(BF16)
        att = jnp.where(amask, _dot_nt(qd, kd), 0.0).astype(BF16)
        out = _dot(att, v) + _dot_nt(qd, st.astype(BF16))
    st_ref[...] = st * jnp.exp(tot) + _dot_tn(v, kend)
    return out


def _gla_body(q_ref, k_ref, v_ref, g_ref, sm_ref, qm_ref, km_ref, vm_ref, smm_ref,
              waf_ref, baf_ref, wab_ref, bab_ref, nw_ref, o_ref, st_ref, *, n_chunks):
    tri_f = _tri(CHUNK, lower=True, inclusive=True)
    tri_b = _tri(CHUNK, lower=False, inclusive=True)
    amask_f = tri_f
    amask_b = _tri(CHUNK, lower=False, inclusive=False)
    tri_f16 = tri_f.astype(BF16)
    tri_b16 = tri_b.astype(BF16)
    waf, baf = waf_ref[...], baf_ref[...]
    wab, bab = wab_ref[...], bab_ref[...]
    nw = nw_ref[...]

    def rows(ref, off):
        return ref[pl.ds(off, CHUNK), :]

    st_ref[...] = jnp.zeros_like(st_ref)

    def bwd_body(i, carry):
        off = pl.multiple_of((n_chunks - 1 - i) * CHUNK, CHUNK)
        out = _gla_chunk(rows(q_ref, off), rows(k_ref, off), rows(v_ref, off), rows(sm_ref, off),
                         wab, bab, st_ref, tri_b16, amask_b, fwd=False)
        o_ref[pl.ds(off, CHUNK), :] = out.astype(BF16)
        return carry

    lax.fori_loop(0, n_chunks, bwd_body, 0)

    st_ref[...] = jnp.zeros_like(st_ref)
    meta_rows = lax.broadcasted_iota(I32, (CHUNK, LANES), 0) >= NPAD
    _gla_chunk(qm_ref[...], km_ref[...], vm_ref[...], smm_ref[...], waf, baf, st_ref,
               tri_f16, amask_f, fwd=True, rowmask=meta_rows, need_out=False)

    def fwd_body(r, carry):
        off = pl.multiple_of(r * CHUNK, CHUNK)
        out = _gla_chunk(rows(q_ref, off), rows(k_ref, off), rows(v_ref, off), rows(sm_ref, off),
                         waf, baf, st_ref, tri_f16, amask_f, fwd=True)
        o = out + o_ref[pl.ds(off, CHUNK), :].astype(F32)
        gate = _silu(rows(g_ref, off).astype(F32))
        o_ref[pl.ds(off, CHUNK), :] = (_rms(o, nw) * gate).astype(BF16)
        return carry

    lax.fori_loop(0, n_chunks, fwd_body, 0)


def _gla(proj, proj_meta, wa_f, ba_f, wa_b, ba_b, norm_w):
    bsz, seq, _ = proj.shape
    kb, vb = COL_K // GLA_HEAD_K, COL_V // GLA_HEAD_V
    gb, sb = COL_G // GLA_HEAD_V, COL_SMALL // LANES
    real = lambda width, base: pl.BlockSpec((None, seq, width), lambda b, h: (b, 0, base + h))
    meta = lambda width, base: pl.BlockSpec((CHUNK, width), lambda b, h: (0, base + h))
    per_head = lambda rows_: pl.BlockSpec((rows_, GLA_HEAD_K), lambda b, h: (0, h))
    return pl.pallas_call(
        functools.partial(_gla_body, n_chunks=seq // CHUNK),
        grid=(bsz, GLA_HEADS),
        in_specs=[
            real(GLA_HEAD_K, 0), real(GLA_HEAD_K, kb), real(GLA_HEAD_V, vb), real(GLA_HEAD_V, gb),
            pl.BlockSpec((None, seq, LANES), lambda b, h: (b, 0, sb)),
            meta(GLA_HEAD_K, 0), meta(GLA_HEAD_K, kb), meta(GLA_HEAD_V, vb),
            pl.BlockSpec((CHUNK, LANES), lambda b, h: (0, sb)),
            per_head(LANES), per_head(1), per_head(LANES), per_head(1),
            pl.BlockSpec((1, GLA_HEAD_V), lambda b, h: (0, 0)),
        ],
        out_specs=pl.BlockSpec((None, seq, GLA_HEAD_V), lambda b, h: (b, 0, h)),
        out_shape=jax.ShapeDtypeStruct((bsz, seq, GLA_DV), BF16),
        scratch_shapes=[pltpu.VMEM((GLA_HEAD_V, GLA_HEAD_K), F32)],
        compiler_params=pltpu.CompilerParams(
            dimension_semantics=("parallel", "parallel"), vmem_limit_bytes=VMEM_LIMIT),
        name="gla",
    )(proj, proj, proj, proj, proj, proj_meta, proj_meta, proj_meta, proj_meta,
      wa_f, ba_f, wa_b, ba_b, norm_w)


def _conv_silu(win, cw, cb):
    half = (SSD_CONV - 1) // 2
    acc = cb
    for j in range(SSD_CONV):
        lo = HALO - half + j
        acc = acc + win[lo:lo + CHUNK, :] * cw[j:j + 1, :]
    return _silu(acc)


def _ssd_chunk(xc, bc, cc, sm, dtb, a_row, e_mat, st_ref, consts, *, fwd, rowmask=None, need_out=True):
    tri16, irep, irep16, ones16, mask_rep, bdmask = consts
    dt = _softplus(sm + dtb)
    if rowmask is not None:
        dt = jnp.where(rowmask, dt, 0.0)
    cs = _sel_dot_l(tri16, dt * a_row)
    cs_e = _sel_dot_r(cs, e_mat)
    dt_e = _sel_dot_r(dt, e_mat)
    tot_e = cs_e[CHUNK - 1:CHUNK] if fwd else cs_e[0:1]
    xdt = xc * dt_e
    xend = (xdt * jnp.exp(tot_e - cs_e)).astype(BF16)
    bc16 = bc.astype(BF16)
    st = st_ref[...]
    y = None
    if need_out:
        cc16 = cc.astype(BF16)
        cb_rep = _dot(_dot_nt(cc16, bc16).astype(BF16), irep16)
        cs_row = _sel_dot_l(ones16, jnp.where(irep, cs_e, 0.0))
        decay = jnp.where(mask_rep, jnp.exp(jnp.minimum(cs_e - cs_row, 0.0)), 0.0)
        w = (cb_rep * decay).astype(BF16)
        xdt16 = xdt.astype(BF16)
        xbd = jnp.where(bdmask, jnp.concatenate([xdt16] * SSD_GROUP_HEADS, axis=0), jnp.zeros((), BF16))
        y = _dot(w, xbd) + _dot(cc16, st.astype(BF16)) * jnp.exp(cs_e)
    st_ref[...] = st * jnp.exp(tot_e) + _dot_tn(bc16, xend)
    return y


def _ssd_body(x_ref, z_ref, b_ref, c_ref, sm_ref, xm_ref, bm_ref, cm_ref, smm_ref,
              cwx_ref, cbx_ref, cwb_ref, cbb_ref, cwc_ref, cbc_ref, dtb_ref, alog_ref,
              ef_ref, eb_ref, dexp_ref, nw_ref, o_ref, st_ref, *, n_chunks):
    gh, p = SSD_GROUP_HEADS, SSD_HEAD_DIM
    width = gh * p
    lane_s = lax.broadcasted_iota(I32, (CHUNK, width), 1) % p
    row_t = lax.broadcasted_iota(I32, (CHUNK, width), 0)
    irep = lane_s == row_t
    irep16 = irep.astype(BF16)
    ones16 = jnp.ones((CHUNK, CHUNK), BF16)
    mask_f = lane_s <= row_t
    mask_b = lane_s > row_t
    bd_r = lax.broadcasted_iota(I32, (width, width), 0) // p
    bd_c = lax.broadcasted_iota(I32, (width, width), 1) // p
    bdmask = bd_r == bd_c
    tri_f16 = _tri(CHUNK, lower=True, inclusive=True).astype(BF16)
    tri_b16 = _tri(CHUNK, lower=False, inclusive=True).astype(BF16)
    consts_f = (tri_f16, irep, irep16, ones16, mask_f, bdmask)
    consts_b = (tri_b16, irep, irep16, ones16, mask_b, bdmask)

    cwx, cbx = cwx_ref[...], cbx_ref[...]
    cwb, cbb = cwb_ref[...], cbb_ref[...]
    cwc, cbc = cwc_ref[...], cbc_ref[...]
    dtb = dtb_ref[...]
    a_row = -jnp.exp(alog_ref[...])
    ef, eb = ef_ref[...], eb_ref[...]
    dexp, nw = dexp_ref[...], nw_ref[...]
    seq = n_chunks * CHUNK

    def window(ref, mref, r):
        off = pl.multiple_of(r * CHUNK, CHUNK)
        poff = pl.multiple_of(jnp.maximum(off - HALO, 0), HALO)
        noff = pl.multiple_of(jnp.minimum(off + CHUNK, seq - HALO), HALO)
        prev = jnp.where(r == 0, mref[CHUNK - HALO:, :], ref[pl.ds(poff, HALO), :]).astype(F32)
        nxt = ref[pl.ds(noff, HALO), :].astype(F32)
        nxt = jnp.where(r == n_chunks - 1, 0.0, nxt)
        return jnp.concatenate([prev, ref[pl.ds(off, CHUNK), :].astype(F32), nxt], axis=0)

    def conv_chunk(r):
        xc = _conv_silu(window(x_ref, xm_ref, r), cwx, cbx)
        bc = _conv_silu(window(b_ref, bm_ref, r), cwb, cbb)
        cc = _conv_silu(window(c_ref, cm_ref, r), cwc, cbc)
        return xc, bc, cc

    st_ref[...] = jnp.zeros_like(st_ref)

    def bwd_body(i, carry):
        r = n_chunks - 1 - i
        off = pl.multiple_of(r * CHUNK, CHUNK)
        xc, bc, cc = conv_chunk(r)
        sm = sm_ref[pl.ds(off, CHUNK), :].astype(F32)
        y = _ssd_chunk(xc, bc, cc, sm, dtb, a_row, eb, st_ref, consts_b, fwd=False)
        o_ref[pl.ds(off, CHUNK), :] = y.astype(BF16)
        return carry

    lax.fori_loop(0, n_chunks, bwd_body, 0)

    st_ref[...] = jnp.zeros_like(st_ref)

    def meta_window(mref, ref):
        zeros = jnp.zeros((HALO, mref.shape[1]), F32)
        return jnp.concatenate([zeros, mref[...].astype(F32), ref[0:HALO, :].astype(F32)], axis=0)

    def meta_mask(width_):
        return lax.broadcasted_iota(I32, (CHUNK, width_), 0) >= NPAD

    xc = jnp.where(meta_mask(width), _conv_silu(meta_window(xm_ref, x_ref), cwx, cbx), 0.0)
    bc = jnp.where(meta_mask(SSD_STATE), _conv_silu(meta_window(bm_ref, b_ref), cwb, cbb), 0.0)
    cc = jnp.where(meta_mask(SSD_STATE), _conv_silu(meta_window(cm_ref, c_ref), cwc, cbc), 0.0)
    _ssd_chunk(xc, bc, cc, smm_ref[...].astype(F32), dtb, a_row, ef, st_ref, consts_f,
               fwd=True, rowmask=meta_mask(LANES), need_out=False)

    def fwd_body(r, carry):
        off = pl.multiple_of(r * CHUNK, CHUNK)
        xc, bc, cc = conv_chunk(r)
        sm = sm_ref[pl.ds(off, CHUNK), :].astype(F32)
        y = _ssd_chunk(xc, bc, cc, sm, dtb, a_row, ef, st_ref, consts_f, fwd=True)
        y = y + o_ref[pl.ds(off, CHUNK), :].astype(F32) + xc * dexp
        y = y * _silu(z_ref[pl.ds(off, CHUNK), :].astype(F32))
        o_ref[pl.ds(off, CHUNK), :] = _rms(y, nw).astype(BF16)
        return carry

    lax.fori_loop(0, n_chunks, fwd_body, 0)


def _ssd(proj, proj_meta, conv_w, conv_b, dtb, alog, e_f, e_b, dexp, norm_w):
    bsz, seq, _ = proj.shape
    gc = SSD_GROUP_CH
    xb, zb = COL_X // gc, COL_Z // gc
    bb, cb, sb = COL_B // SSD_STATE, COL_C // SSD_STATE, COL_SMALL // LANES
    real = lambda width, base: pl.BlockSpec((None, seq, width), lambda b, g: (b, 0, base + g))
    meta = lambda width, base: pl.BlockSpec((CHUNK, width), lambda b, g: (0, base + g))
    cpar = lambda rows_, width, base: pl.BlockSpec((rows_, width), lambda b, g: (0, base + g))
    cbb_, ccb_ = SSD_INNER // SSD_STATE, SSD_INNER // SSD_STATE + SSD_GROUPS
    const = lambda shape: pl.BlockSpec(shape, lambda b, g: (0, 0))
    return pl.pallas_call(
        functools.partial(_ssd_body, n_chunks=seq // CHUNK),
        grid=(bsz, SSD_GROUPS),
        in_specs=[
            real(gc, xb), real(gc, zb), real(SSD_STATE, bb), real(SSD_STATE, cb),
            pl.BlockSpec((None, seq, LANES), lambda b, g: (b, 0, sb)),
            meta(gc, xb), meta(SSD_STATE, bb), meta(SSD_STATE, cb),
            pl.BlockSpec((CHUNK, LANES), lambda b, g: (0, sb)),
            cpar(SSD_CONV, gc, 0), cpar(1, gc, 0),
            cpar(SSD_CONV, SSD_STATE, cbb_), cpar(1, SSD_STATE, cbb_),
            cpar(SSD_CONV, SSD_STATE, ccb_), cpar(1, SSD_STATE, ccb_),
            const((1, LANES)), const((1, LANES)),
            pl.BlockSpec((None, LANES, gc), lambda b, g: (g, 0, 0)),
            pl.BlockSpec((None, LANES, gc), lambda b, g: (g, 0, 0)),
            pl.BlockSpec((1, gc), lambda b, g: (0, g)),
            pl.BlockSpec((1, gc), lambda b, g: (0, g)),
        ],
        out_specs=pl.BlockSpec((None, seq, gc), lambda b, g: (b, 0, g)),
        out_shape=jax.ShapeDtypeStruct((bsz, seq, SSD_INNER), BF16),
        scratch_shapes=[pltpu.VMEM((SSD_STATE, gc), F32)],
        compiler_params=pltpu.CompilerParams(
            dimension_semantics=("parallel", "parallel"), vmem_limit_bytes=VMEM_LIMIT),
        name="ssd",
    )(proj, proj, proj, proj, proj, proj_meta, proj_meta, proj_meta, proj_meta,
      conv_w, conv_b, conv_w, conv_b, conv_w, conv_b, dtb, alog, e_f, e_b, dexp, norm_w)


def _outproj_body(x_ref, og_ref, ys_ref, wo1_ref, wo2_ref, n2w_ref, wrh_ref, wrl_ref, br_ref,
                  h2_ref, n2p_ref, idx_ref, gate_ref, rank_ref, cnt_ref, carry_ref, *, tile):
    i = pl.program_id(0)

    @pl.when(i == 0)
    def _():
        carry_ref[...] = jnp.zeros_like(carry_ref)

    h2 = x_ref[...] + _dot(og_ref[...], wo1_ref[...]) + _dot(ys_ref[...], wo2_ref[...])
    h2_ref[...] = h2
    n2 = _rms(h2, n2w_ref[...])
    n2p_ref[...] = _pack_bf16_pairs(n2)

    nh, nl = _split(n2)
    wrh, wrl = wrh_ref[...], wrl_ref[...]
    logits = _dot(nh, wrh) + _dot(nh, wrl) + _dot(nl, wrh) + br_ref[...]

    lane = lax.broadcasted_iota(I32, (tile, LANES), 1)
    lane_f = lane.astype(F32)
    lane4 = lax.broadcasted_iota(I32, (tile, TOP_K), 1)
    vals, onehots = [], []
    idx_out = jnp.zeros((tile, TOP_K), I32)
    work = logits
    for k in range(TOP_K):
        m = jnp.max(work, axis=-1, keepdims=True)
        first = jnp.min(jnp.where(work == m, lane_f, float(LANES)), axis=-1, keepdims=True)
        oh = lane_f == first
        work = jnp.where(oh, -jnp.inf, work)
        vals.append(m)
        onehots.append(oh)
        idx_out = jnp.where(lane4 == k, first.astype(I32), idx_out)
    idx_ref[...] = idx_out

    exps = [jnp.exp(v - vals[0]) for v in vals]
    inv = 1.0 / (exps[0] + exps[1] + exps[2] + exps[3])
    gate_out = jnp.zeros((tile, TOP_K), F32)
    for k in range(TOP_K):
        gate_out = jnp.where(lane4 == k, exps[k] * inv, gate_out)
    gate_ref[...] = gate_out

    any_oh = (onehots[0] | onehots[1] | onehots[2] | onehots[3])
    any16 = jnp.where(any_oh, 1.0, 0.0).astype(BF16)
    before = _dot(_tri(tile, lower=True, inclusive=False).astype(BF16), any16) + carry_ref[...]
    rank_out = jnp.zeros((tile, TOP_K), I32)
    for k in range(TOP_K):
        rk = jnp.sum(jnp.where(onehots[k], before, 0.0), axis=-1, keepdims=True)
        rank_out = jnp.where(lane4 == k, rk.astype(I32), rank_out)
    rank_ref[...] = rank_out
    carry = carry_ref[...] + jnp.sum(any16.astype(F32), axis=0, keepdims=True)
    carry_ref[...] = carry
    cnt_ref[...] = carry


def _outproj(x2d, o_gla, y_ssd, w_out1, w_out2, norm2_w, wr_hi, wr_lo, b_r, tile):
    rows = x2d.shape[0]
    row = lambda width: pl.BlockSpec((tile, width), lambda i: (i, 0))
    const = lambda shape: pl.BlockSpec(shape, lambda i: (0, 0))
    return pl.pallas_call(
        functools.partial(_outproj_body, tile=tile),
        grid=(rows // tile,),
        in_specs=[
            row(D_MODEL), row(GLA_DV), row(SSD_INNER),
            const((GLA_DV, D_MODEL)), const((SSD_INNER, D_MODEL)), const((1, D_MODEL)),
            const((D_MODEL, LANES)), const((D_MODEL, LANES)), const((1, LANES)),
        ],
        out_specs=[
            row(D_MODEL), row(D_MODEL // 2), row(TOP_K), row(TOP_K), row(TOP_K), const((1, LANES)),
        ],
        out_shape=[
            jax.ShapeDtypeStruct((rows, D_MODEL), F32),
            jax.ShapeDtypeStruct((rows, D_MODEL // 2), U32),
            jax.ShapeDtypeStruct((rows, TOP_K), I32),
            jax.ShapeDtypeStruct((rows, TOP_K), F32),
            jax.ShapeDtypeStruct((rows, TOP_K), I32),
            jax.ShapeDtypeStruct((1, LANES), F32),
        ],
        scratch_shapes=[pltpu.VMEM((1, LANES), F32)],
        compiler_params=pltpu.CompilerParams(
            dimension_semantics=("arbitrary",), vmem_limit_bytes=VMEM_LIMIT),
        name="outproj_router",
    )(x2d, o_gla, y_ssd, w_out1, w_out2, norm2_w, wr_hi, wr_lo, b_r)


def _row_copy(src, src_row, dst, dst_row, sem):
    return pltpu.make_async_copy(src.at[pl.ds(src_row, 1)], dst.at[pl.ds(dst_row, 1)], sem)


def _dispatch_body(dest_ref, n2p_hbm, xs_hbm, sem, *, tile):
    base = pl.program_id(0) * tile

    def start(t, carry):
        for k in range(TOP_K):
            _row_copy(n2p_hbm, base + t, xs_hbm, dest_ref[t * TOP_K + k], sem).start()
        return carry

    lax.fori_loop(0, tile, start, 0)

    def wait(t, carry):
        for k in range(TOP_K):
            _row_copy(n2p_hbm, 0, xs_hbm, 0, sem).wait()
        return carry

    lax.fori_loop(0, tile, wait, 0)


def _dispatch(dest_flat, n2p, tile):
    rows, width = n2p.shape
    return pl.pallas_call(
        functools.partial(_dispatch_body, tile=tile),
        grid=(rows // tile,),
        in_specs=[
            pl.BlockSpec((tile * TOP_K,), lambda i: (i,), memory_space=pltpu.SMEM),
            pl.BlockSpec(memory_space=pl.ANY),
        ],
        out_specs=pl.BlockSpec(memory_space=pl.ANY),
        out_shape=jax.ShapeDtypeStruct((rows * TOP_K, width), U32),
        scratch_shapes=[pltpu.SemaphoreType.DMA(())],
        compiler_params=pltpu.CompilerParams(dimension_semantics=("arbitrary",)),
        name="dispatch",
    )(dest_flat, n2p)


def _expert_body(blk_ref, exp_ref, lo_ref, hi_ref, first_ref, nw_ref,
                 x_ref, wgu_ref, bgu_ref, wdn_ref, bdn_ref, o_ref, *, tile):
    w = pl.program_id(0)

    @pl.when(w < nw_ref[0])
    def _():
        x = _unpack_bf16_pairs(x_ref[...]).astype(BF16)
        acc = jnp.zeros((tile, D_MODEL), F32) + bdn_ref[...]
        half = D_FF // 2
        for j in range(2):
            gt = _dot(x, wgu_ref[:, j * half:(j + 1) * half]) + bgu_ref[:, j * half:(j + 1) * half]
            up = (_dot(x, wgu_ref[:, D_FF + j * half:D_FF + (j + 1) * half])
                  + bgu_ref[:, D_FF + j * half:D_FF + (j + 1) * half])
            gt = jnp.minimum(gt, SWIGLU_LIMIT)
            up = jnp.clip(up, -SWIGLU_LIMIT, SWIGLU_LIMIT)
            act = (up + 1.0) * gt * jax.nn.sigmoid(gt * SWIGLU_ALPHA)
            acc = acc + _dot(act.astype(BF16), wdn_ref[j * half:(j + 1) * half, :])
        packed = _pack_bf16_pairs(acc)

        @pl.when(first_ref[w] == 1)
        def _():
            o_ref[...] = packed

        @pl.when(first_ref[w] == 0)
        def _():
            r = lax.broadcasted_iota(I32, packed.shape, 0)
            mine = (r >= lo_ref[w]) & (r < hi_ref[w])
            o_ref[...] = jnp.where(mine, packed, o_ref[...])


def _experts(tables, xs, w_gu, b_gu, w_dn, b_dn, tile, n_work):
    rows, width = xs.shape
    grid_spec = pltpu.PrefetchScalarGridSpec(
        num_scalar_prefetch=6,
        grid=(n_work,),
        in_specs=[
            pl.BlockSpec((tile, width), lambda w, blk, ex, lo, hi, fi, nw: (blk[w], 0)),
            pl.BlockSpec((None, D_MODEL, 2 * D_FF), lambda w, blk, ex, lo, hi, fi, nw: (ex[w], 0, 0)),
            pl.BlockSpec((None, 1, 2 * D_FF), lambda w, blk, ex, lo, hi, fi, nw: (ex[w], 0, 0)),
            pl.BlockSpec((None, D_FF, D_MODEL), lambda w, blk, ex, lo, hi, fi, nw: (ex[w], 0, 0)),
            pl.BlockSpec((None, 1, D_MODEL), lambda w, blk, ex, lo, hi, fi, nw: (ex[w], 0, 0)),
        ],
        out_specs=pl.BlockSpec((tile, width), lambda w, blk, ex, lo, hi, fi, nw: (blk[w], 0)),
    )
    return pl.pallas_call(
        functools.partial(_expert_body, tile=tile),
        grid_spec=grid_spec,
        out_shape=jax.ShapeDtypeStruct((rows, width), U32),
        compiler_params=pltpu.CompilerParams(
            dimension_semantics=("arbitrary",), vmem_limit_bytes=VMEM_LIMIT),
        name="experts",
    )(*tables, xs, w_gu, b_gu, w_dn, b_dn)


def _combine_body(dest_ref, gate_ref, h2_ref, nfw_ref, ys_hbm, o_ref, buf, sem, *, tile):
    def start(t, carry):
        for k in range(TOP_K):
            pltpu.make_async_copy(ys_hbm.at[pl.ds(dest_ref[t * TOP_K + k], 1)],
                                  buf.at[k, pl.ds(t, 1)], sem).start()
        return carry

    lax.fori_loop(0, tile, start, 0)

    def wait(t, carry):
        for k in range(TOP_K):
            pltpu.make_async_copy(ys_hbm.at[pl.ds(0, 1)], buf.at[k, pl.ds(0, 1)], sem).wait()
        return carry

    lax.fori_loop(0, tile, wait, 0)

    gate = gate_ref[...]
    h3 = h2_ref[...]
    for k in range(TOP_K):
        h3 = h3 + gate[:, k:k + 1] * _unpack_bf16_pairs(buf[k])
    o_ref[...] = _rms(h3, nfw_ref[...])


def _combine(dest_flat, gates, h2, norm_f_w, ys, tile):
    rows = h2.shape[0]
    width = ys.shape[1]
    return pl.pallas_call(
        functools.partial(_combine_body, tile=tile),
        grid=(rows // tile,),
        in_specs=[
            pl.BlockSpec((tile * TOP_K,), lambda i: (i,), memory_space=pltpu.SMEM),
            pl.BlockSpec((tile, TOP_K), lambda i: (i, 0)),
            pl.BlockSpec((tile, D_MODEL), lambda i: (i, 0)),
            pl.BlockSpec((1, D_MODEL), lambda i: (0, 0)),
            pl.BlockSpec(memory_space=pl.ANY),
        ],
        out_specs=pl.BlockSpec((tile, D_MODEL), lambda i: (i, 0)),
        out_shape=jax.ShapeDtypeStruct((rows, D_MODEL), F32),
        scratch_shapes=[pltpu.VMEM((TOP_K, tile, width), U32), pltpu.SemaphoreType.DMA(())],
        compiler_params=pltpu.CompilerParams(
            dimension_semantics=("arbitrary",), vmem_limit_bytes=VMEM_LIMIT),
        name="combine_final",
    )(dest_flat, gates, h2, norm_f_w, ys)


def _work_tables(counts, n_rows, tile):
    n_blocks = n_rows // tile
    n_work = n_blocks + N_EXPERTS
    ends = jnp.cumsum(counts)
    starts = ends - counts
    first_blk = starts // tile
    last_blk = (ends - 1) // tile
    nb = jnp.where(counts > 0, last_blk - first_blk + 1, 0)
    wend = jnp.cumsum(nb)
    wstart = wend - nb
    total = wend[-1]
    w = jnp.minimum(jnp.arange(n_work, dtype=I32), total - 1)
    ex = jnp.minimum(jnp.searchsorted(wend, w, side="right"), N_EXPERTS - 1).astype(I32)
    blk = (first_blk[ex] + (w - wstart[ex])).astype(I32)
    lo = (jnp.maximum(starts[ex], blk * tile) - blk * tile).astype(I32)
    hi = (jnp.minimum(ends[ex], (blk + 1) * tile) - blk * tile).astype(I32)
    prev_blk = jnp.concatenate([jnp.full((1,), -1, I32), blk[:-1]])
    first = (blk != prev_blk).astype(I32)
    return (blk, ex, lo, hi, first, total.reshape(1).astype(I32)), starts, n_work


def kernel(x, meta, norm1_w, w_in, gla_wa2_f, gla_ba2_f, gla_wa2_b, gla_ba2_b, gla_norm_w, conv_w, conv_b, dt_bias_f, dt_bias_b, a_log_f, a_log_b, ssd_d, ssd_norm_w, w_out, norm2_w, w_router, b_router, w_gu, b_gu, w_dn, b_dn, norm_f_w):
    bsz, seq, d = x.shape
    n_tok = bsz * seq
    l = 0

    wi = w_in[l]
    a_cols = wi[:, 3072:3104]
    dt_cols = wi[:, 5664:5696]
    w_perm = jnp.concatenate(
        [wi[:, :3072], wi[:, 3104:5664], a_cols, dt_cols,
         jnp.zeros((d, N_PROJ - COL_SMALL - 64), F32)], axis=1).astype(BF16)

    def lane_rows(w, lane0):
        return jnp.zeros((LANES, w.shape[1]), F32).at[lane0:lane0 + w.shape[0]].set(w)

    wa_f = lane_rows(gla_wa2_f[l], LANE_AF).astype(BF16)
    wa_b = lane_rows(gla_wa2_b[l], LANE_AB).astype(BF16)
    ba_f = gla_ba2_f[l][None, :]
    ba_b = gla_ba2_b[l][None, :]

    def lane_vec(vf, vb):
        z = jnp.zeros((1, LANES), F32)
        return z.at[0, LANE_DTF:LANE_DTF + SSD_HEADS].set(vf).at[0, LANE_DTB:LANE_DTB + SSD_HEADS].set(vb)

    dtb = lane_vec(dt_bias_f[l], dt_bias_b[l])
    alog = lane_vec(a_log_f[l], a_log_b[l])
    lane_id = jnp.arange(LANES)[None, :, None]
    head_id = (jnp.arange(SSD_GROUP_CH) // SSD_HEAD_DIM)[None, None, :]
    grp = jnp.arange(SSD_GROUPS)[:, None, None] * SSD_GROUP_HEADS
    e_f = (lane_id == LANE_DTF + grp + head_id).astype(BF16)
    e_b = (lane_id == LANE_DTB + grp + head_id).astype(BF16)
    dexp = jnp.repeat(ssd_d[l], SSD_HEAD_DIM)[None, :]

    wr = jnp.zeros((d, LANES), F32).at[:, :N_EXPERTS].set(w_router[l])
    wr_hi = wr.astype(BF16)
    wr_lo = (wr - wr_hi.astype(F32)).astype(BF16)
    b_r = jnp.full((1, LANES), -1e30, F32).at[0, :N_EXPERTS].set(b_router[l])

    x2d = x.reshape(n_tok, d)
    x_meta = jnp.pad(meta.astype(F32), ((NPAD, 0), (0, 0)))
    n1 = norm1_w[l][None, :]
    proj = _inproj(x2d, n1, w_perm, ROW_TILE).reshape(bsz, seq, N_PROJ)
    proj_meta = _inproj(x_meta, n1, w_perm, CHUNK)
    o_gla = _gla(proj, proj_meta, wa_f, ba_f, wa_b, ba_b, gla_norm_w[l][None, :])
    y_ssd = _ssd(proj, proj_meta, conv_w[l], conv_b[l][None, :], dtb, alog, e_f, e_b, dexp,
                 ssd_norm_w[l][None, :])

    wo = w_out[l].astype(BF16)
    h2, n2p, idx, gates, rank, cnt = _outproj(
        x2d, o_gla.reshape(n_tok, GLA_DV), y_ssd.reshape(n_tok, SSD_INNER),
        wo[:GLA_DV], wo[GLA_DV:], norm2_w[l][None, :], wr_hi, wr_lo, b_r, ROW_TILE)

    counts = cnt[0, :N_EXPERTS].astype(I32)
    tables, starts, n_work = _work_tables(counts, n_tok * TOP_K, EXPERT_TILE)
    dest = (rank + jnp.take(starts, idx)).astype(I32).reshape(-1)

    xs = _dispatch(dest, n2p, ROW_TILE)
    ys = _experts(tables, xs, w_gu[l].astype(BF16), b_gu[l][:, None, :],
                  w_dn[l].astype(BF16), b_dn[l][:, None, :], EXPERT_TILE, n_work)
    out = _combine(dest, gates, h2, norm_f_w[None, :], ys, COMBINE_TILE)
    return out.reshape(bsz, seq, d)
```

```python
import functools

import jax
import jax.numpy as jnp
from jax import lax
from jax.experimental import pallas as pl
from jax.experimental.pallas import tpu as pltpu

F32 = jnp.float32
BF16 = jnp.bfloat16
I32 = jnp.int32
U32 = jnp.uint32

D_MODEL = 1024
N_META = 16
CHUNK = 64
NPAD = CHUNK - N_META
GLA_HEADS = 4
GLA_DK = 512
GLA_DV = 1024
GLA_HEAD_K = GLA_DK // GLA_HEADS
GLA_HEAD_V = GLA_DV // GLA_HEADS
GLA_RANK = 16
GLA_GATE_NORM = 16.0
SSD_INNER = 1024
SSD_HEAD_DIM = 64
SSD_HEADS = SSD_INNER // SSD_HEAD_DIM
SSD_GROUPS = 2
SSD_GROUP_HEADS = SSD_HEADS // SSD_GROUPS
SSD_GROUP_CH = SSD_INNER // SSD_GROUPS
SSD_STATE = 128
SSD_CONV = 5
N_EXPERTS = 32
TOP_K = 4
D_FF = 1024
SWIGLU_LIMIT = 7.0
SWIGLU_ALPHA = 1.702
EPS = 1e-6

LANES = 128
HALO = 16

COL_Q = 0
COL_K = 512
COL_V = 1024
COL_G = 2048
COL_Z = 3072
COL_X = 4096
COL_B = 5120
COL_C = 5376
COL_SMALL = 5632
N_PROJ = 5760
LANE_AF = 0
LANE_AB = 16
LANE_DTF = 32
LANE_DTB = 48

ROW_TILE = 512
EXPERT_TILE = 512
COMBINE_TILE = 256
VMEM_LIMIT = 56 * 1024 * 1024


def _dot(a, b):
    return jnp.dot(a, b, preferred_element_type=F32)


def _dot_nt(a, b):
    return lax.dot_general(a, b, (((1,), (1,)), ((), ())), preferred_element_type=F32)


def _dot_tn(a, b):
    return lax.dot_general(a, b, (((0,), (0,)), ((), ())), preferred_element_type=F32)


def _split(x):
    hi = x.astype(BF16)
    lo = (x - hi.astype(F32)).astype(BF16)
    return hi, lo


def _sel_dot_l(m01, x):
    hi, lo = _split(x)
    return _dot(m01, hi) + _dot(m01, lo)


def _sel_dot_r(x, m01):
    hi, lo = _split(x)
    return _dot(hi, m01) + _dot(lo, m01)


def _softplus(x):
    return jnp.maximum(x, 0.0) + jnp.log(1.0 + jnp.exp(-jnp.abs(x)))


def _log_sigmoid(x):
    return jnp.minimum(x, 0.0) - jnp.log(1.0 + jnp.exp(-jnp.abs(x)))


def _silu(x):
    return x * jax.nn.sigmoid(x)


def _rms(x, w):
    return x * lax.rsqrt(jnp.mean(x * x, axis=-1, keepdims=True) + EPS) * w


def _pack_bf16_pairs(x):
    w = x.shape[1] // 2
    bits = lax.bitcast_convert_type(x.astype(BF16).astype(F32), U32)
    return (bits[:, :w] >> 16) | (bits[:, w:] & jnp.uint32(0xFFFF0000))


def _unpack_bf16_pairs(p):
    lo = lax.bitcast_convert_type(p << 16, F32)
    hi = lax.bitcast_convert_type(p & jnp.uint32(0xFFFF0000), F32)
    return jnp.concatenate([lo, hi], axis=1)


def _tri(n, *, lower, inclusive):
    r = lax.broadcasted_iota(I32, (n, n), 0)
    c = lax.broadcasted_iota(I32, (n, n), 1)
    if lower:
        return (c <= r) if inclusive else (c < r)
    return (c >= r) if inclusive else (c > r)


def _inproj_body(x_ref, nw_ref, w_ref, o_ref, *, col_chunks):
    xn = _rms(x_ref[...], nw_ref[...]).astype(BF16)
    for lo, hi in col_chunks:
        o_ref[:, lo:hi] = _dot(xn, w_ref[:, lo:hi]).astype(BF16)


def _inproj(x2d, norm_w, w_perm, tile):
    rows = x2d.shape[0]
    col_chunks = tuple((c, min(c + 1024, N_PROJ)) for c in range(0, N_PROJ, 1024))
    return pl.pallas_call(
        functools.partial(_inproj_body, col_chunks=col_chunks),
        grid=(rows // tile,),
        in_specs=[
            pl.BlockSpec((tile, D_MODEL), lambda i: (i, 0)),
            pl.BlockSpec((1, D_MODEL), lambda i: (0, 0)),
            pl.BlockSpec((D_MODEL, N_PROJ), lambda i: (0, 0)),
        ],
        out_specs=pl.BlockSpec((tile, N_PROJ), lambda i: (i, 0)),
        out_shape=jax.ShapeDtypeStruct((rows, N_PROJ), BF16),
        compiler_params=pltpu.CompilerParams(
            dimension_semantics=("arbitrary",), vmem_limit_bytes=VMEM_LIMIT),
        name="inproj",
    )(x2d, norm_w, w_perm)


def _gla_chunk(q, k, v, sm, wa, ba, st_ref, tri, amask, *, fwd, rowmask=None, need_out=True):
    z = _dot(sm, wa) + ba
    lg = _log_sigmoid(z) * (1.0 / GLA_GATE_NORM)
    if rowmask is not None:
        lg = jnp.where(rowmask, lg, 0.0)
    b = _sel_dot_l(tri, lg)
    tot = b[CHUNK - 1:CHUNK] if fwd else b[0:1]
    kf = k.astype(F32)
    kend = (kf * jnp.exp(tot - b)).astype(BF16)
    st = st_ref[...]
    out = None
    if need_out:
        qd = (q.astype(F32) * (GLA_HEAD_K ** -0.5) * jnp.exp(b)).astype(BF16)
        kd = (kf * jnp.exp(-b)).astype(BF16)
        att = jnp.where(amask, _dot_nt(qd, kd), 0.0).astype(BF16)
        out = _dot(att, v) + _dot_nt(qd, st.astype(BF16))
    st_ref[...] = st * jnp.exp(tot) + _dot_tn(v, kend)
    return out


def _gla_body(q_ref, k_ref, v_ref, g_ref, sm_ref, qm_ref, km_ref, vm_ref, smm_ref,
              waf_ref, baf_ref, wab_ref, bab_ref, nw_ref, o_ref, st_ref, *, n_chunks):
    tri_f = _tri(CHUNK, lower=True, inclusive=True)
    tri_b = _tri(CHUNK, lower=False, inclusive=True)
    amask_f = tri_f
    amask_b = _tri(CHUNK, lower=False, inclusive=False)
    tri_f16 = tri_f.astype(BF16)
    tri_b16 = tri_b.astype(BF16)
    waf, baf = waf_ref[...], baf_ref[...]
    wab, bab = wab_ref[...], bab_ref[...]
    nw = nw_ref[...]

    def rows(ref, off):
        return ref[pl.ds(off, CHUNK), :]

    st_ref[...] = jnp.zeros_like(st_ref)

    def bwd_body(i, carry):
        off = pl.multiple_of((n_chunks - 1 - i) * CHUNK, CHUNK)
        out = _gla_chunk(rows(q_ref, off), rows(k_ref, off), rows(v_ref, off), rows(sm_ref, off),
                         wab, bab, st_ref, tri_b16, amask_b, fwd=False)
        o_ref[pl.ds(off, CHUNK), :] = out.astype(BF16)
        return carry

    lax.fori_loop(0, n_chunks, bwd_body, 0)

    st_ref[...] = jnp.zeros_like(st_ref)
    meta_rows = lax.broadcasted_iota(I32, (CHUNK, LANES), 0) >= NPAD
    _gla_chunk(qm_ref[...], km_ref[...], vm_ref[...], smm_ref[...], waf, baf, st_ref,
               tri_f16, amask_f, fwd=True, rowmask=meta_rows, need_out=False)

    def fwd_body(r, carry):
        off = pl.multiple_of(r * CHUNK, CHUNK)
        out = _gla_chunk(rows(q_ref, off), rows(k_ref, off), rows(v_ref, off), rows(sm_ref, off),
                         waf, baf, st_ref, tri_f16, amask_f, fwd=True)
        o = out + o_ref[pl.ds(off, CHUNK), :].astype(F32)
        gate = _silu(rows(g_ref, off).astype(F32))
        o_ref[pl.ds(off, CHUNK), :] = (_rms(o, nw) * gate).astype(BF16)
        return carry

    lax.fori_loop(0, n_chunks, fwd_body, 0)


def _gla(proj, proj_meta, wa_f, ba_f, wa_b, ba_b, norm_w):
    bsz, seq, _ = proj.shape
    kb, vb = COL_K // GLA_HEAD_K, COL_V // GLA_HEAD_V
    gb, sb = COL_G // GLA_HEAD_V, COL_SMALL // LANES
    real = lambda width, base: pl.BlockSpec((None, seq, width), lambda b, h: (b, 0, base + h))
    meta = lambda width, base: pl.BlockSpec((CHUNK, width), lambda b, h: (0, base + h))
    per_head = lambda rows_: pl.BlockSpec((rows_, GLA_HEAD_K), lambda b, h: (0, h))
    return pl.pallas_call(
        functools.partial(_gla_body, n_chunks=seq // CHUNK),
        grid=(bsz, GLA_HEADS),
        in_specs=[
            real(GLA_HEAD_K, 0), real(GLA_HEAD_K, kb), real(GLA_HEAD_V, vb), real(GLA_HEAD_V, gb),
            pl.BlockSpec((None, seq, LANES), lambda b, h: (b, 0, sb)),
            meta(GLA_HEAD_K, 0), meta(GLA_HEAD_K, kb), meta(GLA_HEAD_V, vb),
            pl.BlockSpec((CHUNK, LANES), lambda b, h: (0, sb)),
            per_head(LANES), per_head(1), per_head(LANES), per_head(1),
            pl.BlockSpec((1, GLA_HEAD_V), lambda b, h: (0, 0)),
        ],
        out_specs=pl.BlockSpec((None, seq, GLA_HEAD_V), lambda b, h: (b, 0, h)),
        out_shape=jax.ShapeDtypeStruct((bsz, seq, GLA_DV), BF16),
        scratch_shapes=[pltpu.VMEM((GLA_HEAD_V, GLA_HEAD_K), F32)],
        compiler_params=pltpu.CompilerParams(
            dimension_semantics=("arbitrary", "arbitrary"), vmem_limit_bytes=VMEM_LIMIT),
        name="gla",
    )(proj, proj, proj, proj, proj, proj_meta, proj_meta, proj_meta, proj_meta,
      wa_f, ba_f, wa_b, ba_b, norm_w)


def _conv_silu(win, cw, cb):
    half = (SSD_CONV - 1) // 2
    acc = cb
    for j in range(SSD_CONV):
        lo = HALO - half + j
        acc = acc + win[lo:lo + CHUNK, :] * cw[j:j + 1, :]
    return _silu(acc)


def _ssd_chunk(xc, bc, cc, sm, dtb, a_row, e_mat, st_ref, consts, *, fwd, rowmask=None, need_out=True):
    tri16, irep, irep16, ones16, mask_rep, bdmask = consts
    dt = _softplus(sm + dtb)
    if rowmask is not None:
        dt = jnp.where(rowmask, dt, 0.0)
    cs = _sel_dot_l(tri16, dt * a_row)
    cs_e = _sel_dot_r(cs, e_mat)
    dt_e = _sel_dot_r(dt, e_mat)
    tot_e = cs_e[CHUNK - 1:CHUNK] if fwd else cs_e[0:1]
    xdt = xc * dt_e
    xend = (xdt * jnp.exp(tot_e - cs_e)).astype(BF16)
    bc16 = bc.astype(BF16)
    st = st_ref[...]
    y = None
    if need_out:
        cc16 = cc.astype(BF16)
        cb_rep = _dot(_dot_nt(cc16, bc16).astype(BF16), irep16)
        cs_row = _sel_dot_l(ones16, jnp.where(irep, cs_e, 0.0))
        decay = jnp.where(mask_rep, jnp.exp(jnp.minimum(cs_e - cs_row, 0.0)), 0.0)
        w = (cb_rep * decay).astype(BF16)
        xdt16 = xdt.astype(BF16)
        xbd = jnp.where(bdmask, jnp.concatenate([xdt16] * SSD_GROUP_HEADS, axis=0), jnp.zeros((), BF16))
        y = _dot(w, xbd) + _dot(cc16, st.astype(BF16)) * jnp.exp(cs_e)
    st_ref[...] = st * jnp.exp(tot_e) + _dot_tn(bc16, xend)
    return y


def _ssd_body(x_ref, z_ref, b_ref, c_ref, sm_ref, xm_ref, bm_ref, cm_ref, smm_ref,
              cwx_ref, cbx_ref, cwb_ref, cbb_ref, cwc_ref, cbc_ref, dtb_ref, alog_ref,
              ef_ref, eb_ref, dexp_ref, nw_ref, o_ref, st_ref, *, n_chunks):
    gh, p = SSD_GROUP_HEADS, SSD_HEAD_DIM
    width = gh * p
    lane_s = lax.broadcasted_iota(I32, (CHUNK, width), 1) % p
    row_t = lax.broadcasted_iota(I32, (CHUNK, width), 0)
    irep = lane_s == row_t
    irep16 = irep.astype(BF16)
    ones16 = jnp.ones((CHUNK, CHUNK), BF16)
    mask_f = lane_s <= row_t
    mask_b = lane_s > row_t
    bd_r = lax.broadcasted_iota(I32, (width, width), 0) // p
    bd_c = lax.broadcasted_iota(I32, (width, width), 1) // p
    bdmask = bd_r == bd_c
    tri_f16 = _tri(CHUNK, lower=True, inclusive=True).astype(BF16)
    tri_b16 = _tri(CHUNK, lower=False, inclusive=True).astype(BF16)
    consts_f = (tri_f16, irep, irep16, ones16, mask_f, bdmask)
    consts_b = (tri_b16, irep, irep16, ones16, mask_b, bdmask)

    cwx, cbx = cwx_ref[...], cbx_ref[...]
    cwb, cbb = cwb_ref[...], cbb_ref[...]
    cwc, cbc = cwc_ref[...], cbc_ref[...]
    dtb = dtb_ref[...]
    a_row = -jnp.exp(alog_ref[...])
    ef, eb = ef_ref[...], eb_ref[...]
    dexp, nw = dexp_ref[...], nw_ref[...]
    seq = n_chunks * CHUNK

    def window(ref, mref, r):
        off = pl.multiple_of(r * CHUNK, CHUNK)
        poff = pl.multiple_of(jnp.maximum(off - HALO, 0), HALO)
        noff = pl.multiple_of(jnp.minimum(off + CHUNK, seq - HALO), HALO)
        prev = jnp.where(r == 0, mref[CHUNK - HALO:, :], ref[pl.ds(poff, HALO), :]).astype(F32)
        nxt = ref[pl.ds(noff, HALO), :].astype(F32)
        nxt = jnp.where(r == n_chunks - 1, 0.0, nxt)
        return jnp.concatenate([prev, ref[pl.ds(off, CHUNK), :].astype(F32), nxt], axis=0)

    def conv_chunk(r):
        xc = _conv_silu(window(x_ref, xm_ref, r), cwx, cbx)
        bc = _conv_silu(window(b_ref, bm_ref, r), cwb, cbb)
        cc = _conv_silu(window(c_ref, cm_ref, r), cwc, cbc)
        return xc, bc, cc

    st_ref[...] = jnp.zeros_like(st_ref)

    def bwd_body(i, carry):
        r = n_chunks - 1 - i
        off = pl.multiple_of(r * CHUNK, CHUNK)
        xc, bc, cc = conv_chunk(r)
        sm = sm_ref[pl.ds(off, CHUNK), :].astype(F32)
        y = _ssd_chunk(xc, bc, cc, sm, dtb, a_row, eb, st_ref, consts_b, fwd=False)
        o_ref[pl.ds(off, CHUNK), :] = y.astype(BF16)
        return carry

    lax.fori_loop(0, n_chunks, bwd_body, 0)

    st_ref[...] = jnp.zeros_like(st_ref)

    def meta_window(mref, ref):
        zeros = jnp.zeros((HALO, mref.shape[1]), F32)
        return jnp.concatenate([zeros, mref[...].astype(F32), ref[0:HALO, :].astype(F32)], axis=0)

    def meta_mask(width_):
        return lax.broadcasted_iota(I32, (CHUNK, width_), 0) >= NPAD

    xc = jnp.where(meta_mask(width), _conv_silu(meta_window(xm_ref, x_ref), cwx, cbx), 0.0)
    bc = jnp.where(meta_mask(SSD_STATE), _conv_silu(meta_window(bm_ref, b_ref), cwb, cbb), 0.0)
    cc = jnp.where(meta_mask(SSD_STATE), _conv_silu(meta_window(cm_ref, c_ref), cwc, cbc), 0.0)
    _ssd_chunk(xc, bc, cc, smm_ref[...].astype(F32), dtb, a_row, ef, st_ref, consts_f,
               fwd=True, rowmask=meta_mask(LANES), need_out=False)

    def fwd_body(r, carry):
        off = pl.multiple_of(r * CHUNK, CHUNK)
        xc, bc, cc = conv_chunk(r)
        sm = sm_ref[pl.ds(off, CHUNK), :].astype(F32)
        y = _ssd_chunk(xc, bc, cc, sm, dtb, a_row, ef, st_ref, consts_f, fwd=True)
        y = y + o_ref[pl.ds(off, CHUNK), :].astype(F32) + xc * dexp
        y = y * _silu(z_ref[pl.ds(off, CHUNK), :].astype(F32))
        o_ref[pl.ds(off, CHUNK), :] = _rms(y, nw).astype(BF16)
        return carry

    lax.fori_loop(0, n_chunks, fwd_body, 0)


def _ssd(proj, proj_meta, conv_w, conv_b, dtb, alog, e_f, e_b, dexp, norm_w):
    bsz, seq, _ = proj.shape
    gc = SSD_GROUP_CH
    xb, zb = COL_X // gc, COL_Z // gc
    bb, cb, sb = COL_B // SSD_STATE, COL_C // SSD_STATE, COL_SMALL // LANES
    real = lambda width, base: pl.BlockSpec((None, seq, width), lambda b, g: (b, 0, base + g))
    meta = lambda width, base: pl.BlockSpec((CHUNK, width), lambda b, g: (0, base + g))
    cpar = lambda rows_, width, base: pl.BlockSpec((rows_, width), lambda b, g: (0, base + g))
    cbb_, ccb_ = SSD_INNER // SSD_STATE, SSD_INNER // SSD_STATE + SSD_GROUPS
    const = lambda shape: pl.BlockSpec(shape, lambda b, g: (0, 0))
    return pl.pallas_call(
        functools.partial(_ssd_body, n_chunks=seq // CHUNK),
        grid=(bsz, SSD_GROUPS),
        in_specs=[
            real(gc, xb), real(gc, zb), real(SSD_STATE, bb), real(SSD_STATE, cb),
            pl.BlockSpec((None, seq, LANES), lambda b, g: (b, 0, sb)),
            meta(gc, xb), meta(SSD_STATE, bb), meta(SSD_STATE, cb),
            pl.BlockSpec((CHUNK, LANES), lambda b, g: (0, sb)),
            cpar(SSD_CONV, gc, 0), cpar(1, gc, 0),
            cpar(SSD_CONV, SSD_STATE, cbb_), cpar(1, SSD_STATE, cbb_),
            cpar(SSD_CONV, SSD_STATE, ccb_), cpar(1, SSD_STATE, ccb_),
            const((1, LANES)), const((1, LANES)),
            pl.BlockSpec((None, LANES, gc), lambda b, g: (g, 0, 0)),
            pl.BlockSpec((None, LANES, gc), lambda b, g: (g, 0, 0)),
            pl.BlockSpec((1, gc), lambda b, g: (0, g)),
            pl.BlockSpec((1, gc), lambda b, g: (0, g)),
        ],
        out_specs=pl.BlockSpec((None, seq, gc), lambda b, g: (b, 0, g)),
        out_shape=jax.ShapeDtypeStruct((bsz, seq, SSD_INNER), BF16),
        scratch_shapes=[pltpu.VMEM((SSD_STATE, gc), F32)],
        compiler_params=pltpu.CompilerParams(
            dimension_semantics=("arbitrary", "arbitrary"), vmem_limit_bytes=VMEM_LIMIT),
        name="ssd",
    )(proj, proj, proj, proj, proj, proj_meta, proj_meta, proj_meta, proj_meta,
      conv_w, conv_b, conv_w, conv_b, conv_w, conv_b, dtb, alog, e_f, e_b, dexp, norm_w)


def _outproj_body(x_ref, og_ref, ys_ref, wo1_ref, wo2_ref, n2w_ref, wrh_ref, wrl_ref, br_ref,
                  h2_ref, n2p_ref, idx_ref, gate_ref, rank_ref, cnt_ref, carry_ref, *, tile):
    i = pl.program_id(0)

    @pl.when(i == 0)
    def _():
        carry_ref[...] = jnp.zeros_like(carry_ref)

    h2 = x_ref[...] + _dot(og_ref[...], wo1_ref[...]) + _dot(ys_ref[...], wo2_ref[...])
    h2_ref[...] = h2
    n2 = _rms(h2, n2w_ref[...])
    n2p_ref[...] = _pack_bf16_pairs(n2)

    nh, nl = _split(n2)
    wrh, wrl = wrh_ref[...], wrl_ref[...]
    logits = _dot(nh, wrh) + _dot(nh, wrl) + _dot(nl, wrh) + br_ref[...]

    lane = lax.broadcasted_iota(I32, (tile, LANES), 1)
    lane_f = lane.astype(F32)
    lane4 = lax.broadcasted_iota(I32, (tile, TOP_K), 1)
    vals, onehots = [], []
    idx_out = jnp.zeros((tile, TOP_K), I32)
    work = logits
    for k in range(TOP_K):
        m = jnp.max(work, axis=-1, keepdims=True)
        first = jnp.min(jnp.where(work == m, lane_f, float(LANES)), axis=-1, keepdims=True)
        oh = lane_f == first
        work = jnp.where(oh, -jnp.inf, work)
        vals.append(m)
        onehots.append(oh)
        idx_out = jnp.where(lane4 == k, first.astype(I32), idx_out)
    idx_ref[...] = idx_out

    exps = [jnp.exp(v - vals[0]) for v in vals]
    inv = 1.0 / (exps[0] + exps[1] + exps[2] + exps[3])
    gate_out = jnp.zeros((tile, TOP_K), F32)
    for k in range(TOP_K):
        gate_out = jnp.where(lane4 == k, exps[k] * inv, gate_out)
    gate_ref[...] = gate_out

    any_oh = (onehots[0] | onehots[1] | onehots[2] | onehots[3])
    any16 = jnp.where(any_oh, 1.0, 0.0).astype(BF16)
    before = _dot(_tri(tile, lower=True, inclusive=False).astype(BF16), any16) + carry_ref[...]
    rank_out = jnp.zeros((tile, TOP_K), I32)
    for k in range(TOP_K):
        rk = jnp.sum(jnp.where(onehots[k], before, 0.0), axis=-1, keepdims=True)
        rank_out = jnp.where(lane4 == k, rk.astype(I32), rank_out)
    rank_ref[...] = rank_out
    carry = carry_ref[...] + jnp.sum(any16.astype(F32), axis=0, keepdims=True)
    carry_ref[...] = carry
    cnt_ref[...] = carry


def _outproj(x2d, o_gla, y_ssd, w_out1, w_out2, norm2_w, wr_hi, wr_lo, b_r, tile):
    rows = x2d.shape[0]
    row = lambda width: pl.BlockSpec((tile, width), lambda i: (i, 0))
    const = lambda shape: pl.BlockSpec(shape, lambda i: (0, 0))
    return pl.pallas_call(
        functools.partial(_outproj_body, tile=tile),
        grid=(rows // tile,),
        in_specs=[
            row(D_MODEL), row(GLA_DV), row(SSD_INNER),
            const((GLA_DV, D_MODEL)), const((SSD_INNER, D_MODEL)), const((1, D_MODEL)),
            const((D_MODEL, LANES)), const((D_MODEL, LANES)), const((1, LANES)),
        ],
        out_specs=[
            row(D_MODEL), row(D_MODEL // 2), row(TOP_K), row(TOP_K), row(TOP_K), const((1, LANES)),
        ],
        out_shape=[
            jax.ShapeDtypeStruct((rows, D_MODEL), F32),
            jax.ShapeDtypeStruct((rows, D_MODEL // 2), U32),
            jax.ShapeDtypeStruct((rows, TOP_K), I32),
            jax.ShapeDtypeStruct((rows, TOP_K), F32),
            jax.ShapeDtypeStruct((rows, TOP_K), I32),
            jax.ShapeDtypeStruct((1, LANES), F32),
        ],
        scratch_shapes=[pltpu.VMEM((1, LANES), F32)],
        compiler_params=pltpu.CompilerParams(
            dimension_semantics=("arbitrary",), vmem_limit_bytes=VMEM_LIMIT),
        name="outproj_router",
    )(x2d, o_gla, y_ssd, w_out1, w_out2, norm2_w, wr_hi, wr_lo, b_r)


def _row_copy(src, src_row, dst, dst_row, sem):
    return pltpu.make_async_copy(src.at[pl.ds(src_row, 1)], dst.at[pl.ds(dst_row, 1)], sem)


def _dispatch_body(dest_ref, n2p_ref, xs_hbm, sem, *, tile):
    def start(t, carry):
        for k in range(TOP_K):
            _row_copy(n2p_ref, t, xs_hbm, dest_ref[t * TOP_K + k], sem).start()
        return carry

    lax.fori_loop(0, tile, start, 0)

    def wait(t, carry):
        for k in range(TOP_K):
            _row_copy(n2p_ref, 0, xs_hbm, 0, sem).wait()
        return carry

    lax.fori_loop(0, tile, wait, 0)


def _dispatch(dest_flat, n2p, tile):
    rows, width = n2p.shape
    return pl.pallas_call(
        functools.partial(_dispatch_body, tile=tile),
        grid=(rows // tile,),
        in_specs=[
            pl.BlockSpec((tile * TOP_K,), lambda i: (i,), memory_space=pltpu.SMEM),
            pl.BlockSpec((tile, width), lambda i: (i, 0)),
        ],
        out_specs=pl.BlockSpec(memory_space=pl.ANY),
        out_shape=jax.ShapeDtypeStruct((rows * TOP_K, width), U32),
        scratch_shapes=[pltpu.SemaphoreType.DMA(())],
        compiler_params=pltpu.CompilerParams(dimension_semantics=("arbitrary",)),
        name="dispatch",
    )(dest_flat, n2p)


def _expert_body(blk_ref, exp_ref, lo_ref, hi_ref, first_ref, nw_ref,
                 x_ref, wgu_ref, bgu_ref, wdn_ref, bdn_ref, o_ref, *, tile):
    w = pl.program_id(0)

    @pl.when(w < nw_ref[0])
    def _():
        x = _unpack_bf16_pairs(x_ref[...]).astype(BF16)
        acc = jnp.zeros((tile, D_MODEL), F32) + bdn_ref[...]
        half = D_FF // 2
        for j in range(2):
            gt = _dot(x, wgu_ref[:, j * half:(j + 1) * half]) + bgu_ref[:, j * half:(j + 1) * half]
            up = (_dot(x, wgu_ref[:, D_FF + j * half:D_FF + (j + 1) * half])
                  + bgu_ref[:, D_FF + j * half:D_FF + (j + 1) * half])
            gt = jnp.minimum(gt, SWIGLU_LIMIT)
            up = jnp.clip(up, -SWIGLU_LIMIT, SWIGLU_LIMIT)
            act = (up + 1.0) * gt * jax.nn.sigmoid(gt * SWIGLU_ALPHA)
            acc = acc + _dot(act.astype(BF16), wdn_ref[j * half:(j + 1) * half, :])
        packed = _pack_bf16_pairs(acc)

        @pl.when(first_ref[w] == 1)
        def _():
            o_ref[...] = packed

        @pl.when(first_ref[w] == 0)
        def _():
            r = lax.broadcasted_iota(I32, packed.shape, 0)
            mine = (r >= lo_ref[w]) & (r < hi_ref[w])
            o_ref[...] = jnp.where(mine, packed, o_ref[...])


def _experts(tables, xs, w_gu, b_gu, w_dn, b_dn, tile, n_work):
    rows, width = xs.shape
    grid_spec = pltpu.PrefetchScalarGridSpec(
        num_scalar_prefetch=6,
        grid=(n_work,),
        in_specs=[
            pl.BlockSpec((tile, width), lambda w, blk, ex, lo, hi, fi, nw: (blk[w], 0)),
            pl.BlockSpec((None, D_MODEL, 2 * D_FF), lambda w, blk, ex, lo, hi, fi, nw: (ex[w], 0, 0)),
            pl.BlockSpec((None, 1, 2 * D_FF), lambda w, blk, ex, lo, hi, fi, nw: (ex[w], 0, 0)),
            pl.BlockSpec((None, D_FF, D_MODEL), lambda w, blk, ex, lo, hi, fi, nw: (ex[w], 0, 0)),
            pl.BlockSpec((None, 1, D_MODEL), lambda w, blk, ex, lo, hi, fi, nw: (ex[w], 0, 0)),
        ],
        out_specs=pl.BlockSpec((tile, width), lambda w, blk, ex, lo, hi, fi, nw: (blk[w], 0)),
    )
    return pl.pallas_call(
        functools.partial(_expert_body, tile=tile),
        grid_spec=grid_spec,
        out_shape=jax.ShapeDtypeStruct((rows, width), U32),
        compiler_params=pltpu.CompilerParams(
            dimension_semantics=("arbitrary",), vmem_limit_bytes=VMEM_LIMIT),
        name="experts",
    )(*tables, xs, w_gu, b_gu, w_dn, b_dn)


def _combine_body(dest_ref, gate_ref, h2_ref, nfw_ref, ys_hbm, o_ref, buf, sem, *, tile):
    def start(t, carry):
        for k in range(TOP_K):
            pltpu.make_async_copy(ys_hbm.at[pl.ds(dest_ref[t * TOP_K + k], 1)],
                                  buf.at[k, pl.ds(t, 1)], sem).start()
        return carry

    lax.fori_loop(0, tile, start, 0)

    def wait(t, carry):
        for k in range(TOP_K):
            pltpu.make_async_copy(ys_hbm.at[pl.ds(0, 1)], buf.at[k, pl.ds(0, 1)], sem).wait()
        return carry

    lax.fori_loop(0, tile, wait, 0)

    gate = gate_ref[...]
    h3 = h2_ref[...]
    for k in range(TOP_K):
        h3 = h3 + gate[:, k:k + 1] * _unpack_bf16_pairs(buf[k])
    o_ref[...] = _rms(h3, nfw_ref[...])


def _combine(dest_flat, gates, h2, norm_f_w, ys, tile):
    rows = h2.shape[0]
    width = ys.shape[1]
    return pl.pallas_call(
        functools.partial(_combine_body, tile=tile),
        grid=(rows // tile,),
        in_specs=[
            pl.BlockSpec((tile * TOP_K,), lambda i: (i,), memory_space=pltpu.SMEM),
            pl.BlockSpec((tile, TOP_K), lambda i: (i, 0)),
            pl.BlockSpec((tile, D_MODEL), lambda i: (i, 0)),
            pl.BlockSpec((1, D_MODEL), lambda i: (0, 0)),
            pl.BlockSpec(memory_space=pl.ANY),
        ],
        out_specs=pl.BlockSpec((tile, D_MODEL), lambda i: (i, 0)),
        out_shape=jax.ShapeDtypeStruct((rows, D_MODEL), F32),
        scratch_shapes=[pltpu.VMEM((TOP_K, tile, width), U32), pltpu.SemaphoreType.DMA(())],
        compiler_params=pltpu.CompilerParams(
            dimension_semantics=("arbitrary",), vmem_limit_bytes=VMEM_LIMIT),
        name="combine_final",
    )(dest_flat, gates, h2, norm_f_w, ys)


def _work_tables(counts, n_rows, tile):
    n_blocks = n_rows // tile
    n_work = n_blocks + N_EXPERTS
    ends = jnp.cumsum(counts)
    starts = ends - counts
    first_blk = starts // tile
    last_blk = (ends - 1) // tile
    nb = jnp.where(counts > 0, last_blk - first_blk + 1, 0)
    wend = jnp.cumsum(nb)
    wstart = wend - nb
    total = wend[-1]
    w = jnp.minimum(jnp.arange(n_work, dtype=I32), total - 1)
    ex = jnp.minimum(jnp.sum(wend[None, :] <= w[:, None], axis=1), N_EXPERTS - 1).astype(I32)
    blk = (first_blk[ex] + (w - wstart[ex])).astype(I32)
    lo = (jnp.maximum(starts[ex], blk * tile) - blk * tile).astype(I32)
    hi = (jnp.minimum(ends[ex], (blk + 1) * tile) - blk * tile).astype(I32)
    prev_blk = jnp.concatenate([jnp.full((1,), -1, I32), blk[:-1]])
    first = (blk != prev_blk).astype(I32)
    return (blk, ex, lo, hi, first, total.reshape(1).astype(I32)), starts, n_work


def kernel(x, meta, norm1_w, w_in, gla_wa2_f, gla_ba2_f, gla_wa2_b, gla_ba2_b, gla_norm_w, conv_w, conv_b, dt_bias_f, dt_bias_b, a_log_f, a_log_b, ssd_d, ssd_norm_w, w_out, norm2_w, w_router, b_router, w_gu, b_gu, w_dn, b_dn, norm_f_w):
    bsz, seq, d = x.shape
    n_tok = bsz * seq
    l = 0

    wi = w_in[l]
    a_cols = wi[:, 3072:3104]
    dt_cols = wi[:, 5664:5696]
    w_perm = jnp.concatenate(
        [wi[:, :3072], wi[:, 3104:5664], a_cols, dt_cols,
         jnp.zeros((d, N_PROJ - COL_SMALL - 64), F32)], axis=1).astype(BF16)

    def lane_rows(w, lane0):
        return jnp.zeros((LANES, w.shape[1]), F32).at[lane0:lane0 + w.shape[0]].set(w)

    wa_f = lane_rows(gla_wa2_f[l], LANE_AF).astype(BF16)
    wa_b = lane_rows(gla_wa2_b[l], LANE_AB).astype(BF16)
    ba_f = gla_ba2_f[l][None, :]
    ba_b = gla_ba2_b[l][None, :]

    def lane_vec(vf, vb):
        z = jnp.zeros((1, LANES), F32)
        return z.at[0, LANE_DTF:LANE_DTF + SSD_HEADS].set(vf).at[0, LANE_DTB:LANE_DTB + SSD_HEADS].set(vb)

    dtb = lane_vec(dt_bias_f[l], dt_bias_b[l])
    alog = lane_vec(a_log_f[l], a_log_b[l])
    lane_id = jnp.arange(LANES)[None, :, None]
    head_id = (jnp.arange(SSD_GROUP_CH) // SSD_HEAD_DIM)[None, None, :]
    grp = jnp.arange(SSD_GROUPS)[:, None, None] * SSD_GROUP_HEADS
    e_f = (lane_id == LANE_DTF + grp + head_id).astype(BF16)
    e_b = (lane_id == LANE_DTB + grp + head_id).astype(BF16)
    dexp = jnp.repeat(ssd_d[l], SSD_HEAD_DIM)[None, :]

    wr = jnp.zeros((d, LANES), F32).at[:, :N_EXPERTS].set(w_router[l])
    wr_hi = wr.astype(BF16)
    wr_lo = (wr - wr_hi.astype(F32)).astype(BF16)
    b_r = jnp.full((1, LANES), -1e30, F32).at[0, :N_EXPERTS].set(b_router[l])

    x2d = x.reshape(n_tok, d)
    x_meta = jnp.pad(meta.astype(F32), ((NPAD, 0), (0, 0)))
    n1 = norm1_w[l][None, :]
    proj = _inproj(x2d, n1, w_perm, ROW_TILE).reshape(bsz, seq, N_PROJ)
    proj_meta = _inproj(x_meta, n1, w_perm, CHUNK)
    o_gla = _gla(proj, proj_meta, wa_f, ba_f, wa_b, ba_b, gla_norm_w[l][None, :])
    y_ssd = _ssd(proj, proj_meta, conv_w[l], conv_b[l][None, :], dtb, alog, e_f, e_b, dexp,
                 ssd_norm_w[l][None, :])

    wo = w_out[l].astype(BF16)
    h2, n2p, idx, gates, rank, cnt = _outproj(
        x2d, o_gla.reshape(n_tok, GLA_DV), y_ssd.reshape(n_tok, SSD_INNER),
        wo[:GLA_DV], wo[GLA_DV:], norm2_w[l][None, :], wr_hi, wr_lo, b_r, ROW_TILE)

    counts = cnt[0, :N_EXPERTS].astype(I32)
    tables, starts, n_work = _work_tables(counts, n_tok * TOP_K, EXPERT_TILE)
    dest = (rank + jnp.take(starts, idx)).astype(I32).reshape(-1)

    xs = _dispatch(dest, n2p, ROW_TILE)
    ys = _experts(tables, xs, w_gu[l].astype(BF16), b_gu[l][:, None, :],
                  w_dn[l].astype(BF16), b_dn[l][:, None, :], EXPERT_TILE, n_work)
    out = _combine(dest, gates, h2, norm_f_w[None, :], ys, COMBINE_TILE)
    return out.reshape(bsz, seq, d)
```

```python
import functools

import jax
import jax.numpy as jnp
from jax import lax
from jax.experimental import pallas as pl
from jax.experimental.pallas import tpu as pltpu

F32 = jnp.float32
BF16 = jnp.bfloat16
I32 = jnp.int32
U32 = jnp.uint32

D_MODEL = 1024
N_META = 16
CHUNK = 64
NPAD = CHUNK - N_META
GLA_HEADS = 4
GLA_DK = 512
GLA_DV = 1024
GLA_HEAD_K = GLA_DK // GLA_HEADS
GLA_HEAD_V = GLA_DV // GLA_HEADS
GLA_RANK = 16
GLA_GATE_NORM = 16.0
SSD_INNER = 1024
SSD_HEAD_DIM = 64
SSD_HEADS = SSD_INNER // SSD_HEAD_DIM
SSD_GROUPS = 2
SSD_GROUP_HEADS = SSD_HEADS // SSD_GROUPS
SSD_GROUP_CH = SSD_INNER // SSD_GROUPS
SSD_STATE = 128
SSD_CONV = 5
N_EXPERTS = 32
TOP_K = 4
D_FF = 1024
SWIGLU_LIMIT = 7.0
SWIGLU_ALPHA = 1.702
EPS = 1e-6

LANES = 128
ROW_WORDS = D_MODEL // 2
ROW_SUB = ROW_WORDS // LANES
HALO = 16

COL_Q = 0
COL_K = 512
COL_V = 1024
COL_G = 2048
COL_Z = 3072
COL_X = 4096
COL_B = 5120
COL_C = 5376
COL_SMALL = 5632
N_PROJ = 5760
LANE_AF = 0
LANE_AB = 16
LANE_DTF = 32
LANE_DTB = 48

ROW_TILE = 512
EXPERT_TILE = 512
COMBINE_TILE = 256
VMEM_LIMIT = 56 * 1024 * 1024


def _dot(a, b):
    return jnp.dot(a, b, preferred_element_type=F32)


def _dot_nt(a, b):
    return lax.dot_general(a, b, (((1,), (1,)), ((), ())), preferred_element_type=F32)


def _dot_tn(a, b):
    return lax.dot_general(a, b, (((0,), (0,)), ((), ())), preferred_element_type=F32)


def _split(x):
    hi = x.astype(BF16)
    lo = (x - hi.astype(F32)).astype(BF16)
    return hi, lo


def _sel_dot_l(m01, x):
    hi, lo = _split(x)
    return _dot(m01, jnp.concatenate([hi, lo], axis=0))


def _sel_dot_r(x, m01):
    hi, lo = _split(x)
    return _dot(jnp.concatenate([hi, lo], axis=1), m01)


def _twice_cols(m):
    return jnp.concatenate([m, m], axis=1)


def _softplus(x):
    return jnp.maximum(x, 0.0) + jnp.log(1.0 + jnp.exp(-jnp.abs(x)))


def _log_sigmoid(x):
    return jnp.minimum(x, 0.0) - jnp.log(1.0 + jnp.exp(-jnp.abs(x)))


def _silu(x):
    return x * jax.nn.sigmoid(x)


def _rms(x, w):
    return x * lax.rsqrt(jnp.mean(x * x, axis=-1, keepdims=True) + EPS) * w


def _pack_bf16_pairs(x):
    w = x.shape[1] // 2
    bits = lax.bitcast_convert_type(x.astype(BF16).astype(F32), U32)
    return (bits[:, :w] >> 16) | (bits[:, w:] & jnp.uint32(0xFFFF0000))


def _unpack_bf16_pairs(p):
    lo = lax.bitcast_convert_type(p << 16, F32)
    hi = lax.bitcast_convert_type(p & jnp.uint32(0xFFFF0000), F32)
    return jnp.concatenate([lo, hi], axis=1)


def _load_rows(ref3):
    return jnp.concatenate([ref3[:, j, :] for j in range(ROW_SUB)], axis=1)


def _store_rows(ref3, val, keep=None):
    for j in range(ROW_SUB):
        piece = val[:, j * LANES:(j + 1) * LANES]
        if keep is not None:
            piece = jnp.where(keep, piece, ref3[:, j, :])
        ref3[:, j, :] = piece


def _tri(n, *, lower, inclusive):
    r = lax.broadcasted_iota(I32, (n, n), 0)
    c = lax.broadcasted_iota(I32, (n, n), 1)
    if lower:
        return (c <= r) if inclusive else (c < r)
    return (c >= r) if inclusive else (c > r)


def _inproj_body(x_ref, nw_ref, w_ref, o_ref, *, col_chunks):
    xn = _rms(x_ref[...], nw_ref[...]).astype(BF16)
    for lo, hi in col_chunks:
        o_ref[:, lo:hi] = _dot(xn, w_ref[:, lo:hi]).astype(BF16)


def _inproj(x2d, norm_w, w_perm, tile):
    rows = x2d.shape[0]
    col_chunks = tuple((c, min(c + 1024, N_PROJ)) for c in range(0, N_PROJ, 1024))
    return pl.pallas_call(
        functools.partial(_inproj_body, col_chunks=col_chunks),
        grid=(rows // tile,),
        in_specs=[
            pl.BlockSpec((tile, D_MODEL), lambda i: (i, 0)),
            pl.BlockSpec((1, D_MODEL), lambda i: (0, 0)),
            pl.BlockSpec((D_MODEL, N_PROJ), lambda i: (0, 0)),
        ],
        out_specs=pl.BlockSpec((tile, N_PROJ), lambda i: (i, 0)),
        out_shape=jax.ShapeDtypeStruct((rows, N_PROJ), BF16),
        compiler_params=pltpu.CompilerParams(
            dimension_semantics=("arbitrary",), vmem_limit_bytes=VMEM_LIMIT),
        name="inproj",
    )(x2d, norm_w, w_perm)


GLA_HEADS_PER_STEP = 2


def _gla_gates(sm, wa, ba, tri16, *, fwd, rowmask=None):
    z = _dot(sm, wa) + ba
    lg = _log_sigmoid(z) * (1.0 / GLA_GATE_NORM)
    if rowmask is not None:
        lg = jnp.where(rowmask, lg, 0.0)
    b = _sel_dot_l(tri16, lg)
    tot = b[CHUNK - 1:CHUNK] if fwd else b[0:1]
    return b, tot


def _gla_head(q, k, v, b, tot, st_ref, amask, *, need_out=True):
    kf = k.astype(F32)
    kend = (kf * jnp.exp(tot - b)).astype(BF16)
    st = st_ref[...]
    out = None
    if need_out:
        qd = (q.astype(F32) * (GLA_HEAD_K ** -0.5) * jnp.exp(b)).astype(BF16)
        kd = (kf * jnp.exp(-b)).astype(BF16)
        att = jnp.where(amask, _dot_nt(qd, kd), 0.0).astype(BF16)
        out = _dot(att, v) + _dot_nt(qd, st.astype(BF16))
    st_ref[...] = st * jnp.exp(tot) + _dot_tn(v, kend)
    return out


def _gla_body(q_ref, k_ref, v_ref, g_ref, sm_ref, qm_ref, km_ref, vm_ref, smm_ref,
              waf_ref, baf_ref, wab_ref, bab_ref, nw_ref, o_ref, stf_ref, stb_ref, *, n_chunks):
    heads = GLA_HEADS_PER_STEP
    tri_f = _tri(CHUNK, lower=True, inclusive=True)
    tri_b = _tri(CHUNK, lower=False, inclusive=True)
    amask_b = _tri(CHUNK, lower=False, inclusive=False)
    tri_f16 = _twice_cols(tri_f.astype(BF16))
    tri_b16 = _twice_cols(tri_b.astype(BF16))
    waf, baf = waf_ref[...], baf_ref[...]
    wab, bab = wab_ref[...], bab_ref[...]
    nw = nw_ref[...]
    dk = lambda h: slice(h * GLA_HEAD_K, (h + 1) * GLA_HEAD_K)
    dv = lambda h: slice(h * GLA_HEAD_V, (h + 1) * GLA_HEAD_V)

    stf_ref[...] = jnp.zeros_like(stf_ref)
    stb_ref[...] = jnp.zeros_like(stb_ref)

    meta_rows = lax.broadcasted_iota(I32, (CHUNK, heads * GLA_HEAD_K), 0) >= NPAD
    b, tot = _gla_gates(smm_ref[...], waf, baf, tri_f16, fwd=True, rowmask=meta_rows)
    for h in range(heads):
        _gla_head(qm_ref[:, dk(h)], km_ref[:, dk(h)], vm_ref[:, dv(h)], b[:, dk(h)], tot[:, dk(h)],
                  stf_ref.at[h], tri_f, need_out=False)

    def pair(i, finalize):
        for fwd in (True, False):
            j = i if fwd else n_chunks - 1 - i
            rows = pl.ds(pl.multiple_of(j * CHUNK, CHUNK), CHUNK)
            if fwd:
                b, tot = _gla_gates(sm_ref[rows, :], waf, baf, tri_f16, fwd=True)
            else:
                b, tot = _gla_gates(sm_ref[rows, :], wab, bab, tri_b16, fwd=False)
            st_ref = stf_ref if fwd else stb_ref
            for h in range(heads):
                out = _gla_head(q_ref[rows, dk(h)], k_ref[rows, dk(h)], v_ref[rows, dv(h)],
                                b[:, dk(h)], tot[:, dk(h)], st_ref.at[h], tri_f if fwd else amask_b)
                if finalize:
                    o = out + o_ref[rows, dv(h)].astype(F32)
                    gate = _silu(g_ref[rows, dv(h)].astype(F32))
                    o_ref[rows, dv(h)] = (_rms(o, nw) * gate).astype(BF16)
                else:
                    o_ref[rows, dv(h)] = out.astype(BF16)

    def first_half(i, carry):
        pair(i, False)
        return carry

    def second_half(i, carry):
        pair(i, True)
        return carry

    lax.fori_loop(0, n_chunks // 2, first_half, 0, unroll=2)
    lax.fori_loop(n_chunks // 2, n_chunks, second_half, 0, unroll=2)


def _gla(proj, proj_meta, wa_f, ba_f, wa_b, ba_b, norm_w):
    bsz, seq, _ = proj.shape
    assert (seq // CHUNK) % 2 == 0
    hp = GLA_HEADS_PER_STEP
    wk, wv = hp * GLA_HEAD_K, hp * GLA_HEAD_V
    kb, vb, gb, sb = COL_K // wk, COL_V // wv, COL_G // wv, COL_SMALL // LANES
    real = lambda width, base: pl.BlockSpec((None, seq, width), lambda b, h: (b, 0, base + h))
    meta = lambda width, base: pl.BlockSpec((CHUNK, width), lambda b, h: (0, base + h))
    per_step = lambda rows_: pl.BlockSpec((rows_, wk), lambda b, h: (0, h))
    return pl.pallas_call(
        functools.partial(_gla_body, n_chunks=seq // CHUNK),
        grid=(bsz, GLA_HEADS // hp),
        in_specs=[
            real(wk, 0), real(wk, kb), real(wv, vb), real(wv, gb),
            pl.BlockSpec((None, seq, LANES), lambda b, h: (b, 0, sb)),
            meta(wk, 0), meta(wk, kb), meta(wv, vb),
            pl.BlockSpec((CHUNK, LANES), lambda b, h: (0, sb)),
            per_step(LANES), per_step(1), per_step(LANES), per_step(1),
            pl.BlockSpec((1, GLA_HEAD_V), lambda b, h: (0, 0)),
        ],
        out_specs=pl.BlockSpec((None, seq, wv), lambda b, h: (b, 0, h)),
        out_shape=jax.ShapeDtypeStruct((bsz, seq, GLA_DV), BF16),
        scratch_shapes=[pltpu.VMEM((hp, GLA_HEAD_V, GLA_HEAD_K), F32),
                        pltpu.VMEM((hp, GLA_HEAD_V, GLA_HEAD_K), F32)],
        compiler_params=pltpu.CompilerParams(
            dimension_semantics=("arbitrary", "arbitrary"), vmem_limit_bytes=VMEM_LIMIT),
        name="gla",
    )(proj, proj, proj, proj, proj, proj_meta, proj_meta, proj_meta, proj_meta,
      wa_f, ba_f, wa_b, ba_b, norm_w)


def _conv_silu(win, cw, cb):
    half = (SSD_CONV - 1) // 2
    acc = cb
    for j in range(SSD_CONV):
        lo = HALO - half + j
        acc = acc + win[lo:lo + CHUNK, :] * cw[j:j + 1, :]
    return _silu(acc)


def _ssd_chunk(xc, bc16, cc16, sm, dtb, a_row, e_mat, st_ref, consts, *, fwd, rowmask=None, need_out=True):
    tri16, irep, irep16, ones16, mask_rep, bdmask = consts
    dt = _softplus(sm + dtb)
    if rowmask is not None:
        dt = jnp.where(rowmask, dt, 0.0)
    cs = _sel_dot_l(tri16, dt * a_row)
    both = _sel_dot_r(jnp.concatenate([cs, dt], axis=0), e_mat)
    cs_e, dt_e = both[:CHUNK], both[CHUNK:]
    tot_e = cs_e[CHUNK - 1:CHUNK] if fwd else cs_e[0:1]
    xdt = xc * dt_e
    xend = (xdt * jnp.exp(tot_e - cs_e)).astype(BF16)
    st = st_ref[...]
    y = None
    if need_out:
        cb_rep = _dot(_dot_nt(cc16, bc16).astype(BF16), irep16)
        cs_row = _sel_dot_l(ones16, jnp.where(irep, cs_e, 0.0))
        decay = jnp.where(mask_rep, jnp.exp(jnp.minimum(cs_e - cs_row, 0.0)), 0.0)
        w = (cb_rep * decay).astype(BF16)
        xdt16 = xdt.astype(BF16)
        xbd = jnp.where(bdmask, jnp.concatenate([xdt16] * SSD_GROUP_HEADS, axis=0), jnp.zeros((), BF16))
        y = _dot(w, xbd) + _dot(cc16, st.astype(BF16)) * jnp.exp(cs_e)
    st_ref[...] = st * jnp.exp(tot_e) + _dot_tn(bc16, xend)
    return y


def _ssd_body(x_ref, z_ref, b_ref, c_ref, sm_ref, xm_ref, bm_ref, cm_ref, smm_ref,
              cwx_ref, cbx_ref, cwb_ref, cbb_ref, cwc_ref, cbc_ref, dtb_ref, alog_ref,
              ef_ref, eb_ref, dexp_ref, nw_ref, o_ref, stf_ref, stb_ref, xc_ref, bc_ref, cc_ref,
              *, n_chunks):
    gh, p = SSD_GROUP_HEADS, SSD_HEAD_DIM
    width = gh * p
    lane_s = lax.broadcasted_iota(I32, (CHUNK, width), 1) % p
    row_t = lax.broadcasted_iota(I32, (CHUNK, width), 0)
    irep = lane_s == row_t
    irep16 = irep.astype(BF16)
    ones16 = jnp.ones((CHUNK, 2 * CHUNK), BF16)
    mask_f = lane_s <= row_t
    mask_b = lane_s > row_t
    bd_r = lax.broadcasted_iota(I32, (width, width), 0) // p
    bd_c = lax.broadcasted_iota(I32, (width, width), 1) // p
    bdmask = bd_r == bd_c
    tri_f16 = _twice_cols(_tri(CHUNK, lower=True, inclusive=True).astype(BF16))
    tri_b16 = _twice_cols(_tri(CHUNK, lower=False, inclusive=True).astype(BF16))
    consts_f = (tri_f16, irep, irep16, ones16, mask_f, bdmask)
    consts_b = (tri_b16, irep, irep16, ones16, mask_b, bdmask)

    cwx, cbx = cwx_ref[...], cbx_ref[...]
    cwb, cbb = cwb_ref[...], cbb_ref[...]
    cwc, cbc = cwc_ref[...], cbc_ref[...]
    dtb = dtb_ref[...]
    a_row = -jnp.exp(alog_ref[...])
    ef, eb = ef_ref[...], eb_ref[...]
    dexp, nw = dexp_ref[...], nw_ref[...]
    seq = n_chunks * CHUNK

    def window(ref, mref, r):
        off = pl.multiple_of(r * CHUNK, CHUNK)
        poff = pl.multiple_of(jnp.maximum(off - HALO, 0), HALO)
        noff = pl.multiple_of(jnp.minimum(off + CHUNK, seq - HALO), HALO)
        prev = jnp.where(r == 0, mref[CHUNK - HALO:, :], ref[pl.ds(poff, HALO), :]).astype(F32)
        nxt = ref[pl.ds(noff, HALO), :].astype(F32)
        nxt = jnp.where(r == n_chunks - 1, 0.0, nxt)
        return jnp.concatenate([prev, ref[pl.ds(off, CHUNK), :].astype(F32), nxt], axis=0)

    def conv_body(r, carry):
        rows = pl.ds(pl.multiple_of(r * CHUNK, CHUNK), CHUNK)
        xc_ref[rows, :] = _conv_silu(window(x_ref, xm_ref, r), cwx, cbx).astype(BF16)
        bc_ref[rows, :] = _conv_silu(window(b_ref, bm_ref, r), cwb, cbb).astype(BF16)
        cc_ref[rows, :] = _conv_silu(window(c_ref, cm_ref, r), cwc, cbc).astype(BF16)
        return carry

    lax.fori_loop(0, n_chunks, conv_body, 0, unroll=2)

    stf_ref[...] = jnp.zeros_like(stf_ref)
    stb_ref[...] = jnp.zeros_like(stb_ref)

    def meta_window(mref, ref):
        zeros = jnp.zeros((HALO, mref.shape[1]), F32)
        return jnp.concatenate([zeros, mref[...].astype(F32), ref[0:HALO, :].astype(F32)], axis=0)

    def meta_mask(width_):
        return lax.broadcasted_iota(I32, (CHUNK, width_), 0) >= NPAD

    xc = jnp.where(meta_mask(width), _conv_silu(meta_window(xm_ref, x_ref), cwx, cbx), 0.0)
    bc = jnp.where(meta_mask(SSD_STATE), _conv_silu(meta_window(bm_ref, b_ref), cwb, cbb), 0.0)
    cc = jnp.where(meta_mask(SSD_STATE), _conv_silu(meta_window(cm_ref, c_ref), cwc, cbc), 0.0)
    _ssd_chunk(xc, bc.astype(BF16), cc.astype(BF16), smm_ref[...].astype(F32), dtb, a_row, ef, stf_ref,
               consts_f, fwd=True, rowmask=meta_mask(LANES), need_out=False)

    def pair(i, finalize):
        for fwd in (True, False):
            j = i if fwd else n_chunks - 1 - i
            rows = pl.ds(pl.multiple_of(j * CHUNK, CHUNK), CHUNK)
            xc = xc_ref[rows, :].astype(F32)
            y = _ssd_chunk(xc, bc_ref[rows, :], cc_ref[rows, :], sm_ref[rows, :].astype(F32), dtb, a_row,
                           ef if fwd else eb, stf_ref if fwd else stb_ref,
                           consts_f if fwd else consts_b, fwd=fwd)
            if finalize:
                y = y + o_ref[rows, :].astype(F32) + xc * dexp
                y = y * _silu(z_ref[rows, :].astype(F32))
                o_ref[rows, :] = _rms(y, nw).astype(BF16)
            else:
                o_ref[rows, :] = y.astype(BF16)

    def first_half(i, carry):
        pair(i, False)
        return carry

    def second_half(i, carry):
        pair(i, True)
        return carry

    lax.fori_loop(0, n_chunks // 2, first_half, 0, unroll=2)
    lax.fori_loop(n_chunks // 2, n_chunks, second_half, 0, unroll=2)


def _ssd(proj, proj_meta, conv_w, conv_b, dtb, alog, e_f, e_b, dexp, norm_w):
    bsz, seq, _ = proj.shape
    gc = SSD_GROUP_CH
    xb, zb = COL_X // gc, COL_Z // gc
    bb, cb, sb = COL_B // SSD_STATE, COL_C // SSD_STATE, COL_SMALL // LANES
    real = lambda width, base: pl.BlockSpec((None, seq, width), lambda b, g: (b, 0, base + g))
    meta = lambda width, base: pl.BlockSpec((CHUNK, width), lambda b, g: (0, base + g))
    cpar = lambda rows_, width, base: pl.BlockSpec((rows_, width), lambda b, g: (0, base + g))
    cbb_, ccb_ = SSD_INNER // SSD_STATE, SSD_INNER // SSD_STATE + SSD_GROUPS
    const = lambda shape: pl.BlockSpec(shape, lambda b, g: (0, 0))
    return pl.pallas_call(
        functools.partial(_ssd_body, n_chunks=seq // CHUNK),
        grid=(bsz, SSD_GROUPS),
        in_specs=[
            real(gc, xb), real(gc, zb), real(SSD_STATE, bb), real(SSD_STATE, cb),
            pl.BlockSpec((None, seq, LANES), lambda b, g: (b, 0, sb)),
            meta(gc, xb), meta(SSD_STATE, bb), meta(SSD_STATE, cb),
            pl.BlockSpec((CHUNK, LANES), lambda b, g: (0, sb)),
            cpar(SSD_CONV, gc, 0), cpar(1, gc, 0),
            cpar(SSD_CONV, SSD_STATE, cbb_), cpar(1, SSD_STATE, cbb_),
            cpar(SSD_CONV, SSD_STATE, ccb_), cpar(1, SSD_STATE, ccb_),
            const((1, LANES)), const((1, LANES)),
            pl.BlockSpec((None, 2 * LANES, gc), lambda b, g: (g, 0, 0)),
            pl.BlockSpec((None, 2 * LANES, gc), lambda b, g: (g, 0, 0)),
            pl.BlockSpec((1, gc), lambda b, g: (0, g)),
            pl.BlockSpec((1, gc), lambda b, g: (0, g)),
        ],
        out_specs=pl.BlockSpec((None, seq, gc), lambda b, g: (b, 0, g)),
        out_shape=jax.ShapeDtypeStruct((bsz, seq, SSD_INNER), BF16),
        scratch_shapes=[pltpu.VMEM((SSD_STATE, gc), F32), pltpu.VMEM((SSD_STATE, gc), F32),
                        pltpu.VMEM((seq, gc), BF16), pltpu.VMEM((seq, SSD_STATE), BF16),
                        pltpu.VMEM((seq, SSD_STATE), BF16)],
        compiler_params=pltpu.CompilerParams(
            dimension_semantics=("arbitrary", "arbitrary"), vmem_limit_bytes=VMEM_LIMIT),
        name="ssd",
    )(proj, proj, proj, proj, proj, proj_meta, proj_meta, proj_meta, proj_meta,
      conv_w, conv_b, conv_w, conv_b, conv_w, conv_b, dtb, alog, e_f, e_b, dexp, norm_w)


def _outproj_body(x_ref, og_ref, ys_ref, wo1_ref, wo2_ref, n2w_ref, wrh_ref, wrl_ref, br_ref,
                  h2_ref, n2p_ref, idx_ref, gate_ref, rank_ref, cnt_ref, carry_ref, *, tile):
    i = pl.program_id(0)

    @pl.when(i == 0)
    def _():
        carry_ref[...] = jnp.zeros_like(carry_ref)

    h2 = x_ref[...] + _dot(og_ref[...], wo1_ref[...]) + _dot(ys_ref[...], wo2_ref[...])
    h2_ref[...] = h2
    n2 = _rms(h2, n2w_ref[...])
    _store_rows(n2p_ref, _pack_bf16_pairs(n2))

    nh, nl = _split(n2)
    wrh, wrl = wrh_ref[...], wrl_ref[...]
    logits = _dot(nh, wrh) + _dot(nh, wrl) + _dot(nl, wrh) + br_ref[...]

    lane = lax.broadcasted_iota(I32, (tile, LANES), 1)
    lane_f = lane.astype(F32)
    lane4 = lax.broadcasted_iota(I32, (tile, TOP_K), 1)
    vals, onehots = [], []
    idx_out = jnp.zeros((tile, TOP_K), I32)
    work = logits
    for k in range(TOP_K):
        m = jnp.max(work, axis=-1, keepdims=True)
        first = jnp.min(jnp.where(work == m, lane_f, float(LANES)), axis=-1, keepdims=True)
        oh = lane_f == first
        work = jnp.where(oh, -jnp.inf, work)
        vals.append(m)
        onehots.append(oh)
        idx_out = jnp.where(lane4 == k, first.astype(I32), idx_out)
    idx_ref[...] = idx_out

    exps = [jnp.exp(v - vals[0]) for v in vals]
    inv = 1.0 / (exps[0] + exps[1] + exps[2] + exps[3])
    gate_out = jnp.zeros((tile, TOP_K), F32)
    for k in range(TOP_K):
        gate_out = jnp.where(lane4 == k, exps[k] * inv, gate_out)
    gate_ref[...] = gate_out

    any_oh = (onehots[0] | onehots[1] | onehots[2] | onehots[3])
    any16 = jnp.where(any_oh, 1.0, 0.0).astype(BF16)
    before = _dot(_tri(tile, lower=True, inclusive=False).astype(BF16), any16) + carry_ref[...]
    rank_out = jnp.zeros((tile, TOP_K), I32)
    for k in range(TOP_K):
        rk = jnp.sum(jnp.where(onehots[k], before, 0.0), axis=-1, keepdims=True)
        rank_out = jnp.where(lane4 == k, rk.astype(I32), rank_out)
    rank_ref[...] = rank_out
    carry = carry_ref[...] + jnp.sum(any16.astype(F32), axis=0, keepdims=True)
    carry_ref[...] = carry
    cnt_ref[...] = carry


def _outproj(x2d, o_gla, y_ssd, w_out1, w_out2, norm2_w, wr_hi, wr_lo, b_r, tile):
    rows = x2d.shape[0]
    row = lambda width: pl.BlockSpec((tile, width), lambda i: (i, 0))
    const = lambda shape: pl.BlockSpec(shape, lambda i: (0, 0))
    return pl.pallas_call(
        functools.partial(_outproj_body, tile=tile),
        grid=(rows // tile,),
        in_specs=[
            row(D_MODEL), row(GLA_DV), row(SSD_INNER),
            const((GLA_DV, D_MODEL)), const((SSD_INNER, D_MODEL)), const((1, D_MODEL)),
            const((D_MODEL, LANES)), const((D_MODEL, LANES)), const((1, LANES)),
        ],
        out_specs=[
            row(D_MODEL), pl.BlockSpec((tile, ROW_SUB, LANES), lambda i: (i, 0, 0)),
            row(TOP_K), row(TOP_K), row(TOP_K), const((1, LANES)),
        ],
        out_shape=[
            jax.ShapeDtypeStruct((rows, D_MODEL), F32),
            jax.ShapeDtypeStruct((rows, ROW_SUB, LANES), U32),
            jax.ShapeDtypeStruct((rows, TOP_K), I32),
            jax.ShapeDtypeStruct((rows, TOP_K), F32),
            jax.ShapeDtypeStruct((rows, TOP_K), I32),
            jax.ShapeDtypeStruct((1, LANES), F32),
        ],
        scratch_shapes=[pltpu.VMEM((1, LANES), F32)],
        compiler_params=pltpu.CompilerParams(
            dimension_semantics=("arbitrary",), vmem_limit_bytes=VMEM_LIMIT),
        name="outproj_router",
    )(x2d, o_gla, y_ssd, w_out1, w_out2, norm2_w, wr_hi, wr_lo, b_r)


def _row_copy(src, src_row, dst, dst_row, sem):
    return pltpu.make_async_copy(src.at[src_row], dst.at[dst_row], sem)


def _dispatch_body(dest_ref, n2p_ref, xs_hbm, sem, *, tile):
    def start(t, carry):
        for k in range(TOP_K):
            _row_copy(n2p_ref, t, xs_hbm, dest_ref[t * TOP_K + k], sem).start()
        return carry

    lax.fori_loop(0, tile, start, 0)

    def wait(t, carry):
        for k in range(TOP_K):
            _row_copy(n2p_ref, 0, xs_hbm, 0, sem).wait()
        return carry

    lax.fori_loop(0, tile, wait, 0)


def _dispatch(dest_flat, n2p, tile):
    rows = n2p.shape[0]
    return pl.pallas_call(
        functools.partial(_dispatch_body, tile=tile),
        grid=(rows // tile,),
        in_specs=[
            pl.BlockSpec((tile * TOP_K,), lambda i: (i,), memory_space=pltpu.SMEM),
            pl.BlockSpec((tile, ROW_SUB, LANES), lambda i: (i, 0, 0)),
        ],
        out_specs=pl.BlockSpec(memory_space=pl.ANY),
        out_shape=jax.ShapeDtypeStruct((rows * TOP_K, ROW_SUB, LANES), U32),
        scratch_shapes=[pltpu.SemaphoreType.DMA(())],
        compiler_params=pltpu.CompilerParams(dimension_semantics=("arbitrary",)),
        name="dispatch",
    )(dest_flat, n2p)


def _expert_body(blk_ref, exp_ref, lo_ref, hi_ref, first_ref, newexp_ref, nw_ref,
                 x_ref, wgu_ref, bgu_ref, wdn_ref, bdn_ref, o_ref, wgu16_ref, wdn16_ref, *, tile):
    w = pl.program_id(0)
    cast_rows = 64

    @pl.when((w < nw_ref[0]) & (newexp_ref[w] == 1))
    def _():
        def cast(i, carry):
            rows = pl.ds(pl.multiple_of(i * cast_rows, cast_rows), cast_rows)
            wgu16_ref[rows, :] = wgu_ref[rows, :].astype(BF16)
            wdn16_ref[rows, :] = wdn_ref[rows, :].astype(BF16)
            return carry

        lax.fori_loop(0, D_MODEL // cast_rows, cast, 0)

    @pl.when(w < nw_ref[0])
    def _():
        x = _unpack_bf16_pairs(_load_rows(x_ref)).astype(BF16)
        acc = jnp.zeros((tile, D_MODEL), F32) + bdn_ref[...]
        half = D_FF // 2
        for j in range(2):
            gt = _dot(x, wgu16_ref[:, j * half:(j + 1) * half]) + bgu_ref[:, j * half:(j + 1) * half]
            up = (_dot(x, wgu16_ref[:, D_FF + j * half:D_FF + (j + 1) * half])
                  + bgu_ref[:, D_FF + j * half:D_FF + (j + 1) * half])
            gt = jnp.minimum(gt, SWIGLU_LIMIT)
            up = jnp.clip(up, -SWIGLU_LIMIT, SWIGLU_LIMIT)
            act = (up + 1.0) * gt * jax.nn.sigmoid(gt * SWIGLU_ALPHA)
            acc = acc + _dot(act.astype(BF16), wdn16_ref[j * half:(j + 1) * half, :])
        packed = _pack_bf16_pairs(acc)

        @pl.when(first_ref[w] == 1)
        def _():
            _store_rows(o_ref, packed)

        @pl.when(first_ref[w] == 0)
        def _():
            r = lax.broadcasted_iota(I32, (tile, LANES), 0)
            _store_rows(o_ref, packed, keep=(r >= lo_ref[w]) & (r < hi_ref[w]))


def _experts(tables, xs, w_gu, b_gu, w_dn, b_dn, tile, n_work):
    rows = xs.shape[0]
    row_block = pl.BlockSpec((tile, ROW_SUB, LANES), lambda w, blk, ex, lo, hi, fi, ne, nw: (blk[w], 0, 0))
    grid_spec = pltpu.PrefetchScalarGridSpec(
        num_scalar_prefetch=7,
        grid=(n_work,),
        in_specs=[
            row_block,
            pl.BlockSpec((None, D_MODEL, 2 * D_FF), lambda w, blk, ex, lo, hi, fi, ne, nw: (ex[w], 0, 0)),
            pl.BlockSpec((None, 1, 2 * D_FF), lambda w, blk, ex, lo, hi, fi, ne, nw: (ex[w], 0, 0)),
            pl.BlockSpec((None, D_FF, D_MODEL), lambda w, blk, ex, lo, hi, fi, ne, nw: (ex[w], 0, 0)),
            pl.BlockSpec((None, 1, D_MODEL), lambda w, blk, ex, lo, hi, fi, ne, nw: (ex[w], 0, 0)),
        ],
        out_specs=row_block,
        scratch_shapes=[pltpu.VMEM((D_MODEL, 2 * D_FF), BF16), pltpu.VMEM((D_FF, D_MODEL), BF16)],
    )
    return pl.pallas_call(
        functools.partial(_expert_body, tile=tile),
        grid_spec=grid_spec,
        out_shape=jax.ShapeDtypeStruct((rows, ROW_SUB, LANES), U32),
        compiler_params=pltpu.CompilerParams(
            dimension_semantics=("arbitrary",), vmem_limit_bytes=VMEM_LIMIT),
        name="experts",
    )(*tables, xs, w_gu, b_gu, w_dn, b_dn)


def _combine_body(dest_ref, gate_ref, h2_ref, nfw_ref, ys_hbm, o_ref, buf, sem, *, tile):
    def start(t, carry):
        for k in range(TOP_K):
            pltpu.make_async_copy(ys_hbm.at[dest_ref[t * TOP_K + k]], buf.at[k, t], sem).start()
        return carry

    lax.fori_loop(0, tile, start, 0)

    def wait(t, carry):
        for k in range(TOP_K):
            pltpu.make_async_copy(ys_hbm.at[0], buf.at[k, 0], sem).wait()
        return carry

    lax.fori_loop(0, tile, wait, 0)

    gate = gate_ref[...]
    h3 = h2_ref[...]
    for k in range(TOP_K):
        h3 = h3 + gate[:, k:k + 1] * _unpack_bf16_pairs(_load_rows(buf.at[k]))
    o_ref[...] = _rms(h3, nfw_ref[...])


def _combine(dest_flat, gates, h2, norm_f_w, ys, tile):
    rows = h2.shape[0]
    return pl.pallas_call(
        functools.partial(_combine_body, tile=tile),
        grid=(rows // tile,),
        in_specs=[
            pl.BlockSpec((tile * TOP_K,), lambda i: (i,), memory_space=pltpu.SMEM),
            pl.BlockSpec((tile, TOP_K), lambda i: (i, 0)),
            pl.BlockSpec((tile, D_MODEL), lambda i: (i, 0)),
            pl.BlockSpec((1, D_MODEL), lambda i: (0, 0)),
            pl.BlockSpec(memory_space=pl.ANY),
        ],
        out_specs=pl.BlockSpec((tile, D_MODEL), lambda i: (i, 0)),
        out_shape=jax.ShapeDtypeStruct((rows, D_MODEL), F32),
        scratch_shapes=[pltpu.VMEM((TOP_K, tile, ROW_SUB, LANES), U32), pltpu.SemaphoreType.DMA(())],
        compiler_params=pltpu.CompilerParams(
            dimension_semantics=("arbitrary",), vmem_limit_bytes=VMEM_LIMIT),
        name="combine_final",
    )(dest_flat, gates, h2, norm_f_w, ys)


def _work_tables(counts, n_rows, tile):
    n_blocks = n_rows // tile
    n_work = n_blocks + N_EXPERTS
    ends = jnp.cumsum(counts)
    starts = ends - counts
    first_blk = starts // tile
    last_blk = (ends - 1) // tile
    nb = jnp.where(counts > 0, last_blk - first_blk + 1, 0)
    wend = jnp.cumsum(nb)
    wstart = wend - nb
    total = wend[-1]
    w = jnp.minimum(jnp.arange(n_work, dtype=I32), total - 1)
    ex = jnp.minimum(jnp.sum(wend[None, :] <= w[:, None], axis=1), N_EXPERTS - 1).astype(I32)
    blk = (first_blk[ex] + (w - wstart[ex])).astype(I32)
    lo = (jnp.maximum(starts[ex], blk * tile) - blk * tile).astype(I32)
    hi = (jnp.minimum(ends[ex], (blk + 1) * tile) - blk * tile).astype(I32)
    prev_blk = jnp.concatenate([jnp.full((1,), -1, I32), blk[:-1]])
    first = (blk != prev_blk).astype(I32)
    prev_ex = jnp.concatenate([jnp.full((1,), -1, I32), ex[:-1]])
    new_ex = (ex != prev_ex).astype(I32)
    return (blk, ex, lo, hi, first, new_ex, total.reshape(1).astype(I32)), starts, n_work


def kernel(x, meta, norm1_w, w_in, gla_wa2_f, gla_ba2_f, gla_wa2_b, gla_ba2_b, gla_norm_w, conv_w, conv_b, dt_bias_f, dt_bias_b, a_log_f, a_log_b, ssd_d, ssd_norm_w, w_out, norm2_w, w_router, b_router, w_gu, b_gu, w_dn, b_dn, norm_f_w):
    bsz, seq, d = x.shape
    n_tok = bsz * seq
    l = 0

    wi = w_in[l]
    a_cols = wi[:, 3072:3104]
    dt_cols = wi[:, 5664:5696]
    w_perm = jnp.concatenate(
        [wi[:, :3072], wi[:, 3104:5664], a_cols, dt_cols,
         jnp.zeros((d, N_PROJ - COL_SMALL - 64), F32)], axis=1).astype(BF16)

    def lane_rows(w, lane0):
        return jnp.zeros((LANES, w.shape[1]), F32).at[lane0:lane0 + w.shape[0]].set(w)

    wa_f = lane_rows(gla_wa2_f[l], LANE_AF).astype(BF16)
    wa_b = lane_rows(gla_wa2_b[l], LANE_AB).astype(BF16)
    ba_f = gla_ba2_f[l][None, :]
    ba_b = gla_ba2_b[l][None, :]

    def lane_vec(vf, vb):
        z = jnp.zeros((1, LANES), F32)
        return z.at[0, LANE_DTF:LANE_DTF + SSD_HEADS].set(vf).at[0, LANE_DTB:LANE_DTB + SSD_HEADS].set(vb)

    dtb = lane_vec(dt_bias_f[l], dt_bias_b[l])
    alog = lane_vec(a_log_f[l], a_log_b[l])
    lane_id = (jnp.arange(2 * LANES) % LANES)[None, :, None]
    head_id = (jnp.arange(SSD_GROUP_CH) // SSD_HEAD_DIM)[None, None, :]
    grp = jnp.arange(SSD_GROUPS)[:, None, None] * SSD_GROUP_HEADS
    e_f = (lane_id == LANE_DTF + grp + head_id).astype(BF16)
    e_b = (lane_id == LANE_DTB + grp + head_id).astype(BF16)
    dexp = jnp.repeat(ssd_d[l], SSD_HEAD_DIM)[None, :]

    wr = jnp.zeros((d, LANES), F32).at[:, :N_EXPERTS].set(w_router[l])
    wr_hi = wr.astype(BF16)
    wr_lo = (wr - wr_hi.astype(F32)).astype(BF16)
    b_r = jnp.full((1, LANES), -1e30, F32).at[0, :N_EXPERTS].set(b_router[l])

    x2d = x.reshape(n_tok, d)
    x_meta = jnp.pad(meta.astype(F32), ((NPAD, 0), (0, 0)))
    n1 = norm1_w[l][None, :]
    proj = _inproj(x2d, n1, w_perm, ROW_TILE).reshape(bsz, seq, N_PROJ)
    proj_meta = _inproj(x_meta, n1, w_perm, CHUNK)
    o_gla = _gla(proj, proj_meta, wa_f, ba_f, wa_b, ba_b, gla_norm_w[l][None, :])
    y_ssd = _ssd(proj, proj_meta, conv_w[l], conv_b[l][None, :], dtb, alog, e_f, e_b, dexp,
                 ssd_norm_w[l][None, :])

    wo = w_out[l].astype(BF16)
    h2, n2p, idx, gates, rank, cnt = _outproj(
        x2d, o_gla.reshape(n_tok, GLA_DV), y_ssd.reshape(n_tok, SSD_INNER),
        wo[:GLA_DV], wo[GLA_DV:], norm2_w[l][None, :], wr_hi, wr_lo, b_r, ROW_TILE)

    counts = cnt[0, :N_EXPERTS].astype(I32)
    tables, starts, n_work = _work_tables(counts, n_tok * TOP_K, EXPERT_TILE)
    dest = (rank + jnp.take(starts, idx)).astype(I32).reshape(-1)

    xs = _dispatch(dest, n2p, ROW_TILE)
    ys = _experts(tables, xs, w_gu[l], b_gu[l][:, None, :], w_dn[l], b_dn[l][:, None, :],
                  EXPERT_TILE, n_work)
    out = _combine(dest, gates, h2, norm_f_w[None, :], ys, COMBINE_TILE)
    return out.reshape(bsz, seq, d)
```

```python
import functools

import jax
import jax.numpy as jnp
from jax import lax
from jax.experimental import pallas as pl
from jax.experimental.pallas import tpu as pltpu

F32 = jnp.float32
BF16 = jnp.bfloat16
I32 = jnp.int32
U32 = jnp.uint32

D_MODEL = 1024
N_META = 16
CHUNK = 64
NPAD = CHUNK - N_META
GLA_HEADS = 4
GLA_DK = 512
GLA_DV = 1024
GLA_HEAD_K = GLA_DK // GLA_HEADS
GLA_HEAD_V = GLA_DV // GLA_HEADS
GLA_RANK = 16
GLA_GATE_NORM = 16.0
SSD_INNER = 1024
SSD_HEAD_DIM = 64
SSD_HEADS = SSD_INNER // SSD_HEAD_DIM
SSD_GROUPS = 2
SSD_GROUP_HEADS = SSD_HEADS // SSD_GROUPS
SSD_GROUP_CH = SSD_INNER // SSD_GROUPS
SSD_STATE = 128
SSD_CONV = 5
N_EXPERTS = 32
TOP_K = 4
D_FF = 1024
SWIGLU_LIMIT = 7.0
SWIGLU_ALPHA = 1.702
EPS = 1e-6

LANES = 128
ROW_WORDS = D_MODEL // 2
ROW_SUB = ROW_WORDS // LANES
HALO = 16

COL_Q = 0
COL_K = 512
COL_V = 1024
COL_G = 2048
COL_Z = 3072
COL_X = 4096
COL_B = 5120
COL_C = 5376
COL_SMALL = 5632
N_PROJ = 5760
LANE_AF = 0
LANE_AB = 16
LANE_DTF = 32
LANE_DTB = 48

ROW_TILE = 512
EXPERT_TILE = 512
COMBINE_TILE = 256
VMEM_LIMIT = 56 * 1024 * 1024


def _dot(a, b):
    return jnp.dot(a, b, preferred_element_type=F32)


def _dot_nt(a, b):
    return lax.dot_general(a, b, (((1,), (1,)), ((), ())), preferred_element_type=F32)


def _dot_tn(a, b):
    return lax.dot_general(a, b, (((0,), (0,)), ((), ())), preferred_element_type=F32)


def _split(x):
    hi = x.astype(BF16)
    lo = (x - hi.astype(F32)).astype(BF16)
    return hi, lo


def _sel_dot_l(m01, x):
    hi, lo = _split(x)
    return _dot(m01, jnp.concatenate([hi, lo], axis=0))


def _sel_dot_r(x, m01):
    hi, lo = _split(x)
    return _dot(jnp.concatenate([hi, lo], axis=1), m01)


def _twice_cols(m):
    return jnp.concatenate([m, m], axis=1)


def _softplus(x):
    return jnp.maximum(x, 0.0) + jnp.log(1.0 + jnp.exp(-jnp.abs(x)))


def _log_sigmoid(x):
    return jnp.minimum(x, 0.0) - jnp.log(1.0 + jnp.exp(-jnp.abs(x)))


def _silu(x):
    return x * jax.nn.sigmoid(x)


def _rms(x, w):
    return x * lax.rsqrt(jnp.mean(x * x, axis=-1, keepdims=True) + EPS) * w


def _pack_bf16_pairs(x):
    w = x.shape[1] // 2
    bits = lax.bitcast_convert_type(x.astype(BF16).astype(F32), U32)
    return (bits[:, :w] >> 16) | (bits[:, w:] & jnp.uint32(0xFFFF0000))


def _unpack_bf16_pairs(p):
    lo = lax.bitcast_convert_type(p << 16, F32)
    hi = lax.bitcast_convert_type(p & jnp.uint32(0xFFFF0000), F32)
    return jnp.concatenate([lo, hi], axis=1)


def _load_rows(ref3):
    return jnp.concatenate([ref3[:, j, :] for j in range(ROW_SUB)], axis=1)


def _store_rows(ref3, val, keep=None):
    for j in range(ROW_SUB):
        piece = val[:, j * LANES:(j + 1) * LANES]
        if keep is not None:
            piece = jnp.where(keep, piece, ref3[:, j, :])
        ref3[:, j, :] = piece


def _tri(n, *, lower, inclusive):
    r = lax.broadcasted_iota(I32, (n, n), 0)
    c = lax.broadcasted_iota(I32, (n, n), 1)
    if lower:
        return (c <= r) if inclusive else (c < r)
    return (c >= r) if inclusive else (c > r)


def _inproj_body(x_ref, nw_ref, w_ref, o_ref, *, col_chunks):
    xn = _rms(x_ref[...], nw_ref[...]).astype(BF16)
    for lo, hi in col_chunks:
        o_ref[:, lo:hi] = _dot(xn, w_ref[:, lo:hi]).astype(BF16)


def _inproj(x2d, norm_w, w_perm, tile):
    rows = x2d.shape[0]
    col_chunks = tuple((c, min(c + 1024, N_PROJ)) for c in range(0, N_PROJ, 1024))
    return pl.pallas_call(
        functools.partial(_inproj_body, col_chunks=col_chunks),
        grid=(rows // tile,),
        in_specs=[
            pl.BlockSpec((tile, D_MODEL), lambda i: (i, 0)),
            pl.BlockSpec((1, D_MODEL), lambda i: (0, 0)),
            pl.BlockSpec((D_MODEL, N_PROJ), lambda i: (0, 0)),
        ],
        out_specs=pl.BlockSpec((tile, N_PROJ), lambda i: (i, 0)),
        out_shape=jax.ShapeDtypeStruct((rows, N_PROJ), BF16),
        compiler_params=pltpu.CompilerParams(
            dimension_semantics=("arbitrary",), vmem_limit_bytes=VMEM_LIMIT),
        name="inproj",
    )(x2d, norm_w, w_perm)


GLA_HEADS_PER_STEP = 2
GLA_STEPS_PER_ITER = 4
SSD_STEPS_PER_ITER = 4


def _gla_body(q_ref, k_ref, v_ref, g_ref, sm_ref, qm_ref, km_ref, vm_ref, smm_ref,
              waf_ref, baf_ref, wab_ref, bab_ref, nw_ref, o_ref, st_ref, *, n_chunks):
    heads = GLA_HEADS_PER_STEP
    nprob = 2 * heads
    rows_all = nprob * CHUNK
    wk = heads * GLA_HEAD_K
    dk, dv = GLA_HEAD_K, GLA_HEAD_V

    tri_f = _tri(CHUNK, lower=True, inclusive=True)
    r2 = lax.broadcasted_iota(I32, (2 * CHUNK, 2 * CHUNK), 0)
    c2 = lax.broadcasted_iota(I32, (2 * CHUNK, 2 * CHUNK), 1)
    cum2 = ((r2 < CHUNK) & (c2 <= r2)) | ((r2 >= CHUNK) & (c2 >= r2))
    tri2 = _twice_cols(cum2.astype(BF16))
    ra = lax.broadcasted_iota(I32, (rows_all, rows_all), 0)
    ca = lax.broadcasted_iota(I32, (rows_all, rows_all), 1)
    same = (ra // CHUNK) == (ca // CHUNK)
    att_mask = same & (((ra < heads * CHUNK) & (ca <= ra)) | ((ra >= heads * CHUNK) & (ca > ra)))
    rb_ = lax.broadcasted_iota(I32, (rows_all, nprob * dk), 0) // CHUNK
    cb_ = lax.broadcasted_iota(I32, (rows_all, nprob * dk), 1) // dk
    own = rb_ == cb_

    waf, baf = waf_ref[...], baf_ref[...]
    wab, bab = wab_ref[...], bab_ref[...]
    wa_cat = jnp.concatenate([waf, wab], axis=1)
    nw = nw_ref[...]
    zero16 = jnp.zeros((), BF16)

    st_ref[...] = jnp.zeros_like(st_ref)

    meta_rows = lax.broadcasted_iota(I32, (CHUNK, wk), 0) >= NPAD
    lg = _log_sigmoid(_dot(smm_ref[...], waf) + baf) * (1.0 / GLA_GATE_NORM)
    bm = _sel_dot_l(_twice_cols(tri_f.astype(BF16)), jnp.where(meta_rows, lg, 0.0))
    for h in range(heads):
        lanes = slice(h * dk, (h + 1) * dk)
        bh, toth = bm[:, lanes], bm[CHUNK - 1:CHUNK, lanes]
        kend = (km_ref[:, lanes].astype(F32) * jnp.exp(toth - bh)).astype(BF16)
        st_ref[:, lanes] = _dot_tn(vm_ref[:, h * dv:(h + 1) * dv], kend)

    def stack(f, b, w):
        return jnp.concatenate([f[:, h * w:(h + 1) * w] for h in range(heads)]
                               + [b[:, h * w:(h + 1) * w] for h in range(heads)], axis=0)

    def steps(j, finalize):
        ids = [j * GLA_STEPS_PER_ITER + s for s in range(GLA_STEPS_PER_ITER)]
        rfs = [pl.ds(pl.multiple_of(i * CHUNK, CHUNK), CHUNK) for i in ids]
        rbs = [pl.ds(pl.multiple_of((n_chunks - 1 - i) * CHUNK, CHUNK), CHUNK) for i in ids]
        zs = [_dot(jnp.concatenate([sm_ref[rf, :], sm_ref[rb, :]], axis=0), wa_cat) for rf, rb in zip(rfs, rbs)]
        b2s = []
        for z in zs:
            lg = jnp.concatenate([z[:CHUNK, :wk] + baf, z[CHUNK:, wk:] + bab], axis=0)
            b2s.append(_sel_dot_l(tri2, _log_sigmoid(lg) * (1.0 / GLA_GATE_NORM)))
        vss, qds, kends, tots, gs = [], [], [], [], []
        for b2, rf, rb in zip(b2s, rfs, rbs):
            tot_f, tot_b = b2[CHUNK - 1:CHUNK], b2[CHUNK:CHUNK + 1]
            bst = stack(b2[:CHUNK], b2[CHUNK:], dk)
            tst = stack(jnp.broadcast_to(tot_f, (CHUNK, wk)), jnp.broadcast_to(tot_b, (CHUNK, wk)), dk)
            qs = stack(q_ref[rf, :], q_ref[rb, :], dk).astype(F32)
            ks = stack(k_ref[rf, :], k_ref[rb, :], dk).astype(F32)
            qd = (qs * (GLA_HEAD_K ** -0.5) * jnp.exp(bst)).astype(BF16)
            kd = (ks * jnp.exp(-bst)).astype(BF16)
            vss.append(stack(v_ref[rf, :], v_ref[rb, :], dv))
            qds.append(qd)
            kends.append((ks * jnp.exp(tst - bst)).astype(BF16))
            tots.append(jnp.concatenate([tot_f, tot_b], axis=1))
            gs.append(_dot_nt(qd, kd))
        intras = [_dot(jnp.where(att_mask, g, 0.0).astype(BF16), vs) for g, vs in zip(gs, vss)]
        upds = [_dot_tn(vs, jnp.where(own, jnp.concatenate([kend] * nprob, axis=1), zero16))
                for vs, kend in zip(vss, kends)]
        st = st_ref[...]
        outs = []
        for qd, tot, upd, intra in zip(qds, tots, upds, intras):
            qd_own = jnp.where(own, jnp.concatenate([qd] * nprob, axis=1), zero16)
            outs.append(intra + _dot_nt(qd_own, st.astype(BF16)))
            st = st * jnp.exp(tot) + upd
        st_ref[...] = st
        for out, rf, rb in zip(outs, rfs, rbs):
            for p in range(nprob):
                rows, h = (rf if p < heads else rb), p % heads
                o = out[p * CHUNK:(p + 1) * CHUNK]
                cols = slice(h * dv, (h + 1) * dv)
                if finalize:
                    o = o + o_ref[rows, cols].astype(F32)
                    gate = _silu(g_ref[rows, cols].astype(F32))
                    o_ref[rows, cols] = (_rms(o, nw) * gate).astype(BF16)
                else:
                    o_ref[rows, cols] = o.astype(BF16)

    def first_half(j, carry):
        steps(j, False)
        return carry

    def second_half(j, carry):
        steps(j, True)
        return carry

    half_iters = n_chunks // 2 // GLA_STEPS_PER_ITER
    lax.fori_loop(0, half_iters, first_half, 0)
    lax.fori_loop(half_iters, 2 * half_iters, second_half, 0)


def _gla(proj, proj_meta, wa_f, ba_f, wa_b, ba_b, norm_w):
    bsz, seq, _ = proj.shape
    assert (seq // CHUNK) % (2 * GLA_STEPS_PER_ITER) == 0
    hp = GLA_HEADS_PER_STEP
    wk, wv = hp * GLA_HEAD_K, hp * GLA_HEAD_V
    kb, vb, gb, sb = COL_K // wk, COL_V // wv, COL_G // wv, COL_SMALL // LANES
    real = lambda width, base: pl.BlockSpec((None, seq, width), lambda b, h: (b, 0, base + h))
    meta = lambda width, base: pl.BlockSpec((CHUNK, width), lambda b, h: (0, base + h))
    per_step = lambda rows_: pl.BlockSpec((rows_, wk), lambda b, h: (0, h))
    return pl.pallas_call(
        functools.partial(_gla_body, n_chunks=seq // CHUNK),
        grid=(bsz, GLA_HEADS // hp),
        in_specs=[
            real(wk, 0), real(wk, kb), real(wv, vb), real(wv, gb),
            pl.BlockSpec((None, seq, LANES), lambda b, h: (b, 0, sb)),
            meta(wk, 0), meta(wk, kb), meta(wv, vb),
            pl.BlockSpec((CHUNK, LANES), lambda b, h: (0, sb)),
            per_step(LANES), per_step(1), per_step(LANES), per_step(1),
            pl.BlockSpec((1, GLA_HEAD_V), lambda b, h: (0, 0)),
        ],
        out_specs=pl.BlockSpec((None, seq, wv), lambda b, h: (b, 0, h)),
        out_shape=jax.ShapeDtypeStruct((bsz, seq, GLA_DV), BF16),
        scratch_shapes=[pltpu.VMEM((GLA_HEAD_V, 2 * hp * GLA_HEAD_K), F32)],
        compiler_params=pltpu.CompilerParams(
            dimension_semantics=("arbitrary", "arbitrary"), vmem_limit_bytes=VMEM_LIMIT),
        name="gla",
    )(proj, proj, proj, proj, proj, proj_meta, proj_meta, proj_meta, proj_meta,
      wa_f, ba_f, wa_b, ba_b, norm_w)


def _conv_silu(win, cw, cb):
    half = (SSD_CONV - 1) // 2
    acc = cb
    for j in range(SSD_CONV):
        lo = HALO - half + j
        acc = acc + win[lo:lo + CHUNK, :] * cw[j:j + 1, :]
    return _silu(acc)


def _ssd_body(x_ref, z_ref, b_ref, c_ref, sm_ref, xm_ref, bm_ref, cm_ref, smm_ref,
              cwx_ref, cbx_ref, cwb_ref, cbb_ref, cwc_ref, cbc_ref, dtb_ref, alog_ref,
              ef_ref, eb_ref, dexp_ref, nw_ref, o_ref, st_ref, xc_ref, bc_ref, cc_ref,
              *, n_chunks):
    gh, p, n = SSD_GROUP_HEADS, SSD_HEAD_DIM, SSD_STATE
    width = gh * p
    two = 2 * CHUNK
    lane_s = lax.broadcasted_iota(I32, (two, width), 1) % p
    row2 = lax.broadcasted_iota(I32, (two, width), 0)
    row_t = row2 % CHUNK
    irep2 = lane_s == row_t
    irep2_16 = irep2.astype(BF16)
    pair_mask = ((row2 < CHUNK) & (lane_s <= row_t)) | ((row2 >= CHUNK) & (lane_s > row_t))
    r2 = lax.broadcasted_iota(I32, (two, two), 0)
    c2 = lax.broadcasted_iota(I32, (two, two), 1)
    same_dir = (r2 < CHUNK) == (c2 < CHUNK)
    ones2 = _twice_cols(same_dir.astype(BF16))
    cum2 = ((r2 < CHUNK) & (c2 <= r2)) | ((r2 >= CHUNK) & (c2 >= r2))
    tri2 = _twice_cols(cum2.astype(BF16))
    bd_r = lax.broadcasted_iota(I32, (width, width), 0) // p
    bd_c = lax.broadcasted_iota(I32, (width, width), 1) // p
    bdmask = bd_r == bd_c
    zero16 = jnp.zeros((), BF16)

    cwx, cbx = cwx_ref[...], cbx_ref[...]
    cwb, cbb = cwb_ref[...], cbb_ref[...]
    cwc, cbc = cwc_ref[...], cbc_ref[...]
    dtb = dtb_ref[...]
    a_row = -jnp.exp(alog_ref[...])
    ef, eb = ef_ref[...], eb_ref[...]
    e_cat = jnp.concatenate([ef, eb], axis=1)
    dexp, nw = dexp_ref[...], nw_ref[...]
    seq = n_chunks * CHUNK

    def window(ref, mref, r):
        off = pl.multiple_of(r * CHUNK, CHUNK)
        poff = pl.multiple_of(jnp.maximum(off - HALO, 0), HALO)
        noff = pl.multiple_of(jnp.minimum(off + CHUNK, seq - HALO), HALO)
        prev = jnp.where(r == 0, mref[CHUNK - HALO:, :], ref[pl.ds(poff, HALO), :]).astype(F32)
        nxt = ref[pl.ds(noff, HALO), :].astype(F32)
        nxt = jnp.where(r == n_chunks - 1, 0.0, nxt)
        return jnp.concatenate([prev, ref[pl.ds(off, CHUNK), :].astype(F32), nxt], axis=0)

    def conv_body(r, carry):
        rows = pl.ds(pl.multiple_of(r * CHUNK, CHUNK), CHUNK)
        xc_ref[rows, :] = _conv_silu(window(x_ref, xm_ref, r), cwx, cbx).astype(BF16)
        bc_ref[rows, :] = _conv_silu(window(b_ref, bm_ref, r), cwb, cbb).astype(BF16)
        cc_ref[rows, :] = _conv_silu(window(c_ref, cm_ref, r), cwc, cbc).astype(BF16)
        return carry

    lax.fori_loop(0, n_chunks, conv_body, 0, unroll=2)

    st_ref[...] = jnp.zeros_like(st_ref)

    def meta_window(mref, ref):
        zeros = jnp.zeros((HALO, mref.shape[1]), F32)
        return jnp.concatenate([zeros, mref[...].astype(F32), ref[0:HALO, :].astype(F32)], axis=0)

    def meta_mask(width_):
        return lax.broadcasted_iota(I32, (CHUNK, width_), 0) >= NPAD

    xc = jnp.where(meta_mask(width), _conv_silu(meta_window(xm_ref, x_ref), cwx, cbx), 0.0)
    bc = jnp.where(meta_mask(n), _conv_silu(meta_window(bm_ref, b_ref), cwb, cbb), 0.0)
    dt = jnp.where(meta_mask(LANES), _softplus(smm_ref[...].astype(F32) + dtb), 0.0)
    cs = _sel_dot_l(_twice_cols(cum2[:CHUNK, :CHUNK].astype(BF16)), dt * a_row)
    both = _sel_dot_r(jnp.concatenate([cs, dt], axis=0), ef)
    cs_e, dt_e = both[:CHUNK], both[CHUNK:]
    xend = (xc * dt_e * jnp.exp(cs_e[CHUNK - 1:CHUNK] - cs_e)).astype(BF16)
    st_ref[0:n, :] = _dot_tn(bc.astype(BF16), xend)

    zeros_n = jnp.zeros((CHUNK, n), BF16)

    def own_dir(a):
        return jnp.concatenate([jnp.concatenate([a[:CHUNK], zeros_n], axis=1),
                                jnp.concatenate([zeros_n, a[CHUNK:]], axis=1)], axis=0)

    def steps(j, finalize):
        ids = [j * SSD_STEPS_PER_ITER + s for s in range(SSD_STEPS_PER_ITER)]
        rfs = [pl.ds(pl.multiple_of(i * CHUNK, CHUNK), CHUNK) for i in ids]
        rbs = [pl.ds(pl.multiple_of((n_chunks - 1 - i) * CHUNK, CHUNK), CHUNK) for i in ids]
        both_rows = lambda ref, rf, rb: jnp.concatenate([ref[rf, :], ref[rb, :]], axis=0)
        dts = [_softplus(both_rows(sm_ref, rf, rb).astype(F32) + dtb) for rf, rb in zip(rfs, rbs)]
        css = [_sel_dot_l(tri2, dt * a_row) for dt in dts]
        exs = [_sel_dot_r(jnp.concatenate([cs, dt], axis=0), e_cat) for cs, dt in zip(css, dts)]
        xcs, cs_es, tots, xdts, xends, cbs, bcs, ccs = [], [], [], [], [], [], [], []
        for ex, rf, rb in zip(exs, rfs, rbs):
            cs_e = jnp.concatenate([ex[:CHUNK, :width], ex[CHUNK:two, width:]], axis=0)
            dt_e = jnp.concatenate([ex[two:two + CHUNK, :width], ex[two + CHUNK:, width:]], axis=0)
            tot_f, tot_b = cs_e[CHUNK - 1:CHUNK], cs_e[CHUNK:CHUNK + 1]
            tot = jnp.concatenate([jnp.broadcast_to(tot_f, (CHUNK, width)),
                                   jnp.broadcast_to(tot_b, (CHUNK, width))], axis=0)
            xc = both_rows(xc_ref, rf, rb).astype(F32)
            bc16, cc16 = both_rows(bc_ref, rf, rb), both_rows(cc_ref, rf, rb)
            xdt = xc * dt_e
            xcs.append(xc)
            cs_es.append(cs_e)
            tots.append((tot_f, tot_b))
            xdts.append(xdt.astype(BF16))
            xends.append((xdt * jnp.exp(tot - cs_e)).astype(BF16))
            bcs.append(bc16)
            ccs.append(cc16)
            cbs.append(_dot_nt(cc16, bc16))
        cb_reps = [_dot(jnp.where(same_dir, cb, 0.0).astype(BF16), irep2_16) for cb in cbs]
        cs_rows = [_sel_dot_l(ones2, jnp.where(irep2, cs_e, 0.0)) for cs_e in cs_es]
        intras = []
        for cb_rep, cs_row, cs_e, xdt16 in zip(cb_reps, cs_rows, cs_es, xdts):
            decay = jnp.where(pair_mask, jnp.exp(jnp.minimum(cs_e - cs_row, 0.0)), 0.0)
            w = (cb_rep * decay).astype(BF16)
            intras.append(jnp.concatenate(
                [_dot(w[d * CHUNK:(d + 1) * CHUNK],
                      jnp.where(bdmask, jnp.concatenate([xdt16[d * CHUNK:(d + 1) * CHUNK]] * gh, axis=0), zero16))
                 for d in range(2)], axis=0))
        upds = [_dot_tn(own_dir(bc16), xend) for bc16, xend in zip(bcs, xends)]
        st = st_ref[...]
        ys = []
        for intra, cc16, cs_e, (tot_f, tot_b), upd in zip(intras, ccs, cs_es, tots, upds):
            ys.append(intra + _dot(own_dir(cc16), st.astype(BF16)) * jnp.exp(cs_e))
            grow = jnp.concatenate([jnp.broadcast_to(jnp.exp(tot_f), (n, width)),
                                    jnp.broadcast_to(jnp.exp(tot_b), (n, width))], axis=0)
            st = st * grow + upd
        st_ref[...] = st
        for y, xc, rf, rb in zip(ys, xcs, rfs, rbs):
            for d, rows in enumerate((rf, rb)):
                yd = y[d * CHUNK:(d + 1) * CHUNK]
                if finalize:
                    yd = yd + o_ref[rows, :].astype(F32) + xc[d * CHUNK:(d + 1) * CHUNK] * dexp
                    yd = yd * _silu(z_ref[rows, :].astype(F32))
                    o_ref[rows, :] = _rms(yd, nw).astype(BF16)
                else:
                    o_ref[rows, :] = yd.astype(BF16)

    def first_half(j, carry):
        steps(j, False)
        return carry

    def second_half(j, carry):
        steps(j, True)
        return carry

    half_iters = n_chunks // 2 // SSD_STEPS_PER_ITER
    lax.fori_loop(0, half_iters, first_half, 0)
    lax.fori_loop(half_iters, 2 * half_iters, second_half, 0)


def _ssd(proj, proj_meta, conv_w, conv_b, dtb, alog, e_f, e_b, dexp, norm_w):
    bsz, seq, _ = proj.shape
    gc = SSD_GROUP_CH
    xb, zb = COL_X // gc, COL_Z // gc
    bb, cb, sb = COL_B // SSD_STATE, COL_C // SSD_STATE, COL_SMALL // LANES
    real = lambda width, base: pl.BlockSpec((None, seq, width), lambda b, g: (b, 0, base + g))
    meta = lambda width, base: pl.BlockSpec((CHUNK, width), lambda b, g: (0, base + g))
    cpar = lambda rows_, width, base: pl.BlockSpec((rows_, width), lambda b, g: (0, base + g))
    cbb_, ccb_ = SSD_INNER // SSD_STATE, SSD_INNER // SSD_STATE + SSD_GROUPS
    const = lambda shape: pl.BlockSpec(shape, lambda b, g: (0, 0))
    return pl.pallas_call(
        functools.partial(_ssd_body, n_chunks=seq // CHUNK),
        grid=(bsz, SSD_GROUPS),
        in_specs=[
            real(gc, xb), real(gc, zb), real(SSD_STATE, bb), real(SSD_STATE, cb),
            pl.BlockSpec((None, seq, LANES), lambda b, g: (b, 0, sb)),
            meta(gc, xb), meta(SSD_STATE, bb), meta(SSD_STATE, cb),
            pl.BlockSpec((CHUNK, LANES), lambda b, g: (0, sb)),
            cpar(SSD_CONV, gc, 0), cpar(1, gc, 0),
            cpar(SSD_CONV, SSD_STATE, cbb_), cpar(1, SSD_STATE, cbb_),
            cpar(SSD_CONV, SSD_STATE, ccb_), cpar(1, SSD_STATE, ccb_),
            const((1, LANES)), const((1, LANES)),
            pl.BlockSpec((None, 2 * LANES, gc), lambda b, g: (g, 0, 0)),
            pl.BlockSpec((None, 2 * LANES, gc), lambda b, g: (g, 0, 0)),
            pl.BlockSpec((1, gc), lambda b, g: (0, g)),
            pl.BlockSpec((1, gc), lambda b, g: (0, g)),
        ],
        out_specs=pl.BlockSpec((None, seq, gc), lambda b, g: (b, 0, g)),
        out_shape=jax.ShapeDtypeStruct((bsz, seq, SSD_INNER), BF16),
        scratch_shapes=[pltpu.VMEM((2 * SSD_STATE, gc), F32),
                        pltpu.VMEM((seq, gc), BF16), pltpu.VMEM((seq, SSD_STATE), BF16),
                        pltpu.VMEM((seq, SSD_STATE), BF16)],
        compiler_params=pltpu.CompilerParams(
            dimension_semantics=("arbitrary", "arbitrary"), vmem_limit_bytes=VMEM_LIMIT),
        name="ssd",
    )(proj, proj, proj, proj, proj, proj_meta, proj_meta, proj_meta, proj_meta,
      conv_w, conv_b, conv_w, conv_b, conv_w, conv_b, dtb, alog, e_f, e_b, dexp, norm_w)


def _outproj_body(x_ref, og_ref, ys_ref, wo1_ref, wo2_ref, n2w_ref, wrh_ref, wrl_ref, br_ref,
                  h2_ref, n2p_ref, idx_ref, gate_ref, rank_ref, cnt_ref, carry_ref, *, tile):
    i = pl.program_id(0)

    @pl.when(i == 0)
    def _():
        carry_ref[...] = jnp.zeros_like(carry_ref)

    h2 = x_ref[...] + _dot(og_ref[...], wo1_ref[...]) + _dot(ys_ref[...], wo2_ref[...])
    h2_ref[...] = h2
    n2 = _rms(h2, n2w_ref[...])
    _store_rows(n2p_ref, _pack_bf16_pairs(n2))

    nh, nl = _split(n2)
    wrh, wrl = wrh_ref[...], wrl_ref[...]
    logits = _dot(nh, wrh) + _dot(nh, wrl) + _dot(nl, wrh) + br_ref[...]

    lane = lax.broadcasted_iota(I32, (tile, LANES), 1)
    lane_f = lane.astype(F32)
    lane4 = lax.broadcasted_iota(I32, (tile, TOP_K), 1)
    vals, onehots = [], []
    idx_out = jnp.zeros((tile, TOP_K), I32)
    work = logits
    for k in range(TOP_K):
        m = jnp.max(work, axis=-1, keepdims=True)
        first = jnp.min(jnp.where(work == m, lane_f, float(LANES)), axis=-1, keepdims=True)
        oh = lane_f == first
        work = jnp.where(oh, -jnp.inf, work)
        vals.append(m)
        onehots.append(oh)
        idx_out = jnp.where(lane4 == k, first.astype(I32), idx_out)
    idx_ref[...] = idx_out

    exps = [jnp.exp(v - vals[0]) for v in vals]
    inv = 1.0 / (exps[0] + exps[1] + exps[2] + exps[3])
    gate_out = jnp.zeros((tile, TOP_K), F32)
    for k in range(TOP_K):
        gate_out = jnp.where(lane4 == k, exps[k] * inv, gate_out)
    gate_ref[...] = gate_out

    any_oh = (onehots[0] | onehots[1] | onehots[2] | onehots[3])
    any16 = jnp.where(any_oh, 1.0, 0.0).astype(BF16)
    before = _dot(_tri(tile, lower=True, inclusive=False).astype(BF16), any16) + carry_ref[...]
    rank_out = jnp.zeros((tile, TOP_K), I32)
    for k in range(TOP_K):
        rk = jnp.sum(jnp.where(onehots[k], before, 0.0), axis=-1, keepdims=True)
        rank_out = jnp.where(lane4 == k, rk.astype(I32), rank_out)
    rank_ref[...] = rank_out
    carry = carry_ref[...] + jnp.sum(any16.astype(F32), axis=0, keepdims=True)
    carry_ref[...] = carry
    cnt_ref[...] = carry


def _outproj(x2d, o_gla, y_ssd, w_out1, w_out2, norm2_w, wr_hi, wr_lo, b_r, tile):
    rows = x2d.shape[0]
    row = lambda width: pl.BlockSpec((tile, width), lambda i: (i, 0))
    const = lambda shape: pl.BlockSpec(shape, lambda i: (0, 0))
    return pl.pallas_call(
        functools.partial(_outproj_body, tile=tile),
        grid=(rows // tile,),
        in_specs=[
            row(D_MODEL), row(GLA_DV), row(SSD_INNER),
            const((GLA_DV, D_MODEL)), const((SSD_INNER, D_MODEL)), const((1, D_MODEL)),
            const((D_MODEL, LANES)), const((D_MODEL, LANES)), const((1, LANES)),
        ],
        out_specs=[
            row(D_MODEL), pl.BlockSpec((tile, ROW_SUB, LANES), lambda i: (i, 0, 0)),
            row(TOP_K), row(TOP_K), row(TOP_K), const((1, LANES)),
        ],
        out_shape=[
            jax.ShapeDtypeStruct((rows, D_MODEL), F32),
            jax.ShapeDtypeStruct((rows, ROW_SUB, LANES), U32),
            jax.ShapeDtypeStruct((rows, TOP_K), I32),
            jax.ShapeDtypeStruct((rows, TOP_K), F32),
            jax.ShapeDtypeStruct((rows, TOP_K), I32),
            jax.ShapeDtypeStruct((1, LANES), F32),
        ],
        scratch_shapes=[pltpu.VMEM((1, LANES), F32)],
        compiler_params=pltpu.CompilerParams(
            dimension_semantics=("arbitrary",), vmem_limit_bytes=VMEM_LIMIT),
        name="outproj_router",
    )(x2d, o_gla, y_ssd, w_out1, w_out2, norm2_w, wr_hi, wr_lo, b_r)


def _row_copy(src, src_row, dst, dst_row, sem):
    return pltpu.make_async_copy(src.at[src_row], dst.at[dst_row], sem)


def _dispatch_body(dest_ref, n2p_ref, xs_hbm, sem, *, tile):
    def start(t, carry):
        for k in range(TOP_K):
            _row_copy(n2p_ref, t, xs_hbm, dest_ref[t * TOP_K + k], sem).start()
        return carry

    lax.fori_loop(0, tile, start, 0)

    for k in range(TOP_K):
        pltpu.make_async_copy(n2p_ref, xs_hbm.at[pl.ds(0, tile)], sem).wait()


def _dispatch(dest_flat, n2p, tile):
    rows = n2p.shape[0]
    return pl.pallas_call(
        functools.partial(_dispatch_body, tile=tile),
        grid=(rows // tile,),
        in_specs=[
            pl.BlockSpec((tile * TOP_K,), lambda i: (i,), memory_space=pltpu.SMEM),
            pl.BlockSpec((tile, ROW_SUB, LANES), lambda i: (i, 0, 0)),
        ],
        out_specs=pl.BlockSpec(memory_space=pl.ANY),
        out_shape=jax.ShapeDtypeStruct((rows * TOP_K, ROW_SUB, LANES), U32),
        scratch_shapes=[pltpu.SemaphoreType.DMA(())],
        compiler_params=pltpu.CompilerParams(dimension_semantics=("arbitrary",)),
        name="dispatch",
    )(dest_flat, n2p)


def _expert_body(blk_ref, exp_ref, lo_ref, hi_ref, first_ref, newexp_ref, nw_ref,
                 x_ref, wgu_ref, bgu_ref, wdn_ref, bdn_ref, o_ref, wgu16_ref, wdn16_ref, *, tile):
    w = pl.program_id(0)
    cast_rows = 64

    @pl.when((w < nw_ref[0]) & (newexp_ref[w] == 1))
    def _():
        def cast(i, carry):
            rows = pl.ds(pl.multiple_of(i * cast_rows, cast_rows), cast_rows)
            wgu16_ref[rows, :] = wgu_ref[rows, :].astype(BF16)
            wdn16_ref[rows, :] = wdn_ref[rows, :].astype(BF16)
            return carry

        lax.fori_loop(0, D_MODEL // cast_rows, cast, 0)

    @pl.when(w < nw_ref[0])
    def _():
        x = _unpack_bf16_pairs(_load_rows(x_ref)).astype(BF16)
        gt = jnp.minimum(_dot(x, wgu16_ref[:, :D_FF]) + bgu_ref[:, :D_FF], SWIGLU_LIMIT)
        up = jnp.clip(_dot(x, wgu16_ref[:, D_FF:]) + bgu_ref[:, D_FF:], -SWIGLU_LIMIT, SWIGLU_LIMIT)
        act = ((up + 1.0) * gt * jax.nn.sigmoid(gt * SWIGLU_ALPHA)).astype(BF16)
        packed = _pack_bf16_pairs(_dot(act, wdn16_ref[...]) + bdn_ref[...])

        @pl.when(first_ref[w] == 1)
        def _():
            _store_rows(o_ref, packed)

        @pl.when(first_ref[w] == 0)
        def _():
            r = lax.broadcasted_iota(I32, (tile, LANES), 0)
            _store_rows(o_ref, packed, keep=(r >= lo_ref[w]) & (r < hi_ref[w]))


def _experts(tables, xs, w_gu, b_gu, w_dn, b_dn, tile, n_work):
    rows = xs.shape[0]
    row_block = pl.BlockSpec((tile, ROW_SUB, LANES), lambda w, blk, ex, lo, hi, fi, ne, nw: (blk[w], 0, 0))
    grid_spec = pltpu.PrefetchScalarGridSpec(
        num_scalar_prefetch=7,
        grid=(n_work,),
        in_specs=[
            row_block,
            pl.BlockSpec((None, D_MODEL, 2 * D_FF), lambda w, blk, ex, lo, hi, fi, ne, nw: (ex[w], 0, 0)),
            pl.BlockSpec((None, 1, 2 * D_FF), lambda w, blk, ex, lo, hi, fi, ne, nw: (ex[w], 0, 0)),
            pl.BlockSpec((None, D_FF, D_MODEL), lambda w, blk, ex, lo, hi, fi, ne, nw: (ex[w], 0, 0)),
            pl.BlockSpec((None, 1, D_MODEL), lambda w, blk, ex, lo, hi, fi, ne, nw: (ex[w], 0, 0)),
        ],
        out_specs=row_block,
        scratch_shapes=[pltpu.VMEM((D_MODEL, 2 * D_FF), BF16), pltpu.VMEM((D_FF, D_MODEL), BF16)],
    )
    return pl.pallas_call(
        functools.partial(_expert_body, tile=tile),
        grid_spec=grid_spec,
        out_shape=jax.ShapeDtypeStruct((rows, ROW_SUB, LANES), U32),
        compiler_params=pltpu.CompilerParams(
            dimension_semantics=("arbitrary",), vmem_limit_bytes=VMEM_LIMIT),
        name="experts",
    )(*tables, xs, w_gu, b_gu, w_dn, b_dn)


def _combine_body(dest_ref, gate_ref, h2_ref, nfw_ref, ys_hbm, o_ref, buf, sem, *, tile):
    def start(t, carry):
        for k in range(TOP_K):
            pltpu.make_async_copy(ys_hbm.at[dest_ref[t * TOP_K + k]], buf.at[k, t], sem).start()
        return carry

    lax.fori_loop(0, tile, start, 0)

    for k in range(TOP_K):
        pltpu.make_async_copy(ys_hbm.at[pl.ds(0, tile)], buf.at[k], sem).wait()

    gate = gate_ref[...]
    h3 = h2_ref[...]
    for k in range(TOP_K):
        h3 = h3 + gate[:, k:k + 1] * _unpack_bf16_pairs(_load_rows(buf.at[k]))
    o_ref[...] = _rms(h3, nfw_ref[...])


def _combine(dest_flat, gates, h2, norm_f_w, ys, tile):
    rows = h2.shape[0]
    return pl.pallas_call(
        functools.partial(_combine_body, tile=tile),
        grid=(rows // tile,),
        in_specs=[
            pl.BlockSpec((tile * TOP_K,), lambda i: (i,), memory_space=pltpu.SMEM),
            pl.BlockSpec((tile, TOP_K), lambda i: (i, 0)),
            pl.BlockSpec((tile, D_MODEL), lambda i: (i, 0)),
            pl.BlockSpec((1, D_MODEL), lambda i: (0, 0)),
            pl.BlockSpec(memory_space=pl.ANY),
        ],
        out_specs=pl.BlockSpec((tile, D_MODEL), lambda i: (i, 0)),
        out_shape=jax.ShapeDtypeStruct((rows, D_MODEL), F32),
        scratch_shapes=[pltpu.VMEM((TOP_K, tile, ROW_SUB, LANES), U32), pltpu.SemaphoreType.DMA(())],
        compiler_params=pltpu.CompilerParams(
            dimension_semantics=("arbitrary",), vmem_limit_bytes=VMEM_LIMIT),
        name="combine_final",
    )(dest_flat, gates, h2, norm_f_w, ys)


def _work_tables(counts, n_rows, tile):
    n_blocks = n_rows // tile
    n_work = n_blocks + N_EXPERTS
    ends = jnp.cumsum(counts)
    starts = ends - counts
    first_blk = starts // tile
    last_blk = (ends - 1) // tile
    nb = jnp.where(counts > 0, last_blk - first_blk + 1, 0)
    wend = jnp.cumsum(nb)
    wstart = wend - nb
    total = wend[-1]
    w = jnp.minimum(jnp.arange(n_work, dtype=I32), total - 1)
    ex = jnp.minimum(jnp.sum(wend[None, :] <= w[:, None], axis=1), N_EXPERTS - 1).astype(I32)
    blk = (first_blk[ex] + (w - wstart[ex])).astype(I32)
    lo = (jnp.maximum(starts[ex], blk * tile) - blk * tile).astype(I32)
    hi = (jnp.minimum(ends[ex], (blk + 1) * tile) - blk * tile).astype(I32)
    prev_blk = jnp.concatenate([jnp.full((1,), -1, I32), blk[:-1]])
    first = (blk != prev_blk).astype(I32)
    prev_ex = jnp.concatenate([jnp.full((1,), -1, I32), ex[:-1]])
    new_ex = (ex != prev_ex).astype(I32)
    return (blk, ex, lo, hi, first, new_ex, total.reshape(1).astype(I32)), starts, n_work


def kernel(x, meta, norm1_w, w_in, gla_wa2_f, gla_ba2_f, gla_wa2_b, gla_ba2_b, gla_norm_w, conv_w, conv_b, dt_bias_f, dt_bias_b, a_log_f, a_log_b, ssd_d, ssd_norm_w, w_out, norm2_w, w_router, b_router, w_gu, b_gu, w_dn, b_dn, norm_f_w):
    bsz, seq, d = x.shape
    n_tok = bsz * seq
    l = 0

    wi = w_in[l]
    a_cols = wi[:, 3072:3104]
    dt_cols = wi[:, 5664:5696]
    w_perm = jnp.concatenate(
        [wi[:, :3072], wi[:, 3104:5664], a_cols, dt_cols,
         jnp.zeros((d, N_PROJ - COL_SMALL - 64), F32)], axis=1).astype(BF16)

    def lane_rows(w, lane0):
        return jnp.zeros((LANES, w.shape[1]), F32).at[lane0:lane0 + w.shape[0]].set(w)

    wa_f = lane_rows(gla_wa2_f[l], LANE_AF).astype(BF16)
    wa_b = lane_rows(gla_wa2_b[l], LANE_AB).astype(BF16)
    ba_f = gla_ba2_f[l][None, :]
    ba_b = gla_ba2_b[l][None, :]

    def lane_vec(vf, vb):
        z = jnp.zeros((1, LANES), F32)
        return z.at[0, LANE_DTF:LANE_DTF + SSD_HEADS].set(vf).at[0, LANE_DTB:LANE_DTB + SSD_HEADS].set(vb)

    dtb = lane_vec(dt_bias_f[l], dt_bias_b[l])
    alog = lane_vec(a_log_f[l], a_log_b[l])
    lane_id = (jnp.arange(2 * LANES) % LANES)[None, :, None]
    head_id = (jnp.arange(SSD_GROUP_CH) // SSD_HEAD_DIM)[None, None, :]
    grp = jnp.arange(SSD_GROUPS)[:, None, None] * SSD_GROUP_HEADS
    e_f = (lane_id == LANE_DTF + grp + head_id).astype(BF16)
    e_b = (lane_id == LANE_DTB + grp + head_id).astype(BF16)
    dexp = jnp.repeat(ssd_d[l], SSD_HEAD_DIM)[None, :]

    wr = jnp.zeros((d, LANES), F32).at[:, :N_EXPERTS].set(w_router[l])
    wr_hi = wr.astype(BF16)
    wr_lo = (wr - wr_hi.astype(F32)).astype(BF16)
    b_r = jnp.full((1, LANES), -1e30, F32).at[0, :N_EXPERTS].set(b_router[l])

    x2d = x.reshape(n_tok, d)
    x_meta = jnp.pad(meta.astype(F32), ((NPAD, 0), (0, 0)))
    n1 = norm1_w[l][None, :]
    proj = _inproj(x2d, n1, w_perm, ROW_TILE).reshape(bsz, seq, N_PROJ)
    proj_meta = _inproj(x_meta, n1, w_perm, CHUNK)
    o_gla = _gla(proj, proj_meta, wa_f, ba_f, wa_b, ba_b, gla_norm_w[l][None, :])
    y_ssd = _ssd(proj, proj_meta, conv_w[l], conv_b[l][None, :], dtb, alog, e_f, e_b, dexp,
                 ssd_norm_w[l][None, :])

    wo = w_out[l].astype(BF16)
    h2, n2p, idx, gates, rank, cnt = _outproj(
        x2d, o_gla.reshape(n_tok, GLA_DV), y_ssd.reshape(n_tok, SSD_INNER),
        wo[:GLA_DV], wo[GLA_DV:], norm2_w[l][None, :], wr_hi, wr_lo, b_r, ROW_TILE)

    counts = cnt[0, :N_EXPERTS].astype(I32)
    tables, starts, n_work = _work_tables(counts, n_tok * TOP_K, EXPERT_TILE)
    dest = (rank + jnp.take(starts, idx)).astype(I32).reshape(-1)

    xs = _dispatch(dest, n2p, ROW_TILE)
    ys = _experts(tables, xs, w_gu[l], b_gu[l][:, None, :], w_dn[l], b_dn[l][:, None, :],
                  EXPERT_TILE, n_work)
    out = _combine(dest, gates, h2, norm_f_w[None, :], ys, COMBINE_TILE)
    return out.reshape(bsz, seq, d)
```

```python
import functools

import jax
import jax.numpy as jnp
from jax import lax
from jax.experimental import pallas as pl
from jax.experimental.pallas import tpu as pltpu

F32 = jnp.float32
BF16 = jnp.bfloat16
I32 = jnp.int32
U32 = jnp.uint32

D_MODEL = 1024
N_META = 16
CHUNK = 64
NPAD = CHUNK - N_META
GLA_HEADS = 4
GLA_DK = 512
GLA_DV = 1024
GLA_HEAD_K = GLA_DK // GLA_HEADS
GLA_HEAD_V = GLA_DV // GLA_HEADS
GLA_RANK = 16
GLA_GATE_NORM = 16.0
SSD_INNER = 1024
SSD_HEAD_DIM = 64
SSD_HEADS = SSD_INNER // SSD_HEAD_DIM
SSD_GROUPS = 2
SSD_GROUP_HEADS = SSD_HEADS // SSD_GROUPS
SSD_GROUP_CH = SSD_INNER // SSD_GROUPS
SSD_STATE = 128
SSD_CONV = 5
N_EXPERTS = 32
TOP_K = 4
D_FF = 1024
SWIGLU_LIMIT = 7.0
SWIGLU_ALPHA = 1.702
EPS = 1e-6

LANES = 128
ROW_WORDS = D_MODEL // 2
ROW_SUB = ROW_WORDS // LANES
HALO = 16

COL_Q = 0
COL_K = 512
COL_V = 1024
COL_G = 2048
COL_Z = 3072
COL_X = 4096
COL_B = 5120
COL_C = 5376
COL_SMALL = 5632
N_PROJ = 5760
LANE_AF = 0
LANE_AB = 16
LANE_DTF = 32
LANE_DTB = 48

ROW_TILE = 512
EXPERT_TILE = 512
EXPERT_SUBTILES = 2
COMBINE_TILE = 256
VMEM_LIMIT = 56 * 1024 * 1024


def _dot(a, b):
    return jnp.dot(a, b, preferred_element_type=F32)


def _dot_nt(a, b):
    return lax.dot_general(a, b, (((1,), (1,)), ((), ())), preferred_element_type=F32)


def _dot_tn(a, b):
    return lax.dot_general(a, b, (((0,), (0,)), ((), ())), preferred_element_type=F32)


def _split(x):
    hi = x.astype(BF16)
    lo = (x - hi.astype(F32)).astype(BF16)
    return hi, lo


def _sel_dot_l(m01, x):
    hi, lo = _split(x)
    return _dot(m01, jnp.concatenate([hi, lo], axis=0))


def _sel_dot_r(x, m01):
    hi, lo = _split(x)
    return _dot(jnp.concatenate([hi, lo], axis=1), m01)


def _twice_cols(m):
    return jnp.concatenate([m, m], axis=1)


def _softplus(x):
    return jnp.maximum(x, 0.0) + jnp.log(1.0 + jnp.exp(-jnp.abs(x)))


def _log_sigmoid(x):
    return jnp.minimum(x, 0.0) - jnp.log(1.0 + jnp.exp(-jnp.abs(x)))


def _silu(x):
    return x * jax.nn.sigmoid(x)


def _rms(x, w):
    return x * lax.rsqrt(jnp.mean(x * x, axis=-1, keepdims=True) + EPS) * w


def _pack_bf16_pairs(x):
    w = x.shape[1] // 2
    return pltpu.pack_elementwise([x[:, :w], x[:, w:]], packed_dtype=BF16)


def _unpack_bf16_pairs(p):
    lo, hi = (pltpu.unpack_elementwise(p, index=i, packed_dtype=BF16, unpacked_dtype=F32) for i in range(2))
    return jnp.concatenate([lo, hi], axis=1)


def _load_rows(ref3):
    return jnp.concatenate([ref3[:, j, :] for j in range(ROW_SUB)], axis=1)


def _store_rows(ref3, val, keep=None):
    for j in range(ROW_SUB):
        piece = val[:, j * LANES:(j + 1) * LANES]
        if keep is not None:
            piece = jnp.where(keep, piece, ref3[:, j, :])
        ref3[:, j, :] = piece


def _tri(n, *, lower, inclusive):
    r = lax.broadcasted_iota(I32, (n, n), 0)
    c = lax.broadcasted_iota(I32, (n, n), 1)
    if lower:
        return (c <= r) if inclusive else (c < r)
    return (c >= r) if inclusive else (c > r)


def _inproj_body(x_ref, nw_ref, w_ref, o_ref, *, col_chunks):
    xn = _rms(x_ref[...], nw_ref[...]).astype(BF16)
    for lo, hi in col_chunks:
        o_ref[:, lo:hi] = _dot(xn, w_ref[:, lo:hi]).astype(BF16)


def _inproj(x2d, norm_w, w_perm, tile):
    rows = x2d.shape[0]
    col_chunks = tuple((c, min(c + 1024, N_PROJ)) for c in range(0, N_PROJ, 1024))
    return pl.pallas_call(
        functools.partial(_inproj_body, col_chunks=col_chunks),
        grid=(rows // tile,),
        in_specs=[
            pl.BlockSpec((tile, D_MODEL), lambda i: (i, 0)),
            pl.BlockSpec((1, D_MODEL), lambda i: (0, 0)),
            pl.BlockSpec((D_MODEL, N_PROJ), lambda i: (0, 0)),
        ],
        out_specs=pl.BlockSpec((tile, N_PROJ), lambda i: (i, 0)),
        out_shape=jax.ShapeDtypeStruct((rows, N_PROJ), BF16),
        compiler_params=pltpu.CompilerParams(
            dimension_semantics=("arbitrary",), vmem_limit_bytes=VMEM_LIMIT),
        name="inproj",
    )(x2d, norm_w, w_perm)


GLA_HEADS_PER_STEP = 2
GLA_STEPS_PER_ITER = 4
SSD_STEPS_PER_ITER = 4


def _gla_body(q_ref, k_ref, v_ref, g_ref, sm_ref, qm_ref, km_ref, vm_ref, smm_ref,
              waf_ref, baf_ref, wab_ref, bab_ref, nw_ref, o_ref, st_ref, *, n_chunks):
    heads = GLA_HEADS_PER_STEP
    nprob = 2 * heads
    rows_all = nprob * CHUNK
    wk = heads * GLA_HEAD_K
    dk, dv = GLA_HEAD_K, GLA_HEAD_V

    tri_f = _tri(CHUNK, lower=True, inclusive=True)
    r2 = lax.broadcasted_iota(I32, (2 * CHUNK, 2 * CHUNK), 0)
    c2 = lax.broadcasted_iota(I32, (2 * CHUNK, 2 * CHUNK), 1)
    cum2 = ((r2 < CHUNK) & (c2 <= r2)) | ((r2 >= CHUNK) & (c2 >= r2))
    tri2 = _twice_cols(cum2.astype(BF16))
    ra = lax.broadcasted_iota(I32, (rows_all, rows_all), 0)
    ca = lax.broadcasted_iota(I32, (rows_all, rows_all), 1)
    same = (ra // CHUNK) == (ca // CHUNK)
    att_mask = same & (((ra < heads * CHUNK) & (ca <= ra)) | ((ra >= heads * CHUNK) & (ca > ra)))
    rb_ = lax.broadcasted_iota(I32, (rows_all, nprob * dk), 0) // CHUNK
    cb_ = lax.broadcasted_iota(I32, (rows_all, nprob * dk), 1) // dk
    own = rb_ == cb_

    waf, baf = waf_ref[...], baf_ref[...]
    wab, bab = wab_ref[...], bab_ref[...]
    wa_cat = jnp.concatenate([waf, wab], axis=1)
    nw = nw_ref[...]
    zero16 = jnp.zeros((), BF16)

    st_ref[...] = jnp.zeros_like(st_ref)

    meta_rows = lax.broadcasted_iota(I32, (CHUNK, wk), 0) >= NPAD
    lg = _log_sigmoid(_dot(smm_ref[...], waf) + baf) * (1.0 / GLA_GATE_NORM)
    bm = _sel_dot_l(_twice_cols(tri_f.astype(BF16)), jnp.where(meta_rows, lg, 0.0))
    for h in range(heads):
        lanes = slice(h * dk, (h + 1) * dk)
        bh, toth = bm[:, lanes], bm[CHUNK - 1:CHUNK, lanes]
        kend = (km_ref[:, lanes].astype(F32) * jnp.exp(toth - bh)).astype(BF16)
        st_ref[:, lanes] = _dot_tn(vm_ref[:, h * dv:(h + 1) * dv], kend)

    def stack(f, b, w):
        return jnp.concatenate([f[:, h * w:(h + 1) * w] for h in range(heads)]
                               + [b[:, h * w:(h + 1) * w] for h in range(heads)], axis=0)

    def steps(j, finalize):
        ids = [j * GLA_STEPS_PER_ITER + s for s in range(GLA_STEPS_PER_ITER)]
        rfs = [pl.ds(pl.multiple_of(i * CHUNK, CHUNK), CHUNK) for i in ids]
        rbs = [pl.ds(pl.multiple_of((n_chunks - 1 - i) * CHUNK, CHUNK), CHUNK) for i in ids]
        zs = [_dot(jnp.concatenate([sm_ref[rf, :], sm_ref[rb, :]], axis=0), wa_cat) for rf, rb in zip(rfs, rbs)]
        b2s = []
        for z in zs:
            lg = jnp.concatenate([z[:CHUNK, :wk] + baf, z[CHUNK:, wk:] + bab], axis=0)
            b2s.append(_sel_dot_l(tri2, _log_sigmoid(lg) * (1.0 / GLA_GATE_NORM)))
        vss, qds, kends, tots, gs = [], [], [], [], []
        for b2, rf, rb in zip(b2s, rfs, rbs):
            tot_f, tot_b = b2[CHUNK - 1:CHUNK], b2[CHUNK:CHUNK + 1]
            bst = stack(b2[:CHUNK], b2[CHUNK:], dk)
            tst = stack(jnp.broadcast_to(tot_f, (CHUNK, wk)), jnp.broadcast_to(tot_b, (CHUNK, wk)), dk)
            qs = stack(q_ref[rf, :], q_ref[rb, :], dk).astype(F32)
            ks = stack(k_ref[rf, :], k_ref[rb, :], dk).astype(F32)
            qd = (qs * (GLA_HEAD_K ** -0.5) * jnp.exp(bst)).astype(BF16)
            kd = (ks * jnp.exp(-bst)).astype(BF16)
            vss.append(stack(v_ref[rf, :], v_ref[rb, :], dv))
            qds.append(qd)
            kends.append((ks * jnp.exp(tst - bst)).astype(BF16))
            tots.append(jnp.concatenate([tot_f, tot_b], axis=1))
            gs.append(_dot_nt(qd, kd))
        intras = [_dot(jnp.where(att_mask, g, 0.0).astype(BF16), vs) for g, vs in zip(gs, vss)]
        upds = [_dot_tn(vs, jnp.where(own, jnp.concatenate([kend] * nprob, axis=1), zero16))
                for vs, kend in zip(vss, kends)]
        st = st_ref[...]
        outs = []
        for qd, tot, upd, intra in zip(qds, tots, upds, intras):
            qd_own = jnp.where(own, jnp.concatenate([qd] * nprob, axis=1), zero16)
            outs.append(intra + _dot_nt(qd_own, st.astype(BF16)))
            st = st * jnp.exp(tot) + upd
        st_ref[...] = st
        for out, rf, rb in zip(outs, rfs, rbs):
            for p in range(nprob):
                rows, h = (rf if p < heads else rb), p % heads
                o = out[p * CHUNK:(p + 1) * CHUNK]
                cols = slice(h * dv, (h + 1) * dv)
                if finalize:
                    o = o + o_ref[rows, cols].astype(F32)
                    gate = _silu(g_ref[rows, cols].astype(F32))
                    o_ref[rows, cols] = (_rms(o, nw) * gate).astype(BF16)
                else:
                    o_ref[rows, cols] = o.astype(BF16)

    def first_half(j, carry):
        steps(j, False)
        return carry

    def second_half(j, carry):
        steps(j, True)
        return carry

    half_iters = n_chunks // 2 // GLA_STEPS_PER_ITER
    lax.fori_loop(0, half_iters, first_half, 0)
    lax.fori_loop(half_iters, 2 * half_iters, second_half, 0)


def _gla(proj, proj_meta, wa_f, ba_f, wa_b, ba_b, norm_w):
    bsz, seq, _ = proj.shape
    assert (seq // CHUNK) % (2 * GLA_STEPS_PER_ITER) == 0
    hp = GLA_HEADS_PER_STEP
    wk, wv = hp * GLA_HEAD_K, hp * GLA_HEAD_V
    kb, vb, gb, sb = COL_K // wk, COL_V // wv, COL_G // wv, COL_SMALL // LANES
    real = lambda width, base: pl.BlockSpec((None, seq, width), lambda b, h: (b, 0, base + h))
    meta = lambda width, base: pl.BlockSpec((CHUNK, width), lambda b, h: (0, base + h))
    per_step = lambda rows_: pl.BlockSpec((rows_, wk), lambda b, h: (0, h))
    return pl.pallas_call(
        functools.partial(_gla_body, n_chunks=seq // CHUNK),
        grid=(bsz, GLA_HEADS // hp),
        in_specs=[
            real(wk, 0), real(wk, kb), real(wv, vb), real(wv, gb),
            pl.BlockSpec((None, seq, LANES), lambda b, h: (b, 0, sb)),
            meta(wk, 0), meta(wk, kb), meta(wv, vb),
            pl.BlockSpec((CHUNK, LANES), lambda b, h: (0, sb)),
            per_step(LANES), per_step(1), per_step(LANES), per_step(1),
            pl.BlockSpec((1, GLA_HEAD_V), lambda b, h: (0, 0)),
        ],
        out_specs=pl.BlockSpec((None, seq, wv), lambda b, h: (b, 0, h)),
        out_shape=jax.ShapeDtypeStruct((bsz, seq, GLA_DV), BF16),
        scratch_shapes=[pltpu.VMEM((GLA_HEAD_V, 2 * hp * GLA_HEAD_K), F32)],
        compiler_params=pltpu.CompilerParams(
            dimension_semantics=("arbitrary", "arbitrary"), vmem_limit_bytes=VMEM_LIMIT),
        name="gla",
    )(proj, proj, proj, proj, proj, proj_meta, proj_meta, proj_meta, proj_meta,
      wa_f, ba_f, wa_b, ba_b, norm_w)


def _conv_silu(win, cw, cb):
    half = (SSD_CONV - 1) // 2
    acc = cb
    for j in range(SSD_CONV):
        lo = HALO - half + j
        acc = acc + win[lo:lo + CHUNK, :] * cw[j:j + 1, :]
    return _silu(acc)


def _ssd_body(x_ref, z_ref, b_ref, c_ref, sm_ref, xm_ref, bm_ref, cm_ref, smm_ref,
              cwx_ref, cbx_ref, cwb_ref, cbb_ref, cwc_ref, cbc_ref, dtb_ref, alog_ref,
              ef_ref, eb_ref, dexp_ref, nw_ref, o_ref, st_ref, xc_ref, bc_ref, cc_ref,
              *, n_chunks):
    gh, p, n = SSD_GROUP_HEADS, SSD_HEAD_DIM, SSD_STATE
    width = gh * p
    two = 2 * CHUNK
    lane_s = lax.broadcasted_iota(I32, (two, width), 1) % p
    row2 = lax.broadcasted_iota(I32, (two, width), 0)
    row_t = row2 % CHUNK
    irep2 = lane_s == row_t
    irep2_16 = irep2.astype(BF16)
    pair_mask = ((row2 < CHUNK) & (lane_s <= row_t)) | ((row2 >= CHUNK) & (lane_s > row_t))
    r2 = lax.broadcasted_iota(I32, (two, two), 0)
    c2 = lax.broadcasted_iota(I32, (two, two), 1)
    same_dir = (r2 < CHUNK) == (c2 < CHUNK)
    ones2 = _twice_cols(same_dir.astype(BF16))
    cum2 = ((r2 < CHUNK) & (c2 <= r2)) | ((r2 >= CHUNK) & (c2 >= r2))
    tri2 = _twice_cols(cum2.astype(BF16))
    bd_r = lax.broadcasted_iota(I32, (width, width), 0) // p
    bd_c = lax.broadcasted_iota(I32, (width, width), 1) // p
    bdmask = bd_r == bd_c
    zero16 = jnp.zeros((), BF16)

    cwx, cbx = cwx_ref[...], cbx_ref[...]
    cwb, cbb = cwb_ref[...], cbb_ref[...]
    cwc, cbc = cwc_ref[...], cbc_ref[...]
    dtb = dtb_ref[...]
    a_row = -jnp.exp(alog_ref[...])
    ef, eb = ef_ref[...], eb_ref[...]
    e_cat = jnp.concatenate([ef, eb], axis=1)
    dexp, nw = dexp_ref[...], nw_ref[...]
    seq = n_chunks * CHUNK

    def window(ref, mref, r):
        off = pl.multiple_of(r * CHUNK, CHUNK)
        poff = pl.multiple_of(jnp.maximum(off - HALO, 0), HALO)
        noff = pl.multiple_of(jnp.minimum(off + CHUNK, seq - HALO), HALO)
        prev = jnp.where(r == 0, mref[CHUNK - HALO:, :], ref[pl.ds(poff, HALO), :]).astype(F32)
        nxt = ref[pl.ds(noff, HALO), :].astype(F32)
        nxt = jnp.where(r == n_chunks - 1, 0.0, nxt)
        return jnp.concatenate([prev, ref[pl.ds(off, CHUNK), :].astype(F32), nxt], axis=0)

    def conv_body(r, carry):
        rows = pl.ds(pl.multiple_of(r * CHUNK, CHUNK), CHUNK)
        xc_ref[rows, :] = _conv_silu(window(x_ref, xm_ref, r), cwx, cbx).astype(BF16)
        bc_ref[rows, :] = _conv_silu(window(b_ref, bm_ref, r), cwb, cbb).astype(BF16)
        cc_ref[rows, :] = _conv_silu(window(c_ref, cm_ref, r), cwc, cbc).astype(BF16)
        return carry

    lax.fori_loop(0, n_chunks, conv_body, 0, unroll=2)

    st_ref[...] = jnp.zeros_like(st_ref)

    def meta_window(mref, ref):
        zeros = jnp.zeros((HALO, mref.shape[1]), F32)
        return jnp.concatenate([zeros, mref[...].astype(F32), ref[0:HALO, :].astype(F32)], axis=0)

    def meta_mask(width_):
        return lax.broadcasted_iota(I32, (CHUNK, width_), 0) >= NPAD

    xc = jnp.where(meta_mask(width), _conv_silu(meta_window(xm_ref, x_ref), cwx, cbx), 0.0)
    bc = jnp.where(meta_mask(n), _conv_silu(meta_window(bm_ref, b_ref), cwb, cbb), 0.0)
    dt = jnp.where(meta_mask(LANES), _softplus(smm_ref[...].astype(F32) + dtb), 0.0)
    cs = _sel_dot_l(_twice_cols(cum2[:CHUNK, :CHUNK].astype(BF16)), dt * a_row)
    both = _sel_dot_r(jnp.concatenate([cs, dt], axis=0), ef)
    cs_e, dt_e = both[:CHUNK], both[CHUNK:]
    xend = (xc * dt_e * jnp.exp(cs_e[CHUNK - 1:CHUNK] - cs_e)).astype(BF16)
    st_ref[0:n, :] = _dot_tn(bc.astype(BF16), xend)

    zeros_n = jnp.zeros((CHUNK, n), BF16)

    def own_dir(a):
        return jnp.concatenate([jnp.concatenate([a[:CHUNK], zeros_n], axis=1),
                                jnp.concatenate([zeros_n, a[CHUNK:]], axis=1)], axis=0)

    def steps(j, finalize):
        ids = [j * SSD_STEPS_PER_ITER + s for s in range(SSD_STEPS_PER_ITER)]
        rfs = [pl.ds(pl.multiple_of(i * CHUNK, CHUNK), CHUNK) for i in ids]
        rbs = [pl.ds(pl.multiple_of((n_chunks - 1 - i) * CHUNK, CHUNK), CHUNK) for i in ids]
        both_rows = lambda ref, rf, rb: jnp.concatenate([ref[rf, :], ref[rb, :]], axis=0)
        dts = [_softplus(both_rows(sm_ref, rf, rb).astype(F32) + dtb) for rf, rb in zip(rfs, rbs)]
        css = [_sel_dot_l(tri2, dt * a_row) for dt in dts]
        exs = [_sel_dot_r(jnp.concatenate([cs, dt], axis=0), e_cat) for cs, dt in zip(css, dts)]
        xcs, cs_es, tots, xdts, xends, cbs, bcs, ccs = [], [], [], [], [], [], [], []
        for ex, rf, rb in zip(exs, rfs, rbs):
            cs_e = jnp.concatenate([ex[:CHUNK, :width], ex[CHUNK:two, width:]], axis=0)
            dt_e = jnp.concatenate([ex[two:two + CHUNK, :width], ex[two + CHUNK:, width:]], axis=0)
            tot_f, tot_b = cs_e[CHUNK - 1:CHUNK], cs_e[CHUNK:CHUNK + 1]
            tot = jnp.concatenate([jnp.broadcast_to(tot_f, (CHUNK, width)),
                                   jnp.broadcast_to(tot_b, (CHUNK, width))], axis=0)
            xc = both_rows(xc_ref, rf, rb).astype(F32)
            bc16, cc16 = both_rows(bc_ref, rf, rb), both_rows(cc_ref, rf, rb)
            xdt = xc * dt_e
            xcs.append(xc)
            cs_es.append(cs_e)
            tots.append((tot_f, tot_b))
            xdts.append(xdt.astype(BF16))
            xends.append((xdt * jnp.exp(tot - cs_e)).astype(BF16))
            bcs.append(bc16)
            ccs.append(cc16)
            cbs.append(_dot_nt(cc16, bc16))
        cb_reps = [_dot(jnp.where(same_dir, cb, 0.0).astype(BF16), irep2_16) for cb in cbs]
        cs_rows = [_sel_dot_l(ones2, jnp.where(irep2, cs_e, 0.0)) for cs_e in cs_es]
        intras = []
        for cb_rep, cs_row, cs_e, xdt16 in zip(cb_reps, cs_rows, cs_es, xdts):
            decay = jnp.where(pair_mask, jnp.exp(jnp.minimum(cs_e - cs_row, 0.0)), 0.0)
            w = (cb_rep * decay).astype(BF16)
            intras.append(jnp.concatenate(
                [_dot(w[d * CHUNK:(d + 1) * CHUNK],
                      jnp.where(bdmask, jnp.concatenate([xdt16[d * CHUNK:(d + 1) * CHUNK]] * gh, axis=0), zero16))
                 for d in range(2)], axis=0))
        upds = [_dot_tn(own_dir(bc16), xend) for bc16, xend in zip(bcs, xends)]
        st = st_ref[...]
        ys = []
        for intra, cc16, cs_e, (tot_f, tot_b), upd in zip(intras, ccs, cs_es, tots, upds):
            ys.append(intra + _dot(own_dir(cc16), st.astype(BF16)) * jnp.exp(cs_e))
            grow = jnp.concatenate([jnp.broadcast_to(jnp.exp(tot_f), (n, width)),
                                    jnp.broadcast_to(jnp.exp(tot_b), (n, width))], axis=0)
            st = st * grow + upd
        st_ref[...] = st
        for y, xc, rf, rb in zip(ys, xcs, rfs, rbs):
            for d, rows in enumerate((rf, rb)):
                yd = y[d * CHUNK:(d + 1) * CHUNK]
                if finalize:
                    yd = yd + o_ref[rows, :].astype(F32) + xc[d * CHUNK:(d + 1) * CHUNK] * dexp
                    yd = yd * _silu(z_ref[rows, :].astype(F32))
                    o_ref[rows, :] = _rms(yd, nw).astype(BF16)
                else:
                    o_ref[rows, :] = yd.astype(BF16)

    def first_half(j, carry):
        steps(j, False)
        return carry

    def second_half(j, carry):
        steps(j, True)
        return carry

    half_iters = n_chunks // 2 // SSD_STEPS_PER_ITER
    lax.fori_loop(0, half_iters, first_half, 0)
    lax.fori_loop(half_iters, 2 * half_iters, second_half, 0)


def _ssd(proj, proj_meta, conv_w, conv_b, dtb, alog, e_f, e_b, dexp, norm_w):
    bsz, seq, _ = proj.shape
    gc = SSD_GROUP_CH
    xb, zb = COL_X // gc, COL_Z // gc
    bb, cb, sb = COL_B // SSD_STATE, COL_C // SSD_STATE, COL_SMALL // LANES
    real = lambda width, base: pl.BlockSpec((None, seq, width), lambda b, g: (b, 0, base + g))
    meta = lambda width, base: pl.BlockSpec((CHUNK, width), lambda b, g: (0, base + g))
    cpar = lambda rows_, width, base: pl.BlockSpec((rows_, width), lambda b, g: (0, base + g))
    cbb_, ccb_ = SSD_INNER // SSD_STATE, SSD_INNER // SSD_STATE + SSD_GROUPS
    const = lambda shape: pl.BlockSpec(shape, lambda b, g: (0, 0))
    return pl.pallas_call(
        functools.partial(_ssd_body, n_chunks=seq // CHUNK),
        grid=(bsz, SSD_GROUPS),
        in_specs=[
            real(gc, xb), real(gc, zb), real(SSD_STATE, bb), real(SSD_STATE, cb),
            pl.BlockSpec((None, seq, LANES), lambda b, g: (b, 0, sb)),
            meta(gc, xb), meta(SSD_STATE, bb), meta(SSD_STATE, cb),
            pl.BlockSpec((CHUNK, LANES), lambda b, g: (0, sb)),
            cpar(SSD_CONV, gc, 0), cpar(1, gc, 0),
            cpar(SSD_CONV, SSD_STATE, cbb_), cpar(1, SSD_STATE, cbb_),
            cpar(SSD_CONV, SSD_STATE, ccb_), cpar(1, SSD_STATE, ccb_),
            const((1, LANES)), const((1, LANES)),
            pl.BlockSpec((None, 2 * LANES, gc), lambda b, g: (g, 0, 0)),
            pl.BlockSpec((None, 2 * LANES, gc), lambda b, g: (g, 0, 0)),
            pl.BlockSpec((1, gc), lambda b, g: (0, g)),
            pl.BlockSpec((1, gc), lambda b, g: (0, g)),
        ],
        out_specs=pl.BlockSpec((None, seq, gc), lambda b, g: (b, 0, g)),
        out_shape=jax.ShapeDtypeStruct((bsz, seq, SSD_INNER), BF16),
        scratch_shapes=[pltpu.VMEM((2 * SSD_STATE, gc), F32),
                        pltpu.VMEM((seq, gc), BF16), pltpu.VMEM((seq, SSD_STATE), BF16),
                        pltpu.VMEM((seq, SSD_STATE), BF16)],
        compiler_params=pltpu.CompilerParams(
            dimension_semantics=("arbitrary", "arbitrary"), vmem_limit_bytes=VMEM_LIMIT),
        name="ssd",
    )(proj, proj, proj, proj, proj, proj_meta, proj_meta, proj_meta, proj_meta,
      conv_w, conv_b, conv_w, conv_b, conv_w, conv_b, dtb, alog, e_f, e_b, dexp, norm_w)


def _outproj_body(x_ref, og_ref, ys_ref, wo1_ref, wo2_ref, n2w_ref, wrh_ref, wrl_ref, br_ref,
                  h2_ref, n2p_ref, idx_ref, gate_ref, rank_ref, cnt_ref, carry_ref, *, tile):
    i = pl.program_id(0)

    @pl.when(i == 0)
    def _():
        carry_ref[...] = jnp.zeros_like(carry_ref)

    h2 = x_ref[...] + _dot(og_ref[...], wo1_ref[...]) + _dot(ys_ref[...], wo2_ref[...])
    h2_ref[...] = h2
    n2 = _rms(h2, n2w_ref[...])
    _store_rows(n2p_ref, _pack_bf16_pairs(n2))

    nh, nl = _split(n2)
    wrh, wrl = wrh_ref[...], wrl_ref[...]
    logits = _dot(nh, wrh) + _dot(nh, wrl) + _dot(nl, wrh) + br_ref[...]

    lane = lax.broadcasted_iota(I32, (tile, LANES), 1)
    lane_f = lane.astype(F32)
    lane4 = lax.broadcasted_iota(I32, (tile, TOP_K), 1)
    vals, onehots = [], []
    idx_out = jnp.zeros((tile, TOP_K), I32)
    work = logits
    for k in range(TOP_K):
        m = jnp.max(work, axis=-1, keepdims=True)
        first = jnp.min(jnp.where(work == m, lane_f, float(LANES)), axis=-1, keepdims=True)
        oh = lane_f == first
        work = jnp.where(oh, -jnp.inf, work)
        vals.append(m)
        onehots.append(oh)
        idx_out = jnp.where(lane4 == k, first.astype(I32), idx_out)
    idx_ref[...] = idx_out

    exps = [jnp.exp(v - vals[0]) for v in vals]
    inv = 1.0 / (exps[0] + exps[1] + exps[2] + exps[3])
    gate_out = jnp.zeros((tile, TOP_K), F32)
    for k in range(TOP_K):
        gate_out = jnp.where(lane4 == k, exps[k] * inv, gate_out)
    gate_ref[...] = gate_out

    any_oh = (onehots[0] | onehots[1] | onehots[2] | onehots[3])
    any16 = jnp.where(any_oh, 1.0, 0.0).astype(BF16)
    before = _dot(_tri(tile, lower=True, inclusive=False).astype(BF16), any16) + carry_ref[...]
    rank_out = jnp.zeros((tile, TOP_K), I32)
    for k in range(TOP_K):
        rk = jnp.sum(jnp.where(onehots[k], before, 0.0), axis=-1, keepdims=True)
        rank_out = jnp.where(lane4 == k, rk.astype(I32), rank_out)
    rank_ref[...] = rank_out
    carry = carry_ref[...] + jnp.sum(any16.astype(F32), axis=0, keepdims=True)
    carry_ref[...] = carry
    cnt_ref[...] = carry


def _outproj(x2d, o_gla, y_ssd, w_out1, w_out2, norm2_w, wr_hi, wr_lo, b_r, tile):
    rows = x2d.shape[0]
    row = lambda width: pl.BlockSpec((tile, width), lambda i: (i, 0))
    const = lambda shape: pl.BlockSpec(shape, lambda i: (0, 0))
    return pl.pallas_call(
        functools.partial(_outproj_body, tile=tile),
        grid=(rows // tile,),
        in_specs=[
            row(D_MODEL), row(GLA_DV), row(SSD_INNER),
            const((GLA_DV, D_MODEL)), const((SSD_INNER, D_MODEL)), const((1, D_MODEL)),
            const((D_MODEL, LANES)), const((D_MODEL, LANES)), const((1, LANES)),
        ],
        out_specs=[
            row(D_MODEL), pl.BlockSpec((tile, ROW_SUB, LANES), lambda i: (i, 0, 0)),
            row(TOP_K), row(TOP_K), row(TOP_K), const((1, LANES)),
        ],
        out_shape=[
            jax.ShapeDtypeStruct((rows, D_MODEL), F32),
            jax.ShapeDtypeStruct((rows, ROW_SUB, LANES), U32),
            jax.ShapeDtypeStruct((rows, TOP_K), I32),
            jax.ShapeDtypeStruct((rows, TOP_K), F32),
            jax.ShapeDtypeStruct((rows, TOP_K), I32),
            jax.ShapeDtypeStruct((1, LANES), F32),
        ],
        scratch_shapes=[pltpu.VMEM((1, LANES), F32)],
        compiler_params=pltpu.CompilerParams(
            dimension_semantics=("arbitrary",), vmem_limit_bytes=VMEM_LIMIT),
        name="outproj_router",
    )(x2d, o_gla, y_ssd, w_out1, w_out2, norm2_w, wr_hi, wr_lo, b_r)


def _row_copy(src, src_row, dst, dst_row, sem):
    return pltpu.make_async_copy(src.at[src_row], dst.at[dst_row], sem)


def _dispatch_body(dest_ref, n2p_ref, xs_hbm, sem, *, tile):
    def start(t, carry):
        for k in range(TOP_K):
            _row_copy(n2p_ref, t, xs_hbm, dest_ref[t * TOP_K + k], sem).start(priority=k % 2)
        return carry

    lax.fori_loop(0, tile, start, 0)

    for k in range(TOP_K):
        pltpu.make_async_copy(n2p_ref, xs_hbm.at[pl.ds(0, tile)], sem).wait()


def _dispatch(dest_flat, n2p, tile):
    rows = n2p.shape[0]
    return pl.pallas_call(
        functools.partial(_dispatch_body, tile=tile),
        grid=(rows // tile,),
        in_specs=[
            pl.BlockSpec((tile * TOP_K,), lambda i: (i,), memory_space=pltpu.SMEM),
            pl.BlockSpec((tile, ROW_SUB, LANES), lambda i: (i, 0, 0)),
        ],
        out_specs=pl.BlockSpec(memory_space=pl.ANY),
        out_shape=jax.ShapeDtypeStruct((rows * TOP_K, ROW_SUB, LANES), U32),
        scratch_shapes=[pltpu.SemaphoreType.DMA(())],
        compiler_params=pltpu.CompilerParams(dimension_semantics=("arbitrary",)),
        name="dispatch",
    )(dest_flat, n2p)


def _expert_body(blk_ref, exp_ref, lo_ref, hi_ref, first_ref, newexp_ref, nw_ref,
                 x_ref, wgu_ref, bgu_ref, wdn_ref, bdn_ref, o_ref, wgu16_ref, wdn16_ref, *, tile):
    w = pl.program_id(0)
    cast_rows = 64

    @pl.when((w < nw_ref[0]) & (newexp_ref[w] == 1))
    def _():
        def cast(i, carry):
            rows = pl.ds(pl.multiple_of(i * cast_rows, cast_rows), cast_rows)
            wgu16_ref[rows, :] = wgu_ref[rows, :].astype(BF16)
            wdn16_ref[rows, :] = wdn_ref[rows, :].astype(BF16)
            return carry

        lax.fori_loop(0, D_MODEL // cast_rows, cast, 0)

    @pl.when((w < nw_ref[0]) & (first_ref[w] == 1))
    def _():
        o_ref[...] = jnp.zeros_like(o_ref)

    @pl.when(w < nw_ref[0])
    def _():
        sub = tile // EXPERT_SUBTILES
        parts = [pl.ds(s * sub, sub) for s in range(EXPERT_SUBTILES)]
        xs = [_unpack_bf16_pairs(_load_rows(x_ref.at[p])).astype(BF16) for p in parts]
        gts = [jnp.minimum(_dot(x, wgu16_ref[:, :D_FF]) + bgu_ref[:, :D_FF], SWIGLU_LIMIT) for x in xs]
        ups = [jnp.clip(_dot(x, wgu16_ref[:, D_FF:]) + bgu_ref[:, D_FF:], -SWIGLU_LIMIT, SWIGLU_LIMIT)
               for x in xs]
        acts = [((up + 1.0) * gt * jax.nn.sigmoid(gt * SWIGLU_ALPHA)).astype(BF16) for gt, up in zip(gts, ups)]
        outs = [_pack_bf16_pairs(_dot(act, wdn16_ref[...]) + bdn_ref[...]) for act in acts]
        r = lax.broadcasted_iota(I32, (sub, LANES), 0)
        for s, (p, out) in enumerate(zip(parts, outs)):
            row = r + s * sub
            _store_rows(o_ref.at[p], out, keep=(row >= lo_ref[w]) & (row < hi_ref[w]))


def _experts(tables, xs, w_gu, b_gu, w_dn, b_dn, tile, n_work):
    rows = xs.shape[0]
    row_block = pl.BlockSpec((tile, ROW_SUB, LANES), lambda w, blk, ex, lo, hi, fi, ne, nw: (blk[w], 0, 0))
    grid_spec = pltpu.PrefetchScalarGridSpec(
        num_scalar_prefetch=7,
        grid=(n_work,),
        in_specs=[
            row_block,
            pl.BlockSpec((None, D_MODEL, 2 * D_FF), lambda w, blk, ex, lo, hi, fi, ne, nw: (ex[w], 0, 0)),
            pl.BlockSpec((None, 1, 2 * D_FF), lambda w, blk, ex, lo, hi, fi, ne, nw: (ex[w], 0, 0)),
            pl.BlockSpec((None, D_FF, D_MODEL), lambda w, blk, ex, lo, hi, fi, ne, nw: (ex[w], 0, 0)),
            pl.BlockSpec((None, 1, D_MODEL), lambda w, blk, ex, lo, hi, fi, ne, nw: (ex[w], 0, 0)),
        ],
        out_specs=row_block,
        scratch_shapes=[pltpu.VMEM((D_MODEL, 2 * D_FF), BF16), pltpu.VMEM((D_FF, D_MODEL), BF16)],
    )
    return pl.pallas_call(
        functools.partial(_expert_body, tile=tile),
        grid_spec=grid_spec,
        out_shape=jax.ShapeDtypeStruct((rows, ROW_SUB, LANES), U32),
        compiler_params=pltpu.CompilerParams(
            dimension_semantics=("arbitrary",), vmem_limit_bytes=VMEM_LIMIT),
        name="experts",
    )(*tables, xs, w_gu, b_gu, w_dn, b_dn)


def _combine_body(dest_ref, dest_next_ref, gate_ref, h2_ref, nfw_ref, ys_hbm, o_ref, buf, sems,
                  *, tile, n_steps):
    i = pl.program_id(0)
    slot = i % 2

    def gather(d_ref, s):
        def start(t, carry):
            for k in range(TOP_K):
                pltpu.make_async_copy(ys_hbm.at[d_ref[t * TOP_K + k]], buf.at[s, k, t],
                                      sems.at[s]).start(priority=k % 2)
            return carry

        lax.fori_loop(0, tile, start, 0)

    @pl.when(i == 0)
    def _():
        gather(dest_ref, 0)

    @pl.when(i + 1 < n_steps)
    def _():
        gather(dest_next_ref, 1 - slot)

    for k in range(TOP_K):
        pltpu.make_async_copy(ys_hbm.at[pl.ds(0, tile)], buf.at[slot, k], sems.at[slot]).wait()

    gate = gate_ref[...]
    h3 = h2_ref[...]
    for k in range(TOP_K):
        h3 = h3 + gate[:, k:k + 1] * _unpack_bf16_pairs(_load_rows(buf.at[slot, k]))
    o_ref[...] = _rms(h3, nfw_ref[...])


def _combine(dest_flat, gates, h2, norm_f_w, ys, tile):
    rows = h2.shape[0]
    n_steps = rows // tile
    return pl.pallas_call(
        functools.partial(_combine_body, tile=tile, n_steps=n_steps),
        grid=(n_steps,),
        in_specs=[
            pl.BlockSpec((tile * TOP_K,), lambda i: (i,), memory_space=pltpu.SMEM),
            pl.BlockSpec((tile * TOP_K,), lambda i: (jnp.minimum(i + 1, n_steps - 1),),
                         memory_space=pltpu.SMEM),
            pl.BlockSpec((tile, TOP_K), lambda i: (i, 0)),
            pl.BlockSpec((tile, D_MODEL), lambda i: (i, 0)),
            pl.BlockSpec((1, D_MODEL), lambda i: (0, 0)),
            pl.BlockSpec(memory_space=pl.ANY),
        ],
        out_specs=pl.BlockSpec((tile, D_MODEL), lambda i: (i, 0)),
        out_shape=jax.ShapeDtypeStruct((rows, D_MODEL), F32),
        scratch_shapes=[pltpu.VMEM((2, TOP_K, tile, ROW_SUB, LANES), U32), pltpu.SemaphoreType.DMA((2,))],
        compiler_params=pltpu.CompilerParams(
            dimension_semantics=("arbitrary",), vmem_limit_bytes=VMEM_LIMIT),
        name="combine_final",
    )(dest_flat, dest_flat, gates, h2, norm_f_w, ys)


def _work_tables(counts, n_rows, tile):
    n_blocks = n_rows // tile
    n_work = n_blocks + N_EXPERTS
    ends = jnp.cumsum(counts)
    starts = ends - counts
    first_blk = starts // tile
    last_blk = (ends - 1) // tile
    nb = jnp.where(counts > 0, last_blk - first_blk + 1, 0)
    wend = jnp.cumsum(nb)
    wstart = wend - nb
    total = wend[-1]
    w = jnp.minimum(jnp.arange(n_work, dtype=I32), total - 1)
    ex = jnp.minimum(jnp.sum(wend[None, :] <= w[:, None], axis=1), N_EXPERTS - 1).astype(I32)
    blk = (first_blk[ex] + (w - wstart[ex])).astype(I32)
    lo = (jnp.maximum(starts[ex], blk * tile) - blk * tile).astype(I32)
    hi = (jnp.minimum(ends[ex], (blk + 1) * tile) - blk * tile).astype(I32)
    prev_blk = jnp.concatenate([jnp.full((1,), -1, I32), blk[:-1]])
    first = (blk != prev_blk).astype(I32)
    prev_ex = jnp.concatenate([jnp.full((1,), -1, I32), ex[:-1]])
    new_ex = (ex != prev_ex).astype(I32)
    return (blk, ex, lo, hi, first, new_ex, total.reshape(1).astype(I32)), starts, n_work


def kernel(x, meta, norm1_w, w_in, gla_wa2_f, gla_ba2_f, gla_wa2_b, gla_ba2_b, gla_norm_w, conv_w, conv_b, dt_bias_f, dt_bias_b, a_log_f, a_log_b, ssd_d, ssd_norm_w, w_out, norm2_w, w_router, b_router, w_gu, b_gu, w_dn, b_dn, norm_f_w):
    bsz, seq, d = x.shape
    n_tok = bsz * seq
    l = 0

    wi = w_in[l]
    a_cols = wi[:, 3072:3104]
    dt_cols = wi[:, 5664:5696]
    w_perm = jnp.concatenate(
        [wi[:, :3072], wi[:, 3104:5664], a_cols, dt_cols,
         jnp.zeros((d, N_PROJ - COL_SMALL - 64), F32)], axis=1).astype(BF16)

    def lane_rows(w, lane0):
        return jnp.zeros((LANES, w.shape[1]), F32).at[lane0:lane0 + w.shape[0]].set(w)

    wa_f = lane_rows(gla_wa2_f[l], LANE_AF).astype(BF16)
    wa_b = lane_rows(gla_wa2_b[l], LANE_AB).astype(BF16)
    ba_f = gla_ba2_f[l][None, :]
    ba_b = gla_ba2_b[l][None, :]

    def lane_vec(vf, vb):
        z = jnp.zeros((1, LANES), F32)
        return z.at[0, LANE_DTF:LANE_DTF + SSD_HEADS].set(vf).at[0, LANE_DTB:LANE_DTB + SSD_HEADS].set(vb)

    dtb = lane_vec(dt_bias_f[l], dt_bias_b[l])
    alog = lane_vec(a_log_f[l], a_log_b[l])
    lane_id = (jnp.arange(2 * LANES) % LANES)[None, :, None]
    head_id = (jnp.arange(SSD_GROUP_CH) // SSD_HEAD_DIM)[None, None, :]
    grp = jnp.arange(SSD_GROUPS)[:, None, None] * SSD_GROUP_HEADS
    e_f = (lane_id == LANE_DTF + grp + head_id).astype(BF16)
    e_b = (lane_id == LANE_DTB + grp + head_id).astype(BF16)
    dexp = jnp.repeat(ssd_d[l], SSD_HEAD_DIM)[None, :]

    wr = jnp.zeros((d, LANES), F32).at[:, :N_EXPERTS].set(w_router[l])
    wr_hi = wr.astype(BF16)
    wr_lo = (wr - wr_hi.astype(F32)).astype(BF16)
    b_r = jnp.full((1, LANES), -1e30, F32).at[0, :N_EXPERTS].set(b_router[l])

    x2d = x.reshape(n_tok, d)
    x_meta = jnp.pad(meta.astype(F32), ((NPAD, 0), (0, 0)))
    n1 = norm1_w[l][None, :]
    proj = _inproj(x2d, n1, w_perm, ROW_TILE).reshape(bsz, seq, N_PROJ)
    proj_meta = _inproj(x_meta, n1, w_perm, CHUNK)
    o_gla = _gla(proj, proj_meta, wa_f, ba_f, wa_b, ba_b, gla_norm_w[l][None, :])
    y_ssd = _ssd(proj, proj_meta, conv_w[l], conv_b[l][None, :], dtb, alog, e_f, e_b, dexp,
                 ssd_norm_w[l][None, :])

    wo = w_out[l].astype(BF16)
    h2, n2p, idx, gates, rank, cnt = _outproj(
        x2d, o_gla.reshape(n_tok, GLA_DV), y_ssd.reshape(n_tok, SSD_INNER),
        wo[:GLA_DV], wo[GLA_DV:], norm2_w[l][None, :], wr_hi, wr_lo, b_r, ROW_TILE)

    counts = cnt[0, :N_EXPERTS].astype(I32)
    tables, starts, n_work = _work_tables(counts, n_tok * TOP_K, EXPERT_TILE)
    dest = (rank + jnp.take(starts, idx)).astype(I32).reshape(-1)

    xs = _dispatch(dest, n2p, ROW_TILE)
    ys = _experts(tables, xs, w_gu[l], b_gu[l][:, None, :], w_dn[l], b_dn[l][:, None, :],
                  EXPERT_TILE, n_work)
    out = _combine(dest, gates, h2, norm_f_w[None, :], ys, COMBINE_TILE)
    return out.reshape(bsz, seq, d)
```

```python
import functools

import jax
import jax.numpy as jnp
from jax import lax
from jax.experimental import pallas as pl
from jax.experimental.pallas import tpu as pltpu

F32 = jnp.float32
BF16 = jnp.bfloat16
I32 = jnp.int32
U32 = jnp.uint32

D_MODEL = 1024
N_META = 16
CHUNK = 64
NPAD = CHUNK - N_META
GLA_HEADS = 4
GLA_DK = 512
GLA_DV = 1024
GLA_HEAD_K = GLA_DK // GLA_HEADS
GLA_HEAD_V = GLA_DV // GLA_HEADS
GLA_RANK = 16
GLA_GATE_NORM = 16.0
SSD_INNER = 1024
SSD_HEAD_DIM = 64
SSD_HEADS = SSD_INNER // SSD_HEAD_DIM
SSD_GROUPS = 2
SSD_GROUP_HEADS = SSD_HEADS // SSD_GROUPS
SSD_GROUP_CH = SSD_INNER // SSD_GROUPS
SSD_STATE = 128
SSD_CONV = 5
N_EXPERTS = 32
TOP_K = 4
D_FF = 1024
SWIGLU_LIMIT = 7.0
SWIGLU_ALPHA = 1.702
EPS = 1e-6

LANES = 128
ROW_WORDS = D_MODEL // 2
ROW_SUB = ROW_WORDS // LANES
HALO = 16

COL_Q = 0
COL_K = 512
COL_V = 1024
COL_G = 2048
COL_Z = 3072
COL_X = 4096
COL_B = 5120
COL_C = 5376
COL_SMALL = 5632
N_PROJ = 5760
LANE_AF = 0
LANE_AB = 16
LANE_DTF = 32
LANE_DTB = 48

ROW_TILE = 512
EXPERT_TILE = 512
EXPERT_SUBTILES = 2
ROUTER_SUBTILES = 2
COMBINE_TILE = 256
VMEM_LIMIT = 56 * 1024 * 1024


def _dot(a, b):
    return jnp.dot(a, b, preferred_element_type=F32)


def _dot_nt(a, b):
    return lax.dot_general(a, b, (((1,), (1,)), ((), ())), preferred_element_type=F32)


def _dot_tn(a, b):
    return lax.dot_general(a, b, (((0,), (0,)), ((), ())), preferred_element_type=F32)


def _split(x):
    hi = x.astype(BF16)
    lo = (x - hi.astype(F32)).astype(BF16)
    return hi, lo


def _sel_dot_l(m01, x):
    hi, lo = _split(x)
    return _dot(m01, jnp.concatenate([hi, lo], axis=0))


def _sel_dot_r(x, m01):
    hi, lo = _split(x)
    return _dot(jnp.concatenate([hi, lo], axis=1), m01)


def _twice_cols(m):
    return jnp.concatenate([m, m], axis=1)


def _softplus(x):
    return jnp.maximum(x, 0.0) + jnp.log(1.0 + jnp.exp(-jnp.abs(x)))


def _log_sigmoid(x):
    return jnp.minimum(x, 0.0) - jnp.log(1.0 + jnp.exp(-jnp.abs(x)))


def _silu(x):
    return x * jax.nn.sigmoid(x)


def _rms(x, w):
    return x * lax.rsqrt(jnp.mean(x * x, axis=-1, keepdims=True) + EPS) * w


def _pack_bf16_pairs(x):
    w = x.shape[1] // 2
    return pltpu.pack_elementwise([x[:, :w], x[:, w:]], packed_dtype=BF16)


def _unpack_bf16_pairs(p):
    lo, hi = (pltpu.unpack_elementwise(p, index=i, packed_dtype=BF16, unpacked_dtype=F32) for i in range(2))
    return jnp.concatenate([lo, hi], axis=1)


def _load_rows(ref3):
    return jnp.concatenate([ref3[:, j, :] for j in range(ROW_SUB)], axis=1)


def _store_rows(ref3, val, keep=None):
    for j in range(ROW_SUB):
        piece = val[:, j * LANES:(j + 1) * LANES]
        if keep is not None:
            piece = jnp.where(keep, piece, ref3[:, j, :])
        ref3[:, j, :] = piece


def _tri(n, *, lower, inclusive):
    r = lax.broadcasted_iota(I32, (n, n), 0)
    c = lax.broadcasted_iota(I32, (n, n), 1)
    if lower:
        return (c <= r) if inclusive else (c < r)
    return (c >= r) if inclusive else (c > r)


def _inproj_body(x_ref, nw_ref, w_ref, o_ref, *, col_chunks):
    xn = _rms(x_ref[...], nw_ref[...]).astype(BF16)
    for lo, hi in col_chunks:
        o_ref[:, lo:hi] = _dot(xn, w_ref[:, lo:hi]).astype(BF16)


def _inproj(x2d, norm_w, w_perm, tile):
    rows = x2d.shape[0]
    col_chunks = tuple((c, min(c + 1024, N_PROJ)) for c in range(0, N_PROJ, 1024))
    return pl.pallas_call(
        functools.partial(_inproj_body, col_chunks=col_chunks),
        grid=(rows // tile,),
        in_specs=[
            pl.BlockSpec((tile, D_MODEL), lambda i: (i, 0)),
            pl.BlockSpec((1, D_MODEL), lambda i: (0, 0)),
            pl.BlockSpec((D_MODEL, N_PROJ), lambda i: (0, 0)),
        ],
        out_specs=pl.BlockSpec((tile, N_PROJ), lambda i: (i, 0)),
        out_shape=jax.ShapeDtypeStruct((rows, N_PROJ), BF16),
        compiler_params=pltpu.CompilerParams(
            dimension_semantics=("arbitrary",), vmem_limit_bytes=VMEM_LIMIT),
        name="inproj",
    )(x2d, norm_w, w_perm)


GLA_HEADS_PER_STEP = 2
GLA_STEPS_PER_ITER = 4
SSD_STEPS_PER_ITER = 4


def _gla_body(q_ref, k_ref, v_ref, g_ref, sm_ref, qm_ref, km_ref, vm_ref, smm_ref,
              waf_ref, baf_ref, wab_ref, bab_ref, nw_ref, o_ref, st_ref, *, n_chunks):
    heads = GLA_HEADS_PER_STEP
    nprob = 2 * heads
    rows_all = nprob * CHUNK
    wk = heads * GLA_HEAD_K
    dk, dv = GLA_HEAD_K, GLA_HEAD_V

    tri_f = _tri(CHUNK, lower=True, inclusive=True)
    r2 = lax.broadcasted_iota(I32, (2 * CHUNK, 2 * CHUNK), 0)
    c2 = lax.broadcasted_iota(I32, (2 * CHUNK, 2 * CHUNK), 1)
    cum2 = ((r2 < CHUNK) & (c2 <= r2)) | ((r2 >= CHUNK) & (c2 >= r2))
    tri2 = _twice_cols(cum2.astype(BF16))
    ra = lax.broadcasted_iota(I32, (rows_all, rows_all), 0)
    ca = lax.broadcasted_iota(I32, (rows_all, rows_all), 1)
    same = (ra // CHUNK) == (ca // CHUNK)
    att_mask = same & (((ra < heads * CHUNK) & (ca <= ra)) | ((ra >= heads * CHUNK) & (ca > ra)))
    rb_ = lax.broadcasted_iota(I32, (rows_all, nprob * dk), 0) // CHUNK
    cb_ = lax.broadcasted_iota(I32, (rows_all, nprob * dk), 1) // dk
    own = rb_ == cb_

    waf, baf = waf_ref[...], baf_ref[...]
    wab, bab = wab_ref[...], bab_ref[...]
    wa_cat = jnp.concatenate([waf, wab], axis=1)
    nw = nw_ref[...]
    zero16 = jnp.zeros((), BF16)

    st_ref[...] = jnp.zeros_like(st_ref)

    meta_rows = lax.broadcasted_iota(I32, (CHUNK, wk), 0) >= NPAD
    lg = _log_sigmoid(_dot(smm_ref[...], waf) + baf) * (1.0 / GLA_GATE_NORM)
    bm = _sel_dot_l(_twice_cols(tri_f.astype(BF16)), jnp.where(meta_rows, lg, 0.0))
    for h in range(heads):
        lanes = slice(h * dk, (h + 1) * dk)
        bh, toth = bm[:, lanes], bm[CHUNK - 1:CHUNK, lanes]
        kend = (km_ref[:, lanes].astype(F32) * jnp.exp(toth - bh)).astype(BF16)
        st_ref[:, lanes] = _dot_tn(vm_ref[:, h * dv:(h + 1) * dv], kend)

    def stack(f, b, w):
        return jnp.concatenate([f[:, h * w:(h + 1) * w] for h in range(heads)]
                               + [b[:, h * w:(h + 1) * w] for h in range(heads)], axis=0)

    def steps(j, finalize):
        ids = [j * GLA_STEPS_PER_ITER + s for s in range(GLA_STEPS_PER_ITER)]
        rfs = [pl.ds(pl.multiple_of(i * CHUNK, CHUNK), CHUNK) for i in ids]
        rbs = [pl.ds(pl.multiple_of((n_chunks - 1 - i) * CHUNK, CHUNK), CHUNK) for i in ids]
        zs = [_dot(jnp.concatenate([sm_ref[rf, :], sm_ref[rb, :]], axis=0), wa_cat) for rf, rb in zip(rfs, rbs)]
        b2s = []
        for z in zs:
            lg = jnp.concatenate([z[:CHUNK, :wk] + baf, z[CHUNK:, wk:] + bab], axis=0)
            b2s.append(_sel_dot_l(tri2, _log_sigmoid(lg) * (1.0 / GLA_GATE_NORM)))
        vss, qds, kends, tots, gs = [], [], [], [], []
        for b2, rf, rb in zip(b2s, rfs, rbs):
            tot_f, tot_b = b2[CHUNK - 1:CHUNK], b2[CHUNK:CHUNK + 1]
            bst = stack(b2[:CHUNK], b2[CHUNK:], dk)
            tst = stack(jnp.broadcast_to(tot_f, (CHUNK, wk)), jnp.broadcast_to(tot_b, (CHUNK, wk)), dk)
            qs = stack(q_ref[rf, :], q_ref[rb, :], dk).astype(F32)
            ks = stack(k_ref[rf, :], k_ref[rb, :], dk).astype(F32)
            qd = (qs * (GLA_HEAD_K ** -0.5) * jnp.exp(bst)).astype(BF16)
            kd = (ks * jnp.exp(-bst)).astype(BF16)
            vss.append(stack(v_ref[rf, :], v_ref[rb, :], dv))
            qds.append(qd)
            kends.append((ks * jnp.exp(tst - bst)).astype(BF16))
            tots.append(jnp.concatenate([tot_f, tot_b], axis=1))
            gs.append(_dot_nt(qd, kd))
        intras = [_dot(jnp.where(att_mask, g, 0.0).astype(BF16), vs) for g, vs in zip(gs, vss)]
        upds = [_dot_tn(vs, jnp.where(own, jnp.concatenate([kend] * nprob, axis=1), zero16))
                for vs, kend in zip(vss, kends)]
        st = st_ref[...]
        outs = []
        for qd, tot, upd, intra in zip(qds, tots, upds, intras):
            qd_own = jnp.where(own, jnp.concatenate([qd] * nprob, axis=1), zero16)
            outs.append(intra + _dot_nt(qd_own, st.astype(BF16)))
            st = st * jnp.exp(tot) + upd
        st_ref[...] = st
        for out, rf, rb in zip(outs, rfs, rbs):
            for p in range(nprob):
                rows, h = (rf if p < heads else rb), p % heads
                o = out[p * CHUNK:(p + 1) * CHUNK]
                cols = slice(h * dv, (h + 1) * dv)
                if finalize:
                    o = o + o_ref[rows, cols].astype(F32)
                    gate = _silu(g_ref[rows, cols].astype(F32))
                    o_ref[rows, cols] = (_rms(o, nw) * gate).astype(BF16)
                else:
                    o_ref[rows, cols] = o.astype(BF16)

    def first_half(j, carry):
        steps(j, False)
        return carry

    def second_half(j, carry):
        steps(j, True)
        return carry

    half_iters = n_chunks // 2 // GLA_STEPS_PER_ITER
    lax.fori_loop(0, half_iters, first_half, 0)
    lax.fori_loop(half_iters, 2 * half_iters, second_half, 0)


def _gla(proj, proj_meta, wa_f, ba_f, wa_b, ba_b, norm_w):
    bsz, seq, _ = proj.shape
    assert (seq // CHUNK) % (2 * GLA_STEPS_PER_ITER) == 0
    hp = GLA_HEADS_PER_STEP
    wk, wv = hp * GLA_HEAD_K, hp * GLA_HEAD_V
    kb, vb, gb, sb = COL_K // wk, COL_V // wv, COL_G // wv, COL_SMALL // LANES
    real = lambda width, base: pl.BlockSpec((None, seq, width), lambda b, h: (b, 0, base + h))
    meta = lambda width, base: pl.BlockSpec((CHUNK, width), lambda b, h: (0, base + h))
    per_step = lambda rows_: pl.BlockSpec((rows_, wk), lambda b, h: (0, h))
    return pl.pallas_call(
        functools.partial(_gla_body, n_chunks=seq // CHUNK),
        grid=(bsz, GLA_HEADS // hp),
        in_specs=[
            real(wk, 0), real(wk, kb), real(wv, vb), real(wv, gb),
            pl.BlockSpec((None, seq, LANES), lambda b, h: (b, 0, sb)),
            meta(wk, 0), meta(wk, kb), meta(wv, vb),
            pl.BlockSpec((CHUNK, LANES), lambda b, h: (0, sb)),
            per_step(LANES), per_step(1), per_step(LANES), per_step(1),
            pl.BlockSpec((1, GLA_HEAD_V), lambda b, h: (0, 0)),
        ],
        out_specs=pl.BlockSpec((None, seq, wv), lambda b, h: (b, 0, h)),
        out_shape=jax.ShapeDtypeStruct((bsz, seq, GLA_DV), BF16),
        scratch_shapes=[pltpu.VMEM((GLA_HEAD_V, 2 * hp * GLA_HEAD_K), F32)],
        compiler_params=pltpu.CompilerParams(
            dimension_semantics=("arbitrary", "arbitrary"), vmem_limit_bytes=VMEM_LIMIT),
        name="gla",
    )(proj, proj, proj, proj, proj, proj_meta, proj_meta, proj_meta, proj_meta,
      wa_f, ba_f, wa_b, ba_b, norm_w)


def _conv_silu(win, cw, cb):
    half = (SSD_CONV - 1) // 2
    acc = cb
    for j in range(SSD_CONV):
        lo = HALO - half + j
        acc = acc + win[lo:lo + CHUNK, :] * cw[j:j + 1, :]
    return _silu(acc)


def _ssd_body(x_ref, z_ref, b_ref, c_ref, sm_ref, xm_ref, bm_ref, cm_ref, smm_ref,
              cwx_ref, cbx_ref, cwb_ref, cbb_ref, cwc_ref, cbc_ref, dtb_ref, alog_ref,
              ef_ref, eb_ref, dexp_ref, nw_ref, o_ref, st_ref, xc_ref, bc_ref, cc_ref,
              *, n_chunks):
    gh, p, n = SSD_GROUP_HEADS, SSD_HEAD_DIM, SSD_STATE
    width = gh * p
    two = 2 * CHUNK
    lane_s = lax.broadcasted_iota(I32, (two, width), 1) % p
    row2 = lax.broadcasted_iota(I32, (two, width), 0)
    row_t = row2 % CHUNK
    irep2 = lane_s == row_t
    irep2_16 = irep2.astype(BF16)
    pair_mask = ((row2 < CHUNK) & (lane_s <= row_t)) | ((row2 >= CHUNK) & (lane_s > row_t))
    r2 = lax.broadcasted_iota(I32, (two, two), 0)
    c2 = lax.broadcasted_iota(I32, (two, two), 1)
    same_dir = (r2 < CHUNK) == (c2 < CHUNK)
    ones2 = _twice_cols(same_dir.astype(BF16))
    cum2 = ((r2 < CHUNK) & (c2 <= r2)) | ((r2 >= CHUNK) & (c2 >= r2))
    tri2 = _twice_cols(cum2.astype(BF16))
    bd_r = lax.broadcasted_iota(I32, (width, width), 0) // p
    bd_c = lax.broadcasted_iota(I32, (width, width), 1) // p
    bdmask = bd_r == bd_c
    zero16 = jnp.zeros((), BF16)

    cwx, cbx = cwx_ref[...], cbx_ref[...]
    cwb, cbb = cwb_ref[...], cbb_ref[...]
    cwc, cbc = cwc_ref[...], cbc_ref[...]
    dtb = dtb_ref[...]
    a_row = -jnp.exp(alog_ref[...])
    ef, eb = ef_ref[...], eb_ref[...]
    dexp, nw = dexp_ref[...], nw_ref[...]
    seq = n_chunks * CHUNK

    def window(ref, mref, r):
        off = pl.multiple_of(r * CHUNK, CHUNK)
        poff = pl.multiple_of(jnp.maximum(off - HALO, 0), HALO)
        noff = pl.multiple_of(jnp.minimum(off + CHUNK, seq - HALO), HALO)
        prev = jnp.where(r == 0, mref[CHUNK - HALO:, :], ref[pl.ds(poff, HALO), :])
        nxt = jnp.where(r == n_chunks - 1, zero16, ref[pl.ds(noff, HALO), :])
        return jnp.concatenate([prev, ref[pl.ds(off, CHUNK), :], nxt], axis=0)

    half = (SSD_CONV - 1) // 2
    side_taps = [j for j in range(SSD_CONV) if j != half]
    win_rows = CHUNK + 2 * HALO
    sr = lax.broadcasted_iota(I32, (len(side_taps) * CHUNK, win_rows), 0)
    sc = lax.broadcasted_iota(I32, (len(side_taps) * CHUNK, win_rows), 1)
    tap_of = sr // CHUNK
    tap_shift = jnp.where(tap_of < half, tap_of, tap_of + 1) - half
    shift_mat = (sc == (sr % CHUNK) + HALO + tap_shift).astype(BF16)
    cw_all = jnp.concatenate([cwx, cwb, cwc], axis=1)
    cb_all = jnp.concatenate([cbx, cbb, cbc], axis=1)

    def conv_body(it, carry):
        chunks = [it * SSD_STEPS_PER_ITER + s for s in range(SSD_STEPS_PER_ITER)]
        wins = [jnp.concatenate([window(x_ref, xm_ref, r), window(b_ref, bm_ref, r), window(c_ref, cm_ref, r)],
                                axis=1) for r in chunks]
        shifts = [_dot(shift_mat, win) for win in wins]
        for r, win, shifted in zip(chunks, wins, shifts):
            rows = pl.ds(pl.multiple_of(r * CHUNK, CHUNK), CHUNK)
            acc = cb_all + win[HALO:HALO + CHUNK].astype(F32) * cw_all[half:half + 1]
            for pos, j in enumerate(side_taps):
                acc = acc + shifted[pos * CHUNK:(pos + 1) * CHUNK] * cw_all[j:j + 1]
            y = _silu(acc).astype(BF16)
            xc_ref[rows, :] = y[:, :width]
            bc_ref[rows, :] = y[:, width:width + n]
            cc_ref[rows, :] = y[:, width + n:]
        return carry

    lax.fori_loop(0, n_chunks // SSD_STEPS_PER_ITER, conv_body, 0)

    st_ref[...] = jnp.zeros_like(st_ref)

    def meta_window(mref, ref):
        zeros = jnp.zeros((HALO, mref.shape[1]), F32)
        return jnp.concatenate([zeros, mref[...].astype(F32), ref[0:HALO, :].astype(F32)], axis=0)

    def meta_mask(width_):
        return lax.broadcasted_iota(I32, (CHUNK, width_), 0) >= NPAD

    xc = jnp.where(meta_mask(width), _conv_silu(meta_window(xm_ref, x_ref), cwx, cbx), 0.0)
    bc = jnp.where(meta_mask(n), _conv_silu(meta_window(bm_ref, b_ref), cwb, cbb), 0.0)
    dt = jnp.where(meta_mask(LANES), _softplus(smm_ref[...].astype(F32) + dtb), 0.0)
    cs = _sel_dot_l(_twice_cols(cum2[:CHUNK, :CHUNK].astype(BF16)), dt * a_row)
    both = _sel_dot_r(jnp.concatenate([cs, dt], axis=0), ef)
    cs_e, dt_e = both[:CHUNK], both[CHUNK:]
    xend = (xc * dt_e * jnp.exp(cs_e[CHUNK - 1:CHUNK] - cs_e)).astype(BF16)
    st_ref[0:n, :] = _dot_tn(bc.astype(BF16), xend)

    zeros_n = jnp.zeros((CHUNK, n), BF16)

    def own_dir(a):
        return jnp.concatenate([jnp.concatenate([a[:CHUNK], zeros_n], axis=1),
                                jnp.concatenate([zeros_n, a[CHUNK:]], axis=1)], axis=0)

    def steps(j, finalize):
        ids = [j * SSD_STEPS_PER_ITER + s for s in range(SSD_STEPS_PER_ITER)]
        rfs = [pl.ds(pl.multiple_of(i * CHUNK, CHUNK), CHUNK) for i in ids]
        rbs = [pl.ds(pl.multiple_of((n_chunks - 1 - i) * CHUNK, CHUNK), CHUNK) for i in ids]
        both_rows = lambda ref, rf, rb: jnp.concatenate([ref[rf, :], ref[rb, :]], axis=0)
        dts = [_softplus(both_rows(sm_ref, rf, rb).astype(F32) + dtb) for rf, rb in zip(rfs, rbs)]
        css = [_sel_dot_l(tri2, dt * a_row) for dt in dts]
        exs = [(_sel_dot_r(jnp.concatenate([cs[:CHUNK], dt[:CHUNK]], axis=0), ef),
                _sel_dot_r(jnp.concatenate([cs[CHUNK:], dt[CHUNK:]], axis=0), eb)) for cs, dt in zip(css, dts)]
        xcs, cs_es, tots, xdts, xends, cbs, bcs, ccs = [], [], [], [], [], [], [], []
        for (ex_f, ex_b), rf, rb in zip(exs, rfs, rbs):
            cs_e = jnp.concatenate([ex_f[:CHUNK], ex_b[:CHUNK]], axis=0)
            dt_e = jnp.concatenate([ex_f[CHUNK:], ex_b[CHUNK:]], axis=0)
            tot_f, tot_b = cs_e[CHUNK - 1:CHUNK], cs_e[CHUNK:CHUNK + 1]
            tot = jnp.concatenate([jnp.broadcast_to(tot_f, (CHUNK, width)),
                                   jnp.broadcast_to(tot_b, (CHUNK, width))], axis=0)
            xc = both_rows(xc_ref, rf, rb).astype(F32)
            bc16, cc16 = both_rows(bc_ref, rf, rb), both_rows(cc_ref, rf, rb)
            xdt = xc * dt_e
            xcs.append(xc)
            cs_es.append(cs_e)
            tots.append((tot_f, tot_b))
            xdts.append(xdt.astype(BF16))
            xends.append((xdt * jnp.exp(tot - cs_e)).astype(BF16))
            bcs.append(bc16)
            ccs.append(cc16)
            cbs.append(_dot_nt(cc16, bc16))
        cb_reps = [_dot(jnp.where(same_dir, cb, 0.0).astype(BF16), irep2_16) for cb in cbs]
        cs_rows = [_sel_dot_l(ones2, jnp.where(irep2, cs_e, 0.0)) for cs_e in cs_es]
        intras = []
        for cb_rep, cs_row, cs_e, xdt16 in zip(cb_reps, cs_rows, cs_es, xdts):
            decay = jnp.where(pair_mask, jnp.exp(jnp.minimum(cs_e - cs_row, 0.0)), 0.0)
            w = (cb_rep * decay).astype(BF16)
            intras.append(jnp.concatenate(
                [_dot(w[d * CHUNK:(d + 1) * CHUNK],
                      jnp.where(bdmask, jnp.concatenate([xdt16[d * CHUNK:(d + 1) * CHUNK]] * gh, axis=0), zero16))
                 for d in range(2)], axis=0))
        upds = [_dot_tn(own_dir(bc16), xend) for bc16, xend in zip(bcs, xends)]
        st = st_ref[...]
        ys = []
        for intra, cc16, cs_e, (tot_f, tot_b), upd in zip(intras, ccs, cs_es, tots, upds):
            ys.append(intra + _dot(own_dir(cc16), st.astype(BF16)) * jnp.exp(cs_e))
            grow = jnp.concatenate([jnp.broadcast_to(jnp.exp(tot_f), (n, width)),
                                    jnp.broadcast_to(jnp.exp(tot_b), (n, width))], axis=0)
            st = st * grow + upd
        st_ref[...] = st
        for y, xc, rf, rb in zip(ys, xcs, rfs, rbs):
            for d, rows in enumerate((rf, rb)):
                yd = y[d * CHUNK:(d + 1) * CHUNK]
                if finalize:
                    yd = yd + o_ref[rows, :].astype(F32) + xc[d * CHUNK:(d + 1) * CHUNK] * dexp
                    yd = yd * _silu(z_ref[rows, :].astype(F32))
                    o_ref[rows, :] = _rms(yd, nw).astype(BF16)
                else:
                    o_ref[rows, :] = yd.astype(BF16)

    def first_half(j, carry):
        steps(j, False)
        return carry

    def second_half(j, carry):
        steps(j, True)
        return carry

    half_iters = n_chunks // 2 // SSD_STEPS_PER_ITER
    lax.fori_loop(0, half_iters, first_half, 0)
    lax.fori_loop(half_iters, 2 * half_iters, second_half, 0)


def _ssd(proj, proj_meta, conv_w, conv_b, dtb, alog, e_f, e_b, dexp, norm_w):
    bsz, seq, _ = proj.shape
    gc = SSD_GROUP_CH
    xb, zb = COL_X // gc, COL_Z // gc
    bb, cb, sb = COL_B // SSD_STATE, COL_C // SSD_STATE, COL_SMALL // LANES
    real = lambda width, base: pl.BlockSpec((None, seq, width), lambda b, g: (b, 0, base + g))
    meta = lambda width, base: pl.BlockSpec((CHUNK, width), lambda b, g: (0, base + g))
    cpar = lambda rows_, width, base: pl.BlockSpec((rows_, width), lambda b, g: (0, base + g))
    cbb_, ccb_ = SSD_INNER // SSD_STATE, SSD_INNER // SSD_STATE + SSD_GROUPS
    const = lambda shape: pl.BlockSpec(shape, lambda b, g: (0, 0))
    return pl.pallas_call(
        functools.partial(_ssd_body, n_chunks=seq // CHUNK),
        grid=(bsz, SSD_GROUPS),
        in_specs=[
            real(gc, xb), real(gc, zb), real(SSD_STATE, bb), real(SSD_STATE, cb),
            pl.BlockSpec((None, seq, LANES), lambda b, g: (b, 0, sb)),
            meta(gc, xb), meta(SSD_STATE, bb), meta(SSD_STATE, cb),
            pl.BlockSpec((CHUNK, LANES), lambda b, g: (0, sb)),
            cpar(SSD_CONV, gc, 0), cpar(1, gc, 0),
            cpar(SSD_CONV, SSD_STATE, cbb_), cpar(1, SSD_STATE, cbb_),
            cpar(SSD_CONV, SSD_STATE, ccb_), cpar(1, SSD_STATE, ccb_),
            const((1, LANES)), const((1, LANES)),
            pl.BlockSpec((None, 2 * LANES, gc), lambda b, g: (g, 0, 0)),
            pl.BlockSpec((None, 2 * LANES, gc), lambda b, g: (g, 0, 0)),
            pl.BlockSpec((1, gc), lambda b, g: (0, g)),
            pl.BlockSpec((1, gc), lambda b, g: (0, g)),
        ],
        out_specs=pl.BlockSpec((None, seq, gc), lambda b, g: (b, 0, g)),
        out_shape=jax.ShapeDtypeStruct((bsz, seq, SSD_INNER), BF16),
        scratch_shapes=[pltpu.VMEM((2 * SSD_STATE, gc), F32),
                        pltpu.VMEM((seq, gc), BF16), pltpu.VMEM((seq, SSD_STATE), BF16),
                        pltpu.VMEM((seq, SSD_STATE), BF16)],
        compiler_params=pltpu.CompilerParams(
            dimension_semantics=("arbitrary", "arbitrary"), vmem_limit_bytes=VMEM_LIMIT),
        name="ssd",
    )(proj, proj, proj, proj, proj, proj_meta, proj_meta, proj_meta, proj_meta,
      conv_w, conv_b, conv_w, conv_b, conv_w, conv_b, dtb, alog, e_f, e_b, dexp, norm_w)


def _outproj_body(x_ref, og_ref, ys_ref, wo1_ref, wo2_ref, n2w_ref, wrh_ref, wrl_ref, br_ref,
                  h2_ref, n2p_ref, code_ref, gate_ref, cnt_ref, carry_ref, *, tile):
    i = pl.program_id(0)

    @pl.when(i == 0)
    def _():
        carry_ref[...] = jnp.zeros_like(carry_ref)

    sub = tile // ROUTER_SUBTILES
    parts = [pl.ds(s * sub, sub) for s in range(ROUTER_SUBTILES)]
    wo1, wo2, n2w = wo1_ref[...], wo2_ref[...], n2w_ref[...]
    wrh, wrl, br = wrh_ref[...], wrl_ref[...], br_ref[...]
    h2s = [x_ref[p, :] + _dot(og_ref[p, :], wo1) + _dot(ys_ref[p, :], wo2) for p in parts]
    n2s = []
    for p, h2 in zip(parts, h2s):
        h2_ref[p, :] = h2
        n2 = _rms(h2, n2w)
        _store_rows(n2p_ref.at[p], _pack_bf16_pairs(n2))
        n2s.append(n2)
    logit_parts = []
    for n2 in n2s:
        nh, nl = _split(n2)
        logit_parts.append(_dot(nh, wrh) + _dot(nh, wrl) + _dot(nl, wrh) + br)

    lane = lax.broadcasted_iota(I32, (sub, LANES), 1)
    lane_f = lane.astype(F32)
    lane4 = lax.broadcasted_iota(I32, (sub, TOP_K), 1)
    per_row = LANES // TOP_K
    tok = lax.broadcasted_iota(I32, (sub, LANES), 0)
    here = (lane // TOP_K) == (tok % per_row)
    gather_rows = (lax.broadcasted_iota(I32, (sub // per_row, sub), 1) // per_row
                   == lax.broadcasted_iota(I32, (sub // per_row, sub), 0)).astype(BF16)
    before_me = _tri(sub, lower=True, inclusive=False).astype(BF16)

    routed = []
    for p, logits in zip(parts, logit_parts):
        vals, onehots, picks = [], [], []
        work = logits
        for k in range(TOP_K):
            m = jnp.max(work, axis=-1, keepdims=True)
            first = jnp.min(jnp.where(work == m, lane_f, float(LANES)), axis=-1, keepdims=True)
            oh = lane_f == first
            work = jnp.where(oh, -jnp.inf, work)
            vals.append(m)
            onehots.append(oh)
            picks.append(first)
        exps = [jnp.exp(v - vals[0]) for v in vals]
        inv = 1.0 / (exps[0] + exps[1] + exps[2] + exps[3])
        gate_out = jnp.zeros((sub, TOP_K), F32)
        for k in range(TOP_K):
            gate_out = jnp.where(lane4 == k, exps[k] * inv, gate_out)
        gate_ref[p, :] = gate_out
        any_oh = (onehots[0] | onehots[1] | onehots[2] | onehots[3])
        any16 = jnp.where(any_oh, 1.0, 0.0).astype(BF16)
        routed.append((onehots, picks, any16, _dot(before_me, any16)))

    carry = carry_ref[...]
    for s, (onehots, picks, any16, before) in enumerate(routed):
        before = before + carry
        carry = carry + jnp.sum(any16.astype(F32), axis=0, keepdims=True)
        rest = [jnp.sum(jnp.where(onehots[k], before, 0.0), axis=-1, keepdims=True) * N_EXPERTS + picks[k]
                for k in range(TOP_K)]
        flat = jnp.zeros((sub // per_row, LANES), F32)
        for scale in (65536.0, 256.0, 1.0):
            piece = [jnp.floor(c * (1.0 / scale)) for c in rest]
            rest = [c - q * scale for c, q in zip(rest, piece)]
            by_k = piece[TOP_K - 1]
            for k in range(TOP_K - 2, -1, -1):
                by_k = jnp.where(lane % TOP_K == k, piece[k], by_k)
            flat = flat + scale * _dot(gather_rows, jnp.where(here, by_k, 0.0).astype(BF16))
        code_ref[pl.ds(s * (sub // per_row), sub // per_row), :] = flat.astype(I32)
    carry_ref[...] = carry
    cnt_ref[...] = carry


def _outproj(x2d, o_gla, y_ssd, w_out1, w_out2, norm2_w, wr_hi, wr_lo, b_r, tile):
    rows = x2d.shape[0]
    row = lambda width: pl.BlockSpec((tile, width), lambda i: (i, 0))
    const = lambda shape: pl.BlockSpec(shape, lambda i: (0, 0))
    return pl.pallas_call(
        functools.partial(_outproj_body, tile=tile),
        grid=(rows // tile,),
        in_specs=[
            row(D_MODEL), row(GLA_DV), row(SSD_INNER),
            const((GLA_DV, D_MODEL)), const((SSD_INNER, D_MODEL)), const((1, D_MODEL)),
            const((D_MODEL, LANES)), const((D_MODEL, LANES)), const((1, LANES)),
        ],
        out_specs=[
            row(D_MODEL), pl.BlockSpec((tile, ROW_SUB, LANES), lambda i: (i, 0, 0)),
            pl.BlockSpec((tile * TOP_K // LANES, LANES), lambda i: (i, 0)), row(TOP_K), const((1, LANES)),
        ],
        out_shape=[
            jax.ShapeDtypeStruct((rows, D_MODEL), F32),
            jax.ShapeDtypeStruct((rows, ROW_SUB, LANES), U32),
            jax.ShapeDtypeStruct((rows * TOP_K // LANES, LANES), I32),
            jax.ShapeDtypeStruct((rows, TOP_K), F32),
            jax.ShapeDtypeStruct((1, LANES), F32),
        ],
        scratch_shapes=[pltpu.VMEM((1, LANES), F32)],
        compiler_params=pltpu.CompilerParams(
            dimension_semantics=("arbitrary",), vmem_limit_bytes=VMEM_LIMIT),
        name="outproj_router",
    )(x2d, o_gla, y_ssd, w_out1, w_out2, norm2_w, wr_hi, wr_lo, b_r)


def _row_copy(src, src_row, dst, dst_row, sem):
    return pltpu.make_async_copy(src.at[src_row], dst.at[dst_row], sem)


def _dispatch_body(dest_ref, n2p_ref, xs_hbm, sem, *, tile):
    def start(t, carry):
        for k in range(TOP_K):
            _row_copy(n2p_ref, t, xs_hbm, dest_ref[t * TOP_K + k], sem).start(priority=k % 2)
        return carry

    lax.fori_loop(0, tile, start, 0)

    for k in range(TOP_K):
        pltpu.make_async_copy(n2p_ref, xs_hbm.at[pl.ds(0, tile)], sem).wait()


def _dispatch(dest_flat, n2p, tile):
    rows = n2p.shape[0]
    return pl.pallas_call(
        functools.partial(_dispatch_body, tile=tile),
        grid=(rows // tile,),
        in_specs=[
            pl.BlockSpec((tile * TOP_K,), lambda i: (i,), memory_space=pltpu.SMEM),
            pl.BlockSpec((tile, ROW_SUB, LANES), lambda i: (i, 0, 0)),
        ],
        out_specs=pl.BlockSpec(memory_space=pl.ANY),
        out_shape=jax.ShapeDtypeStruct((rows * TOP_K, ROW_SUB, LANES), U32),
        scratch_shapes=[pltpu.SemaphoreType.DMA(())],
        compiler_params=pltpu.CompilerParams(dimension_semantics=("arbitrary",)),
        name="dispatch",
    )(dest_flat, n2p)


def _expert_body(blk_ref, exp_ref, lo_ref, hi_ref, first_ref, newexp_ref, nw_ref,
                 x_ref, wgu_ref, bgu_ref, wdn_ref, bdn_ref, o_ref, wgu16_ref, wdn16_ref, *, tile):
    w = pl.program_id(0)
    cast_rows = 64

    @pl.when((w < nw_ref[0]) & (newexp_ref[w] == 1))
    def _():
        def cast(i, carry):
            rows = pl.ds(pl.multiple_of(i * cast_rows, cast_rows), cast_rows)
            wgu16_ref[rows, :] = wgu_ref[rows, :].astype(BF16)
            wdn16_ref[rows, :] = wdn_ref[rows, :].astype(BF16)
            return carry

        lax.fori_loop(0, D_MODEL // cast_rows, cast, 0)

    @pl.when((w < nw_ref[0]) & (first_ref[w] == 1))
    def _():
        o_ref[...] = jnp.zeros_like(o_ref)

    @pl.when(w < nw_ref[0])
    def _():
        sub = tile // EXPERT_SUBTILES
        parts = [pl.ds(s * sub, sub) for s in range(EXPERT_SUBTILES)]
        xs = [_unpack_bf16_pairs(_load_rows(x_ref.at[p])).astype(BF16) for p in parts]
        gts = [jnp.minimum(_dot(x, wgu16_ref[:, :D_FF]) + bgu_ref[:, :D_FF], SWIGLU_LIMIT) for x in xs]
        ups = [jnp.clip(_dot(x, wgu16_ref[:, D_FF:]) + bgu_ref[:, D_FF:], -SWIGLU_LIMIT, SWIGLU_LIMIT)
               for x in xs]
        acts = [((up + 1.0) * gt * jax.nn.sigmoid(gt * SWIGLU_ALPHA)).astype(BF16) for gt, up in zip(gts, ups)]
        outs = [_pack_bf16_pairs(_dot(act, wdn16_ref[...]) + bdn_ref[...]) for act in acts]
        r = lax.broadcasted_iota(I32, (sub, LANES), 0)
        for s, (p, out) in enumerate(zip(parts, outs)):
            row = r + s * sub
            _store_rows(o_ref.at[p], out, keep=(row >= lo_ref[w]) & (row < hi_ref[w]))


def _experts(tables, xs, w_gu, b_gu, w_dn, b_dn, tile, n_work):
    rows = xs.shape[0]
    row_block = pl.BlockSpec((tile, ROW_SUB, LANES), lambda w, blk, ex, lo, hi, fi, ne, nw: (blk[w], 0, 0))
    grid_spec = pltpu.PrefetchScalarGridSpec(
        num_scalar_prefetch=7,
        grid=(n_work,),
        in_specs=[
            row_block,
            pl.BlockSpec((None, D_MODEL, 2 * D_FF), lambda w, blk, ex, lo, hi, fi, ne, nw: (ex[w], 0, 0)),
            pl.BlockSpec((None, 1, 2 * D_FF), lambda w, blk, ex, lo, hi, fi, ne, nw: (ex[w], 0, 0)),
            pl.BlockSpec((None, D_FF, D_MODEL), lambda w, blk, ex, lo, hi, fi, ne, nw: (ex[w], 0, 0)),
            pl.BlockSpec((None, 1, D_MODEL), lambda w, blk, ex, lo, hi, fi, ne, nw: (ex[w], 0, 0)),
        ],
        out_specs=row_block,
        scratch_shapes=[pltpu.VMEM((D_MODEL, 2 * D_FF), BF16), pltpu.VMEM((D_FF, D_MODEL), BF16)],
    )
    return pl.pallas_call(
        functools.partial(_expert_body, tile=tile),
        grid_spec=grid_spec,
        out_shape=jax.ShapeDtypeStruct((rows, ROW_SUB, LANES), U32),
        compiler_params=pltpu.CompilerParams(
            dimension_semantics=("arbitrary",), vmem_limit_bytes=VMEM_LIMIT),
        name="experts",
    )(*tables, xs, w_gu, b_gu, w_dn, b_dn)


def _combine_body(dest_ref, dest_next_ref, gate_ref, h2_ref, nfw_ref, ys_hbm, o_ref, buf, sems,
                  *, tile, n_steps):
    i = pl.program_id(0)
    slot = i % 2

    def gather(d_ref, s):
        def start(t, carry):
            for k in range(TOP_K):
                pltpu.make_async_copy(ys_hbm.at[d_ref[t * TOP_K + k]], buf.at[s, k, t],
                                      sems.at[s]).start(priority=k % 2)
            return carry

        lax.fori_loop(0, tile, start, 0)

    @pl.when(i == 0)
    def _():
        gather(dest_ref, 0)

    @pl.when(i + 1 < n_steps)
    def _():
        gather(dest_next_ref, 1 - slot)

    for k in range(TOP_K):
        pltpu.make_async_copy(ys_hbm.at[pl.ds(0, tile)], buf.at[slot, k], sems.at[slot]).wait()

    gate = gate_ref[...]
    h3 = h2_ref[...]
    for k in range(TOP_K):
        h3 = h3 + gate[:, k:k + 1] * _unpack_bf16_pairs(_load_rows(buf.at[slot, k]))
    o_ref[...] = _rms(h3, nfw_ref[...])


def _combine(dest_flat, gates, h2, norm_f_w, ys, tile):
    rows = h2.shape[0]
    n_steps = rows // tile
    return pl.pallas_call(
        functools.partial(_combine_body, tile=tile, n_steps=n_steps),
        grid=(n_steps,),
        in_specs=[
            pl.BlockSpec((tile * TOP_K,), lambda i: (i,), memory_space=pltpu.SMEM),
            pl.BlockSpec((tile * TOP_K,), lambda i: (jnp.minimum(i + 1, n_steps - 1),),
                         memory_space=pltpu.SMEM),
            pl.BlockSpec((tile, TOP_K), lambda i: (i, 0)),
            pl.BlockSpec((tile, D_MODEL), lambda i: (i, 0)),
            pl.BlockSpec((1, D_MODEL), lambda i: (0, 0)),
            pl.BlockSpec(memory_space=pl.ANY),
        ],
        out_specs=pl.BlockSpec((tile, D_MODEL), lambda i: (i, 0)),
        out_shape=jax.ShapeDtypeStruct((rows, D_MODEL), F32),
        scratch_shapes=[pltpu.VMEM((2, TOP_K, tile, ROW_SUB, LANES), U32), pltpu.SemaphoreType.DMA((2,))],
        compiler_params=pltpu.CompilerParams(
            dimension_semantics=("arbitrary",), vmem_limit_bytes=VMEM_LIMIT),
        name="combine_final",
    )(dest_flat, dest_flat, gates, h2, norm_f_w, ys)


def _work_tables(counts, n_rows, tile):
    n_blocks = n_rows // tile
    n_work = n_blocks + N_EXPERTS
    ends = jnp.cumsum(counts)
    starts = ends - counts
    first_blk = starts // tile
    last_blk = (ends - 1) // tile
    nb = jnp.where(counts > 0, last_blk - first_blk + 1, 0)
    wend = jnp.cumsum(nb)
    wstart = wend - nb
    total = wend[-1]
    w = jnp.minimum(jnp.arange(n_work, dtype=I32), total - 1)
    ex = jnp.minimum(jnp.sum(wend[None, :] <= w[:, None], axis=1), N_EXPERTS - 1).astype(I32)
    mine = ex[:, None] == jnp.arange(N_EXPERTS, dtype=I32)[None, :]
    of_ex = lambda v: jnp.sum(jnp.where(mine, v[None, :], 0), axis=1)
    blk = (of_ex(first_blk - wstart) + w).astype(I32)
    lo = (jnp.maximum(of_ex(starts), blk * tile) - blk * tile).astype(I32)
    hi = (jnp.minimum(of_ex(ends), (blk + 1) * tile) - blk * tile).astype(I32)
    prev_blk = jnp.concatenate([jnp.full((1,), -1, I32), blk[:-1]])
    first = (blk != prev_blk).astype(I32)
    prev_ex = jnp.concatenate([jnp.full((1,), -1, I32), ex[:-1]])
    new_ex = (ex != prev_ex).astype(I32)
    return (blk, ex, lo, hi, first, new_ex, total.reshape(1).astype(I32)), starts, n_work


def kernel(x, meta, norm1_w, w_in, gla_wa2_f, gla_ba2_f, gla_wa2_b, gla_ba2_b, gla_norm_w, conv_w, conv_b, dt_bias_f, dt_bias_b, a_log_f, a_log_b, ssd_d, ssd_norm_w, w_out, norm2_w, w_router, b_router, w_gu, b_gu, w_dn, b_dn, norm_f_w):
    bsz, seq, d = x.shape
    n_tok = bsz * seq
    l = 0

    wi = w_in[l]
    a_cols = wi[:, 3072:3104]
    dt_cols = wi[:, 5664:5696]
    w_perm = jnp.concatenate(
        [wi[:, :3072], wi[:, 3104:5664], a_cols, dt_cols,
         jnp.zeros((d, N_PROJ - COL_SMALL - 64), F32)], axis=1).astype(BF16)

    def lane_rows(w, lane0):
        return jnp.zeros((LANES, w.shape[1]), F32).at[lane0:lane0 + w.shape[0]].set(w)

    wa_f = lane_rows(gla_wa2_f[l], LANE_AF).astype(BF16)
    wa_b = lane_rows(gla_wa2_b[l], LANE_AB).astype(BF16)
    ba_f = gla_ba2_f[l][None, :]
    ba_b = gla_ba2_b[l][None, :]

    def lane_vec(vf, vb):
        z = jnp.zeros((1, LANES), F32)
        return z.at[0, LANE_DTF:LANE_DTF + SSD_HEADS].set(vf).at[0, LANE_DTB:LANE_DTB + SSD_HEADS].set(vb)

    dtb = lane_vec(dt_bias_f[l], dt_bias_b[l])
    alog = lane_vec(a_log_f[l], a_log_b[l])
    lane_id = (jnp.arange(2 * LANES) % LANES)[None, :, None]
    head_id = (jnp.arange(SSD_GROUP_CH) // SSD_HEAD_DIM)[None, None, :]
    grp = jnp.arange(SSD_GROUPS)[:, None, None] * SSD_GROUP_HEADS
    e_f = (lane_id == LANE_DTF + grp + head_id).astype(BF16)
    e_b = (lane_id == LANE_DTB + grp + head_id).astype(BF16)
    dexp = jnp.repeat(ssd_d[l], SSD_HEAD_DIM)[None, :]

    wr = jnp.zeros((d, LANES), F32).at[:, :N_EXPERTS].set(w_router[l])
    wr_hi = wr.astype(BF16)
    wr_lo = (wr - wr_hi.astype(F32)).astype(BF16)
    b_r = jnp.full((1, LANES), -1e30, F32).at[0, :N_EXPERTS].set(b_router[l])

    x2d = x.reshape(n_tok, d)
    x_meta = jnp.pad(meta.astype(F32), ((NPAD, 0), (0, 0)))
    n1 = norm1_w[l][None, :]
    proj = _inproj(x2d, n1, w_perm, ROW_TILE).reshape(bsz, seq, N_PROJ)
    proj_meta = _inproj(x_meta, n1, w_perm, CHUNK)
    o_gla = _gla(proj, proj_meta, wa_f, ba_f, wa_b, ba_b, gla_norm_w[l][None, :])
    y_ssd = _ssd(proj, proj_meta, conv_w[l], conv_b[l][None, :], dtb, alog, e_f, e_b, dexp,
                 ssd_norm_w[l][None, :])

    wo = w_out[l].astype(BF16)
    h2, n2p, code, gates, cnt = _outproj(
        x2d, o_gla.reshape(n_tok, GLA_DV), y_ssd.reshape(n_tok, SSD_INNER),
        wo[:GLA_DV], wo[GLA_DV:], norm2_w[l][None, :], wr_hi, wr_lo, b_r, ROW_TILE)

    counts = cnt[0, :N_EXPERTS].astype(I32)
    tables, starts, n_work = _work_tables(counts, n_tok * TOP_K, EXPERT_TILE)
    expert_of = (code % N_EXPERTS)[..., None] == jnp.arange(N_EXPERTS, dtype=I32)
    dest = (code // N_EXPERTS + jnp.sum(jnp.where(expert_of, starts, 0), axis=-1)).astype(I32).reshape(-1)

    xs = _dispatch(dest, n2p, ROW_TILE)
    ys = _experts(tables, xs, w_gu[l], b_gu[l][:, None, :], w_dn[l], b_dn[l][:, None, :],
                  EXPERT_TILE, n_work)
    out = _combine(dest, gates, h2, norm_f_w[None, :], ys, COMBINE_TILE)
    return out.reshape(bsz, seq, d)
```

```python
import functools

import jax
import jax.numpy as jnp
from jax import lax
from jax.experimental import pallas as pl
from jax.experimental.pallas import tpu as pltpu

F32 = jnp.float32
BF16 = jnp.bfloat16
I32 = jnp.int32
U32 = jnp.uint32

D_MODEL = 1024
N_META = 16
CHUNK = 64
NPAD = CHUNK - N_META
GLA_HEADS = 4
GLA_DK = 512
GLA_DV = 1024
GLA_HEAD_K = GLA_DK // GLA_HEADS
GLA_HEAD_V = GLA_DV // GLA_HEADS
GLA_RANK = 16
GLA_GATE_NORM = 16.0
SSD_INNER = 1024
SSD_HEAD_DIM = 64
SSD_HEADS = SSD_INNER // SSD_HEAD_DIM
SSD_GROUPS = 2
SSD_GROUP_HEADS = SSD_HEADS // SSD_GROUPS
SSD_GROUP_CH = SSD_INNER // SSD_GROUPS
SSD_STATE = 128
SSD_CONV = 5
N_EXPERTS = 32
TOP_K = 4
D_FF = 1024
SWIGLU_LIMIT = 7.0
SWIGLU_ALPHA = 1.702
EPS = 1e-6

LANES = 128
ROW_WORDS = D_MODEL // 2
ROW_SUB = ROW_WORDS // LANES
HALO = 16

COL_Q = 0
COL_K = 512
COL_V = 1024
COL_G = 2048
COL_Z = 3072
COL_X = 4096
COL_B = 5120
COL_C = 5376
COL_SMALL = 5632
N_PROJ = 5760
LANE_AF = 0
LANE_AB = 16
LANE_DTF = 32
LANE_DTB = 48

ROW_TILE = 512
EXPERT_TILE = 512
EXPERT_SUBTILES = 2
ROUTER_SUBTILES = 2
COMBINE_TILE = 256
COMBINE_CHUNK = 16
VMEM_LIMIT = 56 * 1024 * 1024


def _dot(a, b):
    return jnp.dot(a, b, preferred_element_type=F32)


def _dot_nt(a, b):
    return lax.dot_general(a, b, (((1,), (1,)), ((), ())), preferred_element_type=F32)


def _dot_tn(a, b):
    return lax.dot_general(a, b, (((0,), (0,)), ((), ())), preferred_element_type=F32)


def _split(x):
    hi = x.astype(BF16)
    lo = (x - hi.astype(F32)).astype(BF16)
    return hi, lo


def _sel_dot_l(m01, x):
    hi, lo = _split(x)
    return _dot(m01, jnp.concatenate([hi, lo], axis=0))


def _sel_dot_r(x, m01):
    hi, lo = _split(x)
    return _dot(jnp.concatenate([hi, lo], axis=1), m01)


def _twice_cols(m):
    return jnp.concatenate([m, m], axis=1)


def _softplus(x):
    return jnp.maximum(x, 0.0) + jnp.log(1.0 + jnp.exp(-jnp.abs(x)))


def _log_sigmoid(x):
    return jnp.minimum(x, 0.0) - jnp.log(1.0 + jnp.exp(-jnp.abs(x)))


def _silu(x):
    return x * jax.nn.sigmoid(x)


def _rms(x, w):
    return x * lax.rsqrt(jnp.mean(x * x, axis=-1, keepdims=True) + EPS) * w


def _pack_bf16_pairs(x):
    w = x.shape[1] // 2
    return pltpu.pack_elementwise([x[:, :w], x[:, w:]], packed_dtype=BF16)


def _unpack_bf16_pairs(p):
    lo, hi = (pltpu.unpack_elementwise(p, index=i, packed_dtype=BF16, unpacked_dtype=F32) for i in range(2))
    return jnp.concatenate([lo, hi], axis=1)


def _load_rows(ref3):
    return jnp.concatenate([ref3[:, j, :] for j in range(ROW_SUB)], axis=1)


def _store_rows(ref3, val, keep=None):
    for j in range(ROW_SUB):
        piece = val[:, j * LANES:(j + 1) * LANES]
        if keep is not None:
            piece = jnp.where(keep, piece, ref3[:, j, :])
        ref3[:, j, :] = piece


def _tri(n, *, lower, inclusive):
    r = lax.broadcasted_iota(I32, (n, n), 0)
    c = lax.broadcasted_iota(I32, (n, n), 1)
    if lower:
        return (c <= r) if inclusive else (c < r)
    return (c >= r) if inclusive else (c > r)


def _inproj_body(x_ref, nw_ref, w_ref, o_ref, *, col_chunks):
    xn = _rms(x_ref[...], nw_ref[...]).astype(BF16)
    for lo, hi in col_chunks:
        o_ref[:, lo:hi] = _dot(xn, w_ref[:, lo:hi]).astype(BF16)


def _inproj(x2d, norm_w, w_perm, tile):
    rows = x2d.shape[0]
    col_chunks = tuple((c, min(c + 1024, N_PROJ)) for c in range(0, N_PROJ, 1024))
    return pl.pallas_call(
        functools.partial(_inproj_body, col_chunks=col_chunks),
        grid=(rows // tile,),
        in_specs=[
            pl.BlockSpec((tile, D_MODEL), lambda i: (i, 0)),
            pl.BlockSpec((1, D_MODEL), lambda i: (0, 0)),
            pl.BlockSpec((D_MODEL, N_PROJ), lambda i: (0, 0)),
        ],
        out_specs=pl.BlockSpec((tile, N_PROJ), lambda i: (i, 0)),
        out_shape=jax.ShapeDtypeStruct((rows, N_PROJ), BF16),
        compiler_params=pltpu.CompilerParams(
            dimension_semantics=("arbitrary",), vmem_limit_bytes=VMEM_LIMIT),
        name="inproj",
    )(x2d, norm_w, w_perm)


GLA_HEADS_PER_STEP = 2
GLA_STEPS_PER_ITER = 4
SSD_STEPS_PER_ITER = 4


def _gla_body(q_ref, k_ref, v_ref, g_ref, sm_ref, qm_ref, km_ref, vm_ref, smm_ref,
              waf_ref, baf_ref, wab_ref, bab_ref, nw_ref, o_ref, st_ref, *, n_chunks):
    heads = GLA_HEADS_PER_STEP
    nprob = 2 * heads
    rows_all = nprob * CHUNK
    wk = heads * GLA_HEAD_K
    dk, dv = GLA_HEAD_K, GLA_HEAD_V

    tri_f = _tri(CHUNK, lower=True, inclusive=True)
    r2 = lax.broadcasted_iota(I32, (2 * CHUNK, 2 * CHUNK), 0)
    c2 = lax.broadcasted_iota(I32, (2 * CHUNK, 2 * CHUNK), 1)
    cum2 = ((r2 < CHUNK) & (c2 <= r2)) | ((r2 >= CHUNK) & (c2 >= r2))
    tri2 = _twice_cols(cum2.astype(BF16))
    ra = lax.broadcasted_iota(I32, (rows_all, rows_all), 0)
    ca = lax.broadcasted_iota(I32, (rows_all, rows_all), 1)
    same = (ra // CHUNK) == (ca // CHUNK)
    att_mask = same & (((ra < heads * CHUNK) & (ca <= ra)) | ((ra >= heads * CHUNK) & (ca > ra)))
    rb_ = lax.broadcasted_iota(I32, (rows_all, nprob * dk), 0) // CHUNK
    cb_ = lax.broadcasted_iota(I32, (rows_all, nprob * dk), 1) // dk
    own = rb_ == cb_

    waf, baf = waf_ref[...], baf_ref[...]
    wab, bab = wab_ref[...], bab_ref[...]
    wa_cat = jnp.concatenate([waf, wab], axis=1)
    nw = nw_ref[...]
    zero16 = jnp.zeros((), BF16)

    st_ref[...] = jnp.zeros_like(st_ref)

    meta_rows = lax.broadcasted_iota(I32, (CHUNK, wk), 0) >= NPAD
    lg = _log_sigmoid(_dot(smm_ref[...], waf) + baf) * (1.0 / GLA_GATE_NORM)
    bm = _sel_dot_l(_twice_cols(tri_f.astype(BF16)), jnp.where(meta_rows, lg, 0.0))
    for h in range(heads):
        lanes = slice(h * dk, (h + 1) * dk)
        bh, toth = bm[:, lanes], bm[CHUNK - 1:CHUNK, lanes]
        kend = (km_ref[:, lanes].astype(F32) * jnp.exp(toth - bh)).astype(BF16)
        st_ref[:, lanes] = _dot_tn(vm_ref[:, h * dv:(h + 1) * dv], kend)

    def stack(f, b, w):
        return jnp.concatenate([f[:, h * w:(h + 1) * w] for h in range(heads)]
                               + [b[:, h * w:(h + 1) * w] for h in range(heads)], axis=0)

    def steps(j, finalize):
        ids = [j * GLA_STEPS_PER_ITER + s for s in range(GLA_STEPS_PER_ITER)]
        rfs = [pl.ds(pl.multiple_of(i * CHUNK, CHUNK), CHUNK) for i in ids]
        rbs = [pl.ds(pl.multiple_of((n_chunks - 1 - i) * CHUNK, CHUNK), CHUNK) for i in ids]
        zs = [_dot(jnp.concatenate([sm_ref[rf, :], sm_ref[rb, :]], axis=0), wa_cat) for rf, rb in zip(rfs, rbs)]
        b2s = []
        for z in zs:
            lg = jnp.concatenate([z[:CHUNK, :wk] + baf, z[CHUNK:, wk:] + bab], axis=0)
            b2s.append(_sel_dot_l(tri2, _log_sigmoid(lg) * (1.0 / GLA_GATE_NORM)))
        vss, qds, kends, tots, gs = [], [], [], [], []
        for b2, rf, rb in zip(b2s, rfs, rbs):
            tot_f, tot_b = b2[CHUNK - 1:CHUNK], b2[CHUNK:CHUNK + 1]
            bst = stack(b2[:CHUNK], b2[CHUNK:], dk)
            tst = stack(jnp.broadcast_to(tot_f, (CHUNK, wk)), jnp.broadcast_to(tot_b, (CHUNK, wk)), dk)
            qs = stack(q_ref[rf, :], q_ref[rb, :], dk).astype(F32)
            ks = stack(k_ref[rf, :], k_ref[rb, :], dk).astype(F32)
            qd = (qs * (GLA_HEAD_K ** -0.5) * jnp.exp(bst)).astype(BF16)
            kd = (ks * jnp.exp(-bst)).astype(BF16)
            vss.append(stack(v_ref[rf, :], v_ref[rb, :], dv))
            qds.append(qd)
            kends.append((ks * jnp.exp(tst - bst)).astype(BF16))
            tots.append(jnp.concatenate([tot_f, tot_b], axis=1))
            gs.append(_dot_nt(qd, kd))
        intras = [_dot(jnp.where(att_mask, g, 0.0).astype(BF16), vs) for g, vs in zip(gs, vss)]
        upds = [_dot_tn(vs, jnp.where(own, jnp.concatenate([kend] * nprob, axis=1), zero16))
                for vs, kend in zip(vss, kends)]
        st = st_ref[...]
        outs = []
        for qd, tot, upd, intra in zip(qds, tots, upds, intras):
            qd_own = jnp.where(own, jnp.concatenate([qd] * nprob, axis=1), zero16)
            outs.append(intra + _dot_nt(qd_own, st.astype(BF16)))
            st = st * jnp.exp(tot) + upd
        st_ref[...] = st
        for out, rf, rb in zip(outs, rfs, rbs):
            for p in range(nprob):
                rows, h = (rf if p < heads else rb), p % heads
                o = out[p * CHUNK:(p + 1) * CHUNK]
                cols = slice(h * dv, (h + 1) * dv)
                if finalize:
                    o = o + o_ref[rows, cols].astype(F32)
                    gate = _silu(g_ref[rows, cols].astype(F32))
                    o_ref[rows, cols] = (_rms(o, nw) * gate).astype(BF16)
                else:
                    o_ref[rows, cols] = o.astype(BF16)

    def first_half(j, carry):
        steps(j, False)
        return carry

    def second_half(j, carry):
        steps(j, True)
        return carry

    half_iters = n_chunks // 2 // GLA_STEPS_PER_ITER
    lax.fori_loop(0, half_iters, first_half, 0)
    lax.fori_loop(half_iters, 2 * half_iters, second_half, 0)


def _gla(proj, proj_meta, wa_f, ba_f, wa_b, ba_b, norm_w):
    bsz, seq, _ = proj.shape
    assert (seq // CHUNK) % (2 * GLA_STEPS_PER_ITER) == 0
    hp = GLA_HEADS_PER_STEP
    wk, wv = hp * GLA_HEAD_K, hp * GLA_HEAD_V
    kb, vb, gb, sb = COL_K // wk, COL_V // wv, COL_G // wv, COL_SMALL // LANES
    real = lambda width, base: pl.BlockSpec((None, seq, width), lambda b, h: (b, 0, base + h))
    meta = lambda width, base: pl.BlockSpec((CHUNK, width), lambda b, h: (0, base + h))
    per_step = lambda rows_: pl.BlockSpec((rows_, wk), lambda b, h: (0, h))
    return pl.pallas_call(
        functools.partial(_gla_body, n_chunks=seq // CHUNK),
        grid=(bsz, GLA_HEADS // hp),
        in_specs=[
            real(wk, 0), real(wk, kb), real(wv, vb), real(wv, gb),
            pl.BlockSpec((None, seq, LANES), lambda b, h: (b, 0, sb)),
            meta(wk, 0), meta(wk, kb), meta(wv, vb),
            pl.BlockSpec((CHUNK, LANES), lambda b, h: (0, sb)),
            per_step(LANES), per_step(1), per_step(LANES), per_step(1),
            pl.BlockSpec((1, GLA_HEAD_V), lambda b, h: (0, 0)),
        ],
        out_specs=pl.BlockSpec((None, seq, wv), lambda b, h: (b, 0, h)),
        out_shape=jax.ShapeDtypeStruct((bsz, seq, GLA_DV), BF16),
        scratch_shapes=[pltpu.VMEM((GLA_HEAD_V, 2 * hp * GLA_HEAD_K), F32)],
        compiler_params=pltpu.CompilerParams(
            dimension_semantics=("arbitrary", "arbitrary"), vmem_limit_bytes=VMEM_LIMIT),
        name="gla",
    )(proj, proj, proj, proj, proj, proj_meta, proj_meta, proj_meta, proj_meta,
      wa_f, ba_f, wa_b, ba_b, norm_w)


def _conv_silu(win, cw, cb):
    half = (SSD_CONV - 1) // 2
    acc = cb
    for j in range(SSD_CONV):
        lo = HALO - half + j
        acc = acc + win[lo:lo + CHUNK, :] * cw[j:j + 1, :]
    return _silu(acc)


def _ssd_body(x_ref, z_ref, b_ref, c_ref, sm_ref, xm_ref, bm_ref, cm_ref, smm_ref,
              cwx_ref, cbx_ref, cwb_ref, cbb_ref, cwc_ref, cbc_ref, dtb_ref, alog_ref,
              ef_ref, eb_ref, dexp_ref, nw_ref, o_ref, st_ref, xc_ref, bc_ref, cc_ref,
              *, n_chunks):
    gh, p, n = SSD_GROUP_HEADS, SSD_HEAD_DIM, SSD_STATE
    width = gh * p
    two = 2 * CHUNK
    lane_s = lax.broadcasted_iota(I32, (two, width), 1) % p
    row2 = lax.broadcasted_iota(I32, (two, width), 0)
    row_t = row2 % CHUNK
    irep2 = lane_s == row_t
    irep2_16 = irep2.astype(BF16)
    pair_mask = ((row2 < CHUNK) & (lane_s <= row_t)) | ((row2 >= CHUNK) & (lane_s > row_t))
    r2 = lax.broadcasted_iota(I32, (two, two), 0)
    c2 = lax.broadcasted_iota(I32, (two, two), 1)
    same_dir = (r2 < CHUNK) == (c2 < CHUNK)
    ones2 = _twice_cols(same_dir.astype(BF16))
    cum2 = ((r2 < CHUNK) & (c2 <= r2)) | ((r2 >= CHUNK) & (c2 >= r2))
    tri2 = _twice_cols(cum2.astype(BF16))
    bd_r = lax.broadcasted_iota(I32, (width, width), 0) // p
    bd_c = lax.broadcasted_iota(I32, (width, width), 1) // p
    bdmask = bd_r == bd_c
    zero16 = jnp.zeros((), BF16)

    cwx, cbx = cwx_ref[...], cbx_ref[...]
    cwb, cbb = cwb_ref[...], cbb_ref[...]
    cwc, cbc = cwc_ref[...], cbc_ref[...]
    dtb = dtb_ref[...]
    a_row = -jnp.exp(alog_ref[...])
    ef, eb = ef_ref[...], eb_ref[...]
    dexp, nw = dexp_ref[...], nw_ref[...]
    seq = n_chunks * CHUNK

    def window(ref, mref, r):
        off = pl.multiple_of(r * CHUNK, CHUNK)
        poff = pl.multiple_of(jnp.maximum(off - HALO, 0), HALO)
        noff = pl.multiple_of(jnp.minimum(off + CHUNK, seq - HALO), HALO)
        prev = jnp.where(r == 0, mref[CHUNK - HALO:, :], ref[pl.ds(poff, HALO), :])
        nxt = jnp.where(r == n_chunks - 1, zero16, ref[pl.ds(noff, HALO), :])
        return jnp.concatenate([prev, ref[pl.ds(off, CHUNK), :], nxt], axis=0)

    half = (SSD_CONV - 1) // 2
    side_taps = [j for j in range(SSD_CONV) if j != half]
    win_rows = CHUNK + 2 * HALO
    sr = lax.broadcasted_iota(I32, (len(side_taps) * CHUNK, win_rows), 0)
    sc = lax.broadcasted_iota(I32, (len(side_taps) * CHUNK, win_rows), 1)
    tap_of = sr // CHUNK
    tap_shift = jnp.where(tap_of < half, tap_of, tap_of + 1) - half
    shift_mat = (sc == (sr % CHUNK) + HALO + tap_shift).astype(BF16)
    cw_all = jnp.concatenate([cwx, cwb, cwc], axis=1)
    cb_all = jnp.concatenate([cbx, cbb, cbc], axis=1)

    def conv_body(it, carry):
        chunks = [it * SSD_STEPS_PER_ITER + s for s in range(SSD_STEPS_PER_ITER)]
        wins = [jnp.concatenate([window(x_ref, xm_ref, r), window(b_ref, bm_ref, r), window(c_ref, cm_ref, r)],
                                axis=1) for r in chunks]
        shifts = [_dot(shift_mat, win) for win in wins]
        for r, win, shifted in zip(chunks, wins, shifts):
            rows = pl.ds(pl.multiple_of(r * CHUNK, CHUNK), CHUNK)
            acc = cb_all + win[HALO:HALO + CHUNK].astype(F32) * cw_all[half:half + 1]
            for pos, j in enumerate(side_taps):
                acc = acc + shifted[pos * CHUNK:(pos + 1) * CHUNK] * cw_all[j:j + 1]
            y = _silu(acc).astype(BF16)
            xc_ref[rows, :] = y[:, :width]
            bc_ref[rows, :] = y[:, width:width + n]
            cc_ref[rows, :] = y[:, width + n:]
        return carry

    lax.fori_loop(0, n_chunks // SSD_STEPS_PER_ITER, conv_body, 0)

    st_ref[...] = jnp.zeros_like(st_ref)

    def meta_window(mref, ref):
        zeros = jnp.zeros((HALO, mref.shape[1]), F32)
        return jnp.concatenate([zeros, mref[...].astype(F32), ref[0:HALO, :].astype(F32)], axis=0)

    def meta_mask(width_):
        return lax.broadcasted_iota(I32, (CHUNK, width_), 0) >= NPAD

    xc = jnp.where(meta_mask(width), _conv_silu(meta_window(xm_ref, x_ref), cwx, cbx), 0.0)
    bc = jnp.where(meta_mask(n), _conv_silu(meta_window(bm_ref, b_ref), cwb, cbb), 0.0)
    dt = jnp.where(meta_mask(LANES), _softplus(smm_ref[...].astype(F32) + dtb), 0.0)
    cs = _sel_dot_l(_twice_cols(cum2[:CHUNK, :CHUNK].astype(BF16)), dt * a_row)
    both = _sel_dot_r(jnp.concatenate([cs, dt], axis=0), ef)
    cs_e, dt_e = both[:CHUNK], both[CHUNK:]
    xend = (xc * dt_e * jnp.exp(cs_e[CHUNK - 1:CHUNK] - cs_e)).astype(BF16)
    st_ref[0:n, :] = _dot_tn(bc.astype(BF16), xend)

    zeros_n = jnp.zeros((CHUNK, n), BF16)

    def own_dir(a):
        return jnp.concatenate([jnp.concatenate([a[:CHUNK], zeros_n], axis=1),
                                jnp.concatenate([zeros_n, a[CHUNK:]], axis=1)], axis=0)

    def steps(j, finalize):
        ids = [j * SSD_STEPS_PER_ITER + s for s in range(SSD_STEPS_PER_ITER)]
        rfs = [pl.ds(pl.multiple_of(i * CHUNK, CHUNK), CHUNK) for i in ids]
        rbs = [pl.ds(pl.multiple_of((n_chunks - 1 - i) * CHUNK, CHUNK), CHUNK) for i in ids]
        both_rows = lambda ref, rf, rb: jnp.concatenate([ref[rf, :], ref[rb, :]], axis=0)
        dts = [_softplus(both_rows(sm_ref, rf, rb).astype(F32) + dtb) for rf, rb in zip(rfs, rbs)]
        css = [_sel_dot_l(tri2, dt * a_row) for dt in dts]
        exs = [(_sel_dot_r(jnp.concatenate([cs[:CHUNK], dt[:CHUNK]], axis=0), ef),
                _sel_dot_r(jnp.concatenate([cs[CHUNK:], dt[CHUNK:]], axis=0), eb)) for cs, dt in zip(css, dts)]
        xcs, cs_es, tots, xdts, xends, cbs, bcs, ccs = [], [], [], [], [], [], [], []
        for (ex_f, ex_b), rf, rb in zip(exs, rfs, rbs):
            cs_e = jnp.concatenate([ex_f[:CHUNK], ex_b[:CHUNK]], axis=0)
            dt_e = jnp.concatenate([ex_f[CHUNK:], ex_b[CHUNK:]], axis=0)
            tot_f, tot_b = cs_e[CHUNK - 1:CHUNK], cs_e[CHUNK:CHUNK + 1]
            tot = jnp.concatenate([jnp.broadcast_to(tot_f, (CHUNK, width)),
                                   jnp.broadcast_to(tot_b, (CHUNK, width))], axis=0)
            xc = both_rows(xc_ref, rf, rb).astype(F32)
            bc16, cc16 = both_rows(bc_ref, rf, rb), both_rows(cc_ref, rf, rb)
            xdt = xc * dt_e
            xcs.append(xc)
            cs_es.append(cs_e)
            tots.append((tot_f, tot_b))
            xdts.append(xdt.astype(BF16))
            xends.append((xdt * jnp.exp(tot - cs_e)).astype(BF16))
            bcs.append(bc16)
            ccs.append(cc16)
            cbs.append(_dot_nt(cc16, bc16))
        cb_reps = [_dot(jnp.where(same_dir, cb, 0.0).astype(BF16), irep2_16) for cb in cbs]
        cs_rows = [_sel_dot_l(ones2, jnp.where(irep2, cs_e, 0.0)) for cs_e in cs_es]
        intras = []
        for cb_rep, cs_row, cs_e, xdt16 in zip(cb_reps, cs_rows, cs_es, xdts):
            decay = jnp.where(pair_mask, jnp.exp(jnp.minimum(cs_e - cs_row, 0.0)), 0.0)
            w = (cb_rep * decay).astype(BF16)
            intras.append(jnp.concatenate(
                [_dot(w[d * CHUNK:(d + 1) * CHUNK],
                      jnp.where(bdmask, jnp.concatenate([xdt16[d * CHUNK:(d + 1) * CHUNK]] * gh, axis=0), zero16))
                 for d in range(2)], axis=0))
        upds = [_dot_tn(own_dir(bc16), xend) for bc16, xend in zip(bcs, xends)]
        st = st_ref[...]
        ys = []
        for intra, cc16, cs_e, (tot_f, tot_b), upd in zip(intras, ccs, cs_es, tots, upds):
            ys.append(intra + _dot(own_dir(cc16), st.astype(BF16)) * jnp.exp(cs_e))
            grow = jnp.concatenate([jnp.broadcast_to(jnp.exp(tot_f), (n, width)),
                                    jnp.broadcast_to(jnp.exp(tot_b), (n, width))], axis=0)
            st = st * grow + upd
        st_ref[...] = st
        for y, xc, rf, rb in zip(ys, xcs, rfs, rbs):
            for d, rows in enumerate((rf, rb)):
                yd = y[d * CHUNK:(d + 1) * CHUNK]
                if finalize:
                    yd = yd + o_ref[rows, :].astype(F32) + xc[d * CHUNK:(d + 1) * CHUNK] * dexp
                    yd = yd * _silu(z_ref[rows, :].astype(F32))
                    o_ref[rows, :] = _rms(yd, nw).astype(BF16)
                else:
                    o_ref[rows, :] = yd.astype(BF16)

    def first_half(j, carry):
        steps(j, False)
        return carry

    def second_half(j, carry):
        steps(j, True)
        return carry

    half_iters = n_chunks // 2 // SSD_STEPS_PER_ITER
    lax.fori_loop(0, half_iters, first_half, 0)
    lax.fori_loop(half_iters, 2 * half_iters, second_half, 0)


def _ssd(proj, proj_meta, conv_w, conv_b, dtb, alog, e_f, e_b, dexp, norm_w):
    bsz, seq, _ = proj.shape
    gc = SSD_GROUP_CH
    xb, zb = COL_X // gc, COL_Z // gc
    bb, cb, sb = COL_B // SSD_STATE, COL_C // SSD_STATE, COL_SMALL // LANES
    real = lambda width, base: pl.BlockSpec((None, seq, width), lambda b, g: (b, 0, base + g))
    meta = lambda width, base: pl.BlockSpec((CHUNK, width), lambda b, g: (0, base + g))
    cpar = lambda rows_, width, base: pl.BlockSpec((rows_, width), lambda b, g: (0, base + g))
    cbb_, ccb_ = SSD_INNER // SSD_STATE, SSD_INNER // SSD_STATE + SSD_GROUPS
    const = lambda shape: pl.BlockSpec(shape, lambda b, g: (0, 0))
    return pl.pallas_call(
        functools.partial(_ssd_body, n_chunks=seq // CHUNK),
        grid=(bsz, SSD_GROUPS),
        in_specs=[
            real(gc, xb), real(gc, zb), real(SSD_STATE, bb), real(SSD_STATE, cb),
            pl.BlockSpec((None, seq, LANES), lambda b, g: (b, 0, sb)),
            meta(gc, xb), meta(SSD_STATE, bb), meta(SSD_STATE, cb),
            pl.BlockSpec((CHUNK, LANES), lambda b, g: (0, sb)),
            cpar(SSD_CONV, gc, 0), cpar(1, gc, 0),
            cpar(SSD_CONV, SSD_STATE, cbb_), cpar(1, SSD_STATE, cbb_),
            cpar(SSD_CONV, SSD_STATE, ccb_), cpar(1, SSD_STATE, ccb_),
            const((1, LANES)), const((1, LANES)),
            pl.BlockSpec((None, 2 * LANES, gc), lambda b, g: (g, 0, 0)),
            pl.BlockSpec((None, 2 * LANES, gc), lambda b, g: (g, 0, 0)),
            pl.BlockSpec((1, gc), lambda b, g: (0, g)),
            pl.BlockSpec((1, gc), lambda b, g: (0, g)),
        ],
        out_specs=pl.BlockSpec((None, seq, gc), lambda b, g: (b, 0, g)),
        out_shape=jax.ShapeDtypeStruct((bsz, seq, SSD_INNER), BF16),
        scratch_shapes=[pltpu.VMEM((2 * SSD_STATE, gc), F32),
                        pltpu.VMEM((seq, gc), BF16), pltpu.VMEM((seq, SSD_STATE), BF16),
                        pltpu.VMEM((seq, SSD_STATE), BF16)],
        compiler_params=pltpu.CompilerParams(
            dimension_semantics=("arbitrary", "arbitrary"), vmem_limit_bytes=VMEM_LIMIT),
        name="ssd",
    )(proj, proj, proj, proj, proj, proj_meta, proj_meta, proj_meta, proj_meta,
      conv_w, conv_b, conv_w, conv_b, conv_w, conv_b, dtb, alog, e_f, e_b, dexp, norm_w)


def _outproj_body(x_ref, og_ref, ys_ref, wo1_ref, wo2_ref, n2w_ref, wrh_ref, wrl_ref, br_ref,
                  h2_ref, n2p_ref, code_ref, gate_ref, cnt_ref, carry_ref, *, tile):
    i = pl.program_id(0)

    @pl.when(i == 0)
    def _():
        carry_ref[...] = jnp.zeros_like(carry_ref)

    sub = tile // ROUTER_SUBTILES
    parts = [pl.ds(s * sub, sub) for s in range(ROUTER_SUBTILES)]
    wo1, wo2, n2w = wo1_ref[...], wo2_ref[...], n2w_ref[...]
    wrh, wrl, br = wrh_ref[...], wrl_ref[...], br_ref[...]
    h2s = [x_ref[p, :] + _dot(og_ref[p, :], wo1) + _dot(ys_ref[p, :], wo2) for p in parts]
    n2s = []
    for p, h2 in zip(parts, h2s):
        h2_ref[p, :] = h2
        n2 = _rms(h2, n2w)
        _store_rows(n2p_ref.at[p], _pack_bf16_pairs(n2))
        n2s.append(n2)
    logit_parts = []
    for n2 in n2s:
        nh, nl = _split(n2)
        logit_parts.append(_dot(nh, wrh) + _dot(nh, wrl) + _dot(nl, wrh) + br)

    lane = lax.broadcasted_iota(I32, (sub, LANES), 1)
    lane_f = lane.astype(F32)
    lane4 = lax.broadcasted_iota(I32, (sub, TOP_K), 1)
    per_row = LANES // TOP_K
    tok = lax.broadcasted_iota(I32, (sub, LANES), 0)
    here = (lane // TOP_K) == (tok % per_row)
    gather_rows = (lax.broadcasted_iota(I32, (sub // per_row, sub), 1) // per_row
                   == lax.broadcasted_iota(I32, (sub // per_row, sub), 0)).astype(BF16)
    before_me = _tri(sub, lower=True, inclusive=False).astype(BF16)

    routed = []
    for p, logits in zip(parts, logit_parts):
        vals, onehots, picks = [], [], []
        work = logits
        for k in range(TOP_K):
            m = jnp.max(work, axis=-1, keepdims=True)
            first = jnp.min(jnp.where(work == m, lane_f, float(LANES)), axis=-1, keepdims=True)
            oh = lane_f == first
            work = jnp.where(oh, -jnp.inf, work)
            vals.append(m)
            onehots.append(oh)
            picks.append(first)
        exps = [jnp.exp(v - vals[0]) for v in vals]
        inv = 1.0 / (exps[0] + exps[1] + exps[2] + exps[3])
        gate_out = jnp.zeros((sub, TOP_K), F32)
        for k in range(TOP_K):
            gate_out = jnp.where(lane4 == k, exps[k] * inv, gate_out)
        gate_ref[p, :] = gate_out
        any_oh = (onehots[0] | onehots[1] | onehots[2] | onehots[3])
        any16 = jnp.where(any_oh, 1.0, 0.0).astype(BF16)
        routed.append((onehots, picks, any16, _dot(before_me, any16)))

    carry = carry_ref[...]
    for s, (onehots, picks, any16, before) in enumerate(routed):
        before = before + carry
        carry = carry + jnp.sum(any16.astype(F32), axis=0, keepdims=True)
        rest = [jnp.sum(jnp.where(onehots[k], before, 0.0), axis=-1, keepdims=True) * N_EXPERTS + picks[k]
                for k in range(TOP_K)]
        flat = jnp.zeros((sub // per_row, LANES), F32)
        for scale in (65536.0, 256.0, 1.0):
            piece = [jnp.floor(c * (1.0 / scale)) for c in rest]
            rest = [c - q * scale for c, q in zip(rest, piece)]
            by_k = piece[TOP_K - 1]
            for k in range(TOP_K - 2, -1, -1):
                by_k = jnp.where(lane % TOP_K == k, piece[k], by_k)
            flat = flat + scale * _dot(gather_rows, jnp.where(here, by_k, 0.0).astype(BF16))
        code_ref[pl.ds(s * (sub // per_row), sub // per_row), :] = flat.astype(I32)
    carry_ref[...] = carry
    cnt_ref[...] = carry


def _outproj(x2d, o_gla, y_ssd, w_out1, w_out2, norm2_w, wr_hi, wr_lo, b_r, tile):
    rows = x2d.shape[0]
    row = lambda width: pl.BlockSpec((tile, width), lambda i: (i, 0))
    const = lambda shape: pl.BlockSpec(shape, lambda i: (0, 0))
    return pl.pallas_call(
        functools.partial(_outproj_body, tile=tile),
        grid=(rows // tile,),
        in_specs=[
            row(D_MODEL), row(GLA_DV), row(SSD_INNER),
            const((GLA_DV, D_MODEL)), const((SSD_INNER, D_MODEL)), const((1, D_MODEL)),
            const((D_MODEL, LANES)), const((D_MODEL, LANES)), const((1, LANES)),
        ],
        out_specs=[
            row(D_MODEL), pl.BlockSpec((tile, ROW_SUB, LANES), lambda i: (i, 0, 0)),
            pl.BlockSpec((tile * TOP_K // LANES, LANES), lambda i: (i, 0)), row(TOP_K), const((1, LANES)),
        ],
        out_shape=[
            jax.ShapeDtypeStruct((rows, D_MODEL), F32),
            jax.ShapeDtypeStruct((rows, ROW_SUB, LANES), U32),
            jax.ShapeDtypeStruct((rows * TOP_K // LANES, LANES), I32),
            jax.ShapeDtypeStruct((rows, TOP_K), F32),
            jax.ShapeDtypeStruct((1, LANES), F32),
        ],
        scratch_shapes=[pltpu.VMEM((1, LANES), F32)],
        compiler_params=pltpu.CompilerParams(
            dimension_semantics=("arbitrary",), vmem_limit_bytes=VMEM_LIMIT),
        name="outproj_router",
    )(x2d, o_gla, y_ssd, w_out1, w_out2, norm2_w, wr_hi, wr_lo, b_r)


def _row_copy(src, src_row, dst, dst_row, sem):
    return pltpu.make_async_copy(src.at[src_row], dst.at[dst_row], sem)


def _dispatch_body(dest_ref, n2p_ref, xs_hbm, sem, *, tile):
    def start(t, carry):
        for k in range(TOP_K):
            _row_copy(n2p_ref, t, xs_hbm, dest_ref[t * TOP_K + k], sem).start(priority=k % 2)
        return carry

    lax.fori_loop(0, tile, start, 0)

    for k in range(TOP_K):
        pltpu.make_async_copy(n2p_ref, xs_hbm.at[pl.ds(0, tile)], sem).wait()


def _dispatch(dest_flat, n2p, tile):
    rows = n2p.shape[0]
    return pl.pallas_call(
        functools.partial(_dispatch_body, tile=tile),
        grid=(rows // tile,),
        in_specs=[
            pl.BlockSpec((tile * TOP_K,), lambda i: (i,), memory_space=pltpu.SMEM),
            pl.BlockSpec((tile, ROW_SUB, LANES), lambda i: (i, 0, 0)),
        ],
        out_specs=pl.BlockSpec(memory_space=pl.ANY),
        out_shape=jax.ShapeDtypeStruct((rows * TOP_K, ROW_SUB, LANES), U32),
        scratch_shapes=[pltpu.SemaphoreType.DMA(())],
        compiler_params=pltpu.CompilerParams(dimension_semantics=("arbitrary",)),
        name="dispatch",
    )(dest_flat, n2p)


def _expert_body(blk_ref, exp_ref, lo_ref, hi_ref, first_ref, newexp_ref, nw_ref,
                 x_ref, wgu_ref, bgu_ref, wdn_ref, bdn_ref, o_ref, wgu16_ref, wdn16_ref, *, tile):
    w = pl.program_id(0)
    cast_rows = 64

    @pl.when((w < nw_ref[0]) & (newexp_ref[w] == 1))
    def _():
        def cast(i, carry):
            rows = pl.ds(pl.multiple_of(i * cast_rows, cast_rows), cast_rows)
            wgu16_ref[rows, :] = wgu_ref[rows, :].astype(BF16)
            wdn16_ref[rows, :] = wdn_ref[rows, :].astype(BF16)
            return carry

        lax.fori_loop(0, D_MODEL // cast_rows, cast, 0)

    @pl.when((w < nw_ref[0]) & (first_ref[w] == 1))
    def _():
        o_ref[...] = jnp.zeros_like(o_ref)

    @pl.when(w < nw_ref[0])
    def _():
        sub = tile // EXPERT_SUBTILES
        parts = [pl.ds(s * sub, sub) for s in range(EXPERT_SUBTILES)]
        xs = [_unpack_bf16_pairs(_load_rows(x_ref.at[p])).astype(BF16) for p in parts]
        gts = [jnp.minimum(_dot(x, wgu16_ref[:, :D_FF]) + bgu_ref[:, :D_FF], SWIGLU_LIMIT) for x in xs]
        ups = [jnp.clip(_dot(x, wgu16_ref[:, D_FF:]) + bgu_ref[:, D_FF:], -SWIGLU_LIMIT, SWIGLU_LIMIT)
               for x in xs]
        acts = [((up + 1.0) * gt * jax.nn.sigmoid(gt * SWIGLU_ALPHA)).astype(BF16) for gt, up in zip(gts, ups)]
        outs = [_pack_bf16_pairs(_dot(act, wdn16_ref[...]) + bdn_ref[...]) for act in acts]
        r = lax.broadcasted_iota(I32, (sub, LANES), 0)
        for s, (p, out) in enumerate(zip(parts, outs)):
            row = r + s * sub
            _store_rows(o_ref.at[p], out, keep=(row >= lo_ref[w]) & (row < hi_ref[w]))


def _experts(tables, xs, w_gu, b_gu, w_dn, b_dn, tile, n_work):
    rows = xs.shape[0]
    row_block = pl.BlockSpec((tile, ROW_SUB, LANES), lambda w, blk, ex, lo, hi, fi, ne, nw: (blk[w], 0, 0))
    grid_spec = pltpu.PrefetchScalarGridSpec(
        num_scalar_prefetch=7,
        grid=(n_work,),
        in_specs=[
            row_block,
            pl.BlockSpec((None, D_MODEL, 2 * D_FF), lambda w, blk, ex, lo, hi, fi, ne, nw: (ex[w], 0, 0)),
            pl.BlockSpec((None, 1, 2 * D_FF), lambda w, blk, ex, lo, hi, fi, ne, nw: (ex[w], 0, 0)),
            pl.BlockSpec((None, D_FF, D_MODEL), lambda w, blk, ex, lo, hi, fi, ne, nw: (ex[w], 0, 0)),
            pl.BlockSpec((None, 1, D_MODEL), lambda w, blk, ex, lo, hi, fi, ne, nw: (ex[w], 0, 0)),
        ],
        out_specs=row_block,
        scratch_shapes=[pltpu.VMEM((D_MODEL, 2 * D_FF), BF16), pltpu.VMEM((D_FF, D_MODEL), BF16)],
    )
    return pl.pallas_call(
        functools.partial(_expert_body, tile=tile),
        grid_spec=grid_spec,
        out_shape=jax.ShapeDtypeStruct((rows, ROW_SUB, LANES), U32),
        compiler_params=pltpu.CompilerParams(
            dimension_semantics=("arbitrary",), vmem_limit_bytes=VMEM_LIMIT),
        name="experts",
    )(*tables, xs, w_gu, b_gu, w_dn, b_dn)


def _combine_body(dest_ref, dest_next_ref, gate_ref, h2_ref, nfw_ref, ys_hbm, o_ref, buf, sems,
                  *, tile, n_steps):
    i = pl.program_id(0)
    slot = i % 2
    other = 1 - slot

    def row_copy(d_ref, s, t, k):
        return pltpu.make_async_copy(ys_hbm.at[d_ref[t * TOP_K + k]], buf.at[s, k, t], sems.at[s])

    def wait_slot(s):
        for k in range(TOP_K):
            pltpu.make_async_copy(ys_hbm.at[pl.ds(0, tile)], buf.at[s, k], sems.at[s]).wait()

    @pl.when(i == 0)
    def _():
        def start(t, carry):
            for k in range(TOP_K):
                row_copy(dest_ref, 0, t, k).start(priority=k % 2)
            return carry

        lax.fori_loop(0, tile, start, 0)

    wait_slot(slot)
    nfw = nfw_ref[...]
    for c in range(tile // COMBINE_CHUNK):
        rows = pl.ds(c * COMBINE_CHUNK, COMBINE_CHUNK)
        gate = gate_ref[rows, :]
        h3 = h2_ref[rows, :]
        packed = [_load_rows(buf.at[slot, k, rows]) for k in range(TOP_K)]
        for t in range(c * COMBINE_CHUNK, (c + 1) * COMBINE_CHUNK):
            for k in range(TOP_K):
                row_copy(dest_next_ref, other, t, k).start(priority=k % 2)
        for k in range(TOP_K):
            h3 = h3 + gate[:, k:k + 1] * _unpack_bf16_pairs(packed[k])
        o_ref[rows, :] = _rms(h3, nfw)

    @pl.when(i == n_steps - 1)
    def _():
        wait_slot(other)


def _combine(dest_flat, gates, h2, norm_f_w, ys, tile):
    rows = h2.shape[0]
    n_steps = rows // tile
    return pl.pallas_call(
        functools.partial(_combine_body, tile=tile, n_steps=n_steps),
        grid=(n_steps,),
        in_specs=[
            pl.BlockSpec((tile * TOP_K,), lambda i: (i,), memory_space=pltpu.SMEM),
            pl.BlockSpec((tile * TOP_K,), lambda i: (jnp.minimum(i + 1, n_steps - 1),),
                         memory_space=pltpu.SMEM),
            pl.BlockSpec((tile, TOP_K), lambda i: (i, 0)),
            pl.BlockSpec((tile, D_MODEL), lambda i: (i, 0)),
            pl.BlockSpec((1, D_MODEL), lambda i: (0, 0)),
            pl.BlockSpec(memory_space=pl.ANY),
        ],
        out_specs=pl.BlockSpec((tile, D_MODEL), lambda i: (i, 0)),
        out_shape=jax.ShapeDtypeStruct((rows, D_MODEL), F32),
        scratch_shapes=[pltpu.VMEM((2, TOP_K, tile, ROW_SUB, LANES), U32), pltpu.SemaphoreType.DMA((2,))],
        compiler_params=pltpu.CompilerParams(
            dimension_semantics=("arbitrary",), vmem_limit_bytes=VMEM_LIMIT),
        name="combine_final",
    )(dest_flat, dest_flat, gates, h2, norm_f_w, ys)


def _work_tables(counts, n_rows, tile):
    n_blocks = n_rows // tile
    n_work = n_blocks + N_EXPERTS
    ends = jnp.cumsum(counts)
    starts = ends - counts
    first_blk = starts // tile
    last_blk = (ends - 1) // tile
    nb = jnp.where(counts > 0, last_blk - first_blk + 1, 0)
    wend = jnp.cumsum(nb)
    wstart = wend - nb
    total = wend[-1]
    w = jnp.minimum(jnp.arange(n_work, dtype=I32), total - 1)
    ex = jnp.minimum(jnp.sum(wend[None, :] <= w[:, None], axis=1), N_EXPERTS - 1).astype(I32)
    mine = ex[:, None] == jnp.arange(N_EXPERTS, dtype=I32)[None, :]
    of_ex = lambda v: jnp.sum(jnp.where(mine, v[None, :], 0), axis=1)
    blk = (of_ex(first_blk - wstart) + w).astype(I32)
    lo = (jnp.maximum(of_ex(starts), blk * tile) - blk * tile).astype(I32)
    hi = (jnp.minimum(of_ex(ends), (blk + 1) * tile) - blk * tile).astype(I32)
    prev_blk = jnp.concatenate([jnp.full((1,), -1, I32), blk[:-1]])
    first = (blk != prev_blk).astype(I32)
    prev_ex = jnp.concatenate([jnp.full((1,), -1, I32), ex[:-1]])
    new_ex = (ex != prev_ex).astype(I32)
    return (blk, ex, lo, hi, first, new_ex, total.reshape(1).astype(I32)), starts, n_work


def kernel(x, meta, norm1_w, w_in, gla_wa2_f, gla_ba2_f, gla_wa2_b, gla_ba2_b, gla_norm_w, conv_w, conv_b, dt_bias_f, dt_bias_b, a_log_f, a_log_b, ssd_d, ssd_norm_w, w_out, norm2_w, w_router, b_router, w_gu, b_gu, w_dn, b_dn, norm_f_w):
    bsz, seq, d = x.shape
    n_tok = bsz * seq
    l = 0

    wi = w_in[l]
    a_cols = wi[:, 3072:3104]
    dt_cols = wi[:, 5664:5696]
    w_perm = jnp.concatenate(
        [wi[:, :3072], wi[:, 3104:5664], a_cols, dt_cols,
         jnp.zeros((d, N_PROJ - COL_SMALL - 64), F32)], axis=1).astype(BF16)

    def lane_rows(w, lane0):
        return jnp.zeros((LANES, w.shape[1]), F32).at[lane0:lane0 + w.shape[0]].set(w)

    wa_f = lane_rows(gla_wa2_f[l], LANE_AF).astype(BF16)
    wa_b = lane_rows(gla_wa2_b[l], LANE_AB).astype(BF16)
    ba_f = gla_ba2_f[l][None, :]
    ba_b = gla_ba2_b[l][None, :]

    def lane_vec(vf, vb):
        z = jnp.zeros((1, LANES), F32)
        return z.at[0, LANE_DTF:LANE_DTF + SSD_HEADS].set(vf).at[0, LANE_DTB:LANE_DTB + SSD_HEADS].set(vb)

    dtb = lane_vec(dt_bias_f[l], dt_bias_b[l])
    alog = lane_vec(a_log_f[l], a_log_b[l])
    lane_id = (jnp.arange(2 * LANES) % LANES)[None, :, None]
    head_id = (jnp.arange(SSD_GROUP_CH) // SSD_HEAD_DIM)[None, None, :]
    grp = jnp.arange(SSD_GROUPS)[:, None, None] * SSD_GROUP_HEADS
    e_f = (lane_id == LANE_DTF + grp + head_id).astype(BF16)
    e_b = (lane_id == LANE_DTB + grp + head_id).astype(BF16)
    dexp = jnp.repeat(ssd_d[l], SSD_HEAD_DIM)[None, :]

    wr = jnp.zeros((d, LANES), F32).at[:, :N_EXPERTS].set(w_router[l])
    wr_hi = wr.astype(BF16)
    wr_lo = (wr - wr_hi.astype(F32)).astype(BF16)
    b_r = jnp.full((1, LANES), -1e30, F32).at[0, :N_EXPERTS].set(b_router[l])

    x2d = x.reshape(n_tok, d)
    x_meta = jnp.pad(meta.astype(F32), ((NPAD, 0), (0, 0)))
    n1 = norm1_w[l][None, :]
    proj = _inproj(x2d, n1, w_perm, ROW_TILE).reshape(bsz, seq, N_PROJ)
    proj_meta = _inproj(x_meta, n1, w_perm, CHUNK)
    o_gla = _gla(proj, proj_meta, wa_f, ba_f, wa_b, ba_b, gla_norm_w[l][None, :])
    y_ssd = _ssd(proj, proj_meta, conv_w[l], conv_b[l][None, :], dtb, alog, e_f, e_b, dexp,
                 ssd_norm_w[l][None, :])

    wo = w_out[l].astype(BF16)
    h2, n2p, code, gates, cnt = _outproj(
        x2d, o_gla.reshape(n_tok, GLA_DV), y_ssd.reshape(n_tok, SSD_INNER),
        wo[:GLA_DV], wo[GLA_DV:], norm2_w[l][None, :], wr_hi, wr_lo, b_r, ROW_TILE)

    counts = cnt[0, :N_EXPERTS].astype(I32)
    tables, starts, n_work = _work_tables(counts, n_tok * TOP_K, EXPERT_TILE)
    expert_of = (code % N_EXPERTS)[..., None] == jnp.arange(N_EXPERTS, dtype=I32)
    dest = (code // N_EXPERTS + jnp.sum(jnp.where(expert_of, starts, 0), axis=-1)).astype(I32).reshape(-1)

    xs = _dispatch(dest, n2p, ROW_TILE)
    ys = _experts(tables, xs, w_gu[l], b_gu[l][:, None, :], w_dn[l], b_dn[l][:, None, :],
                  EXPERT_TILE, n_work)
    out = _combine(dest, gates, h2, norm_f_w[None, :], ys, COMBINE_TILE)
    return out.reshape(bsz, seq, d)
```

```python
import functools

import jax
import jax.numpy as jnp
from jax import lax
from jax.experimental import pallas as pl
from jax.experimental.pallas import tpu as pltpu

F32 = jnp.float32
BF16 = jnp.bfloat16
I32 = jnp.int32
U32 = jnp.uint32

D_MODEL = 1024
N_META = 16
CHUNK = 64
NPAD = CHUNK - N_META
GLA_HEADS = 4
GLA_DK = 512
GLA_DV = 1024
GLA_HEAD_K = GLA_DK // GLA_HEADS
GLA_HEAD_V = GLA_DV // GLA_HEADS
GLA_RANK = 16
GLA_GATE_NORM = 16.0
SSD_INNER = 1024
SSD_HEAD_DIM = 64
SSD_HEADS = SSD_INNER // SSD_HEAD_DIM
SSD_GROUPS = 2
SSD_GROUP_HEADS = SSD_HEADS // SSD_GROUPS
SSD_GROUP_CH = SSD_INNER // SSD_GROUPS
SSD_STATE = 128
SSD_CONV = 5
N_EXPERTS = 32
TOP_K = 4
D_FF = 1024
SWIGLU_LIMIT = 7.0
SWIGLU_ALPHA = 1.702
EPS = 1e-6

LANES = 128
ROW_WORDS = D_MODEL // 2
ROW_SUB = ROW_WORDS // LANES
HALO = 16

COL_Q = 0
COL_K = 512
COL_V = 1024
COL_G = 2048
COL_Z = 3072
COL_X = 4096
COL_B = 5120
COL_C = 5376
COL_SMALL = 5632
N_PROJ = 5760
LANE_AF = 0
LANE_AB = 16
LANE_DTF = 32
LANE_DTB = 48

ROW_TILE = 512
EXPERT_TILE = 512
EXPERT_SUBTILES = 2
ROUTER_SUBTILES = 2
COMBINE_TILE = 256
COMBINE_CHUNK = 16
VMEM_LIMIT = 56 * 1024 * 1024


def _dot(a, b):
    return jnp.dot(a, b, preferred_element_type=F32)


def _dot_nt(a, b):
    return lax.dot_general(a, b, (((1,), (1,)), ((), ())), preferred_element_type=F32)


def _dot_tn(a, b):
    return lax.dot_general(a, b, (((0,), (0,)), ((), ())), preferred_element_type=F32)


def _split(x):
    hi = x.astype(BF16)
    lo = (x - hi.astype(F32)).astype(BF16)
    return hi, lo


def _sel_dot_l(m01, x):
    hi, lo = _split(x)
    return _dot(m01, jnp.concatenate([hi, lo], axis=0))


def _sel_dot_r(x, m01):
    hi, lo = _split(x)
    return _dot(jnp.concatenate([hi, lo], axis=1), m01)


def _twice_cols(m):
    return jnp.concatenate([m, m], axis=1)


def _softplus(x):
    return jnp.maximum(x, 0.0) + jnp.log(1.0 + jnp.exp(-jnp.abs(x)))


def _log_sigmoid(x):
    return jnp.minimum(x, 0.0) - jnp.log(1.0 + jnp.exp(-jnp.abs(x)))


def _silu(x):
    return x * jax.nn.sigmoid(x)


def _rms(x, w):
    return x * lax.rsqrt(jnp.mean(x * x, axis=-1, keepdims=True) + EPS) * w


def _pack_bf16_pairs(x):
    w = x.shape[1] // 2
    return pltpu.pack_elementwise([x[:, :w], x[:, w:]], packed_dtype=BF16)


def _unpack_bf16_pairs(p):
    lo, hi = (pltpu.unpack_elementwise(p, index=i, packed_dtype=BF16, unpacked_dtype=F32) for i in range(2))
    return jnp.concatenate([lo, hi], axis=1)


def _load_rows(ref3):
    return jnp.concatenate([ref3[:, j, :] for j in range(ROW_SUB)], axis=1)


def _store_rows(ref3, val, keep=None):
    for j in range(ROW_SUB):
        piece = val[:, j * LANES:(j + 1) * LANES]
        if keep is not None:
            piece = jnp.where(keep, piece, ref3[:, j, :])
        ref3[:, j, :] = piece


def _tri(n, *, lower, inclusive):
    r = lax.broadcasted_iota(I32, (n, n), 0)
    c = lax.broadcasted_iota(I32, (n, n), 1)
    if lower:
        return (c <= r) if inclusive else (c < r)
    return (c >= r) if inclusive else (c > r)


def _inproj_body(x_ref, nw_ref, w_ref, o_ref, *, col_chunks):
    xn = _rms(x_ref[...], nw_ref[...]).astype(BF16)
    for lo, hi in col_chunks:
        o_ref[:, lo:hi] = _dot(xn, w_ref[:, lo:hi]).astype(BF16)


def _inproj(x2d, norm_w, w_perm, tile):
    rows = x2d.shape[0]
    col_chunks = tuple((c, min(c + 1024, N_PROJ)) for c in range(0, N_PROJ, 1024))
    return pl.pallas_call(
        functools.partial(_inproj_body, col_chunks=col_chunks),
        grid=(rows // tile,),
        in_specs=[
            pl.BlockSpec((tile, D_MODEL), lambda i: (i, 0)),
            pl.BlockSpec((1, D_MODEL), lambda i: (0, 0)),
            pl.BlockSpec((D_MODEL, N_PROJ), lambda i: (0, 0)),
        ],
        out_specs=pl.BlockSpec((tile, N_PROJ), lambda i: (i, 0)),
        out_shape=jax.ShapeDtypeStruct((rows, N_PROJ), BF16),
        compiler_params=pltpu.CompilerParams(
            dimension_semantics=("arbitrary",), vmem_limit_bytes=VMEM_LIMIT),
        name="inproj",
    )(x2d, norm_w, w_perm)


GLA_HEADS_PER_STEP = 2
GLA_STEPS_PER_ITER = 4
SSD_STEPS_PER_ITER = 4


def _gla_body(q_ref, k_ref, v_ref, g_ref, sm_ref, qm_ref, km_ref, vm_ref, smm_ref,
              waf_ref, baf_ref, wab_ref, bab_ref, nw_ref, o_ref, st_ref, *, n_chunks):
    heads = GLA_HEADS_PER_STEP
    nprob = 2 * heads
    rows_all = nprob * CHUNK
    wk = heads * GLA_HEAD_K
    dk, dv = GLA_HEAD_K, GLA_HEAD_V

    tri_f = _tri(CHUNK, lower=True, inclusive=True)
    r2 = lax.broadcasted_iota(I32, (2 * CHUNK, 2 * CHUNK), 0)
    c2 = lax.broadcasted_iota(I32, (2 * CHUNK, 2 * CHUNK), 1)
    cum2 = ((r2 < CHUNK) & (c2 <= r2)) | ((r2 >= CHUNK) & (c2 >= r2))
    tri2 = _twice_cols(cum2.astype(BF16))
    ra = lax.broadcasted_iota(I32, (rows_all, rows_all), 0)
    ca = lax.broadcasted_iota(I32, (rows_all, rows_all), 1)
    same = (ra // CHUNK) == (ca // CHUNK)
    att_mask = same & (((ra < heads * CHUNK) & (ca <= ra)) | ((ra >= heads * CHUNK) & (ca > ra)))
    rb_ = lax.broadcasted_iota(I32, (rows_all, nprob * dk), 0) // CHUNK
    cb_ = lax.broadcasted_iota(I32, (rows_all, nprob * dk), 1) // dk
    own = rb_ == cb_

    waf, baf = waf_ref[...], baf_ref[...]
    wab, bab = wab_ref[...], bab_ref[...]
    wa_cat = jnp.concatenate([waf, wab], axis=1)
    nw = nw_ref[...]
    zero16 = jnp.zeros((), BF16)

    st_ref[...] = jnp.zeros_like(st_ref)

    meta_rows = lax.broadcasted_iota(I32, (CHUNK, wk), 0) >= NPAD
    lg = _log_sigmoid(_dot(smm_ref[...], waf) + baf) * (1.0 / GLA_GATE_NORM)
    bm = _sel_dot_l(_twice_cols(tri_f.astype(BF16)), jnp.where(meta_rows, lg, 0.0))
    for h in range(heads):
        lanes = slice(h * dk, (h + 1) * dk)
        bh, toth = bm[:, lanes], bm[CHUNK - 1:CHUNK, lanes]
        kend = (km_ref[:, lanes].astype(F32) * jnp.exp(toth - bh)).astype(BF16)
        st_ref[:, lanes] = _dot_tn(vm_ref[:, h * dv:(h + 1) * dv], kend)

    def stack(f, b, w):
        return jnp.concatenate([f[:, h * w:(h + 1) * w] for h in range(heads)]
                               + [b[:, h * w:(h + 1) * w] for h in range(heads)], axis=0)

    def steps(j, finalize):
        ids = [j * GLA_STEPS_PER_ITER + s for s in range(GLA_STEPS_PER_ITER)]
        rfs = [pl.ds(pl.multiple_of(i * CHUNK, CHUNK), CHUNK) for i in ids]
        rbs = [pl.ds(pl.multiple_of((n_chunks - 1 - i) * CHUNK, CHUNK), CHUNK) for i in ids]
        zs = [_dot(jnp.concatenate([sm_ref[rf, :], sm_ref[rb, :]], axis=0), wa_cat) for rf, rb in zip(rfs, rbs)]
        b2s = []
        for z in zs:
            lg = jnp.concatenate([z[:CHUNK, :wk] + baf, z[CHUNK:, wk:] + bab], axis=0)
            b2s.append(_sel_dot_l(tri2, _log_sigmoid(lg) * (1.0 / GLA_GATE_NORM)))
        vss, qds, kends, tots, gs = [], [], [], [], []
        for b2, rf, rb in zip(b2s, rfs, rbs):
            tot_f, tot_b = b2[CHUNK - 1:CHUNK], b2[CHUNK:CHUNK + 1]
            bst = stack(b2[:CHUNK], b2[CHUNK:], dk)
            tst = stack(jnp.broadcast_to(tot_f, (CHUNK, wk)), jnp.broadcast_to(tot_b, (CHUNK, wk)), dk)
            qs = stack(q_ref[rf, :], q_ref[rb, :], dk).astype(F32)
            ks = stack(k_ref[rf, :], k_ref[rb, :], dk).astype(F32)
            qd = (qs * (GLA_HEAD_K ** -0.5) * jnp.exp(bst)).astype(BF16)
            kd = (ks * jnp.exp(-bst)).astype(BF16)
            vss.append(stack(v_ref[rf, :], v_ref[rb, :], dv))
            qds.append(qd)
            kends.append((ks * jnp.exp(tst - bst)).astype(BF16))
            tots.append(jnp.concatenate([tot_f, tot_b], axis=1))
            gs.append(_dot_nt(qd, kd))
        intras = [_dot(jnp.where(att_mask, g, 0.0).astype(BF16), vs) for g, vs in zip(gs, vss)]
        upds = [_dot_tn(vs, jnp.where(own, jnp.concatenate([kend] * nprob, axis=1), zero16))
                for vs, kend in zip(vss, kends)]
        st = st_ref[...]
        outs = []
        for qd, tot, upd, intra in zip(qds, tots, upds, intras):
            qd_own = jnp.where(own, jnp.concatenate([qd] * nprob, axis=1), zero16)
            outs.append(intra + _dot_nt(qd_own, st.astype(BF16)))
            st = st * jnp.exp(tot) + upd
        st_ref[...] = st
        for out, rf, rb in zip(outs, rfs, rbs):
            for p in range(nprob):
                rows, h = (rf if p < heads else rb), p % heads
                o = out[p * CHUNK:(p + 1) * CHUNK]
                cols = slice(h * dv, (h + 1) * dv)
                if finalize:
                    o = o + o_ref[rows, cols].astype(F32)
                    gate = _silu(g_ref[rows, cols].astype(F32))
                    o_ref[rows, cols] = (_rms(o, nw) * gate).astype(BF16)
                else:
                    o_ref[rows, cols] = o.astype(BF16)

    def first_half(j, carry):
        steps(j, False)
        return carry

    def second_half(j, carry):
        steps(j, True)
        return carry

    half_iters = n_chunks // 2 // GLA_STEPS_PER_ITER
    lax.fori_loop(0, half_iters, first_half, 0)
    lax.fori_loop(half_iters, 2 * half_iters, second_half, 0)


def _gla(proj, proj_meta, wa_f, ba_f, wa_b, ba_b, norm_w):
    bsz, seq, _ = proj.shape
    assert (seq // CHUNK) % (2 * GLA_STEPS_PER_ITER) == 0
    hp = GLA_HEADS_PER_STEP
    wk, wv = hp * GLA_HEAD_K, hp * GLA_HEAD_V
    kb, vb, gb, sb = COL_K // wk, COL_V // wv, COL_G // wv, COL_SMALL // LANES
    real = lambda width, base: pl.BlockSpec((None, seq, width), lambda b, h: (b, 0, base + h))
    meta = lambda width, base: pl.BlockSpec((CHUNK, width), lambda b, h: (0, base + h))
    per_step = lambda rows_: pl.BlockSpec((rows_, wk), lambda b, h: (0, h))
    return pl.pallas_call(
        functools.partial(_gla_body, n_chunks=seq // CHUNK),
        grid=(bsz, GLA_HEADS // hp),
        in_specs=[
            real(wk, 0), real(wk, kb), real(wv, vb), real(wv, gb),
            pl.BlockSpec((None, seq, LANES), lambda b, h: (b, 0, sb)),
            meta(wk, 0), meta(wk, kb), meta(wv, vb),
            pl.BlockSpec((CHUNK, LANES), lambda b, h: (0, sb)),
            per_step(LANES), per_step(1), per_step(LANES), per_step(1),
            pl.BlockSpec((1, GLA_HEAD_V), lambda b, h: (0, 0)),
        ],
        out_specs=pl.BlockSpec((None, seq, wv), lambda b, h: (b, 0, h)),
        out_shape=jax.ShapeDtypeStruct((bsz, seq, GLA_DV), BF16),
        scratch_shapes=[pltpu.VMEM((GLA_HEAD_V, 2 * hp * GLA_HEAD_K), F32)],
        compiler_params=pltpu.CompilerParams(
            dimension_semantics=("arbitrary", "arbitrary"), vmem_limit_bytes=VMEM_LIMIT),
        name="gla",
    )(proj, proj, proj, proj, proj, proj_meta, proj_meta, proj_meta, proj_meta,
      wa_f, ba_f, wa_b, ba_b, norm_w)


def _conv_silu(win, cw, cb):
    half = (SSD_CONV - 1) // 2
    acc = cb
    for j in range(SSD_CONV):
        lo = HALO - half + j
        acc = acc + win[lo:lo + CHUNK, :] * cw[j:j + 1, :]
    return _silu(acc)


def _ssd_body(x_ref, z_ref, b_ref, c_ref, sm_ref, xm_ref, bm_ref, cm_ref, smm_ref,
              cwx_ref, cbx_ref, cwb_ref, cbb_ref, cwc_ref, cbc_ref, dtb_ref, alog_ref,
              ef_ref, eb_ref, dexp_ref, nw_ref, o_ref, st_ref, xc_ref, bc_ref, cc_ref,
              *, n_chunks):
    gh, p, n = SSD_GROUP_HEADS, SSD_HEAD_DIM, SSD_STATE
    width = gh * p
    two = 2 * CHUNK
    lane_s = lax.broadcasted_iota(I32, (two, width), 1) % p
    row2 = lax.broadcasted_iota(I32, (two, width), 0)
    row_t = row2 % CHUNK
    irep2 = lane_s == row_t
    irep2_16 = irep2.astype(BF16)
    pair_mask = ((row2 < CHUNK) & (lane_s <= row_t)) | ((row2 >= CHUNK) & (lane_s > row_t))
    r2 = lax.broadcasted_iota(I32, (two, two), 0)
    c2 = lax.broadcasted_iota(I32, (two, two), 1)
    same_dir = (r2 < CHUNK) == (c2 < CHUNK)
    ones2 = _twice_cols(same_dir.astype(BF16))
    cum2 = ((r2 < CHUNK) & (c2 <= r2)) | ((r2 >= CHUNK) & (c2 >= r2))
    tri2 = _twice_cols(cum2.astype(BF16))
    bd_r = lax.broadcasted_iota(I32, (width, width), 0) // p
    bd_c = lax.broadcasted_iota(I32, (width, width), 1) // p
    bdmask = bd_r == bd_c
    zero16 = jnp.zeros((), BF16)

    cwx, cbx = cwx_ref[...], cbx_ref[...]
    cwb, cbb = cwb_ref[...], cbb_ref[...]
    cwc, cbc = cwc_ref[...], cbc_ref[...]
    dtb = dtb_ref[...]
    a_row = -jnp.exp(alog_ref[...])
    ef, eb = ef_ref[...], eb_ref[...]
    dexp, nw = dexp_ref[...], nw_ref[...]
    seq = n_chunks * CHUNK

    def window(ref, mref, r):
        off = pl.multiple_of(r * CHUNK, CHUNK)
        poff = pl.multiple_of(jnp.maximum(off - HALO, 0), HALO)
        noff = pl.multiple_of(jnp.minimum(off + CHUNK, seq - HALO), HALO)
        prev = jnp.where(r == 0, mref[CHUNK - HALO:, :], ref[pl.ds(poff, HALO), :])
        nxt = jnp.where(r == n_chunks - 1, zero16, ref[pl.ds(noff, HALO), :])
        return jnp.concatenate([prev, ref[pl.ds(off, CHUNK), :], nxt], axis=0)

    half = (SSD_CONV - 1) // 2
    side_taps = [j for j in range(SSD_CONV) if j != half]
    win_rows = CHUNK + 2 * HALO
    sr = lax.broadcasted_iota(I32, (len(side_taps) * CHUNK, win_rows), 0)
    sc = lax.broadcasted_iota(I32, (len(side_taps) * CHUNK, win_rows), 1)
    tap_of = sr // CHUNK
    tap_shift = jnp.where(tap_of < half, tap_of, tap_of + 1) - half
    shift_mat = (sc == (sr % CHUNK) + HALO + tap_shift).astype(BF16)
    cw_all = jnp.concatenate([cwx, cwb, cwc], axis=1)
    cb_all = jnp.concatenate([cbx, cbb, cbc], axis=1)

    def conv_body(it, carry):
        chunks = [it * SSD_STEPS_PER_ITER + s for s in range(SSD_STEPS_PER_ITER)]
        wins = [jnp.concatenate([window(x_ref, xm_ref, r), window(b_ref, bm_ref, r), window(c_ref, cm_ref, r)],
                                axis=1) for r in chunks]
        shifts = [_dot(shift_mat, win) for win in wins]
        for r, win, shifted in zip(chunks, wins, shifts):
            rows = pl.ds(pl.multiple_of(r * CHUNK, CHUNK), CHUNK)
            acc = cb_all + win[HALO:HALO + CHUNK].astype(F32) * cw_all[half:half + 1]
            for pos, j in enumerate(side_taps):
                acc = acc + shifted[pos * CHUNK:(pos + 1) * CHUNK] * cw_all[j:j + 1]
            y = _silu(acc).astype(BF16)
            xc_ref[rows, :] = y[:, :width]
            bc_ref[rows, :] = y[:, width:width + n]
            cc_ref[rows, :] = y[:, width + n:]
        return carry

    lax.fori_loop(0, n_chunks // SSD_STEPS_PER_ITER, conv_body, 0)

    st_ref[...] = jnp.zeros_like(st_ref)

    def meta_window(mref, ref):
        zeros = jnp.zeros((HALO, mref.shape[1]), F32)
        return jnp.concatenate([zeros, mref[...].astype(F32), ref[0:HALO, :].astype(F32)], axis=0)

    def meta_mask(width_):
        return lax.broadcasted_iota(I32, (CHUNK, width_), 0) >= NPAD

    xc = jnp.where(meta_mask(width), _conv_silu(meta_window(xm_ref, x_ref), cwx, cbx), 0.0)
    bc = jnp.where(meta_mask(n), _conv_silu(meta_window(bm_ref, b_ref), cwb, cbb), 0.0)
    dt = jnp.where(meta_mask(LANES), _softplus(smm_ref[...].astype(F32) + dtb), 0.0)
    cs = _sel_dot_l(_twice_cols(cum2[:CHUNK, :CHUNK].astype(BF16)), dt * a_row)
    both = _sel_dot_r(jnp.concatenate([cs, dt], axis=0), ef)
    cs_e, dt_e = both[:CHUNK], both[CHUNK:]
    xend = (xc * dt_e * jnp.exp(cs_e[CHUNK - 1:CHUNK] - cs_e)).astype(BF16)
    st_ref[0:n, :] = _dot_tn(bc.astype(BF16), xend)

    zeros_n = jnp.zeros((CHUNK, n), BF16)

    def own_dir(a):
        return jnp.concatenate([jnp.concatenate([a[:CHUNK], zeros_n], axis=1),
                                jnp.concatenate([zeros_n, a[CHUNK:]], axis=1)], axis=0)

    def steps(j, finalize):
        ids = [j * SSD_STEPS_PER_ITER + s for s in range(SSD_STEPS_PER_ITER)]
        rfs = [pl.ds(pl.multiple_of(i * CHUNK, CHUNK), CHUNK) for i in ids]
        rbs = [pl.ds(pl.multiple_of((n_chunks - 1 - i) * CHUNK, CHUNK), CHUNK) for i in ids]
        both_rows = lambda ref, rf, rb: jnp.concatenate([ref[rf, :], ref[rb, :]], axis=0)
        dts = [_softplus(both_rows(sm_ref, rf, rb).astype(F32) + dtb) for rf, rb in zip(rfs, rbs)]
        css = [_sel_dot_l(tri2, dt * a_row) for dt in dts]
        exs = [(_sel_dot_r(jnp.concatenate([cs[:CHUNK], dt[:CHUNK]], axis=0), ef),
                _sel_dot_r(jnp.concatenate([cs[CHUNK:], dt[CHUNK:]], axis=0), eb)) for cs, dt in zip(css, dts)]
        xcs, cs_es, tots, xdts, xends, cbs, bcs, ccs = [], [], [], [], [], [], [], []
        for (ex_f, ex_b), rf, rb in zip(exs, rfs, rbs):
            cs_e = jnp.concatenate([ex_f[:CHUNK], ex_b[:CHUNK]], axis=0)
            dt_e = jnp.concatenate([ex_f[CHUNK:], ex_b[CHUNK:]], axis=0)
            tot_f, tot_b = cs_e[CHUNK - 1:CHUNK], cs_e[CHUNK:CHUNK + 1]
            tot = jnp.concatenate([jnp.broadcast_to(tot_f, (CHUNK, width)),
                                   jnp.broadcast_to(tot_b, (CHUNK, width))], axis=0)
            xc = both_rows(xc_ref, rf, rb).astype(F32)
            bc16, cc16 = both_rows(bc_ref, rf, rb), both_rows(cc_ref, rf, rb)
            xdt = xc * dt_e
            xcs.append(xc)
            cs_es.append(cs_e)
            tots.append((tot_f, tot_b))
            xdts.append(xdt.astype(BF16))
            xends.append((xdt * jnp.exp(tot - cs_e)).astype(BF16))
            bcs.append(bc16)
            ccs.append(cc16)
            cbs.append(_dot_nt(cc16, bc16))
        cb_reps = [_dot(jnp.where(same_dir, cb, 0.0).astype(BF16), irep2_16) for cb in cbs]
        cs_rows = [_sel_dot_l(ones2, jnp.where(irep2, cs_e, 0.0)) for cs_e in cs_es]
        intras = []
        for cb_rep, cs_row, cs_e, xdt16 in zip(cb_reps, cs_rows, cs_es, xdts):
            decay = jnp.where(pair_mask, jnp.exp(jnp.minimum(cs_e - cs_row, 0.0)), 0.0)
            w = (cb_rep * decay).astype(BF16)
            intras.append(jnp.concatenate(
                [_dot(w[d * CHUNK:(d + 1) * CHUNK],
                      jnp.where(bdmask, jnp.concatenate([xdt16[d * CHUNK:(d + 1) * CHUNK]] * gh, axis=0), zero16))
                 for d in range(2)], axis=0))
        upds = [_dot_tn(own_dir(bc16), xend) for bc16, xend in zip(bcs, xends)]
        st = st_ref[...]
        ys = []
        for intra, cc16, cs_e, (tot_f, tot_b), upd in zip(intras, ccs, cs_es, tots, upds):
            ys.append(intra + _dot(own_dir(cc16), st.astype(BF16)) * jnp.exp(cs_e))
            grow = jnp.concatenate([jnp.broadcast_to(jnp.exp(tot_f), (n, width)),
                                    jnp.broadcast_to(jnp.exp(tot_b), (n, width))], axis=0)
            st = st * grow + upd
        st_ref[...] = st
        for y, xc, rf, rb in zip(ys, xcs, rfs, rbs):
            for d, rows in enumerate((rf, rb)):
                yd = y[d * CHUNK:(d + 1) * CHUNK]
                if finalize:
                    yd = yd + o_ref[rows, :].astype(F32) + xc[d * CHUNK:(d + 1) * CHUNK] * dexp
                    yd = yd * _silu(z_ref[rows, :].astype(F32))
                    o_ref[rows, :] = _rms(yd, nw).astype(BF16)
                else:
                    o_ref[rows, :] = yd.astype(BF16)

    def first_half(j, carry):
        steps(j, False)
        return carry

    def second_half(j, carry):
        steps(j, True)
        return carry

    half_iters = n_chunks // 2 // SSD_STEPS_PER_ITER
    lax.fori_loop(0, half_iters, first_half, 0)
    lax.fori_loop(half_iters, 2 * half_iters, second_half, 0)


def _ssd(proj, proj_meta, conv_w, conv_b, dtb, alog, e_f, e_b, dexp, norm_w):
    bsz, seq, _ = proj.shape
    gc = SSD_GROUP_CH
    xb, zb = COL_X // gc, COL_Z // gc
    bb, cb, sb = COL_B // SSD_STATE, COL_C // SSD_STATE, COL_SMALL // LANES
    real = lambda width, base: pl.BlockSpec((None, seq, width), lambda b, g: (b, 0, base + g))
    meta = lambda width, base: pl.BlockSpec((CHUNK, width), lambda b, g: (0, base + g))
    cpar = lambda rows_, width, base: pl.BlockSpec((rows_, width), lambda b, g: (0, base + g))
    cbb_, ccb_ = SSD_INNER // SSD_STATE, SSD_INNER // SSD_STATE + SSD_GROUPS
    const = lambda shape: pl.BlockSpec(shape, lambda b, g: (0, 0))
    return pl.pallas_call(
        functools.partial(_ssd_body, n_chunks=seq // CHUNK),
        grid=(bsz, SSD_GROUPS),
        in_specs=[
            real(gc, xb), real(gc, zb), real(SSD_STATE, bb), real(SSD_STATE, cb),
            pl.BlockSpec((None, seq, LANES), lambda b, g: (b, 0, sb)),
            meta(gc, xb), meta(SSD_STATE, bb), meta(SSD_STATE, cb),
            pl.BlockSpec((CHUNK, LANES), lambda b, g: (0, sb)),
            cpar(SSD_CONV, gc, 0), cpar(1, gc, 0),
            cpar(SSD_CONV, SSD_STATE, cbb_), cpar(1, SSD_STATE, cbb_),
            cpar(SSD_CONV, SSD_STATE, ccb_), cpar(1, SSD_STATE, ccb_),
            const((1, LANES)), const((1, LANES)),
            pl.BlockSpec((None, 2 * LANES, gc), lambda b, g: (g, 0, 0)),
            pl.BlockSpec((None, 2 * LANES, gc), lambda b, g: (g, 0, 0)),
            pl.BlockSpec((1, gc), lambda b, g: (0, g)),
            pl.BlockSpec((1, gc), lambda b, g: (0, g)),
        ],
        out_specs=pl.BlockSpec((None, seq, gc), lambda b, g: (b, 0, g)),
        out_shape=jax.ShapeDtypeStruct((bsz, seq, SSD_INNER), BF16),
        scratch_shapes=[pltpu.VMEM((2 * SSD_STATE, gc), F32),
                        pltpu.VMEM((seq, gc), BF16), pltpu.VMEM((seq, SSD_STATE), BF16),
                        pltpu.VMEM((seq, SSD_STATE), BF16)],
        compiler_params=pltpu.CompilerParams(
            dimension_semantics=("arbitrary", "arbitrary"), vmem_limit_bytes=VMEM_LIMIT),
        name="ssd",
    )(proj, proj, proj, proj, proj, proj_meta, proj_meta, proj_meta, proj_meta,
      conv_w, conv_b, conv_w, conv_b, conv_w, conv_b, dtb, alog, e_f, e_b, dexp, norm_w)


def _outproj_body(x_ref, og_ref, ys_ref, wo1_ref, wo2_ref, n2w_ref, wrh_ref, wrl_ref, br_ref,
                  h2_ref, n2p_ref, code_ref, gate_ref, cnt_ref, carry_ref, *, tile):
    i = pl.program_id(0)

    @pl.when(i == 0)
    def _():
        carry_ref[...] = jnp.zeros_like(carry_ref)

    sub = tile // ROUTER_SUBTILES
    parts = [pl.ds(s * sub, sub) for s in range(ROUTER_SUBTILES)]
    wo1, wo2, n2w = wo1_ref[...], wo2_ref[...], n2w_ref[...]
    wrh, wrl, br = wrh_ref[...], wrl_ref[...], br_ref[...]
    h2s = [x_ref[p, :] + _dot(og_ref[p, :], wo1) + _dot(ys_ref[p, :], wo2) for p in parts]
    n2s = []
    for p, h2 in zip(parts, h2s):
        h2_ref[p, :] = h2
        n2 = _rms(h2, n2w)
        _store_rows(n2p_ref.at[p], _pack_bf16_pairs(n2))
        n2s.append(n2)
    logit_parts = []
    for n2 in n2s:
        nh, nl = _split(n2)
        logit_parts.append(_dot(nh, wrh) + _dot(nh, wrl) + _dot(nl, wrh) + br)

    lane = lax.broadcasted_iota(I32, (sub, LANES), 1)
    lane_f = lane.astype(F32)
    lane4 = lax.broadcasted_iota(I32, (sub, TOP_K), 1)
    per_row = LANES // TOP_K
    tok = lax.broadcasted_iota(I32, (sub, LANES), 0)
    here = (lane // TOP_K) == (tok % per_row)
    gather_rows = (lax.broadcasted_iota(I32, (sub // per_row, sub), 1) // per_row
                   == lax.broadcasted_iota(I32, (sub // per_row, sub), 0)).astype(BF16)
    before_me = _tri(sub, lower=True, inclusive=False).astype(BF16)

    routed = []
    for p, logits in zip(parts, logit_parts):
        vals, onehots, picks = [], [], []
        work = logits
        for k in range(TOP_K):
            m = jnp.max(work, axis=-1, keepdims=True)
            first = jnp.min(jnp.where(work == m, lane_f, float(LANES)), axis=-1, keepdims=True)
            oh = lane_f == first
            work = jnp.where(oh, -jnp.inf, work)
            vals.append(m)
            onehots.append(oh)
            picks.append(first)
        exps = [jnp.exp(v - vals[0]) for v in vals]
        inv = 1.0 / (exps[0] + exps[1] + exps[2] + exps[3])
        gate_out = jnp.zeros((sub, TOP_K), F32)
        for k in range(TOP_K):
            gate_out = jnp.where(lane4 == k, exps[k] * inv, gate_out)
        gate_ref[p, :] = gate_out
        any_oh = (onehots[0] | onehots[1] | onehots[2] | onehots[3])
        any16 = jnp.where(any_oh, 1.0, 0.0).astype(BF16)
        routed.append((onehots, picks, any16, _dot(before_me, any16)))

    carry = carry_ref[...]
    for s, (onehots, picks, any16, before) in enumerate(routed):
        before = before + carry
        carry = carry + jnp.sum(any16.astype(F32), axis=0, keepdims=True)
        rest = [jnp.sum(jnp.where(onehots[k], before, 0.0), axis=-1, keepdims=True) * N_EXPERTS + picks[k]
                for k in range(TOP_K)]
        flat = jnp.zeros((sub // per_row, LANES), F32)
        for scale in (65536.0, 256.0, 1.0):
            piece = [jnp.floor(c * (1.0 / scale)) for c in rest]
            rest = [c - q * scale for c, q in zip(rest, piece)]
            by_k = piece[TOP_K - 1]
            for k in range(TOP_K - 2, -1, -1):
                by_k = jnp.where(lane % TOP_K == k, piece[k], by_k)
            flat = flat + scale * _dot(gather_rows, jnp.where(here, by_k, 0.0).astype(BF16))
        code_ref[pl.ds(s * (sub // per_row), sub // per_row), :] = flat.astype(I32)
    carry_ref[...] = carry
    cnt_ref[...] = carry


def _outproj(x2d, o_gla, y_ssd, w_out1, w_out2, norm2_w, wr_hi, wr_lo, b_r, tile):
    rows = x2d.shape[0]
    row = lambda width: pl.BlockSpec((tile, width), lambda i: (i, 0))
    const = lambda shape: pl.BlockSpec(shape, lambda i: (0, 0))
    return pl.pallas_call(
        functools.partial(_outproj_body, tile=tile),
        grid=(rows // tile,),
        in_specs=[
            row(D_MODEL), row(GLA_DV), row(SSD_INNER),
            const((GLA_DV, D_MODEL)), const((SSD_INNER, D_MODEL)), const((1, D_MODEL)),
            const((D_MODEL, LANES)), const((D_MODEL, LANES)), const((1, LANES)),
        ],
        out_specs=[
            row(D_MODEL), pl.BlockSpec((tile, ROW_SUB, LANES), lambda i: (i, 0, 0)),
            pl.BlockSpec((tile * TOP_K // LANES, LANES), lambda i: (i, 0)), row(TOP_K), const((1, LANES)),
        ],
        out_shape=[
            jax.ShapeDtypeStruct((rows, D_MODEL), F32),
            jax.ShapeDtypeStruct((rows, ROW_SUB, LANES), U32),
            jax.ShapeDtypeStruct((rows * TOP_K // LANES, LANES), I32),
            jax.ShapeDtypeStruct((rows, TOP_K), F32),
            jax.ShapeDtypeStruct((1, LANES), F32),
        ],
        scratch_shapes=[pltpu.VMEM((1, LANES), F32)],
        compiler_params=pltpu.CompilerParams(
            dimension_semantics=("arbitrary",), vmem_limit_bytes=VMEM_LIMIT),
        name="outproj_router",
    )(x2d, o_gla, y_ssd, w_out1, w_out2, norm2_w, wr_hi, wr_lo, b_r)


def _row_copy(src, src_row, dst, dst_row, sem):
    return pltpu.make_async_copy(src.at[src_row], dst.at[dst_row], sem)


def _dispatch_body(fill_start_ref, fill_len_ref, dest_ref, n2p_ref, xs_hbm, zeros_ref, sem, fill_sem, *, tile):
    @pl.when(pl.program_id(0) == 0)
    def _():
        zeros_ref[...] = jnp.zeros_like(zeros_ref)

        def fill(e, size, wait):
            length = fill_len_ref[e]

            @pl.when((length & size) != 0)
            def _():
                pos = fill_start_ref[e] + (length & (size - 1))
                copy = pltpu.make_async_copy(zeros_ref.at[pl.ds(0, size)], xs_hbm.at[pl.ds(pos, size)], fill_sem)
                if wait:
                    copy.wait()
                else:
                    copy.start()

        for wait in (False, True):
            for e in range(N_EXPERTS):
                size = 1
                while size < tile:
                    fill(e, size, wait)
                    size *= 2

    def start(t, carry):
        for k in range(TOP_K):
            _row_copy(n2p_ref, t, xs_hbm, dest_ref[t * TOP_K + k], sem).start(priority=k % 2)
        return carry

    lax.fori_loop(0, tile, start, 0)

    for k in range(TOP_K):
        pltpu.make_async_copy(n2p_ref, xs_hbm.at[pl.ds(0, tile)], sem).wait()


def _dispatch(fill_start, fill_len, dest_flat, n2p, tile, out_rows):
    rows = n2p.shape[0]
    grid_spec = pltpu.PrefetchScalarGridSpec(
        num_scalar_prefetch=2,
        grid=(rows // tile,),
        in_specs=[
            pl.BlockSpec((tile * TOP_K,), lambda i, fs, fl: (i,), memory_space=pltpu.SMEM),
            pl.BlockSpec((tile, ROW_SUB, LANES), lambda i, fs, fl: (i, 0, 0)),
        ],
        out_specs=pl.BlockSpec(memory_space=pl.ANY),
        scratch_shapes=[pltpu.VMEM((tile // 2, ROW_SUB, LANES), U32),
                        pltpu.SemaphoreType.DMA(()), pltpu.SemaphoreType.DMA(())],
    )
    return pl.pallas_call(
        functools.partial(_dispatch_body, tile=tile),
        grid_spec=grid_spec,
        out_shape=jax.ShapeDtypeStruct((out_rows, ROW_SUB, LANES), U32),
        compiler_params=pltpu.CompilerParams(dimension_semantics=("arbitrary",)),
        name="dispatch",
    )(fill_start, fill_len, dest_flat, n2p)


def _expert_body(blk_ref, exp_ref, newexp_ref, nw_ref,
                 x_ref, wgu_ref, bgu_ref, wdn_ref, bdn_ref, o_ref, wgu16_ref, wdn16_ref, *, tile):
    w = pl.program_id(0)
    cast_rows = 64

    @pl.when((w < nw_ref[0]) & (newexp_ref[w] == 1))
    def _():
        def cast(i, carry):
            rows = pl.ds(pl.multiple_of(i * cast_rows, cast_rows), cast_rows)
            wgu16_ref[rows, :] = wgu_ref[rows, :].astype(BF16)
            wdn16_ref[rows, :] = wdn_ref[rows, :].astype(BF16)
            return carry

        lax.fori_loop(0, D_MODEL // cast_rows, cast, 0)

    @pl.when(w < nw_ref[0])
    def _():
        sub = tile // EXPERT_SUBTILES
        parts = [pl.ds(s * sub, sub) for s in range(EXPERT_SUBTILES)]
        xs = [_unpack_bf16_pairs(_load_rows(x_ref.at[p])).astype(BF16) for p in parts]
        gts = [jnp.minimum(_dot(x, wgu16_ref[:, :D_FF]) + bgu_ref[:, :D_FF], SWIGLU_LIMIT) for x in xs]
        ups = [jnp.clip(_dot(x, wgu16_ref[:, D_FF:]) + bgu_ref[:, D_FF:], -SWIGLU_LIMIT, SWIGLU_LIMIT)
               for x in xs]
        acts = [((up + 1.0) * gt * jax.nn.sigmoid(gt * SWIGLU_ALPHA)).astype(BF16) for gt, up in zip(gts, ups)]
        for p, act in zip(parts, acts):
            _store_rows(o_ref.at[p], _pack_bf16_pairs(_dot(act, wdn16_ref[...]) + bdn_ref[...]))


def _experts(tables, xs, w_gu, b_gu, w_dn, b_dn, tile, n_work):
    rows = xs.shape[0]
    row_block = pl.BlockSpec((tile, ROW_SUB, LANES), lambda w, blk, ex, ne, nw: (blk[w], 0, 0))
    grid_spec = pltpu.PrefetchScalarGridSpec(
        num_scalar_prefetch=4,
        grid=(n_work,),
        in_specs=[
            row_block,
            pl.BlockSpec((None, D_MODEL, 2 * D_FF), lambda w, blk, ex, ne, nw: (ex[w], 0, 0)),
            pl.BlockSpec((None, 1, 2 * D_FF), lambda w, blk, ex, ne, nw: (ex[w], 0, 0)),
            pl.BlockSpec((None, D_FF, D_MODEL), lambda w, blk, ex, ne, nw: (ex[w], 0, 0)),
            pl.BlockSpec((None, 1, D_MODEL), lambda w, blk, ex, ne, nw: (ex[w], 0, 0)),
        ],
        out_specs=row_block,
        scratch_shapes=[pltpu.VMEM((D_MODEL, 2 * D_FF), BF16), pltpu.VMEM((D_FF, D_MODEL), BF16)],
    )
    return pl.pallas_call(
        functools.partial(_expert_body, tile=tile),
        grid_spec=grid_spec,
        out_shape=jax.ShapeDtypeStruct((rows, ROW_SUB, LANES), U32),
        compiler_params=pltpu.CompilerParams(
            dimension_semantics=("arbitrary",), vmem_limit_bytes=VMEM_LIMIT),
        name="experts",
    )(*tables, xs, w_gu, b_gu, w_dn, b_dn)


def _combine_body(dest_ref, dest_next_ref, gate_ref, h2_ref, nfw_ref, ys_hbm, o_ref, buf, sems,
                  *, tile, n_steps):
    i = pl.program_id(0)
    slot = i % 2
    other = 1 - slot

    def row_copy(d_ref, s, t, k):
        return pltpu.make_async_copy(ys_hbm.at[d_ref[t * TOP_K + k]], buf.at[s, k, t], sems.at[s])

    def wait_slot(s):
        for k in range(TOP_K):
            pltpu.make_async_copy(ys_hbm.at[pl.ds(0, tile)], buf.at[s, k], sems.at[s]).wait()

    @pl.when(i == 0)
    def _():
        def start(t, carry):
            for k in range(TOP_K):
                row_copy(dest_ref, 0, t, k).start(priority=k % 2)
            return carry

        lax.fori_loop(0, tile, start, 0)

    wait_slot(slot)
    nfw = nfw_ref[...]
    for c in range(tile // COMBINE_CHUNK):
        rows = pl.ds(c * COMBINE_CHUNK, COMBINE_CHUNK)
        gate = gate_ref[rows, :]
        h3 = h2_ref[rows, :]
        packed = [_load_rows(buf.at[slot, k, rows]) for k in range(TOP_K)]
        for t in range(c * COMBINE_CHUNK, (c + 1) * COMBINE_CHUNK):
            for k in range(TOP_K):
                row_copy(dest_next_ref, other, t, k).start(priority=k % 2)
        for k in range(TOP_K):
            h3 = h3 + gate[:, k:k + 1] * _unpack_bf16_pairs(packed[k])
        o_ref[rows, :] = _rms(h3, nfw)

    @pl.when(i == n_steps - 1)
    def _():
        wait_slot(other)


def _combine(dest_flat, gates, h2, norm_f_w, ys, tile):
    rows = h2.shape[0]
    n_steps = rows // tile
    return pl.pallas_call(
        functools.partial(_combine_body, tile=tile, n_steps=n_steps),
        grid=(n_steps,),
        in_specs=[
            pl.BlockSpec((tile * TOP_K,), lambda i: (i,), memory_space=pltpu.SMEM),
            pl.BlockSpec((tile * TOP_K,), lambda i: (jnp.minimum(i + 1, n_steps - 1),),
                         memory_space=pltpu.SMEM),
            pl.BlockSpec((tile, TOP_K), lambda i: (i, 0)),
            pl.BlockSpec((tile, D_MODEL), lambda i: (i, 0)),
            pl.BlockSpec((1, D_MODEL), lambda i: (0, 0)),
            pl.BlockSpec(memory_space=pl.ANY),
        ],
        out_specs=pl.BlockSpec((tile, D_MODEL), lambda i: (i, 0)),
        out_shape=jax.ShapeDtypeStruct((rows, D_MODEL), F32),
        scratch_shapes=[pltpu.VMEM((2, TOP_K, tile, ROW_SUB, LANES), U32), pltpu.SemaphoreType.DMA((2,))],
        compiler_params=pltpu.CompilerParams(
            dimension_semantics=("arbitrary",), vmem_limit_bytes=VMEM_LIMIT),
        name="combine_final",
    )(dest_flat, dest_flat, gates, h2, norm_f_w, ys)


def _expert_tables(counts, n_assign, tile):
    n_work = n_assign // tile + N_EXPERTS
    nb = (counts + tile - 1) // tile
    bend = jnp.cumsum(nb)
    total = bend[-1]
    pstarts = (bend - nb) * tile
    blk = jnp.minimum(jnp.arange(n_work, dtype=I32), total - 1).astype(I32)
    ex = jnp.minimum(jnp.sum(bend[None, :] <= blk[:, None], axis=1), N_EXPERTS - 1).astype(I32)
    prev_ex = jnp.concatenate([jnp.full((1,), -1, I32), ex[:-1]])
    new_ex = (ex != prev_ex).astype(I32)
    fill_start = (pstarts + counts).astype(I32)
    fill_len = (nb * tile - counts).astype(I32)
    return (blk, ex, new_ex, total.reshape(1).astype(I32)), pstarts, fill_start, fill_len, n_work


def kernel(x, meta, norm1_w, w_in, gla_wa2_f, gla_ba2_f, gla_wa2_b, gla_ba2_b, gla_norm_w, conv_w, conv_b, dt_bias_f, dt_bias_b, a_log_f, a_log_b, ssd_d, ssd_norm_w, w_out, norm2_w, w_router, b_router, w_gu, b_gu, w_dn, b_dn, norm_f_w):
    bsz, seq, d = x.shape
    n_tok = bsz * seq
    l = 0

    wi = w_in[l]
    a_cols = wi[:, 3072:3104]
    dt_cols = wi[:, 5664:5696]
    w_perm = jnp.concatenate(
        [wi[:, :3072], wi[:, 3104:5664], a_cols, dt_cols,
         jnp.zeros((d, N_PROJ - COL_SMALL - 64), F32)], axis=1).astype(BF16)

    def lane_rows(w, lane0):
        return jnp.zeros((LANES, w.shape[1]), F32).at[lane0:lane0 + w.shape[0]].set(w)

    wa_f = lane_rows(gla_wa2_f[l], LANE_AF).astype(BF16)
    wa_b = lane_rows(gla_wa2_b[l], LANE_AB).astype(BF16)
    ba_f = gla_ba2_f[l][None, :]
    ba_b = gla_ba2_b[l][None, :]

    def lane_vec(vf, vb):
        z = jnp.zeros((1, LANES), F32)
        return z.at[0, LANE_DTF:LANE_DTF + SSD_HEADS].set(vf).at[0, LANE_DTB:LANE_DTB + SSD_HEADS].set(vb)

    dtb = lane_vec(dt_bias_f[l], dt_bias_b[l])
    alog = lane_vec(a_log_f[l], a_log_b[l])
    lane_id = (jnp.arange(2 * LANES) % LANES)[None, :, None]
    head_id = (jnp.arange(SSD_GROUP_CH) // SSD_HEAD_DIM)[None, None, :]
    grp = jnp.arange(SSD_GROUPS)[:, None, None] * SSD_GROUP_HEADS
    e_f = (lane_id == LANE_DTF + grp + head_id).astype(BF16)
    e_b = (lane_id == LANE_DTB + grp + head_id).astype(BF16)
    dexp = jnp.repeat(ssd_d[l], SSD_HEAD_DIM)[None, :]

    wr = jnp.zeros((d, LANES), F32).at[:, :N_EXPERTS].set(w_router[l])
    wr_hi = wr.astype(BF16)
    wr_lo = (wr - wr_hi.astype(F32)).astype(BF16)
    b_r = jnp.full((1, LANES), -1e30, F32).at[0, :N_EXPERTS].set(b_router[l])

    x2d = x.reshape(n_tok, d)
    x_meta = jnp.pad(meta.astype(F32), ((NPAD, 0), (0, 0)))
    n1 = norm1_w[l][None, :]
    proj = _inproj(x2d, n1, w_perm, ROW_TILE).reshape(bsz, seq, N_PROJ)
    proj_meta = _inproj(x_meta, n1, w_perm, CHUNK)
    o_gla = _gla(proj, proj_meta, wa_f, ba_f, wa_b, ba_b, gla_norm_w[l][None, :])
    y_ssd = _ssd(proj, proj_meta, conv_w[l], conv_b[l][None, :], dtb, alog, e_f, e_b, dexp,
                 ssd_norm_w[l][None, :])

    wo = w_out[l].astype(BF16)
    h2, n2p, code, gates, cnt = _outproj(
        x2d, o_gla.reshape(n_tok, GLA_DV), y_ssd.reshape(n_tok, SSD_INNER),
        wo[:GLA_DV], wo[GLA_DV:], norm2_w[l][None, :], wr_hi, wr_lo, b_r, ROW_TILE)

    counts = cnt[0, :N_EXPERTS].astype(I32)
    tables, starts, fill_start, fill_len, n_work = _expert_tables(counts, n_tok * TOP_K, EXPERT_TILE)
    expert_of = (code % N_EXPERTS)[..., None] == jnp.arange(N_EXPERTS, dtype=I32)
    dest = (code // N_EXPERTS + jnp.sum(jnp.where(expert_of, starts, 0), axis=-1)).astype(I32).reshape(-1)

    xs = _dispatch(fill_start, fill_len, dest, n2p, ROW_TILE, n_work * EXPERT_TILE)
    ys = _experts(tables, xs, w_gu[l], b_gu[l][:, None, :], w_dn[l], b_dn[l][:, None, :],
                  EXPERT_TILE, n_work)
    out = _combine(dest, gates, h2, norm_f_w[None, :], ys, COMBINE_TILE)
    return out.reshape(bsz, seq, d)
```

```python
import functools

import jax
import jax.numpy as jnp
from jax import lax
from jax.experimental import pallas as pl
from jax.experimental.pallas import tpu as pltpu

F32 = jnp.float32
BF16 = jnp.bfloat16
I32 = jnp.int32
U32 = jnp.uint32

D_MODEL = 1024
N_META = 16
CHUNK = 64
NPAD = CHUNK - N_META
GLA_HEADS = 4
GLA_DK = 512
GLA_DV = 1024
GLA_HEAD_K = GLA_DK // GLA_HEADS
GLA_HEAD_V = GLA_DV // GLA_HEADS
GLA_RANK = 16
GLA_GATE_NORM = 16.0
SSD_INNER = 1024
SSD_HEAD_DIM = 64
SSD_HEADS = SSD_INNER // SSD_HEAD_DIM
SSD_GROUPS = 2
SSD_GROUP_HEADS = SSD_HEADS // SSD_GROUPS
SSD_GROUP_CH = SSD_INNER // SSD_GROUPS
SSD_STATE = 128
SSD_CONV = 5
N_EXPERTS = 32
TOP_K = 4
D_FF = 1024
SWIGLU_LIMIT = 7.0
SWIGLU_ALPHA = 1.702
EPS = 1e-6

LANES = 128
ROW_WORDS = D_MODEL // 2
ROW_SUB = ROW_WORDS // LANES
HALO = 16

COL_Q = 0
COL_K = 512
COL_V = 1024
COL_G = 2048
COL_Z = 3072
COL_X = 4096
COL_B = 5120
COL_C = 5376
COL_SMALL = 5632
N_PROJ = 5760
LANE_AF = 0
LANE_AB = 16
LANE_DTF = 32
LANE_DTB = 48

ROW_TILE = 512
PROJ_TILE = 1024
EXPERT_TILE = 512
EXPERT_SUBTILES = 2
ROUTER_SUBTILES = 2
COMBINE_TILE = 256
COMBINE_CHUNK = 16
VMEM_LIMIT = 56 * 1024 * 1024


def _dot(a, b):
    return jnp.dot(a, b, preferred_element_type=F32)


def _dot_nt(a, b):
    return lax.dot_general(a, b, (((1,), (1,)), ((), ())), preferred_element_type=F32)


def _dot_tn(a, b):
    return lax.dot_general(a, b, (((0,), (0,)), ((), ())), preferred_element_type=F32)


def _split(x):
    hi = x.astype(BF16)
    lo = (x - hi.astype(F32)).astype(BF16)
    return hi, lo


def _sel_dot_l(m01, x):
    hi, lo = _split(x)
    return _dot(m01, jnp.concatenate([hi, lo], axis=0))


def _sel_dot_r(x, m01):
    hi, lo = _split(x)
    return _dot(jnp.concatenate([hi, lo], axis=1), m01)


def _twice_cols(m):
    return jnp.concatenate([m, m], axis=1)


def _softplus(x):
    return jnp.maximum(x, 0.0) + jnp.log(1.0 + jnp.exp(-jnp.abs(x)))


def _log_sigmoid(x):
    return jnp.minimum(x, 0.0) - jnp.log(1.0 + jnp.exp(-jnp.abs(x)))


def _silu(x):
    return x * jax.nn.sigmoid(x)


def _rms(x, w):
    return x * lax.rsqrt(jnp.mean(x * x, axis=-1, keepdims=True) + EPS) * w


def _pack_bf16_pairs(x):
    w = x.shape[1] // 2
    return pltpu.pack_elementwise([x[:, :w], x[:, w:]], packed_dtype=BF16)


def _unpack_bf16_pairs(p):
    lo, hi = (pltpu.unpack_elementwise(p, index=i, packed_dtype=BF16, unpacked_dtype=F32) for i in range(2))
    return jnp.concatenate([lo, hi], axis=1)


def _load_rows(ref3):
    return jnp.concatenate([ref3[:, j, :] for j in range(ROW_SUB)], axis=1)


def _store_rows(ref3, val, keep=None):
    for j in range(ROW_SUB):
        piece = val[:, j * LANES:(j + 1) * LANES]
        if keep is not None:
            piece = jnp.where(keep, piece, ref3[:, j, :])
        ref3[:, j, :] = piece


def _tri(n, *, lower, inclusive):
    r = lax.broadcasted_iota(I32, (n, n), 0)
    c = lax.broadcasted_iota(I32, (n, n), 1)
    if lower:
        return (c <= r) if inclusive else (c < r)
    return (c >= r) if inclusive else (c > r)


def _inproj_body(x_ref, nw_ref, w_ref, o_ref, *, col_chunks):
    xn = _rms(x_ref[...], nw_ref[...]).astype(BF16)
    for lo, hi in col_chunks:
        o_ref[:, lo:hi] = _dot(xn, w_ref[:, lo:hi]).astype(BF16)


def _inproj(x2d, norm_w, w_perm, tile):
    rows = x2d.shape[0]
    tile = min(tile, rows)
    col_chunks = tuple((c, min(c + 1024, N_PROJ)) for c in range(0, N_PROJ, 1024))
    return pl.pallas_call(
        functools.partial(_inproj_body, col_chunks=col_chunks),
        grid=(rows // tile,),
        in_specs=[
            pl.BlockSpec((tile, D_MODEL), lambda i: (i, 0)),
            pl.BlockSpec((1, D_MODEL), lambda i: (0, 0)),
            pl.BlockSpec((D_MODEL, N_PROJ), lambda i: (0, 0), pipeline_mode=pl.Buffered(1)),
        ],
        out_specs=pl.BlockSpec((tile, N_PROJ), lambda i: (i, 0)),
        out_shape=jax.ShapeDtypeStruct((rows, N_PROJ), BF16),
        compiler_params=pltpu.CompilerParams(
            dimension_semantics=("arbitrary",), vmem_limit_bytes=VMEM_LIMIT),
        name="inproj",
    )(x2d, norm_w, w_perm)


GLA_HEADS_PER_STEP = 2
GLA_STEPS_PER_ITER = 8
SSD_STEPS_PER_ITER = 8


def _gla_body(q_ref, k_ref, v_ref, g_ref, sm_ref, qm_ref, km_ref, vm_ref, smm_ref,
              waf_ref, baf_ref, wab_ref, bab_ref, nw_ref, o_ref, st_ref, *, n_chunks):
    heads = GLA_HEADS_PER_STEP
    nprob = 2 * heads
    rows_all = nprob * CHUNK
    wk = heads * GLA_HEAD_K
    dk, dv = GLA_HEAD_K, GLA_HEAD_V

    tri_f = _tri(CHUNK, lower=True, inclusive=True)
    r2 = lax.broadcasted_iota(I32, (2 * CHUNK, 2 * CHUNK), 0)
    c2 = lax.broadcasted_iota(I32, (2 * CHUNK, 2 * CHUNK), 1)
    cum2 = ((r2 < CHUNK) & (c2 <= r2)) | ((r2 >= CHUNK) & (c2 >= r2))
    tri2 = _twice_cols(cum2.astype(BF16))
    ra = lax.broadcasted_iota(I32, (rows_all, rows_all), 0)
    ca = lax.broadcasted_iota(I32, (rows_all, rows_all), 1)
    same = (ra // CHUNK) == (ca // CHUNK)
    att_mask = same & (((ra < heads * CHUNK) & (ca <= ra)) | ((ra >= heads * CHUNK) & (ca > ra)))
    rb_ = lax.broadcasted_iota(I32, (rows_all, nprob * dk), 0) // CHUNK
    cb_ = lax.broadcasted_iota(I32, (rows_all, nprob * dk), 1) // dk
    own = rb_ == cb_

    waf, baf = waf_ref[...], baf_ref[...]
    wab, bab = wab_ref[...], bab_ref[...]
    wa_cat = jnp.concatenate([waf, wab], axis=1)
    nw = nw_ref[...]
    zero16 = jnp.zeros((), BF16)

    st_ref[...] = jnp.zeros_like(st_ref)

    meta_rows = lax.broadcasted_iota(I32, (CHUNK, wk), 0) >= NPAD
    lg = _log_sigmoid(_dot(smm_ref[...], waf) + baf) * (1.0 / GLA_GATE_NORM)
    bm = _sel_dot_l(_twice_cols(tri_f.astype(BF16)), jnp.where(meta_rows, lg, 0.0))
    for h in range(heads):
        lanes = slice(h * dk, (h + 1) * dk)
        bh, toth = bm[:, lanes], bm[CHUNK - 1:CHUNK, lanes]
        kend = (km_ref[:, lanes].astype(F32) * jnp.exp(toth - bh)).astype(BF16)
        st_ref[:, lanes] = _dot_tn(vm_ref[:, h * dv:(h + 1) * dv], kend)

    def stack(f, b, w):
        return jnp.concatenate([f[:, h * w:(h + 1) * w] for h in range(heads)]
                               + [b[:, h * w:(h + 1) * w] for h in range(heads)], axis=0)

    def steps(j, finalize):
        ids = [j * GLA_STEPS_PER_ITER + s for s in range(GLA_STEPS_PER_ITER)]
        rfs = [pl.ds(pl.multiple_of(i * CHUNK, CHUNK), CHUNK) for i in ids]
        rbs = [pl.ds(pl.multiple_of((n_chunks - 1 - i) * CHUNK, CHUNK), CHUNK) for i in ids]
        zs = [_dot(jnp.concatenate([sm_ref[rf, :], sm_ref[rb, :]], axis=0), wa_cat) for rf, rb in zip(rfs, rbs)]
        b2s = []
        for z in zs:
            lg = jnp.concatenate([z[:CHUNK, :wk] + baf, z[CHUNK:, wk:] + bab], axis=0)
            b2s.append(_sel_dot_l(tri2, _log_sigmoid(lg) * (1.0 / GLA_GATE_NORM)))
        vss, qds, kends, tots, gs = [], [], [], [], []
        for b2, rf, rb in zip(b2s, rfs, rbs):
            tot_f, tot_b = b2[CHUNK - 1:CHUNK], b2[CHUNK:CHUNK + 1]
            bst = stack(b2[:CHUNK], b2[CHUNK:], dk)
            tst = stack(jnp.broadcast_to(tot_f, (CHUNK, wk)), jnp.broadcast_to(tot_b, (CHUNK, wk)), dk)
            qs = stack(q_ref[rf, :], q_ref[rb, :], dk).astype(F32)
            ks = stack(k_ref[rf, :], k_ref[rb, :], dk).astype(F32)
            qd = (qs * (GLA_HEAD_K ** -0.5) * jnp.exp(bst)).astype(BF16)
            kd = (ks * jnp.exp(-bst)).astype(BF16)
            vss.append(stack(v_ref[rf, :], v_ref[rb, :], dv))
            qds.append(qd)
            kends.append((ks * jnp.exp(tst - bst)).astype(BF16))
            tots.append(jnp.concatenate([tot_f, tot_b], axis=1))
            gs.append(_dot_nt(qd, kd))
        intras = [_dot(jnp.where(att_mask, g, 0.0).astype(BF16), vs) for g, vs in zip(gs, vss)]
        upds = [_dot_tn(vs, jnp.where(own, jnp.concatenate([kend] * nprob, axis=1), zero16))
                for vs, kend in zip(vss, kends)]
        st = st_ref[...]
        outs = []
        for qd, tot, upd, intra in zip(qds, tots, upds, intras):
            qd_own = jnp.where(own, jnp.concatenate([qd] * nprob, axis=1), zero16)
            outs.append(intra + _dot_nt(qd_own, st.astype(BF16)))
            st = st * jnp.exp(tot) + upd
        st_ref[...] = st
        for out, rf, rb in zip(outs, rfs, rbs):
            for p in range(nprob):
                rows, h = (rf if p < heads else rb), p % heads
                o = out[p * CHUNK:(p + 1) * CHUNK]
                cols = slice(h * dv, (h + 1) * dv)
                if finalize:
                    o = o + o_ref[rows, cols].astype(F32)
                    gate = _silu(g_ref[rows, cols].astype(F32))
                    o_ref[rows, cols] = (_rms(o, nw) * gate).astype(BF16)
                else:
                    o_ref[rows, cols] = o.astype(BF16)

    def first_half(j, carry):
        steps(j, False)
        return carry

    def second_half(j, carry):
        steps(j, True)
        return carry

    half_iters = n_chunks // 2 // GLA_STEPS_PER_ITER
    lax.fori_loop(0, half_iters, first_half, 0)
    lax.fori_loop(half_iters, 2 * half_iters, second_half, 0)


def _gla(proj, proj_meta, wa_f, ba_f, wa_b, ba_b, norm_w):
    bsz, seq, _ = proj.shape
    assert (seq // CHUNK) % (2 * GLA_STEPS_PER_ITER) == 0
    hp = GLA_HEADS_PER_STEP
    wk, wv = hp * GLA_HEAD_K, hp * GLA_HEAD_V
    kb, vb, gb, sb = COL_K // wk, COL_V // wv, COL_G // wv, COL_SMALL // LANES
    real = lambda width, base: pl.BlockSpec((None, seq, width), lambda b, h: (b, 0, base + h))
    meta = lambda width, base: pl.BlockSpec((CHUNK, width), lambda b, h: (0, base + h))
    per_step = lambda rows_: pl.BlockSpec((rows_, wk), lambda b, h: (0, h))
    return pl.pallas_call(
        functools.partial(_gla_body, n_chunks=seq // CHUNK),
        grid=(bsz, GLA_HEADS // hp),
        in_specs=[
            real(wk, 0), real(wk, kb), real(wv, vb), real(wv, gb),
            pl.BlockSpec((None, seq, LANES), lambda b, h: (b, 0, sb)),
            meta(wk, 0), meta(wk, kb), meta(wv, vb),
            pl.BlockSpec((CHUNK, LANES), lambda b, h: (0, sb)),
            per_step(LANES), per_step(1), per_step(LANES), per_step(1),
            pl.BlockSpec((1, GLA_HEAD_V), lambda b, h: (0, 0)),
        ],
        out_specs=pl.BlockSpec((None, seq, wv), lambda b, h: (b, 0, h)),
        out_shape=jax.ShapeDtypeStruct((bsz, seq, GLA_DV), BF16),
        scratch_shapes=[pltpu.VMEM((GLA_HEAD_V, 2 * hp * GLA_HEAD_K), F32)],
        compiler_params=pltpu.CompilerParams(
            dimension_semantics=("arbitrary", "arbitrary"), vmem_limit_bytes=VMEM_LIMIT),
        name="gla",
    )(proj, proj, proj, proj, proj, proj_meta, proj_meta, proj_meta, proj_meta,
      wa_f, ba_f, wa_b, ba_b, norm_w)


def _conv_silu(win, cw, cb):
    half = (SSD_CONV - 1) // 2
    acc = cb
    for j in range(SSD_CONV):
        lo = HALO - half + j
        acc = acc + win[lo:lo + CHUNK, :] * cw[j:j + 1, :]
    return _silu(acc)


def _ssd_body(x_ref, z_ref, b_ref, c_ref, sm_ref, xm_ref, bm_ref, cm_ref, smm_ref,
              cwx_ref, cbx_ref, cwb_ref, cbb_ref, cwc_ref, cbc_ref, dtb_ref, alog_ref,
              ef_ref, eb_ref, dexp_ref, nw_ref, o_ref, st_ref, xc_ref, bc_ref, cc_ref,
              *, n_chunks):
    gh, p, n = SSD_GROUP_HEADS, SSD_HEAD_DIM, SSD_STATE
    width = gh * p
    two = 2 * CHUNK
    lane_s = lax.broadcasted_iota(I32, (two, width), 1) % p
    row2 = lax.broadcasted_iota(I32, (two, width), 0)
    row_t = row2 % CHUNK
    irep2 = lane_s == row_t
    irep2_16 = irep2.astype(BF16)
    pair_mask = ((row2 < CHUNK) & (lane_s <= row_t)) | ((row2 >= CHUNK) & (lane_s > row_t))
    r2 = lax.broadcasted_iota(I32, (two, two), 0)
    c2 = lax.broadcasted_iota(I32, (two, two), 1)
    same_dir = (r2 < CHUNK) == (c2 < CHUNK)
    ones2 = _twice_cols(same_dir.astype(BF16))
    cum2 = ((r2 < CHUNK) & (c2 <= r2)) | ((r2 >= CHUNK) & (c2 >= r2))
    tri2 = _twice_cols(cum2.astype(BF16))
    bd_r = lax.broadcasted_iota(I32, (width, width), 0) // p
    bd_c = lax.broadcasted_iota(I32, (width, width), 1) // p
    bdmask = bd_r == bd_c
    zero16 = jnp.zeros((), BF16)

    cwx, cbx = cwx_ref[...], cbx_ref[...]
    cwb, cbb = cwb_ref[...], cbb_ref[...]
    cwc, cbc = cwc_ref[...], cbc_ref[...]
    dtb = dtb_ref[...]
    a_row = -jnp.exp(alog_ref[...])
    ef, eb = ef_ref[...], eb_ref[...]
    dexp, nw = dexp_ref[...], nw_ref[...]
    seq = n_chunks * CHUNK

    def window(ref, mref, r):
        off = pl.multiple_of(r * CHUNK, CHUNK)
        poff = pl.multiple_of(jnp.maximum(off - HALO, 0), HALO)
        noff = pl.multiple_of(jnp.minimum(off + CHUNK, seq - HALO), HALO)
        prev = jnp.where(r == 0, mref[CHUNK - HALO:, :], ref[pl.ds(poff, HALO), :])
        nxt = jnp.where(r == n_chunks - 1, zero16, ref[pl.ds(noff, HALO), :])
        return jnp.concatenate([prev, ref[pl.ds(off, CHUNK), :], nxt], axis=0)

    half = (SSD_CONV - 1) // 2
    side_taps = [j for j in range(SSD_CONV) if j != half]
    win_rows = CHUNK + 2 * HALO
    sr = lax.broadcasted_iota(I32, (len(side_taps) * CHUNK, win_rows), 0)
    sc = lax.broadcasted_iota(I32, (len(side_taps) * CHUNK, win_rows), 1)
    tap_of = sr // CHUNK
    tap_shift = jnp.where(tap_of < half, tap_of, tap_of + 1) - half
    shift_mat = (sc == (sr % CHUNK) + HALO + tap_shift).astype(BF16)
    cw_all = jnp.concatenate([cwx, cwb, cwc], axis=1)
    cb_all = jnp.concatenate([cbx, cbb, cbc], axis=1)

    def conv_body(it, carry):
        chunks = [it * SSD_STEPS_PER_ITER + s for s in range(SSD_STEPS_PER_ITER)]
        wins = [jnp.concatenate([window(x_ref, xm_ref, r), window(b_ref, bm_ref, r), window(c_ref, cm_ref, r)],
                                axis=1) for r in chunks]
        shifts = [_dot(shift_mat, win) for win in wins]
        for r, win, shifted in zip(chunks, wins, shifts):
            rows = pl.ds(pl.multiple_of(r * CHUNK, CHUNK), CHUNK)
            acc = cb_all + win[HALO:HALO + CHUNK].astype(F32) * cw_all[half:half + 1]
            for pos, j in enumerate(side_taps):
                acc = acc + shifted[pos * CHUNK:(pos + 1) * CHUNK] * cw_all[j:j + 1]
            y = _silu(acc).astype(BF16)
            xc_ref[rows, :] = y[:, :width]
            bc_ref[rows, :] = y[:, width:width + n]
            cc_ref[rows, :] = y[:, width + n:]
        return carry

    lax.fori_loop(0, n_chunks // SSD_STEPS_PER_ITER, conv_body, 0)

    st_ref[...] = jnp.zeros_like(st_ref)

    def meta_window(mref, ref):
        zeros = jnp.zeros((HALO, mref.shape[1]), F32)
        return jnp.concatenate([zeros, mref[...].astype(F32), ref[0:HALO, :].astype(F32)], axis=0)

    def meta_mask(width_):
        return lax.broadcasted_iota(I32, (CHUNK, width_), 0) >= NPAD

    xc = jnp.where(meta_mask(width), _conv_silu(meta_window(xm_ref, x_ref), cwx, cbx), 0.0)
    bc = jnp.where(meta_mask(n), _conv_silu(meta_window(bm_ref, b_ref), cwb, cbb), 0.0)
    dt = jnp.where(meta_mask(LANES), _softplus(smm_ref[...].astype(F32) + dtb), 0.0)
    cs = _sel_dot_l(_twice_cols(cum2[:CHUNK, :CHUNK].astype(BF16)), dt * a_row)
    both = _sel_dot_r(jnp.concatenate([cs, dt], axis=0), ef)
    cs_e, dt_e = both[:CHUNK], both[CHUNK:]
    xend = (xc * dt_e * jnp.exp(cs_e[CHUNK - 1:CHUNK] - cs_e)).astype(BF16)
    st_ref[0:n, :] = _dot_tn(bc.astype(BF16), xend)

    zeros_n = jnp.zeros((CHUNK, n), BF16)

    def own_dir(a):
        return jnp.concatenate([jnp.concatenate([a[:CHUNK], zeros_n], axis=1),
                                jnp.concatenate([zeros_n, a[CHUNK:]], axis=1)], axis=0)

    def steps(j, finalize):
        ids = [j * SSD_STEPS_PER_ITER + s for s in range(SSD_STEPS_PER_ITER)]
        rfs = [pl.ds(pl.multiple_of(i * CHUNK, CHUNK), CHUNK) for i in ids]
        rbs = [pl.ds(pl.multiple_of((n_chunks - 1 - i) * CHUNK, CHUNK), CHUNK) for i in ids]
        both_rows = lambda ref, rf, rb: jnp.concatenate([ref[rf, :], ref[rb, :]], axis=0)
        dts = [_softplus(both_rows(sm_ref, rf, rb).astype(F32) + dtb) for rf, rb in zip(rfs, rbs)]
        css = [_sel_dot_l(tri2, dt * a_row) for dt in dts]
        exs = [(_sel_dot_r(jnp.concatenate([cs[:CHUNK], dt[:CHUNK]], axis=0), ef),
                _sel_dot_r(jnp.concatenate([cs[CHUNK:], dt[CHUNK:]], axis=0), eb)) for cs, dt in zip(css, dts)]
        xcs, cs_es, tots, xdts, xends, cbs, bcs, ccs = [], [], [], [], [], [], [], []
        for (ex_f, ex_b), rf, rb in zip(exs, rfs, rbs):
            cs_e = jnp.concatenate([ex_f[:CHUNK], ex_b[:CHUNK]], axis=0)
            dt_e = jnp.concatenate([ex_f[CHUNK:], ex_b[CHUNK:]], axis=0)
            tot_f, tot_b = cs_e[CHUNK - 1:CHUNK], cs_e[CHUNK:CHUNK + 1]
            tot = jnp.concatenate([jnp.broadcast_to(tot_f, (CHUNK, width)),
                                   jnp.broadcast_to(tot_b, (CHUNK, width))], axis=0)
            xc = both_rows(xc_ref, rf, rb).astype(F32)
            bc16, cc16 = both_rows(bc_ref, rf, rb), both_rows(cc_ref, rf, rb)
            xdt = xc * dt_e
            xcs.append(xc)
            cs_es.append(cs_e)
            tots.append((tot_f, tot_b))
            xdts.append(xdt.astype(BF16))
            xends.append((xdt * jnp.exp(tot - cs_e)).astype(BF16))
            bcs.append(bc16)
            ccs.append(cc16)
            cbs.append(_dot_nt(cc16, bc16))
        cb_reps = [_dot(jnp.where(same_dir, cb, 0.0).astype(BF16), irep2_16) for cb in cbs]
        cs_rows = [_sel_dot_l(ones2, jnp.where(irep2, cs_e, 0.0)) for cs_e in cs_es]
        intras = []
        for cb_rep, cs_row, cs_e, xdt16 in zip(cb_reps, cs_rows, cs_es, xdts):
            decay = jnp.where(pair_mask, jnp.exp(jnp.minimum(cs_e - cs_row, 0.0)), 0.0)
            w = (cb_rep * decay).astype(BF16)
            intras.append(jnp.concatenate(
                [_dot(w[d * CHUNK:(d + 1) * CHUNK],
                      jnp.where(bdmask, jnp.concatenate([xdt16[d * CHUNK:(d + 1) * CHUNK]] * gh, axis=0), zero16))
                 for d in range(2)], axis=0))
        upds = [_dot_tn(own_dir(bc16), xend) for bc16, xend in zip(bcs, xends)]
        st = st_ref[...]
        ys = []
        for intra, cc16, cs_e, (tot_f, tot_b), upd in zip(intras, ccs, cs_es, tots, upds):
            ys.append(intra + _dot(own_dir(cc16), st.astype(BF16)) * jnp.exp(cs_e))
            grow = jnp.concatenate([jnp.broadcast_to(jnp.exp(tot_f), (n, width)),
                                    jnp.broadcast_to(jnp.exp(tot_b), (n, width))], axis=0)
            st = st * grow + upd
        st_ref[...] = st
        for y, xc, rf, rb in zip(ys, xcs, rfs, rbs):
            for d, rows in enumerate((rf, rb)):
                yd = y[d * CHUNK:(d + 1) * CHUNK]
                if finalize:
                    yd = yd + o_ref[rows, :].astype(F32) + xc[d * CHUNK:(d + 1) * CHUNK] * dexp
                    yd = yd * _silu(z_ref[rows, :].astype(F32))
                    o_ref[rows, :] = _rms(yd, nw).astype(BF16)
                else:
                    o_ref[rows, :] = yd.astype(BF16)

    def first_half(j, carry):
        steps(j, False)
        return carry

    def second_half(j, carry):
        steps(j, True)
        return carry

    half_iters = n_chunks // 2 // SSD_STEPS_PER_ITER
    lax.fori_loop(0, half_iters, first_half, 0)
    lax.fori_loop(half_iters, 2 * half_iters, second_half, 0)


def _ssd(proj, proj_meta, conv_w, conv_b, dtb, alog, e_f, e_b, dexp, norm_w):
    bsz, seq, _ = proj.shape
    gc = SSD_GROUP_CH
    xb, zb = COL_X // gc, COL_Z // gc
    bb, cb, sb = COL_B // SSD_STATE, COL_C // SSD_STATE, COL_SMALL // LANES
    real = lambda width, base: pl.BlockSpec((None, seq, width), lambda b, g: (b, 0, base + g))
    meta = lambda width, base: pl.BlockSpec((CHUNK, width), lambda b, g: (0, base + g))
    cpar = lambda rows_, width, base: pl.BlockSpec((rows_, width), lambda b, g: (0, base + g))
    cbb_, ccb_ = SSD_INNER // SSD_STATE, SSD_INNER // SSD_STATE + SSD_GROUPS
    const = lambda shape: pl.BlockSpec(shape, lambda b, g: (0, 0))
    return pl.pallas_call(
        functools.partial(_ssd_body, n_chunks=seq // CHUNK),
        grid=(bsz, SSD_GROUPS),
        in_specs=[
            real(gc, xb), real(gc, zb), real(SSD_STATE, bb), real(SSD_STATE, cb),
            pl.BlockSpec((None, seq, LANES), lambda b, g: (b, 0, sb)),
            meta(gc, xb), meta(SSD_STATE, bb), meta(SSD_STATE, cb),
            pl.BlockSpec((CHUNK, LANES), lambda b, g: (0, sb)),
            cpar(SSD_CONV, gc, 0), cpar(1, gc, 0),
            cpar(SSD_CONV, SSD_STATE, cbb_), cpar(1, SSD_STATE, cbb_),
            cpar(SSD_CONV, SSD_STATE, ccb_), cpar(1, SSD_STATE, ccb_),
            const((1, LANES)), const((1, LANES)),
            pl.BlockSpec((None, 2 * LANES, gc), lambda b, g: (g, 0, 0)),
            pl.BlockSpec((None, 2 * LANES, gc), lambda b, g: (g, 0, 0)),
            pl.BlockSpec((1, gc), lambda b, g: (0, g)),
            pl.BlockSpec((1, gc), lambda b, g: (0, g)),
        ],
        out_specs=pl.BlockSpec((None, seq, gc), lambda b, g: (b, 0, g)),
        out_shape=jax.ShapeDtypeStruct((bsz, seq, SSD_INNER), BF16),
        scratch_shapes=[pltpu.VMEM((2 * SSD_STATE, gc), F32),
                        pltpu.VMEM((seq, gc), BF16), pltpu.VMEM((seq, SSD_STATE), BF16),
                        pltpu.VMEM((seq, SSD_STATE), BF16)],
        compiler_params=pltpu.CompilerParams(
            dimension_semantics=("arbitrary", "arbitrary"), vmem_limit_bytes=VMEM_LIMIT),
        name="ssd",
    )(proj, proj, proj, proj, proj, proj_meta, proj_meta, proj_meta, proj_meta,
      conv_w, conv_b, conv_w, conv_b, conv_w, conv_b, dtb, alog, e_f, e_b, dexp, norm_w)


def _outproj_body(x_ref, og_ref, ys_ref, wo1_ref, wo2_ref, n2w_ref, wrh_ref, wrl_ref, br_ref,
                  h2_ref, n2p_ref, code_ref, gate_ref, cnt_ref, carry_ref, *, tile):
    i = pl.program_id(0)

    @pl.when(i == 0)
    def _():
        carry_ref[...] = jnp.zeros_like(carry_ref)

    sub = tile // ROUTER_SUBTILES
    parts = [pl.ds(s * sub, sub) for s in range(ROUTER_SUBTILES)]
    wo1, wo2, n2w = wo1_ref[...], wo2_ref[...], n2w_ref[...]
    wrh, wrl, br = wrh_ref[...], wrl_ref[...], br_ref[...]
    h2s = [x_ref[p, :] + _dot(og_ref[p, :], wo1) + _dot(ys_ref[p, :], wo2) for p in parts]
    n2s = []
    for p, h2 in zip(parts, h2s):
        h2_ref[p, :] = h2
        n2 = _rms(h2, n2w)
        _store_rows(n2p_ref.at[p], _pack_bf16_pairs(n2))
        n2s.append(n2)
    logit_parts = []
    for n2 in n2s:
        nh, nl = _split(n2)
        logit_parts.append(_dot(nh, wrh) + _dot(nh, wrl) + _dot(nl, wrh) + br)

    lane = lax.broadcasted_iota(I32, (sub, LANES), 1)
    lane_f = lane.astype(F32)
    lane4 = lax.broadcasted_iota(I32, (sub, TOP_K), 1)
    per_row = LANES // TOP_K
    tok = lax.broadcasted_iota(I32, (sub, LANES), 0)
    here = (lane // TOP_K) == (tok % per_row)
    gather_rows = (lax.broadcasted_iota(I32, (sub // per_row, sub), 1) // per_row
                   == lax.broadcasted_iota(I32, (sub // per_row, sub), 0)).astype(BF16)
    before_me = _tri(sub, lower=True, inclusive=False).astype(BF16)

    routed = []
    for p, logits in zip(parts, logit_parts):
        vals, onehots, picks = [], [], []
        work = logits
        for k in range(TOP_K):
            m = jnp.max(work, axis=-1, keepdims=True)
            first = jnp.min(jnp.where(work == m, lane_f, float(LANES)), axis=-1, keepdims=True)
            oh = lane_f == first
            work = jnp.where(oh, -jnp.inf, work)
            vals.append(m)
            onehots.append(oh)
            picks.append(first)
        exps = [jnp.exp(v - vals[0]) for v in vals]
        inv = 1.0 / (exps[0] + exps[1] + exps[2] + exps[3])
        gate_out = jnp.zeros((sub, TOP_K), F32)
        for k in range(TOP_K):
            gate_out = jnp.where(lane4 == k, exps[k] * inv, gate_out)
        gate_ref[p, :] = gate_out
        any_oh = (onehots[0] | onehots[1] | onehots[2] | onehots[3])
        any16 = jnp.where(any_oh, 1.0, 0.0).astype(BF16)
        routed.append((onehots, picks, any16, _dot(before_me, any16)))

    carry = carry_ref[...]
    for s, (onehots, picks, any16, before) in enumerate(routed):
        before = before + carry
        carry = carry + jnp.sum(any16.astype(F32), axis=0, keepdims=True)
        rest = [jnp.sum(jnp.where(onehots[k], before, 0.0), axis=-1, keepdims=True) * N_EXPERTS + picks[k]
                for k in range(TOP_K)]
        flat = jnp.zeros((sub // per_row, LANES), F32)
        for scale in (65536.0, 256.0, 1.0):
            piece = [jnp.floor(c * (1.0 / scale)) for c in rest]
            rest = [c - q * scale for c, q in zip(rest, piece)]
            by_k = piece[TOP_K - 1]
            for k in range(TOP_K - 2, -1, -1):
                by_k = jnp.where(lane % TOP_K == k, piece[k], by_k)
            flat = flat + scale * _dot(gather_rows, jnp.where(here, by_k, 0.0).astype(BF16))
        code_ref[pl.ds(s * (sub // per_row), sub // per_row), :] = flat.astype(I32)
    carry_ref[...] = carry
    cnt_ref[...] = carry


def _outproj(x2d, o_gla, y_ssd, w_out1, w_out2, norm2_w, wr_hi, wr_lo, b_r, tile):
    rows = x2d.shape[0]
    tile = min(tile, rows)
    row = lambda width: pl.BlockSpec((tile, width), lambda i: (i, 0))
    const = lambda shape: pl.BlockSpec(shape, lambda i: (0, 0))
    return pl.pallas_call(
        functools.partial(_outproj_body, tile=tile),
        grid=(rows // tile,),
        in_specs=[
            row(D_MODEL), row(GLA_DV), row(SSD_INNER),
            const((GLA_DV, D_MODEL)), const((SSD_INNER, D_MODEL)), const((1, D_MODEL)),
            const((D_MODEL, LANES)), const((D_MODEL, LANES)), const((1, LANES)),
        ],
        out_specs=[
            row(D_MODEL), pl.BlockSpec((tile, ROW_SUB, LANES), lambda i: (i, 0, 0)),
            pl.BlockSpec((tile * TOP_K // LANES, LANES), lambda i: (i, 0)), row(TOP_K), const((1, LANES)),
        ],
        out_shape=[
            jax.ShapeDtypeStruct((rows, D_MODEL), F32),
            jax.ShapeDtypeStruct((rows, ROW_SUB, LANES), U32),
            jax.ShapeDtypeStruct((rows * TOP_K // LANES, LANES), I32),
            jax.ShapeDtypeStruct((rows, TOP_K), F32),
            jax.ShapeDtypeStruct((1, LANES), F32),
        ],
        scratch_shapes=[pltpu.VMEM((1, LANES), F32)],
        compiler_params=pltpu.CompilerParams(
            dimension_semantics=("arbitrary",), vmem_limit_bytes=VMEM_LIMIT),
        name="outproj_router",
    )(x2d, o_gla, y_ssd, w_out1, w_out2, norm2_w, wr_hi, wr_lo, b_r)


def _row_copy(src, src_row, dst, dst_row, sem):
    return pltpu.make_async_copy(src.at[src_row], dst.at[dst_row], sem)


def _dispatch_body(fill_start_ref, fill_len_ref, dest_ref, n2p_ref, xs_hbm, zeros_ref, sem, fill_sem, *, tile):
    @pl.when(pl.program_id(0) == 0)
    def _():
        zeros_ref[...] = jnp.zeros_like(zeros_ref)

        def fill(e, size, wait):
            length = fill_len_ref[e]

            @pl.when((length & size) != 0)
            def _():
                pos = fill_start_ref[e] + (length & (size - 1))
                copy = pltpu.make_async_copy(zeros_ref.at[pl.ds(0, size)], xs_hbm.at[pl.ds(pos, size)], fill_sem)
                if wait:
                    copy.wait()
                else:
                    copy.start()

        for wait in (False, True):
            for e in range(N_EXPERTS):
                size = 1
                while size < tile:
                    fill(e, size, wait)
                    size *= 2

    def start(t, carry):
        for k in range(TOP_K):
            _row_copy(n2p_ref, t, xs_hbm, dest_ref[t * TOP_K + k], sem).start(priority=k % 2)
        return carry

    lax.fori_loop(0, tile, start, 0)

    for k in range(TOP_K):
        pltpu.make_async_copy(n2p_ref, xs_hbm.at[pl.ds(0, tile)], sem).wait()


def _dispatch(fill_start, fill_len, dest_flat, n2p, tile, out_rows):
    rows = n2p.shape[0]
    grid_spec = pltpu.PrefetchScalarGridSpec(
        num_scalar_prefetch=2,
        grid=(rows // tile,),
        in_specs=[
            pl.BlockSpec((tile * TOP_K,), lambda i, fs, fl: (i,), memory_space=pltpu.SMEM),
            pl.BlockSpec((tile, ROW_SUB, LANES), lambda i, fs, fl: (i, 0, 0)),
        ],
        out_specs=pl.BlockSpec(memory_space=pl.ANY),
        scratch_shapes=[pltpu.VMEM((tile // 2, ROW_SUB, LANES), U32),
                        pltpu.SemaphoreType.DMA(()), pltpu.SemaphoreType.DMA(())],
    )
    return pl.pallas_call(
        functools.partial(_dispatch_body, tile=tile),
        grid_spec=grid_spec,
        out_shape=jax.ShapeDtypeStruct((out_rows, ROW_SUB, LANES), U32),
        compiler_params=pltpu.CompilerParams(dimension_semantics=("arbitrary",)),
        name="dispatch",
    )(fill_start, fill_len, dest_flat, n2p)


def _expert_body(blk_ref, exp_ref, newexp_ref, nw_ref,
                 x_ref, wgu_ref, bgu_ref, wdn_ref, bdn_ref, o_ref, wgu16_ref, wdn16_ref, *, tile):
    w = pl.program_id(0)
    cast_rows = 64

    @pl.when((w < nw_ref[0]) & (newexp_ref[w] == 1))
    def _():
        def cast(i, carry):
            rows = pl.ds(pl.multiple_of(i * cast_rows, cast_rows), cast_rows)
            wgu16_ref[rows, :] = wgu_ref[rows, :].astype(BF16)
            wdn16_ref[rows, :] = wdn_ref[rows, :].astype(BF16)
            return carry

        lax.fori_loop(0, D_MODEL // cast_rows, cast, 0)

    @pl.when(w < nw_ref[0])
    def _():
        sub = tile // EXPERT_SUBTILES
        parts = [pl.ds(s * sub, sub) for s in range(EXPERT_SUBTILES)]
        xs = [_unpack_bf16_pairs(_load_rows(x_ref.at[p])).astype(BF16) for p in parts]
        gts = [jnp.minimum(_dot(x, wgu16_ref[:, :D_FF]) + bgu_ref[:, :D_FF], SWIGLU_LIMIT) for x in xs]
        ups = [jnp.clip(_dot(x, wgu16_ref[:, D_FF:]) + bgu_ref[:, D_FF:], -SWIGLU_LIMIT, SWIGLU_LIMIT)
               for x in xs]
        acts = [((up + 1.0) * gt * jax.nn.sigmoid(gt * SWIGLU_ALPHA)).astype(BF16) for gt, up in zip(gts, ups)]
        for p, act in zip(parts, acts):
            _store_rows(o_ref.at[p], _pack_bf16_pairs(_dot(act, wdn16_ref[...]) + bdn_ref[...]))


def _experts(tables, xs, w_gu, b_gu, w_dn, b_dn, tile, n_work):
    rows = xs.shape[0]
    row_block = pl.BlockSpec((tile, ROW_SUB, LANES), lambda w, blk, ex, ne, nw: (blk[w], 0, 0))
    grid_spec = pltpu.PrefetchScalarGridSpec(
        num_scalar_prefetch=4,
        grid=(n_work,),
        in_specs=[
            row_block,
            pl.BlockSpec((None, D_MODEL, 2 * D_FF), lambda w, blk, ex, ne, nw: (ex[w], 0, 0)),
            pl.BlockSpec((None, 1, 2 * D_FF), lambda w, blk, ex, ne, nw: (ex[w], 0, 0)),
            pl.BlockSpec((None, D_FF, D_MODEL), lambda w, blk, ex, ne, nw: (ex[w], 0, 0)),
            pl.BlockSpec((None, 1, D_MODEL), lambda w, blk, ex, ne, nw: (ex[w], 0, 0)),
        ],
        out_specs=row_block,
        scratch_shapes=[pltpu.VMEM((D_MODEL, 2 * D_FF), BF16), pltpu.VMEM((D_FF, D_MODEL), BF16)],
    )
    return pl.pallas_call(
        functools.partial(_expert_body, tile=tile),
        grid_spec=grid_spec,
        out_shape=jax.ShapeDtypeStruct((rows, ROW_SUB, LANES), U32),
        compiler_params=pltpu.CompilerParams(
            dimension_semantics=("arbitrary",), vmem_limit_bytes=VMEM_LIMIT),
        name="experts",
    )(*tables, xs, w_gu, b_gu, w_dn, b_dn)


def _combine_body(dest_ref, dest_next_ref, gate_ref, h2_ref, nfw_ref, ys_hbm, o_ref, buf, sems,
                  *, tile, n_steps):
    i = pl.program_id(0)
    slot = i % 2
    other = 1 - slot

    def row_copy(d_ref, s, t, k):
        return pltpu.make_async_copy(ys_hbm.at[d_ref[t * TOP_K + k]], buf.at[s, k, t], sems.at[s])

    def wait_slot(s):
        for k in range(TOP_K):
            pltpu.make_async_copy(ys_hbm.at[pl.ds(0, tile)], buf.at[s, k], sems.at[s]).wait()

    @pl.when(i == 0)
    def _():
        def start(t, carry):
            for k in range(TOP_K):
                row_copy(dest_ref, 0, t, k).start(priority=k % 2)
            return carry

        lax.fori_loop(0, tile, start, 0)

    wait_slot(slot)
    nfw = nfw_ref[...]
    for c in range(tile // COMBINE_CHUNK):
        rows = pl.ds(c * COMBINE_CHUNK, COMBINE_CHUNK)
        gate = gate_ref[rows, :]
        h3 = h2_ref[rows, :]
        packed = [_load_rows(buf.at[slot, k, rows]) for k in range(TOP_K)]
        for t in range(c * COMBINE_CHUNK, (c + 1) * COMBINE_CHUNK):
            for k in range(TOP_K):
                row_copy(dest_next_ref, other, t, k).start(priority=k % 2)
        for k in range(TOP_K):
            h3 = h3 + gate[:, k:k + 1] * _unpack_bf16_pairs(packed[k])
        o_ref[rows, :] = _rms(h3, nfw)

    @pl.when(i == n_steps - 1)
    def _():
        wait_slot(other)


def _combine(dest_flat, gates, h2, norm_f_w, ys, tile):
    rows = h2.shape[0]
    n_steps = rows // tile
    return pl.pallas_call(
        functools.partial(_combine_body, tile=tile, n_steps=n_steps),
        grid=(n_steps,),
        in_specs=[
            pl.BlockSpec((tile * TOP_K,), lambda i: (i,), memory_space=pltpu.SMEM),
            pl.BlockSpec((tile * TOP_K,), lambda i: (jnp.minimum(i + 1, n_steps - 1),),
                         memory_space=pltpu.SMEM),
            pl.BlockSpec((tile, TOP_K), lambda i: (i, 0)),
            pl.BlockSpec((tile, D_MODEL), lambda i: (i, 0)),
            pl.BlockSpec((1, D_MODEL), lambda i: (0, 0)),
            pl.BlockSpec(memory_space=pl.ANY),
        ],
        out_specs=pl.BlockSpec((tile, D_MODEL), lambda i: (i, 0)),
        out_shape=jax.ShapeDtypeStruct((rows, D_MODEL), F32),
        scratch_shapes=[pltpu.VMEM((2, TOP_K, tile, ROW_SUB, LANES), U32), pltpu.SemaphoreType.DMA((2,))],
        compiler_params=pltpu.CompilerParams(
            dimension_semantics=("arbitrary",), vmem_limit_bytes=VMEM_LIMIT),
        name="combine_final",
    )(dest_flat, dest_flat, gates, h2, norm_f_w, ys)


def _expert_tables(counts, n_assign, tile):
    n_work = n_assign // tile + N_EXPERTS
    nb = (counts + tile - 1) // tile
    bend = jnp.cumsum(nb)
    total = bend[-1]
    pstarts = (bend - nb) * tile
    blk = jnp.minimum(jnp.arange(n_work, dtype=I32), total - 1).astype(I32)
    ex = jnp.minimum(jnp.sum(bend[None, :] <= blk[:, None], axis=1), N_EXPERTS - 1).astype(I32)
    prev_ex = jnp.concatenate([jnp.full((1,), -1, I32), ex[:-1]])
    new_ex = (ex != prev_ex).astype(I32)
    fill_start = (pstarts + counts).astype(I32)
    fill_len = (nb * tile - counts).astype(I32)
    return (blk, ex, new_ex, total.reshape(1).astype(I32)), pstarts, fill_start, fill_len, n_work


def kernel(x, meta, norm1_w, w_in, gla_wa2_f, gla_ba2_f, gla_wa2_b, gla_ba2_b, gla_norm_w, conv_w, conv_b, dt_bias_f, dt_bias_b, a_log_f, a_log_b, ssd_d, ssd_norm_w, w_out, norm2_w, w_router, b_router, w_gu, b_gu, w_dn, b_dn, norm_f_w):
    bsz, seq, d = x.shape
    n_tok = bsz * seq
    l = 0

    wi = w_in[l]
    a_cols = wi[:, 3072:3104]
    dt_cols = wi[:, 5664:5696]
    w_perm = jnp.concatenate(
        [wi[:, :3072], wi[:, 3104:5664], a_cols, dt_cols,
         jnp.zeros((d, N_PROJ - COL_SMALL - 64), F32)], axis=1).astype(BF16)

    def lane_rows(w, lane0):
        return jnp.zeros((LANES, w.shape[1]), F32).at[lane0:lane0 + w.shape[0]].set(w)

    wa_f = lane_rows(gla_wa2_f[l], LANE_AF).astype(BF16)
    wa_b = lane_rows(gla_wa2_b[l], LANE_AB).astype(BF16)
    ba_f = gla_ba2_f[l][None, :]
    ba_b = gla_ba2_b[l][None, :]

    def lane_vec(vf, vb):
        z = jnp.zeros((1, LANES), F32)
        return z.at[0, LANE_DTF:LANE_DTF + SSD_HEADS].set(vf).at[0, LANE_DTB:LANE_DTB + SSD_HEADS].set(vb)

    dtb = lane_vec(dt_bias_f[l], dt_bias_b[l])
    alog = lane_vec(a_log_f[l], a_log_b[l])
    lane_id = (jnp.arange(2 * LANES) % LANES)[None, :, None]
    head_id = (jnp.arange(SSD_GROUP_CH) // SSD_HEAD_DIM)[None, None, :]
    grp = jnp.arange(SSD_GROUPS)[:, None, None] * SSD_GROUP_HEADS
    e_f = (lane_id == LANE_DTF + grp + head_id).astype(BF16)
    e_b = (lane_id == LANE_DTB + grp + head_id).astype(BF16)
    dexp = jnp.repeat(ssd_d[l], SSD_HEAD_DIM)[None, :]

    wr = jnp.zeros((d, LANES), F32).at[:, :N_EXPERTS].set(w_router[l])
    wr_hi = wr.astype(BF16)
    wr_lo = (wr - wr_hi.astype(F32)).astype(BF16)
    b_r = jnp.full((1, LANES), -1e30, F32).at[0, :N_EXPERTS].set(b_router[l])

    x2d = x.reshape(n_tok, d)
    x_meta = jnp.pad(meta.astype(F32), ((NPAD, 0), (0, 0)))
    n1 = norm1_w[l][None, :]
    proj = _inproj(x2d, n1, w_perm, PROJ_TILE).reshape(bsz, seq, N_PROJ)
    proj_meta = _inproj(x_meta, n1, w_perm, CHUNK)
    o_gla = _gla(proj, proj_meta, wa_f, ba_f, wa_b, ba_b, gla_norm_w[l][None, :])
    y_ssd = _ssd(proj, proj_meta, conv_w[l], conv_b[l][None, :], dtb, alog, e_f, e_b, dexp,
                 ssd_norm_w[l][None, :])

    wo = w_out[l].astype(BF16)
    h2, n2p, code, gates, cnt = _outproj(
        x2d, o_gla.reshape(n_tok, GLA_DV), y_ssd.reshape(n_tok, SSD_INNER),
        wo[:GLA_DV], wo[GLA_DV:], norm2_w[l][None, :], wr_hi, wr_lo, b_r, PROJ_TILE)

    counts = cnt[0, :N_EXPERTS].astype(I32)
    tables, starts, fill_start, fill_len, n_work = _expert_tables(counts, n_tok * TOP_K, EXPERT_TILE)
    expert_of = (code % N_EXPERTS)[..., None] == jnp.arange(N_EXPERTS, dtype=I32)
    dest = (code // N_EXPERTS + jnp.sum(jnp.where(expert_of, starts, 0), axis=-1)).astype(I32).reshape(-1)

    xs = _dispatch(fill_start, fill_len, dest, n2p, ROW_TILE, n_work * EXPERT_TILE)
    ys = _experts(tables, xs, w_gu[l], b_gu[l][:, None, :], w_dn[l], b_dn[l][:, None, :],
                  EXPERT_TILE, n_work)
    out = _combine(dest, gates, h2, norm_f_w[None, :], ys, COMBINE_TILE)
    return out.reshape(bsz, seq, d)
```

```python
import functools

import jax
import jax.numpy as jnp
from jax import lax
from jax.experimental import pallas as pl
from jax.experimental.pallas import tpu as pltpu

F32 = jnp.float32
BF16 = jnp.bfloat16
I32 = jnp.int32
U32 = jnp.uint32

D_MODEL = 1024
N_META = 16
CHUNK = 64
NPAD = CHUNK - N_META
GLA_HEADS = 4
GLA_DK = 512
GLA_DV = 1024
GLA_HEAD_K = GLA_DK // GLA_HEADS
GLA_HEAD_V = GLA_DV // GLA_HEADS
GLA_RANK = 16
GLA_GATE_NORM = 16.0
SSD_INNER = 1024
SSD_HEAD_DIM = 64
SSD_HEADS = SSD_INNER // SSD_HEAD_DIM
SSD_GROUPS = 2
SSD_GROUP_HEADS = SSD_HEADS // SSD_GROUPS
SSD_GROUP_CH = SSD_INNER // SSD_GROUPS
SSD_STATE = 128
SSD_CONV = 5
N_EXPERTS = 32
TOP_K = 4
D_FF = 1024
SWIGLU_LIMIT = 7.0
SWIGLU_ALPHA = 1.702
EPS = 1e-6

LANES = 128
ROW_WORDS = D_MODEL // 2
ROW_SUB = ROW_WORDS // LANES
HALO = 16

COL_Q = 0
COL_K = 512
COL_V = 1024
COL_G = 2048
COL_Z = 3072
COL_X = 4096
COL_B = 5120
COL_C = 5376
COL_SMALL = 5632
N_PROJ = 5760
LANE_AF = 0
LANE_AB = 16
LANE_DTF = 32
LANE_DTB = 48

ROW_TILE = 512
PROJ_TILE = 1024
EXPERT_TILE = 512
EXPERT_SUBTILES = 2
ROUTER_SUBTILES = 2
COMBINE_TILE = 256
COMBINE_CHUNK = 16
VMEM_LIMIT = 56 * 1024 * 1024


def _dot(a, b):
    return jnp.dot(a, b, preferred_element_type=F32)


def _dot_nt(a, b):
    return lax.dot_general(a, b, (((1,), (1,)), ((), ())), preferred_element_type=F32)


def _dot_tn(a, b):
    return lax.dot_general(a, b, (((0,), (0,)), ((), ())), preferred_element_type=F32)


def _split(x):
    hi = x.astype(BF16)
    lo = (x - hi.astype(F32)).astype(BF16)
    return hi, lo


def _sel_dot_l(m01, x):
    hi, lo = _split(x)
    return _dot(m01, jnp.concatenate([hi, lo], axis=0))


def _sel_dot_r(x, m01):
    hi, lo = _split(x)
    return _dot(jnp.concatenate([hi, lo], axis=1), m01)


def _twice_cols(m):
    return jnp.concatenate([m, m], axis=1)


def _softplus(x):
    return jnp.maximum(x, 0.0) + jnp.log(1.0 + jnp.exp(-jnp.abs(x)))


def _log_sigmoid(x):
    return jnp.minimum(x, 0.0) - jnp.log(1.0 + jnp.exp(-jnp.abs(x)))


def _silu(x):
    return x * jax.nn.sigmoid(x)


def _rms(x, w):
    return x * lax.rsqrt(jnp.mean(x * x, axis=-1, keepdims=True) + EPS) * w


def _pack_bf16_pairs(x):
    w = x.shape[1] // 2
    return pltpu.pack_elementwise([x[:, :w], x[:, w:]], packed_dtype=BF16)


def _unpack_bf16_pairs(p):
    lo, hi = (pltpu.unpack_elementwise(p, index=i, packed_dtype=BF16, unpacked_dtype=F32) for i in range(2))
    return jnp.concatenate([lo, hi], axis=1)


def _tokens(first, count):
    if not isinstance(first, int):
        first = pl.multiple_of(first * ROW_SUB, ROW_SUB)
    else:
        first = first * ROW_SUB
    return pl.ds(first, count * ROW_SUB)


def _load_rows(ref2):
    r = ref2.shape[0] // ROW_SUB
    return jnp.concatenate([ref2[pl.ds(j, r, stride=ROW_SUB), :] for j in range(ROW_SUB)], axis=1)


def _store_rows(ref2, val):
    r = ref2.shape[0] // ROW_SUB
    for j in range(ROW_SUB):
        ref2[pl.ds(j, r, stride=ROW_SUB), :] = val[:, j * LANES:(j + 1) * LANES]


def _tri(n, *, lower, inclusive):
    r = lax.broadcasted_iota(I32, (n, n), 0)
    c = lax.broadcasted_iota(I32, (n, n), 1)
    if lower:
        return (c <= r) if inclusive else (c < r)
    return (c >= r) if inclusive else (c > r)


def _inproj_body(x_ref, nw_ref, w_ref, o_ref, *, col_chunks):
    xn = _rms(x_ref[...], nw_ref[...]).astype(BF16)
    for lo, hi in col_chunks:
        o_ref[:, lo:hi] = _dot(xn, w_ref[:, lo:hi]).astype(BF16)


def _inproj(x2d, norm_w, w_perm, tile):
    rows = x2d.shape[0]
    tile = min(tile, rows)
    col_chunks = tuple((c, min(c + 1024, N_PROJ)) for c in range(0, N_PROJ, 1024))
    return pl.pallas_call(
        functools.partial(_inproj_body, col_chunks=col_chunks),
        grid=(rows // tile,),
        in_specs=[
            pl.BlockSpec((tile, D_MODEL), lambda i: (i, 0)),
            pl.BlockSpec((1, D_MODEL), lambda i: (0, 0)),
            pl.BlockSpec((D_MODEL, N_PROJ), lambda i: (0, 0), pipeline_mode=pl.Buffered(1)),
        ],
        out_specs=pl.BlockSpec((tile, N_PROJ), lambda i: (i, 0)),
        out_shape=jax.ShapeDtypeStruct((rows, N_PROJ), BF16),
        compiler_params=pltpu.CompilerParams(
            dimension_semantics=("arbitrary",), vmem_limit_bytes=VMEM_LIMIT),
        name="inproj",
    )(x2d, norm_w, w_perm)


GLA_HEADS_PER_STEP = 2
GLA_STEPS_PER_ITER = 8
SSD_STEPS_PER_ITER = 8


def _gla_body(q_ref, k_ref, v_ref, g_ref, sm_ref, qm_ref, km_ref, vm_ref, smm_ref,
              waf_ref, baf_ref, wab_ref, bab_ref, nw_ref, o_ref, st_ref, *, n_chunks):
    heads = GLA_HEADS_PER_STEP
    nprob = 2 * heads
    rows_all = nprob * CHUNK
    wk = heads * GLA_HEAD_K
    dk, dv = GLA_HEAD_K, GLA_HEAD_V

    tri_f = _tri(CHUNK, lower=True, inclusive=True)
    r2 = lax.broadcasted_iota(I32, (2 * CHUNK, 2 * CHUNK), 0)
    c2 = lax.broadcasted_iota(I32, (2 * CHUNK, 2 * CHUNK), 1)
    cum2 = ((r2 < CHUNK) & (c2 <= r2)) | ((r2 >= CHUNK) & (c2 >= r2))
    tri2 = _twice_cols(cum2.astype(BF16))
    ra = lax.broadcasted_iota(I32, (rows_all, rows_all), 0)
    ca = lax.broadcasted_iota(I32, (rows_all, rows_all), 1)
    same = (ra // CHUNK) == (ca // CHUNK)
    att_mask = same & (((ra < heads * CHUNK) & (ca <= ra)) | ((ra >= heads * CHUNK) & (ca > ra)))
    rb_ = lax.broadcasted_iota(I32, (rows_all, nprob * dk), 0) // CHUNK
    cb_ = lax.broadcasted_iota(I32, (rows_all, nprob * dk), 1) // dk
    own = rb_ == cb_

    waf, baf = waf_ref[...], baf_ref[...]
    wab, bab = wab_ref[...], bab_ref[...]
    wa_cat = jnp.concatenate([waf, wab], axis=1)
    nw = nw_ref[...]
    zero16 = jnp.zeros((), BF16)

    st_ref[...] = jnp.zeros_like(st_ref)

    meta_rows = lax.broadcasted_iota(I32, (CHUNK, wk), 0) >= NPAD
    lg = _log_sigmoid(_dot(smm_ref[...], waf) + baf) * (1.0 / GLA_GATE_NORM)
    bm = _sel_dot_l(_twice_cols(tri_f.astype(BF16)), jnp.where(meta_rows, lg, 0.0))
    for h in range(heads):
        lanes = slice(h * dk, (h + 1) * dk)
        bh, toth = bm[:, lanes], bm[CHUNK - 1:CHUNK, lanes]
        kend = (km_ref[:, lanes].astype(F32) * jnp.exp(toth - bh)).astype(BF16)
        st_ref[:, lanes] = _dot_tn(vm_ref[:, h * dv:(h + 1) * dv], kend)

    def stack(f, b, w):
        return jnp.concatenate([f[:, h * w:(h + 1) * w] for h in range(heads)]
                               + [b[:, h * w:(h + 1) * w] for h in range(heads)], axis=0)

    def steps(j, finalize):
        ids = [j * GLA_STEPS_PER_ITER + s for s in range(GLA_STEPS_PER_ITER)]
        rfs = [pl.ds(pl.multiple_of(i * CHUNK, CHUNK), CHUNK) for i in ids]
        rbs = [pl.ds(pl.multiple_of((n_chunks - 1 - i) * CHUNK, CHUNK), CHUNK) for i in ids]
        zs = [_dot(jnp.concatenate([sm_ref[rf, :], sm_ref[rb, :]], axis=0), wa_cat) for rf, rb in zip(rfs, rbs)]
        b2s = []
        for z in zs:
            lg = jnp.concatenate([z[:CHUNK, :wk] + baf, z[CHUNK:, wk:] + bab], axis=0)
            b2s.append(_sel_dot_l(tri2, _log_sigmoid(lg) * (1.0 / GLA_GATE_NORM)))
        vss, qds, kends, tots, gs = [], [], [], [], []
        for b2, rf, rb in zip(b2s, rfs, rbs):
            tot_f, tot_b = b2[CHUNK - 1:CHUNK], b2[CHUNK:CHUNK + 1]
            bst = stack(b2[:CHUNK], b2[CHUNK:], dk)
            tst = stack(jnp.broadcast_to(tot_f, (CHUNK, wk)), jnp.broadcast_to(tot_b, (CHUNK, wk)), dk)
            qs = stack(q_ref[rf, :], q_ref[rb, :], dk).astype(F32)
            ks = stack(k_ref[rf, :], k_ref[rb, :], dk).astype(F32)
            qd = (qs * (GLA_HEAD_K ** -0.5) * jnp.exp(bst)).astype(BF16)
            kd = (ks * jnp.exp(-bst)).astype(BF16)
            vss.append(stack(v_ref[rf, :], v_ref[rb, :], dv))
            qds.append(qd)
            kends.append((ks * jnp.exp(tst - bst)).astype(BF16))
            tots.append(jnp.concatenate([tot_f, tot_b], axis=1))
            gs.append(_dot_nt(qd, kd))
        intras = [_dot(jnp.where(att_mask, g, 0.0).astype(BF16), vs) for g, vs in zip(gs, vss)]
        upds = [_dot_tn(vs, jnp.where(own, jnp.concatenate([kend] * nprob, axis=1), zero16))
                for vs, kend in zip(vss, kends)]
        st = st_ref[...]
        outs = []
        for qd, tot, upd, intra in zip(qds, tots, upds, intras):
            qd_own = jnp.where(own, jnp.concatenate([qd] * nprob, axis=1), zero16)
            outs.append(intra + _dot_nt(qd_own, st.astype(BF16)))
            st = st * jnp.exp(tot) + upd
        st_ref[...] = st
        for out, rf, rb in zip(outs, rfs, rbs):
            for p in range(nprob):
                rows, h = (rf if p < heads else rb), p % heads
                o = out[p * CHUNK:(p + 1) * CHUNK]
                cols = slice(h * dv, (h + 1) * dv)
                if finalize:
                    o = o + o_ref[rows, cols].astype(F32)
                    gate = _silu(g_ref[rows, cols].astype(F32))
                    o_ref[rows, cols] = (_rms(o, nw) * gate).astype(BF16)
                else:
                    o_ref[rows, cols] = o.astype(BF16)

    def first_half(j, carry):
        steps(j, False)
        return carry

    def second_half(j, carry):
        steps(j, True)
        return carry

    half_iters = n_chunks // 2 // GLA_STEPS_PER_ITER
    lax.fori_loop(0, half_iters, first_half, 0)
    lax.fori_loop(half_iters, 2 * half_iters, second_half, 0)


def _gla(proj, proj_meta, wa_f, ba_f, wa_b, ba_b, norm_w):
    bsz, seq, _ = proj.shape
    assert (seq // CHUNK) % (2 * GLA_STEPS_PER_ITER) == 0
    hp = GLA_HEADS_PER_STEP
    wk, wv = hp * GLA_HEAD_K, hp * GLA_HEAD_V
    kb, vb, gb, sb = COL_K // wk, COL_V // wv, COL_G // wv, COL_SMALL // LANES
    real = lambda width, base: pl.BlockSpec((None, seq, width), lambda b, h: (b, 0, base + h))
    meta = lambda width, base: pl.BlockSpec((CHUNK, width), lambda b, h: (0, base + h))
    per_step = lambda rows_: pl.BlockSpec((rows_, wk), lambda b, h: (0, h))
    return pl.pallas_call(
        functools.partial(_gla_body, n_chunks=seq // CHUNK),
        grid=(bsz, GLA_HEADS // hp),
        in_specs=[
            real(wk, 0), real(wk, kb), real(wv, vb), real(wv, gb),
            pl.BlockSpec((None, seq, LANES), lambda b, h: (b, 0, sb)),
            meta(wk, 0), meta(wk, kb), meta(wv, vb),
            pl.BlockSpec((CHUNK, LANES), lambda b, h: (0, sb)),
            per_step(LANES), per_step(1), per_step(LANES), per_step(1),
            pl.BlockSpec((1, GLA_HEAD_V), lambda b, h: (0, 0)),
        ],
        out_specs=pl.BlockSpec((None, seq, wv), lambda b, h: (b, 0, h)),
        out_shape=jax.ShapeDtypeStruct((bsz, seq, GLA_DV), BF16),
        scratch_shapes=[pltpu.VMEM((GLA_HEAD_V, 2 * hp * GLA_HEAD_K), F32)],
        compiler_params=pltpu.CompilerParams(
            dimension_semantics=("arbitrary", "arbitrary"), vmem_limit_bytes=VMEM_LIMIT),
        name="gla",
    )(proj, proj, proj, proj, proj, proj_meta, proj_meta, proj_meta, proj_meta,
      wa_f, ba_f, wa_b, ba_b, norm_w)


def _conv_silu(win, cw, cb):
    half = (SSD_CONV - 1) // 2
    acc = cb
    for j in range(SSD_CONV):
        lo = HALO - half + j
        acc = acc + win[lo:lo + CHUNK, :] * cw[j:j + 1, :]
    return _silu(acc)


def _ssd_body(x_ref, z_ref, b_ref, c_ref, sm_ref, xm_ref, bm_ref, cm_ref, smm_ref,
              cwx_ref, cbx_ref, cwb_ref, cbb_ref, cwc_ref, cbc_ref, dtb_ref, alog_ref,
              ef_ref, eb_ref, dexp_ref, nw_ref, o_ref, st_ref, xc_ref, bc_ref, cc_ref,
              *, n_chunks):
    gh, p, n = SSD_GROUP_HEADS, SSD_HEAD_DIM, SSD_STATE
    width = gh * p
    two = 2 * CHUNK
    lane_s = lax.broadcasted_iota(I32, (two, width), 1) % p
    row2 = lax.broadcasted_iota(I32, (two, width), 0)
    row_t = row2 % CHUNK
    irep2 = lane_s == row_t
    irep2_16 = irep2.astype(BF16)
    pair_mask = ((row2 < CHUNK) & (lane_s <= row_t)) | ((row2 >= CHUNK) & (lane_s > row_t))
    r2 = lax.broadcasted_iota(I32, (two, two), 0)
    c2 = lax.broadcasted_iota(I32, (two, two), 1)
    same_dir = (r2 < CHUNK) == (c2 < CHUNK)
    ones2 = _twice_cols(same_dir.astype(BF16))
    cum2 = ((r2 < CHUNK) & (c2 <= r2)) | ((r2 >= CHUNK) & (c2 >= r2))
    tri2 = _twice_cols(cum2.astype(BF16))
    bd_r = lax.broadcasted_iota(I32, (width, width), 0) // p
    bd_c = lax.broadcasted_iota(I32, (width, width), 1) // p
    bdmask = bd_r == bd_c
    zero16 = jnp.zeros((), BF16)

    cwx, cbx = cwx_ref[...], cbx_ref[...]
    cwb, cbb = cwb_ref[...], cbb_ref[...]
    cwc, cbc = cwc_ref[...], cbc_ref[...]
    dtb = dtb_ref[...]
    a_row = -jnp.exp(alog_ref[...])
    ef, eb = ef_ref[...], eb_ref[...]
    dexp, nw = dexp_ref[...], nw_ref[...]
    seq = n_chunks * CHUNK

    def window(ref, mref, r):
        off = pl.multiple_of(r * CHUNK, CHUNK)
        poff = pl.multiple_of(jnp.maximum(off - HALO, 0), HALO)
        noff = pl.multiple_of(jnp.minimum(off + CHUNK, seq - HALO), HALO)
        prev = jnp.where(r == 0, mref[CHUNK - HALO:, :], ref[pl.ds(poff, HALO), :])
        nxt = jnp.where(r == n_chunks - 1, zero16, ref[pl.ds(noff, HALO), :])
        return jnp.concatenate([prev, ref[pl.ds(off, CHUNK), :], nxt], axis=0)

    half = (SSD_CONV - 1) // 2
    side_taps = [j for j in range(SSD_CONV) if j != half]
    win_rows = CHUNK + 2 * HALO
    sr = lax.broadcasted_iota(I32, (len(side_taps) * CHUNK, win_rows), 0)
    sc = lax.broadcasted_iota(I32, (len(side_taps) * CHUNK, win_rows), 1)
    tap_of = sr // CHUNK
    tap_shift = jnp.where(tap_of < half, tap_of, tap_of + 1) - half
    shift_mat = (sc == (sr % CHUNK) + HALO + tap_shift).astype(BF16)
    cw_all = jnp.concatenate([cwx, cwb, cwc], axis=1)
    cb_all = jnp.concatenate([cbx, cbb, cbc], axis=1)

    def conv_body(it, carry):
        chunks = [it * SSD_STEPS_PER_ITER + s for s in range(SSD_STEPS_PER_ITER)]
        wins = [jnp.concatenate([window(x_ref, xm_ref, r), window(b_ref, bm_ref, r), window(c_ref, cm_ref, r)],
                                axis=1) for r in chunks]
        shifts = [_dot(shift_mat, win) for win in wins]
        for r, win, shifted in zip(chunks, wins, shifts):
            rows = pl.ds(pl.multiple_of(r * CHUNK, CHUNK), CHUNK)
            acc = cb_all + win[HALO:HALO + CHUNK].astype(F32) * cw_all[half:half + 1]
            for pos, j in enumerate(side_taps):
                acc = acc + shifted[pos * CHUNK:(pos + 1) * CHUNK] * cw_all[j:j + 1]
            y = _silu(acc).astype(BF16)
            xc_ref[rows, :] = y[:, :width]
            bc_ref[rows, :] = y[:, width:width + n]
            cc_ref[rows, :] = y[:, width + n:]
        return carry

    lax.fori_loop(0, n_chunks // SSD_STEPS_PER_ITER, conv_body, 0)

    st_ref[...] = jnp.zeros_like(st_ref)

    def meta_window(mref, ref):
        zeros = jnp.zeros((HALO, mref.shape[1]), F32)
        return jnp.concatenate([zeros, mref[...].astype(F32), ref[0:HALO, :].astype(F32)], axis=0)

    def meta_mask(width_):
        return lax.broadcasted_iota(I32, (CHUNK, width_), 0) >= NPAD

    xc = jnp.where(meta_mask(width), _conv_silu(meta_window(xm_ref, x_ref), cwx, cbx), 0.0)
    bc = jnp.where(meta_mask(n), _conv_silu(meta_window(bm_ref, b_ref), cwb, cbb), 0.0)
    dt = jnp.where(meta_mask(LANES), _softplus(smm_ref[...].astype(F32) + dtb), 0.0)
    cs = _sel_dot_l(_twice_cols(cum2[:CHUNK, :CHUNK].astype(BF16)), dt * a_row)
    both = _sel_dot_r(jnp.concatenate([cs, dt], axis=0), ef)
    cs_e, dt_e = both[:CHUNK], both[CHUNK:]
    xend = (xc * dt_e * jnp.exp(cs_e[CHUNK - 1:CHUNK] - cs_e)).astype(BF16)
    st_ref[0:n, :] = _dot_tn(bc.astype(BF16), xend)

    zeros_n = jnp.zeros((CHUNK, n), BF16)

    def own_dir(a):
        return jnp.concatenate([jnp.concatenate([a[:CHUNK], zeros_n], axis=1),
                                jnp.concatenate([zeros_n, a[CHUNK:]], axis=1)], axis=0)

    def steps(j, finalize):
        ids = [j * SSD_STEPS_PER_ITER + s for s in range(SSD_STEPS_PER_ITER)]
        rfs = [pl.ds(pl.multiple_of(i * CHUNK, CHUNK), CHUNK) for i in ids]
        rbs = [pl.ds(pl.multiple_of((n_chunks - 1 - i) * CHUNK, CHUNK), CHUNK) for i in ids]
        both_rows = lambda ref, rf, rb: jnp.concatenate([ref[rf, :], ref[rb, :]], axis=0)
        dts = [_softplus(both_rows(sm_ref, rf, rb).astype(F32) + dtb) for rf, rb in zip(rfs, rbs)]
        css = [_sel_dot_l(tri2, dt * a_row) for dt in dts]
        exs = [(_sel_dot_r(jnp.concatenate([cs[:CHUNK], dt[:CHUNK]], axis=0), ef),
                _sel_dot_r(jnp.concatenate([cs[CHUNK:], dt[CHUNK:]], axis=0), eb)) for cs, dt in zip(css, dts)]
        xcs, cs_es, tots, xdts, xends, cbs, bcs, ccs = [], [], [], [], [], [], [], []
        for (ex_f, ex_b), rf, rb in zip(exs, rfs, rbs):
            cs_e = jnp.concatenate([ex_f[:CHUNK], ex_b[:CHUNK]], axis=0)
            dt_e = jnp.concatenate([ex_f[CHUNK:], ex_b[CHUNK:]], axis=0)
            tot_f, tot_b = cs_e[CHUNK - 1:CHUNK], cs_e[CHUNK:CHUNK + 1]
            tot = jnp.concatenate([jnp.broadcast_to(tot_f, (CHUNK, width)),
                                   jnp.broadcast_to(tot_b, (CHUNK, width))], axis=0)
            xc = both_rows(xc_ref, rf, rb).astype(F32)
            bc16, cc16 = both_rows(bc_ref, rf, rb), both_rows(cc_ref, rf, rb)
            xdt = xc * dt_e
            xcs.append(xc)
            cs_es.append(cs_e)
            tots.append((tot_f, tot_b))
            xdts.append(xdt.astype(BF16))
            xends.append((xdt * jnp.exp(tot - cs_e)).astype(BF16))
            bcs.append(bc16)
            ccs.append(cc16)
            cbs.append(_dot_nt(cc16, bc16))
        cb_reps = [_dot(jnp.where(same_dir, cb, 0.0).astype(BF16), irep2_16) for cb in cbs]
        cs_rows = [_sel_dot_l(ones2, jnp.where(irep2, cs_e, 0.0)) for cs_e in cs_es]
        intras = []
        for cb_rep, cs_row, cs_e, xdt16 in zip(cb_reps, cs_rows, cs_es, xdts):
            decay = jnp.where(pair_mask, jnp.exp(jnp.minimum(cs_e - cs_row, 0.0)), 0.0)
            w = (cb_rep * decay).astype(BF16)
            intras.append(jnp.concatenate(
                [_dot(w[d * CHUNK:(d + 1) * CHUNK],
                      jnp.where(bdmask, jnp.concatenate([xdt16[d * CHUNK:(d + 1) * CHUNK]] * gh, axis=0), zero16))
                 for d in range(2)], axis=0))
        upds = [_dot_tn(own_dir(bc16), xend) for bc16, xend in zip(bcs, xends)]
        st = st_ref[...]
        ys = []
        for intra, cc16, cs_e, (tot_f, tot_b), upd in zip(intras, ccs, cs_es, tots, upds):
            ys.append(intra + _dot(own_dir(cc16), st.astype(BF16)) * jnp.exp(cs_e))
            grow = jnp.concatenate([jnp.broadcast_to(jnp.exp(tot_f), (n, width)),
                                    jnp.broadcast_to(jnp.exp(tot_b), (n, width))], axis=0)
            st = st * grow + upd
        st_ref[...] = st
        for y, xc, rf, rb in zip(ys, xcs, rfs, rbs):
            for d, rows in enumerate((rf, rb)):
                yd = y[d * CHUNK:(d + 1) * CHUNK]
                if finalize:
                    yd = yd + o_ref[rows, :].astype(F32) + xc[d * CHUNK:(d + 1) * CHUNK] * dexp
                    yd = yd * _silu(z_ref[rows, :].astype(F32))
                    o_ref[rows, :] = _rms(yd, nw).astype(BF16)
                else:
                    o_ref[rows, :] = yd.astype(BF16)

    def first_half(j, carry):
        steps(j, False)
        return carry

    def second_half(j, carry):
        steps(j, True)
        return carry

    half_iters = n_chunks // 2 // SSD_STEPS_PER_ITER
    lax.fori_loop(0, half_iters, first_half, 0)
    lax.fori_loop(half_iters, 2 * half_iters, second_half, 0)


def _ssd(proj, proj_meta, conv_w, conv_b, dtb, alog, e_f, e_b, dexp, norm_w):
    bsz, seq, _ = proj.shape
    gc = SSD_GROUP_CH
    xb, zb = COL_X // gc, COL_Z // gc
    bb, cb, sb = COL_B // SSD_STATE, COL_C // SSD_STATE, COL_SMALL // LANES
    real = lambda width, base: pl.BlockSpec((None, seq, width), lambda b, g: (b, 0, base + g))
    meta = lambda width, base: pl.BlockSpec((CHUNK, width), lambda b, g: (0, base + g))
    cpar = lambda rows_, width, base: pl.BlockSpec((rows_, width), lambda b, g: (0, base + g))
    cbb_, ccb_ = SSD_INNER // SSD_STATE, SSD_INNER // SSD_STATE + SSD_GROUPS
    const = lambda shape: pl.BlockSpec(shape, lambda b, g: (0, 0))
    return pl.pallas_call(
        functools.partial(_ssd_body, n_chunks=seq // CHUNK),
        grid=(bsz, SSD_GROUPS),
        in_specs=[
            real(gc, xb), real(gc, zb), real(SSD_STATE, bb), real(SSD_STATE, cb),
            pl.BlockSpec((None, seq, LANES), lambda b, g: (b, 0, sb)),
            meta(gc, xb), meta(SSD_STATE, bb), meta(SSD_STATE, cb),
            pl.BlockSpec((CHUNK, LANES), lambda b, g: (0, sb)),
            cpar(SSD_CONV, gc, 0), cpar(1, gc, 0),
            cpar(SSD_CONV, SSD_STATE, cbb_), cpar(1, SSD_STATE, cbb_),
            cpar(SSD_CONV, SSD_STATE, ccb_), cpar(1, SSD_STATE, ccb_),
            const((1, LANES)), const((1, LANES)),
            pl.BlockSpec((None, 2 * LANES, gc), lambda b, g: (g, 0, 0)),
            pl.BlockSpec((None, 2 * LANES, gc), lambda b, g: (g, 0, 0)),
            pl.BlockSpec((1, gc), lambda b, g: (0, g)),
            pl.BlockSpec((1, gc), lambda b, g: (0, g)),
        ],
        out_specs=pl.BlockSpec((None, seq, gc), lambda b, g: (b, 0, g)),
        out_shape=jax.ShapeDtypeStruct((bsz, seq, SSD_INNER), BF16),
        scratch_shapes=[pltpu.VMEM((2 * SSD_STATE, gc), F32),
                        pltpu.VMEM((seq, gc), BF16), pltpu.VMEM((seq, SSD_STATE), BF16),
                        pltpu.VMEM((seq, SSD_STATE), BF16)],
        compiler_params=pltpu.CompilerParams(
            dimension_semantics=("arbitrary", "arbitrary"), vmem_limit_bytes=VMEM_LIMIT),
        name="ssd",
    )(proj, proj, proj, proj, proj, proj_meta, proj_meta, proj_meta, proj_meta,
      conv_w, conv_b, conv_w, conv_b, conv_w, conv_b, dtb, alog, e_f, e_b, dexp, norm_w)


def _outproj_body(x_ref, og_ref, ys_ref, wo1_ref, wo2_ref, n2w_ref, wrh_ref, wrl_ref, br_ref,
                  h2_ref, n2p_ref, code_ref, gate_ref, cnt_ref, carry_ref, *, tile):
    i = pl.program_id(0)

    @pl.when(i == 0)
    def _():
        carry_ref[...] = jnp.zeros_like(carry_ref)

    sub = tile // ROUTER_SUBTILES
    parts = [pl.ds(s * sub, sub) for s in range(ROUTER_SUBTILES)]
    wo1, wo2, n2w = wo1_ref[...], wo2_ref[...], n2w_ref[...]
    wrh, wrl, br = wrh_ref[...], wrl_ref[...], br_ref[...]
    h2s = [x_ref[p, :] + _dot(og_ref[p, :], wo1) + _dot(ys_ref[p, :], wo2) for p in parts]
    n2s = []
    for p, h2 in zip(parts, h2s):
        h2_ref[p, :] = h2
        n2 = _rms(h2, n2w)
        _store_rows(n2p_ref.at[_tokens(p.start, sub)], _pack_bf16_pairs(n2))
        n2s.append(n2)
    logit_parts = []
    for n2 in n2s:
        nh, nl = _split(n2)
        logit_parts.append(_dot(nh, wrh) + _dot(nh, wrl) + _dot(nl, wrh) + br)

    lane = lax.broadcasted_iota(I32, (sub, LANES), 1)
    lane_f = lane.astype(F32)
    lane4 = lax.broadcasted_iota(I32, (sub, TOP_K), 1)
    per_row = LANES // TOP_K
    tok = lax.broadcasted_iota(I32, (sub, LANES), 0)
    here = (lane // TOP_K) == (tok % per_row)
    gather_rows = (lax.broadcasted_iota(I32, (sub // per_row, sub), 1) // per_row
                   == lax.broadcasted_iota(I32, (sub // per_row, sub), 0)).astype(BF16)
    before_me = _tri(sub, lower=True, inclusive=False).astype(BF16)

    routed = []
    for p, logits in zip(parts, logit_parts):
        vals, onehots, picks = [], [], []
        work = logits
        for k in range(TOP_K):
            m = jnp.max(work, axis=-1, keepdims=True)
            first = jnp.min(jnp.where(work == m, lane_f, float(LANES)), axis=-1, keepdims=True)
            oh = lane_f == first
            work = jnp.where(oh, -jnp.inf, work)
            vals.append(m)
            onehots.append(oh)
            picks.append(first)
        exps = [jnp.exp(v - vals[0]) for v in vals]
        inv = 1.0 / (exps[0] + exps[1] + exps[2] + exps[3])
        gate_out = jnp.zeros((sub, TOP_K), F32)
        for k in range(TOP_K):
            gate_out = jnp.where(lane4 == k, exps[k] * inv, gate_out)
        gate_ref[p, :] = gate_out
        any_oh = (onehots[0] | onehots[1] | onehots[2] | onehots[3])
        any16 = jnp.where(any_oh, 1.0, 0.0).astype(BF16)
        routed.append((onehots, picks, any16, _dot(before_me, any16)))

    carry = carry_ref[...]
    for s, (onehots, picks, any16, before) in enumerate(routed):
        before = before + carry
        carry = carry + jnp.sum(any16.astype(F32), axis=0, keepdims=True)
        rest = [jnp.sum(jnp.where(onehots[k], before, 0.0), axis=-1, keepdims=True) * N_EXPERTS + picks[k]
                for k in range(TOP_K)]
        flat = jnp.zeros((sub // per_row, LANES), F32)
        for scale in (65536.0, 256.0, 1.0):
            piece = [jnp.floor(c * (1.0 / scale)) for c in rest]
            rest = [c - q * scale for c, q in zip(rest, piece)]
            by_k = piece[TOP_K - 1]
            for k in range(TOP_K - 2, -1, -1):
                by_k = jnp.where(lane % TOP_K == k, piece[k], by_k)
            flat = flat + scale * _dot(gather_rows, jnp.where(here, by_k, 0.0).astype(BF16))
        code_ref[pl.ds(s * (sub // per_row), sub // per_row), :] = flat.astype(I32)
    carry_ref[...] = carry
    cnt_ref[...] = carry


def _outproj(x2d, o_gla, y_ssd, w_out1, w_out2, norm2_w, wr_hi, wr_lo, b_r, tile):
    rows = x2d.shape[0]
    tile = min(tile, rows)
    row = lambda width: pl.BlockSpec((tile, width), lambda i: (i, 0))
    const = lambda shape: pl.BlockSpec(shape, lambda i: (0, 0))
    return pl.pallas_call(
        functools.partial(_outproj_body, tile=tile),
        grid=(rows // tile,),
        in_specs=[
            row(D_MODEL), row(GLA_DV), row(SSD_INNER),
            const((GLA_DV, D_MODEL)), const((SSD_INNER, D_MODEL)), const((1, D_MODEL)),
            const((D_MODEL, LANES)), const((D_MODEL, LANES)), const((1, LANES)),
        ],
        out_specs=[
            row(D_MODEL), pl.BlockSpec((tile * ROW_SUB, LANES), lambda i: (i, 0)),
            pl.BlockSpec((tile * TOP_K // LANES, LANES), lambda i: (i, 0)), row(TOP_K), const((1, LANES)),
        ],
        out_shape=[
            jax.ShapeDtypeStruct((rows, D_MODEL), F32),
            jax.ShapeDtypeStruct((rows * ROW_SUB, LANES), U32),
            jax.ShapeDtypeStruct((rows * TOP_K // LANES, LANES), I32),
            jax.ShapeDtypeStruct((rows, TOP_K), F32),
            jax.ShapeDtypeStruct((1, LANES), F32),
        ],
        scratch_shapes=[pltpu.VMEM((1, LANES), F32)],
        compiler_params=pltpu.CompilerParams(
            dimension_semantics=("arbitrary",), vmem_limit_bytes=VMEM_LIMIT),
        name="outproj_router",
    )(x2d, o_gla, y_ssd, w_out1, w_out2, norm2_w, wr_hi, wr_lo, b_r)


def _row_copy(src, src_row, dst, dst_row, sem):
    return pltpu.make_async_copy(src.at[_tokens(src_row, 1)], dst.at[_tokens(dst_row, 1)], sem)


def _dispatch_body(fill_start_ref, fill_len_ref, dest_ref, n2p_ref, xs_hbm, zeros_ref, sem, fill_sem, *, tile):
    @pl.when(pl.program_id(0) == 0)
    def _():
        zeros_ref[...] = jnp.zeros_like(zeros_ref)

        def fill(e, size, wait):
            length = fill_len_ref[e]

            @pl.when((length & size) != 0)
            def _():
                pos = fill_start_ref[e] + (length & (size - 1))
                copy = pltpu.make_async_copy(zeros_ref.at[_tokens(0, size)], xs_hbm.at[_tokens(pos, size)], fill_sem)
                if wait:
                    copy.wait()
                else:
                    copy.start()

        for wait in (False, True):
            for e in range(N_EXPERTS):
                size = 1
                while size < tile:
                    fill(e, size, wait)
                    size *= 2

    def start(t, carry):
        for k in range(TOP_K):
            _row_copy(n2p_ref, t, xs_hbm, dest_ref[t * TOP_K + k], sem).start(priority=k % 2)
        return carry

    lax.fori_loop(0, tile, start, 0)

    for k in range(TOP_K):
        pltpu.make_async_copy(n2p_ref, xs_hbm.at[_tokens(0, tile)], sem).wait()


def _dispatch(fill_start, fill_len, dest_flat, n2p, tile, out_rows):
    rows = n2p.shape[0] // ROW_SUB
    grid_spec = pltpu.PrefetchScalarGridSpec(
        num_scalar_prefetch=2,
        grid=(rows // tile,),
        in_specs=[
            pl.BlockSpec((tile * TOP_K,), lambda i, fs, fl: (i,), memory_space=pltpu.SMEM),
            pl.BlockSpec((tile * ROW_SUB, LANES), lambda i, fs, fl: (i, 0)),
        ],
        out_specs=pl.BlockSpec(memory_space=pl.ANY),
        scratch_shapes=[pltpu.VMEM((tile // 2 * ROW_SUB, LANES), U32),
                        pltpu.SemaphoreType.DMA(()), pltpu.SemaphoreType.DMA(())],
    )
    return pl.pallas_call(
        functools.partial(_dispatch_body, tile=tile),
        grid_spec=grid_spec,
        out_shape=jax.ShapeDtypeStruct((out_rows * ROW_SUB, LANES), U32),
        compiler_params=pltpu.CompilerParams(dimension_semantics=("arbitrary",)),
        name="dispatch",
    )(fill_start, fill_len, dest_flat, n2p)


def _expert_body(blk_ref, exp_ref, newexp_ref, nw_ref,
                 x_ref, wgu_ref, bgu_ref, wdn_ref, bdn_ref, o_ref, wgu16_ref, wdn16_ref, *, tile):
    w = pl.program_id(0)
    cast_rows = 64

    @pl.when((w < nw_ref[0]) & (newexp_ref[w] == 1))
    def _():
        def cast(i, carry):
            rows = pl.ds(pl.multiple_of(i * cast_rows, cast_rows), cast_rows)
            wgu16_ref[rows, :] = wgu_ref[rows, :].astype(BF16)
            wdn16_ref[rows, :] = wdn_ref[rows, :].astype(BF16)
            return carry

        lax.fori_loop(0, D_MODEL // cast_rows, cast, 0)

    @pl.when(w < nw_ref[0])
    def _():
        sub = tile // EXPERT_SUBTILES
        parts = [_tokens(s * sub, sub) for s in range(EXPERT_SUBTILES)]
        xs = [_unpack_bf16_pairs(_load_rows(x_ref.at[p])).astype(BF16) for p in parts]
        gts = [jnp.minimum(_dot(x, wgu16_ref[:, :D_FF]) + bgu_ref[:, :D_FF], SWIGLU_LIMIT) for x in xs]
        ups = [jnp.clip(_dot(x, wgu16_ref[:, D_FF:]) + bgu_ref[:, D_FF:], -SWIGLU_LIMIT, SWIGLU_LIMIT)
               for x in xs]
        acts = [((up + 1.0) * gt * jax.nn.sigmoid(gt * SWIGLU_ALPHA)).astype(BF16) for gt, up in zip(gts, ups)]
        for p, act in zip(parts, acts):
            _store_rows(o_ref.at[p], _pack_bf16_pairs(_dot(act, wdn16_ref[...]) + bdn_ref[...]))


def _experts(tables, xs, w_gu, b_gu, w_dn, b_dn, tile, n_work):
    rows = xs.shape[0]
    row_block = pl.BlockSpec((tile * ROW_SUB, LANES), lambda w, blk, ex, ne, nw: (blk[w], 0))
    grid_spec = pltpu.PrefetchScalarGridSpec(
        num_scalar_prefetch=4,
        grid=(n_work,),
        in_specs=[
            row_block,
            pl.BlockSpec((None, D_MODEL, 2 * D_FF), lambda w, blk, ex, ne, nw: (ex[w], 0, 0)),
            pl.BlockSpec((None, 1, 2 * D_FF), lambda w, blk, ex, ne, nw: (ex[w], 0, 0)),
            pl.BlockSpec((None, D_FF, D_MODEL), lambda w, blk, ex, ne, nw: (ex[w], 0, 0)),
            pl.BlockSpec((None, 1, D_MODEL), lambda w, blk, ex, ne, nw: (ex[w], 0, 0)),
        ],
        out_specs=row_block,
        scratch_shapes=[pltpu.VMEM((D_MODEL, 2 * D_FF), BF16), pltpu.VMEM((D_FF, D_MODEL), BF16)],
    )
    return pl.pallas_call(
        functools.partial(_expert_body, tile=tile),
        grid_spec=grid_spec,
        out_shape=jax.ShapeDtypeStruct(xs.shape, U32),
        compiler_params=pltpu.CompilerParams(
            dimension_semantics=("arbitrary",), vmem_limit_bytes=VMEM_LIMIT),
        name="experts",
    )(*tables, xs, w_gu, b_gu, w_dn, b_dn)


def _combine_body(dest_ref, dest_next_ref, gate_ref, h2_ref, nfw_ref, ys_hbm, o_ref, buf, sems,
                  *, tile, n_steps):
    i = pl.program_id(0)
    slot = i % 2
    other = 1 - slot

    def row_copy(d_ref, s, t, k):
        return pltpu.make_async_copy(ys_hbm.at[_tokens(d_ref[t * TOP_K + k], 1)], buf.at[s, k, _tokens(t, 1)], sems.at[s])

    def wait_slot(s):
        for k in range(TOP_K):
            pltpu.make_async_copy(ys_hbm.at[_tokens(0, tile)], buf.at[s, k], sems.at[s]).wait()

    @pl.when(i == 0)
    def _():
        def start(t, carry):
            for k in range(TOP_K):
                row_copy(dest_ref, 0, t, k).start(priority=k % 2)
            return carry

        lax.fori_loop(0, tile, start, 0)

    wait_slot(slot)
    nfw = nfw_ref[...]
    for c in range(tile // COMBINE_CHUNK):
        rows = pl.ds(c * COMBINE_CHUNK, COMBINE_CHUNK)
        gate = gate_ref[rows, :]
        h3 = h2_ref[rows, :]
        packed = [_load_rows(buf.at[slot, k, _tokens(c * COMBINE_CHUNK, COMBINE_CHUNK)]) for k in range(TOP_K)]
        for t in range(c * COMBINE_CHUNK, (c + 1) * COMBINE_CHUNK):
            for k in range(TOP_K):
                row_copy(dest_next_ref, other, t, k).start(priority=k % 2)
        for k in range(TOP_K):
            h3 = h3 + gate[:, k:k + 1] * _unpack_bf16_pairs(packed[k])
        o_ref[rows, :] = _rms(h3, nfw)

    @pl.when(i == n_steps - 1)
    def _():
        wait_slot(other)


def _combine(dest_flat, gates, h2, norm_f_w, ys, tile):
    rows = h2.shape[0]
    n_steps = rows // tile
    return pl.pallas_call(
        functools.partial(_combine_body, tile=tile, n_steps=n_steps),
        grid=(n_steps,),
        in_specs=[
            pl.BlockSpec((tile * TOP_K,), lambda i: (i,), memory_space=pltpu.SMEM),
            pl.BlockSpec((tile * TOP_K,), lambda i: (jnp.minimum(i + 1, n_steps - 1),),
                         memory_space=pltpu.SMEM),
            pl.BlockSpec((tile, TOP_K), lambda i: (i, 0)),
            pl.BlockSpec((tile, D_MODEL), lambda i: (i, 0)),
            pl.BlockSpec((1, D_MODEL), lambda i: (0, 0)),
            pl.BlockSpec(memory_space=pl.ANY),
        ],
        out_specs=pl.BlockSpec((tile, D_MODEL), lambda i: (i, 0)),
        out_shape=jax.ShapeDtypeStruct((rows, D_MODEL), F32),
        scratch_shapes=[pltpu.VMEM((2, TOP_K, tile * ROW_SUB, LANES), U32), pltpu.SemaphoreType.DMA((2,))],
        compiler_params=pltpu.CompilerParams(
            dimension_semantics=("arbitrary",), vmem_limit_bytes=VMEM_LIMIT),
        name="combine_final",
    )(dest_flat, dest_flat, gates, h2, norm_f_w, ys)


def _expert_tables(counts, n_assign, tile):
    n_work = n_assign // tile + N_EXPERTS
    nb = (counts + tile - 1) // tile
    bend = jnp.cumsum(nb)
    total = bend[-1]
    pstarts = (bend - nb) * tile
    blk = jnp.minimum(jnp.arange(n_work, dtype=I32), total - 1).astype(I32)
    ex = jnp.minimum(jnp.sum(bend[None, :] <= blk[:, None], axis=1), N_EXPERTS - 1).astype(I32)
    prev_ex = jnp.concatenate([jnp.full((1,), -1, I32), ex[:-1]])
    new_ex = (ex != prev_ex).astype(I32)
    fill_start = (pstarts + counts).astype(I32)
    fill_len = (nb * tile - counts).astype(I32)
    return (blk, ex, new_ex, total.reshape(1).astype(I32)), pstarts, fill_start, fill_len, n_work


def kernel(x, meta, norm1_w, w_in, gla_wa2_f, gla_ba2_f, gla_wa2_b, gla_ba2_b, gla_norm_w, conv_w, conv_b, dt_bias_f, dt_bias_b, a_log_f, a_log_b, ssd_d, ssd_norm_w, w_out, norm2_w, w_router, b_router, w_gu, b_gu, w_dn, b_dn, norm_f_w):
    bsz, seq, d = x.shape
    n_tok = bsz * seq
    l = 0

    wi = w_in[l]
    a_cols = wi[:, 3072:3104]
    dt_cols = wi[:, 5664:5696]
    w_perm = jnp.concatenate(
        [wi[:, :3072], wi[:, 3104:5664], a_cols, dt_cols,
         jnp.zeros((d, N_PROJ - COL_SMALL - 64), F32)], axis=1).astype(BF16)

    def lane_rows(w, lane0):
        return jnp.zeros((LANES, w.shape[1]), F32).at[lane0:lane0 + w.shape[0]].set(w)

    wa_f = lane_rows(gla_wa2_f[l], LANE_AF).astype(BF16)
    wa_b = lane_rows(gla_wa2_b[l], LANE_AB).astype(BF16)
    ba_f = gla_ba2_f[l][None, :]
    ba_b = gla_ba2_b[l][None, :]

    def lane_vec(vf, vb):
        z = jnp.zeros((1, LANES), F32)
        return z.at[0, LANE_DTF:LANE_DTF + SSD_HEADS].set(vf).at[0, LANE_DTB:LANE_DTB + SSD_HEADS].set(vb)

    dtb = lane_vec(dt_bias_f[l], dt_bias_b[l])
    alog = lane_vec(a_log_f[l], a_log_b[l])
    lane_id = (jnp.arange(2 * LANES) % LANES)[None, :, None]
    head_id = (jnp.arange(SSD_GROUP_CH) // SSD_HEAD_DIM)[None, None, :]
    grp = jnp.arange(SSD_GROUPS)[:, None, None] * SSD_GROUP_HEADS
    e_f = (lane_id == LANE_DTF + grp + head_id).astype(BF16)
    e_b = (lane_id == LANE_DTB + grp + head_id).astype(BF16)
    dexp = jnp.repeat(ssd_d[l], SSD_HEAD_DIM)[None, :]

    wr = jnp.zeros((d, LANES), F32).at[:, :N_EXPERTS].set(w_router[l])
    wr_hi = wr.astype(BF16)
    wr_lo = (wr - wr_hi.astype(F32)).astype(BF16)
    b_r = jnp.full((1, LANES), -1e30, F32).at[0, :N_EXPERTS].set(b_router[l])

    x2d = x.reshape(n_tok, d)
    x_meta = jnp.pad(meta.astype(F32), ((NPAD, 0), (0, 0)))
    n1 = norm1_w[l][None, :]
    proj = _inproj(x2d, n1, w_perm, PROJ_TILE).reshape(bsz, seq, N_PROJ)
    proj_meta = _inproj(x_meta, n1, w_perm, CHUNK)
    o_gla = _gla(proj, proj_meta, wa_f, ba_f, wa_b, ba_b, gla_norm_w[l][None, :])
    y_ssd = _ssd(proj, proj_meta, conv_w[l], conv_b[l][None, :], dtb, alog, e_f, e_b, dexp,
                 ssd_norm_w[l][None, :])

    wo = w_out[l].astype(BF16)
    h2, n2p, code, gates, cnt = _outproj(
        x2d, o_gla.reshape(n_tok, GLA_DV), y_ssd.reshape(n_tok, SSD_INNER),
        wo[:GLA_DV], wo[GLA_DV:], norm2_w[l][None, :], wr_hi, wr_lo, b_r, PROJ_TILE)

    counts = cnt[0, :N_EXPERTS].astype(I32)
    tables, starts, fill_start, fill_len, n_work = _expert_tables(counts, n_tok * TOP_K, EXPERT_TILE)
    expert_of = (code % N_EXPERTS)[..., None] == jnp.arange(N_EXPERTS, dtype=I32)
    dest = (code // N_EXPERTS + jnp.sum(jnp.where(expert_of, starts, 0), axis=-1)).astype(I32).reshape(-1)

    xs = _dispatch(fill_start, fill_len, dest, n2p, ROW_TILE, n_work * EXPERT_TILE)
    ys = _experts(tables, xs, w_gu[l], b_gu[l][:, None, :], w_dn[l], b_dn[l][:, None, :],
                  EXPERT_TILE, n_work)
    out = _combine(dest, gates, h2, norm_f_w[None, :], ys, COMBINE_TILE)
    return out.reshape(bsz, seq, d)
```

```python
import functools

import jax
import jax.numpy as jnp
from jax import lax
from jax.experimental import pallas as pl
from jax.experimental.pallas import tpu as pltpu

F32 = jnp.float32
BF16 = jnp.bfloat16
I32 = jnp.int32
U32 = jnp.uint32

D_MODEL = 1024
N_META = 16
CHUNK = 64
NPAD = CHUNK - N_META
GLA_HEADS = 4
GLA_DK = 512
GLA_DV = 1024
GLA_HEAD_K = GLA_DK // GLA_HEADS
GLA_HEAD_V = GLA_DV // GLA_HEADS
GLA_RANK = 16
GLA_GATE_NORM = 16.0
SSD_INNER = 1024
SSD_HEAD_DIM = 64
SSD_HEADS = SSD_INNER // SSD_HEAD_DIM
SSD_GROUPS = 2
SSD_GROUP_HEADS = SSD_HEADS // SSD_GROUPS
SSD_GROUP_CH = SSD_INNER // SSD_GROUPS
SSD_STATE = 128
SSD_CONV = 5
N_EXPERTS = 32
TOP_K = 4
D_FF = 1024
SWIGLU_LIMIT = 7.0
SWIGLU_ALPHA = 1.702
EPS = 1e-6

LANES = 128
ROW_WORDS = D_MODEL // 2
ROW_SUB = ROW_WORDS // LANES
HALO = 16

COL_Q = 0
COL_K = 512
COL_V = 1024
COL_G = 2048
COL_Z = 3072
COL_X = 4096
COL_B = 5120
COL_C = 5376
COL_SMALL = 5632
N_PROJ = 5760
LANE_AF = 0
LANE_AB = 16
LANE_DTF = 32
LANE_DTB = 48

ROW_TILE = 512
PROJ_TILE = 1024
EXPERT_TILE = 512
EXPERT_SUBTILES = 2
ROUTER_SUBTILES = 2
COMBINE_TILE = 256
COMBINE_CHUNK = 16
VMEM_LIMIT = 56 * 1024 * 1024


def _dot(a, b):
    return jnp.dot(a, b, preferred_element_type=F32)


def _dot_nt(a, b):
    return lax.dot_general(a, b, (((1,), (1,)), ((), ())), preferred_element_type=F32)


def _dot_tn(a, b):
    return lax.dot_general(a, b, (((0,), (0,)), ((), ())), preferred_element_type=F32)


def _split(x):
    hi = x.astype(BF16)
    lo = (x - hi.astype(F32)).astype(BF16)
    return hi, lo


def _sel_dot_l(m01, x):
    hi, lo = _split(x)
    return _dot(m01, jnp.concatenate([hi, lo], axis=0))


def _sel_dot_r(x, m01):
    hi, lo = _split(x)
    return _dot(jnp.concatenate([hi, lo], axis=1), m01)


def _twice_cols(m):
    return jnp.concatenate([m, m], axis=1)


def _softplus(x):
    return jnp.maximum(x, 0.0) + jnp.log(1.0 + jnp.exp(-jnp.abs(x)))


def _log_sigmoid(x):
    return jnp.minimum(x, 0.0) - jnp.log(1.0 + jnp.exp(-jnp.abs(x)))


def _silu(x):
    return x * jax.nn.sigmoid(x)


def _rms(x, w):
    return x * lax.rsqrt(jnp.mean(x * x, axis=-1, keepdims=True) + EPS) * w


def _pack_bf16_pairs(x):
    w = x.shape[1] // 2
    return pltpu.pack_elementwise([x[:, :w], x[:, w:]], packed_dtype=BF16)


def _unpack_bf16_pairs(p):
    lo, hi = (pltpu.unpack_elementwise(p, index=i, packed_dtype=BF16, unpacked_dtype=F32) for i in range(2))
    return jnp.concatenate([lo, hi], axis=1)


def _tokens(first, count):
    if not isinstance(first, int):
        first = pl.multiple_of(first * ROW_SUB, ROW_SUB)
    else:
        first = first * ROW_SUB
    return pl.ds(first, count * ROW_SUB)


def _load_rows(ref2):
    r = ref2.shape[0] // ROW_SUB
    return jnp.concatenate([ref2[pl.ds(j, r, stride=ROW_SUB), :] for j in range(ROW_SUB)], axis=1)


def _store_rows(ref2, val):
    r = ref2.shape[0] // ROW_SUB
    for j in range(ROW_SUB):
        ref2[pl.ds(j, r, stride=ROW_SUB), :] = val[:, j * LANES:(j + 1) * LANES]


def _tri(n, *, lower, inclusive):
    r = lax.broadcasted_iota(I32, (n, n), 0)
    c = lax.broadcasted_iota(I32, (n, n), 1)
    if lower:
        return (c <= r) if inclusive else (c < r)
    return (c >= r) if inclusive else (c > r)


def _inproj_body(x_ref, nw_ref, w_ref, o_ref, *, col_chunks):
    xn = _rms(x_ref[...], nw_ref[...]).astype(BF16)
    for lo, hi in col_chunks:
        o_ref[:, lo:hi] = _dot(xn, w_ref[:, lo:hi]).astype(BF16)


def _inproj(x2d, norm_w, w_perm, tile):
    rows = x2d.shape[0]
    tile = min(tile, rows)
    col_chunks = tuple((c, min(c + 1024, N_PROJ)) for c in range(0, N_PROJ, 1024))
    return pl.pallas_call(
        functools.partial(_inproj_body, col_chunks=col_chunks),
        grid=(rows // tile,),
        in_specs=[
            pl.BlockSpec((tile, D_MODEL), lambda i: (i, 0)),
            pl.BlockSpec((1, D_MODEL), lambda i: (0, 0)),
            pl.BlockSpec((D_MODEL, N_PROJ), lambda i: (0, 0), pipeline_mode=pl.Buffered(1)),
        ],
        out_specs=pl.BlockSpec((tile, N_PROJ), lambda i: (i, 0)),
        out_shape=jax.ShapeDtypeStruct((rows, N_PROJ), BF16),
        compiler_params=pltpu.CompilerParams(
            dimension_semantics=("arbitrary",), vmem_limit_bytes=VMEM_LIMIT),
        name="inproj",
    )(x2d, norm_w, w_perm)


GLA_HEADS_PER_STEP = 2
GLA_STEPS_PER_ITER = 8
SSD_STEPS_PER_ITER = 8


def _gla_body(q_ref, k_ref, v_ref, g_ref, sm_ref, qm_ref, km_ref, vm_ref, smm_ref,
              waf_ref, baf_ref, wab_ref, bab_ref, nw_ref, o_ref, st_ref, *, n_chunks):
    heads = GLA_HEADS_PER_STEP
    nprob = 2 * heads
    rows_all = nprob * CHUNK
    wk = heads * GLA_HEAD_K
    dk, dv = GLA_HEAD_K, GLA_HEAD_V

    tri_f = _tri(CHUNK, lower=True, inclusive=True)
    r2 = lax.broadcasted_iota(I32, (2 * CHUNK, 2 * CHUNK), 0)
    c2 = lax.broadcasted_iota(I32, (2 * CHUNK, 2 * CHUNK), 1)
    cum2 = ((r2 < CHUNK) & (c2 <= r2)) | ((r2 >= CHUNK) & (c2 >= r2))
    tri2 = _twice_cols(cum2.astype(BF16))
    ra = lax.broadcasted_iota(I32, (rows_all, rows_all), 0)
    ca = lax.broadcasted_iota(I32, (rows_all, rows_all), 1)
    same = (ra // CHUNK) == (ca // CHUNK)
    att_mask = same & (((ra < heads * CHUNK) & (ca <= ra)) | ((ra >= heads * CHUNK) & (ca > ra)))
    rb_ = lax.broadcasted_iota(I32, (rows_all, nprob * dk), 0) // CHUNK
    cb_ = lax.broadcasted_iota(I32, (rows_all, nprob * dk), 1) // dk
    own = rb_ == cb_

    waf, baf = waf_ref[...], baf_ref[...]
    wab, bab = wab_ref[...], bab_ref[...]
    wa_cat = jnp.concatenate([waf, wab], axis=1)
    nw = nw_ref[...]
    zero16 = jnp.zeros((), BF16)

    st_ref[...] = jnp.zeros_like(st_ref)

    meta_rows = lax.broadcasted_iota(I32, (CHUNK, wk), 0) >= NPAD
    lg = _log_sigmoid(_dot(smm_ref[...], waf) + baf) * (1.0 / GLA_GATE_NORM)
    bm = _sel_dot_l(_twice_cols(tri_f.astype(BF16)), jnp.where(meta_rows, lg, 0.0))
    for h in range(heads):
        lanes = slice(h * dk, (h + 1) * dk)
        bh, toth = bm[:, lanes], bm[CHUNK - 1:CHUNK, lanes]
        kend = (km_ref[:, lanes].astype(F32) * jnp.exp(toth - bh)).astype(BF16)
        st_ref[:, lanes] = _dot_tn(vm_ref[:, h * dv:(h + 1) * dv], kend)

    def stack(f, b, w):
        return jnp.concatenate([f[:, h * w:(h + 1) * w] for h in range(heads)]
                               + [b[:, h * w:(h + 1) * w] for h in range(heads)], axis=0)

    def steps(j, finalize):
        ids = [j * GLA_STEPS_PER_ITER + s for s in range(GLA_STEPS_PER_ITER)]
        rfs = [pl.ds(pl.multiple_of(i * CHUNK, CHUNK), CHUNK) for i in ids]
        rbs = [pl.ds(pl.multiple_of((n_chunks - 1 - i) * CHUNK, CHUNK), CHUNK) for i in ids]
        zs = [_dot(jnp.concatenate([sm_ref[rf, :], sm_ref[rb, :]], axis=0), wa_cat) for rf, rb in zip(rfs, rbs)]
        b2s = []
        for z in zs:
            lg = jnp.concatenate([z[:CHUNK, :wk] + baf, z[CHUNK:, wk:] + bab], axis=0)
            b2s.append(_sel_dot_l(tri2, _log_sigmoid(lg) * (1.0 / GLA_GATE_NORM)))
        vss, qds, kends, tots, gs = [], [], [], [], []
        for b2, rf, rb in zip(b2s, rfs, rbs):
            tot_f, tot_b = b2[CHUNK - 1:CHUNK], b2[CHUNK:CHUNK + 1]
            bst = stack(b2[:CHUNK], b2[CHUNK:], dk)
            tst = stack(jnp.broadcast_to(tot_f, (CHUNK, wk)), jnp.broadcast_to(tot_b, (CHUNK, wk)), dk)
            qs = stack(q_ref[rf, :], q_ref[rb, :], dk).astype(F32)
            ks = stack(k_ref[rf, :], k_ref[rb, :], dk).astype(F32)
            qd = (qs * (GLA_HEAD_K ** -0.5) * jnp.exp(bst)).astype(BF16)
            kd = (ks * jnp.exp(-bst)).astype(BF16)
            vss.append(stack(v_ref[rf, :], v_ref[rb, :], dv))
            qds.append(qd)
            kends.append((ks * jnp.exp(tst - bst)).astype(BF16))
            tots.append(jnp.concatenate([tot_f, tot_b], axis=1))
            gs.append(_dot_nt(qd, kd))
        intras = [_dot(jnp.where(att_mask, g, 0.0).astype(BF16), vs) for g, vs in zip(gs, vss)]
        upds = [_dot_tn(vs, jnp.where(own, jnp.concatenate([kend] * nprob, axis=1), zero16))
                for vs, kend in zip(vss, kends)]
        st = st_ref[...]
        outs = []
        for qd, tot, upd, intra in zip(qds, tots, upds, intras):
            qd_own = jnp.where(own, jnp.concatenate([qd] * nprob, axis=1), zero16)
            outs.append(intra + _dot_nt(qd_own, st.astype(BF16)))
            st = st * jnp.exp(tot) + upd
        st_ref[...] = st
        for out, rf, rb in zip(outs, rfs, rbs):
            for p in range(nprob):
                rows, h = (rf if p < heads else rb), p % heads
                o = out[p * CHUNK:(p + 1) * CHUNK]
                cols = slice(h * dv, (h + 1) * dv)
                if finalize:
                    o = o + o_ref[rows, cols].astype(F32)
                    gate = _silu(g_ref[rows, cols].astype(F32))
                    o_ref[rows, cols] = (_rms(o, nw) * gate).astype(BF16)
                else:
                    o_ref[rows, cols] = o.astype(BF16)

    def first_half(j, carry):
        steps(j, False)
        return carry

    def second_half(j, carry):
        steps(j, True)
        return carry

    half_iters = n_chunks // 2 // GLA_STEPS_PER_ITER
    lax.fori_loop(0, half_iters, first_half, 0)
    lax.fori_loop(half_iters, 2 * half_iters, second_half, 0)


def _gla(proj, proj_meta, wa_f, ba_f, wa_b, ba_b, norm_w):
    bsz, seq, _ = proj.shape
    assert (seq // CHUNK) % (2 * GLA_STEPS_PER_ITER) == 0
    hp = GLA_HEADS_PER_STEP
    wk, wv = hp * GLA_HEAD_K, hp * GLA_HEAD_V
    kb, vb, gb, sb = COL_K // wk, COL_V // wv, COL_G // wv, COL_SMALL // LANES
    real = lambda width, base: pl.BlockSpec((None, seq, width), lambda b, h: (b, 0, base + h))
    meta = lambda width, base: pl.BlockSpec((CHUNK, width), lambda b, h: (0, base + h))
    per_step = lambda rows_: pl.BlockSpec((rows_, wk), lambda b, h: (0, h))
    return pl.pallas_call(
        functools.partial(_gla_body, n_chunks=seq // CHUNK),
        grid=(bsz, GLA_HEADS // hp),
        in_specs=[
            real(wk, 0), real(wk, kb), real(wv, vb), real(wv, gb),
            pl.BlockSpec((None, seq, LANES), lambda b, h: (b, 0, sb)),
            meta(wk, 0), meta(wk, kb), meta(wv, vb),
            pl.BlockSpec((CHUNK, LANES), lambda b, h: (0, sb)),
            per_step(LANES), per_step(1), per_step(LANES), per_step(1),
            pl.BlockSpec((1, GLA_HEAD_V), lambda b, h: (0, 0)),
        ],
        out_specs=pl.BlockSpec((None, seq, wv), lambda b, h: (b, 0, h)),
        out_shape=jax.ShapeDtypeStruct((bsz, seq, GLA_DV), BF16),
        scratch_shapes=[pltpu.VMEM((GLA_HEAD_V, 2 * hp * GLA_HEAD_K), F32)],
        compiler_params=pltpu.CompilerParams(
            dimension_semantics=("arbitrary", "arbitrary"), vmem_limit_bytes=VMEM_LIMIT),
        name="gla",
    )(proj, proj, proj, proj, proj, proj_meta, proj_meta, proj_meta, proj_meta,
      wa_f, ba_f, wa_b, ba_b, norm_w)


def _conv_silu(win, cw, cb):
    half = (SSD_CONV - 1) // 2
    acc = cb
    for j in range(SSD_CONV):
        lo = HALO - half + j
        acc = acc + win[lo:lo + CHUNK, :] * cw[j:j + 1, :]
    return _silu(acc)


def _ssd_body(x_ref, z_ref, b_ref, c_ref, sm_ref, xm_ref, bm_ref, cm_ref, smm_ref,
              cwx_ref, cbx_ref, cwb_ref, cbb_ref, cwc_ref, cbc_ref, dtb_ref, alog_ref,
              ef_ref, eb_ref, dexp_ref, nw_ref, o_ref, st_ref, xc_ref, bc_ref, cc_ref,
              *, n_chunks):
    gh, p, n = SSD_GROUP_HEADS, SSD_HEAD_DIM, SSD_STATE
    width = gh * p
    two = 2 * CHUNK
    lane_s = lax.broadcasted_iota(I32, (two, width), 1) % p
    row2 = lax.broadcasted_iota(I32, (two, width), 0)
    row_t = row2 % CHUNK
    irep2 = lane_s == row_t
    irep2_16 = irep2.astype(BF16)
    pair_mask = ((row2 < CHUNK) & (lane_s <= row_t)) | ((row2 >= CHUNK) & (lane_s > row_t))
    r2 = lax.broadcasted_iota(I32, (two, two), 0)
    c2 = lax.broadcasted_iota(I32, (two, two), 1)
    same_dir = (r2 < CHUNK) == (c2 < CHUNK)
    ones2 = _twice_cols(same_dir.astype(BF16))
    cum2 = ((r2 < CHUNK) & (c2 <= r2)) | ((r2 >= CHUNK) & (c2 >= r2))
    tri2 = _twice_cols(cum2.astype(BF16))
    bd_r = lax.broadcasted_iota(I32, (width, width), 0) // p
    bd_c = lax.broadcasted_iota(I32, (width, width), 1) // p
    bdmask = bd_r == bd_c
    zero16 = jnp.zeros((), BF16)

    cwx, cbx = cwx_ref[...], cbx_ref[...]
    cwb, cbb = cwb_ref[...], cbb_ref[...]
    cwc, cbc = cwc_ref[...], cbc_ref[...]
    dtb = dtb_ref[...]
    a_row = -jnp.exp(alog_ref[...])
    ef, eb = ef_ref[...], eb_ref[...]
    dexp, nw = dexp_ref[...], nw_ref[...]
    seq = n_chunks * CHUNK

    def window(ref, mref, r):
        off = pl.multiple_of(r * CHUNK, CHUNK)
        poff = pl.multiple_of(jnp.maximum(off - HALO, 0), HALO)
        noff = pl.multiple_of(jnp.minimum(off + CHUNK, seq - HALO), HALO)
        prev = jnp.where(r == 0, mref[CHUNK - HALO:, :], ref[pl.ds(poff, HALO), :])
        nxt = jnp.where(r == n_chunks - 1, zero16, ref[pl.ds(noff, HALO), :])
        return jnp.concatenate([prev, ref[pl.ds(off, CHUNK), :], nxt], axis=0)

    half = (SSD_CONV - 1) // 2
    side_taps = [j for j in range(SSD_CONV) if j != half]
    win_rows = CHUNK + 2 * HALO
    sr = lax.broadcasted_iota(I32, (len(side_taps) * CHUNK, win_rows), 0)
    sc = lax.broadcasted_iota(I32, (len(side_taps) * CHUNK, win_rows), 1)
    tap_of = sr // CHUNK
    tap_shift = jnp.where(tap_of < half, tap_of, tap_of + 1) - half
    shift_mat = (sc == (sr % CHUNK) + HALO + tap_shift).astype(BF16)
    cw_all = jnp.concatenate([cwx, cwb, cwc], axis=1)
    cb_all = jnp.concatenate([cbx, cbb, cbc], axis=1)

    def conv_body(it, carry):
        chunks = [it * SSD_STEPS_PER_ITER + s for s in range(SSD_STEPS_PER_ITER)]
        wins = [jnp.concatenate([window(x_ref, xm_ref, r), window(b_ref, bm_ref, r), window(c_ref, cm_ref, r)],
                                axis=1) for r in chunks]
        shifts = [_dot(shift_mat, win) for win in wins]
        for r, win, shifted in zip(chunks, wins, shifts):
            rows = pl.ds(pl.multiple_of(r * CHUNK, CHUNK), CHUNK)
            acc = cb_all + win[HALO:HALO + CHUNK].astype(F32) * cw_all[half:half + 1]
            for pos, j in enumerate(side_taps):
                acc = acc + shifted[pos * CHUNK:(pos + 1) * CHUNK] * cw_all[j:j + 1]
            y = _silu(acc).astype(BF16)
            xc_ref[rows, :] = y[:, :width]
            bc_ref[rows, :] = y[:, width:width + n]
            cc_ref[rows, :] = y[:, width + n:]
        return carry

    lax.fori_loop(0, n_chunks // SSD_STEPS_PER_ITER, conv_body, 0)

    st_ref[...] = jnp.zeros_like(st_ref)

    def meta_window(mref, ref):
        zeros = jnp.zeros((HALO, mref.shape[1]), F32)
        return jnp.concatenate([zeros, mref[...].astype(F32), ref[0:HALO, :].astype(F32)], axis=0)

    def meta_mask(width_):
        return lax.broadcasted_iota(I32, (CHUNK, width_), 0) >= NPAD

    xc = jnp.where(meta_mask(width), _conv_silu(meta_window(xm_ref, x_ref), cwx, cbx), 0.0)
    bc = jnp.where(meta_mask(n), _conv_silu(meta_window(bm_ref, b_ref), cwb, cbb), 0.0)
    dt = jnp.where(meta_mask(LANES), _softplus(smm_ref[...].astype(F32) + dtb), 0.0)
    cs = _sel_dot_l(_twice_cols(cum2[:CHUNK, :CHUNK].astype(BF16)), dt * a_row)
    both = _sel_dot_r(jnp.concatenate([cs, dt], axis=0), ef)
    cs_e, dt_e = both[:CHUNK], both[CHUNK:]
    xend = (xc * dt_e * jnp.exp(cs_e[CHUNK - 1:CHUNK] - cs_e)).astype(BF16)
    st_ref[0:n, :] = _dot_tn(bc.astype(BF16), xend)

    zeros_n = jnp.zeros((CHUNK, n), BF16)

    def own_dir(a):
        return jnp.concatenate([jnp.concatenate([a[:CHUNK], zeros_n], axis=1),
                                jnp.concatenate([zeros_n, a[CHUNK:]], axis=1)], axis=0)

    def steps(j, finalize):
        ids = [j * SSD_STEPS_PER_ITER + s for s in range(SSD_STEPS_PER_ITER)]
        rfs = [pl.ds(pl.multiple_of(i * CHUNK, CHUNK), CHUNK) for i in ids]
        rbs = [pl.ds(pl.multiple_of((n_chunks - 1 - i) * CHUNK, CHUNK), CHUNK) for i in ids]
        both_rows = lambda ref, rf, rb: jnp.concatenate([ref[rf, :], ref[rb, :]], axis=0)
        dts = [_softplus(both_rows(sm_ref, rf, rb).astype(F32) + dtb) for rf, rb in zip(rfs, rbs)]
        css = [_sel_dot_l(tri2, dt * a_row) for dt in dts]
        exs = [(_sel_dot_r(jnp.concatenate([cs[:CHUNK], dt[:CHUNK]], axis=0), ef),
                _sel_dot_r(jnp.concatenate([cs[CHUNK:], dt[CHUNK:]], axis=0), eb)) for cs, dt in zip(css, dts)]
        xcs, cs_es, tots, xdts, xends, cbs, bcs, ccs = [], [], [], [], [], [], [], []
        for (ex_f, ex_b), rf, rb in zip(exs, rfs, rbs):
            cs_e = jnp.concatenate([ex_f[:CHUNK], ex_b[:CHUNK]], axis=0)
            dt_e = jnp.concatenate([ex_f[CHUNK:], ex_b[CHUNK:]], axis=0)
            tot_f, tot_b = cs_e[CHUNK - 1:CHUNK], cs_e[CHUNK:CHUNK + 1]
            tot = jnp.concatenate([jnp.broadcast_to(tot_f, (CHUNK, width)),
                                   jnp.broadcast_to(tot_b, (CHUNK, width))], axis=0)
            xc = both_rows(xc_ref, rf, rb).astype(F32)
            bc16, cc16 = both_rows(bc_ref, rf, rb), both_rows(cc_ref, rf, rb)
            xdt = xc * dt_e
            xcs.append(xc)
            cs_es.append(cs_e)
            tots.append((tot_f, tot_b))
            xdts.append(xdt.astype(BF16))
            xends.append((xdt * jnp.exp(tot - cs_e)).astype(BF16))
            bcs.append(bc16)
            ccs.append(cc16)
            cbs.append(_dot_nt(cc16, bc16))
        cb_reps = [_dot(jnp.where(same_dir, cb, 0.0).astype(BF16), irep2_16) for cb in cbs]
        cs_rows = [_sel_dot_l(ones2, jnp.where(irep2, cs_e, 0.0)) for cs_e in cs_es]
        intras = []
        for cb_rep, cs_row, cs_e, xdt16 in zip(cb_reps, cs_rows, cs_es, xdts):
            decay = jnp.where(pair_mask, jnp.exp(jnp.minimum(cs_e - cs_row, 0.0)), 0.0)
            w = (cb_rep * decay).astype(BF16)
            intras.append(jnp.concatenate(
                [_dot(w[d * CHUNK:(d + 1) * CHUNK],
                      jnp.where(bdmask, jnp.concatenate([xdt16[d * CHUNK:(d + 1) * CHUNK]] * gh, axis=0), zero16))
                 for d in range(2)], axis=0))
        upds = [_dot_tn(own_dir(bc16), xend) for bc16, xend in zip(bcs, xends)]
        st = st_ref[...]
        ys = []
        for intra, cc16, cs_e, (tot_f, tot_b), upd in zip(intras, ccs, cs_es, tots, upds):
            ys.append(intra + _dot(own_dir(cc16), st.astype(BF16)) * jnp.exp(cs_e))
            grow = jnp.concatenate([jnp.broadcast_to(jnp.exp(tot_f), (n, width)),
                                    jnp.broadcast_to(jnp.exp(tot_b), (n, width))], axis=0)
            st = st * grow + upd
        st_ref[...] = st
        for y, xc, rf, rb in zip(ys, xcs, rfs, rbs):
            for d, rows in enumerate((rf, rb)):
                yd = y[d * CHUNK:(d + 1) * CHUNK]
                if finalize:
                    yd = yd + o_ref[rows, :].astype(F32) + xc[d * CHUNK:(d + 1) * CHUNK] * dexp
                    yd = yd * _silu(z_ref[rows, :].astype(F32))
                    o_ref[rows, :] = _rms(yd, nw).astype(BF16)
                else:
                    o_ref[rows, :] = yd.astype(BF16)

    def first_half(j, carry):
        steps(j, False)
        return carry

    def second_half(j, carry):
        steps(j, True)
        return carry

    half_iters = n_chunks // 2 // SSD_STEPS_PER_ITER
    lax.fori_loop(0, half_iters, first_half, 0)
    lax.fori_loop(half_iters, 2 * half_iters, second_half, 0)


def _ssd(proj, proj_meta, conv_w, conv_b, dtb, alog, e_f, e_b, dexp, norm_w):
    bsz, seq, _ = proj.shape
    gc = SSD_GROUP_CH
    xb, zb = COL_X // gc, COL_Z // gc
    bb, cb, sb = COL_B // SSD_STATE, COL_C // SSD_STATE, COL_SMALL // LANES
    real = lambda width, base: pl.BlockSpec((None, seq, width), lambda b, g: (b, 0, base + g))
    meta = lambda width, base: pl.BlockSpec((CHUNK, width), lambda b, g: (0, base + g))
    cpar = lambda rows_, width, base: pl.BlockSpec((rows_, width), lambda b, g: (0, base + g))
    cbb_, ccb_ = SSD_INNER // SSD_STATE, SSD_INNER // SSD_STATE + SSD_GROUPS
    const = lambda shape: pl.BlockSpec(shape, lambda b, g: (0, 0))
    return pl.pallas_call(
        functools.partial(_ssd_body, n_chunks=seq // CHUNK),
        grid=(bsz, SSD_GROUPS),
        in_specs=[
            real(gc, xb), real(gc, zb), real(SSD_STATE, bb), real(SSD_STATE, cb),
            pl.BlockSpec((None, seq, LANES), lambda b, g: (b, 0, sb)),
            meta(gc, xb), meta(SSD_STATE, bb), meta(SSD_STATE, cb),
            pl.BlockSpec((CHUNK, LANES), lambda b, g: (0, sb)),
            cpar(SSD_CONV, gc, 0), cpar(1, gc, 0),
            cpar(SSD_CONV, SSD_STATE, cbb_), cpar(1, SSD_STATE, cbb_),
            cpar(SSD_CONV, SSD_STATE, ccb_), cpar(1, SSD_STATE, ccb_),
            const((1, LANES)), const((1, LANES)),
            pl.BlockSpec((None, 2 * LANES, gc), lambda b, g: (g, 0, 0)),
            pl.BlockSpec((None, 2 * LANES, gc), lambda b, g: (g, 0, 0)),
            pl.BlockSpec((1, gc), lambda b, g: (0, g)),
            pl.BlockSpec((1, gc), lambda b, g: (0, g)),
        ],
        out_specs=pl.BlockSpec((None, seq, gc), lambda b, g: (b, 0, g)),
        out_shape=jax.ShapeDtypeStruct((bsz, seq, SSD_INNER), BF16),
        scratch_shapes=[pltpu.VMEM((2 * SSD_STATE, gc), F32),
                        pltpu.VMEM((seq, gc), BF16), pltpu.VMEM((seq, SSD_STATE), BF16),
                        pltpu.VMEM((seq, SSD_STATE), BF16)],
        compiler_params=pltpu.CompilerParams(
            dimension_semantics=("arbitrary", "arbitrary"), vmem_limit_bytes=VMEM_LIMIT),
        name="ssd",
    )(proj, proj, proj, proj, proj, proj_meta, proj_meta, proj_meta, proj_meta,
      conv_w, conv_b, conv_w, conv_b, conv_w, conv_b, dtb, alog, e_f, e_b, dexp, norm_w)


def _outproj_body(x_ref, og_ref, ys_ref, wo1_ref, wo2_ref, n2w_ref, wrh_ref, wrl_ref, br_ref,
                  h2_ref, n2p_ref, code_ref, gate_ref, cnt_ref, carry_ref, *, tile):
    i = pl.program_id(0)

    @pl.when(i == 0)
    def _():
        carry_ref[...] = jnp.zeros_like(carry_ref)

    sub = tile // ROUTER_SUBTILES
    parts = [pl.ds(s * sub, sub) for s in range(ROUTER_SUBTILES)]
    wo1, wo2, n2w = wo1_ref[...], wo2_ref[...], n2w_ref[...]
    wrh, wrl, br = wrh_ref[...], wrl_ref[...], br_ref[...]
    h2s = [x_ref[p, :] + _dot(og_ref[p, :], wo1) + _dot(ys_ref[p, :], wo2) for p in parts]
    n2s = []
    for p, h2 in zip(parts, h2s):
        h2_ref[p, :] = h2
        n2 = _rms(h2, n2w)
        _store_rows(n2p_ref.at[_tokens(p.start, sub)], _pack_bf16_pairs(n2))
        n2s.append(n2)
    logit_parts = []
    for n2 in n2s:
        nh, nl = _split(n2)
        logit_parts.append(_dot(nh, wrh) + _dot(nh, wrl) + _dot(nl, wrh) + br)

    lane = lax.broadcasted_iota(I32, (sub, LANES), 1)
    lane_f = lane.astype(F32)
    lane4 = lax.broadcasted_iota(I32, (sub, TOP_K), 1)
    per_row = LANES // TOP_K
    tok = lax.broadcasted_iota(I32, (sub, LANES), 0)
    here = (lane // TOP_K) == (tok % per_row)
    gather_rows = (lax.broadcasted_iota(I32, (sub // per_row, sub), 1) // per_row
                   == lax.broadcasted_iota(I32, (sub // per_row, sub), 0)).astype(BF16)
    before_me = _tri(sub, lower=True, inclusive=False).astype(BF16)

    routed = []
    for p, logits in zip(parts, logit_parts):
        vals, onehots, picks = [], [], []
        work = logits
        for k in range(TOP_K):
            m = jnp.max(work, axis=-1, keepdims=True)
            first = jnp.min(jnp.where(work == m, lane_f, float(LANES)), axis=-1, keepdims=True)
            oh = lane_f == first
            work = jnp.where(oh, -jnp.inf, work)
            vals.append(m)
            onehots.append(oh)
            picks.append(first)
        exps = [jnp.exp(v - vals[0]) for v in vals]
        inv = 1.0 / (exps[0] + exps[1] + exps[2] + exps[3])
        gate_out = jnp.zeros((sub, TOP_K), F32)
        for k in range(TOP_K):
            gate_out = jnp.where(lane4 == k, exps[k] * inv, gate_out)
        gate_ref[p, :] = gate_out
        any_oh = (onehots[0] | onehots[1] | onehots[2] | onehots[3])
        any16 = jnp.where(any_oh, 1.0, 0.0).astype(BF16)
        routed.append((onehots, picks, any16, _dot(before_me, any16)))

    carry = carry_ref[...]
    for s, (onehots, picks, any16, before) in enumerate(routed):
        before = before + carry
        carry = carry + jnp.sum(any16.astype(F32), axis=0, keepdims=True)
        rest = [jnp.sum(jnp.where(onehots[k], before, 0.0), axis=-1, keepdims=True) * N_EXPERTS + picks[k]
                for k in range(TOP_K)]
        flat = jnp.zeros((sub // per_row, LANES), F32)
        for scale in (65536.0, 256.0, 1.0):
            piece = [jnp.floor(c * (1.0 / scale)) for c in rest]
            rest = [c - q * scale for c, q in zip(rest, piece)]
            by_k = piece[TOP_K - 1]
            for k in range(TOP_K - 2, -1, -1):
                by_k = jnp.where(lane % TOP_K == k, piece[k], by_k)
            flat = flat + scale * _dot(gather_rows, jnp.where(here, by_k, 0.0).astype(BF16))
        code_ref[pl.ds(s * (sub // per_row), sub // per_row), :] = flat.astype(I32)
    carry_ref[...] = carry
    cnt_ref[...] = carry


def _outproj(x2d, o_gla, y_ssd, w_out1, w_out2, norm2_w, wr_hi, wr_lo, b_r, tile):
    rows = x2d.shape[0]
    tile = min(tile, rows)
    row = lambda width: pl.BlockSpec((tile, width), lambda i: (i, 0))
    const = lambda shape: pl.BlockSpec(shape, lambda i: (0, 0))
    return pl.pallas_call(
        functools.partial(_outproj_body, tile=tile),
        grid=(rows // tile,),
        in_specs=[
            row(D_MODEL), row(GLA_DV), row(SSD_INNER),
            const((GLA_DV, D_MODEL)), const((SSD_INNER, D_MODEL)), const((1, D_MODEL)),
            const((D_MODEL, LANES)), const((D_MODEL, LANES)), const((1, LANES)),
        ],
        out_specs=[
            row(D_MODEL), pl.BlockSpec((tile * ROW_SUB, LANES), lambda i: (i, 0)),
            pl.BlockSpec((tile * TOP_K // LANES, LANES), lambda i: (i, 0)), row(TOP_K), const((1, LANES)),
        ],
        out_shape=[
            jax.ShapeDtypeStruct((rows, D_MODEL), F32),
            jax.ShapeDtypeStruct((rows * ROW_SUB, LANES), U32),
            jax.ShapeDtypeStruct((rows * TOP_K // LANES, LANES), I32),
            jax.ShapeDtypeStruct((rows, TOP_K), F32),
            jax.ShapeDtypeStruct((1, LANES), F32),
        ],
        scratch_shapes=[pltpu.VMEM((1, LANES), F32)],
        compiler_params=pltpu.CompilerParams(
            dimension_semantics=("arbitrary",), vmem_limit_bytes=VMEM_LIMIT),
        name="outproj_router",
    )(x2d, o_gla, y_ssd, w_out1, w_out2, norm2_w, wr_hi, wr_lo, b_r)


def _row_copy(src, src_row, dst, dst_row, sem):
    return pltpu.make_async_copy(src.at[_tokens(src_row, 1)], dst.at[_tokens(dst_row, 1)], sem)


def _dispatch_body(fill_start_ref, fill_len_ref, dest_ref, n2p_ref, xs_hbm, zeros_ref, sem, fill_sem, *, tile):
    @pl.when(pl.program_id(0) == 0)
    def _():
        zeros_ref[...] = jnp.zeros_like(zeros_ref)

        def fill(e, size, wait):
            length = fill_len_ref[e]

            @pl.when((length & size) != 0)
            def _():
                pos = fill_start_ref[e] + (length & (size - 1))
                copy = pltpu.make_async_copy(zeros_ref.at[_tokens(0, size)], xs_hbm.at[_tokens(pos, size)], fill_sem)
                if wait:
                    copy.wait()
                else:
                    copy.start()

        for wait in (False, True):
            for e in range(N_EXPERTS):
                size = 1
                while size < tile:
                    fill(e, size, wait)
                    size *= 2

    for t in range(tile):
        for k in range(TOP_K):
            _row_copy(n2p_ref, t, xs_hbm, dest_ref[t * TOP_K + k], sem).start(priority=k % 2)

    for k in range(TOP_K):
        pltpu.make_async_copy(n2p_ref, xs_hbm.at[_tokens(0, tile)], sem).wait()


def _dispatch(fill_start, fill_len, dest_flat, n2p, tile, out_rows):
    rows = n2p.shape[0] // ROW_SUB
    grid_spec = pltpu.PrefetchScalarGridSpec(
        num_scalar_prefetch=2,
        grid=(rows // tile,),
        in_specs=[
            pl.BlockSpec((tile * TOP_K,), lambda i, fs, fl: (i,), memory_space=pltpu.SMEM),
            pl.BlockSpec((tile * ROW_SUB, LANES), lambda i, fs, fl: (i, 0)),
        ],
        out_specs=pl.BlockSpec(memory_space=pl.ANY),
        scratch_shapes=[pltpu.VMEM((tile // 2 * ROW_SUB, LANES), U32),
                        pltpu.SemaphoreType.DMA(()), pltpu.SemaphoreType.DMA(())],
    )
    return pl.pallas_call(
        functools.partial(_dispatch_body, tile=tile),
        grid_spec=grid_spec,
        out_shape=jax.ShapeDtypeStruct((out_rows * ROW_SUB, LANES), U32),
        compiler_params=pltpu.CompilerParams(dimension_semantics=("arbitrary",)),
        name="dispatch",
    )(fill_start, fill_len, dest_flat, n2p)


def _expert_body(blk_ref, exp_ref, newexp_ref, nw_ref,
                 x_ref, wgu_ref, bgu_ref, wdn_ref, bdn_ref, o_ref, wgu16_ref, wdn16_ref, *, tile):
    w = pl.program_id(0)
    cast_rows = 64

    @pl.when((w < nw_ref[0]) & (newexp_ref[w] == 1))
    def _():
        def cast(i, carry):
            rows = pl.ds(pl.multiple_of(i * cast_rows, cast_rows), cast_rows)
            wgu16_ref[rows, :] = wgu_ref[rows, :].astype(BF16)
            wdn16_ref[rows, :] = wdn_ref[rows, :].astype(BF16)
            return carry

        lax.fori_loop(0, D_MODEL // cast_rows, cast, 0)

    @pl.when(w < nw_ref[0])
    def _():
        sub = tile // EXPERT_SUBTILES
        parts = [_tokens(s * sub, sub) for s in range(EXPERT_SUBTILES)]
        xs = [_unpack_bf16_pairs(_load_rows(x_ref.at[p])).astype(BF16) for p in parts]
        gts = [jnp.minimum(_dot(x, wgu16_ref[:, :D_FF]) + bgu_ref[:, :D_FF], SWIGLU_LIMIT) for x in xs]
        ups = [jnp.clip(_dot(x, wgu16_ref[:, D_FF:]) + bgu_ref[:, D_FF:], -SWIGLU_LIMIT, SWIGLU_LIMIT)
               for x in xs]
        acts = [((up + 1.0) * gt * jax.nn.sigmoid(gt * SWIGLU_ALPHA)).astype(BF16) for gt, up in zip(gts, ups)]
        for p, act in zip(parts, acts):
            _store_rows(o_ref.at[p], _pack_bf16_pairs(_dot(act, wdn16_ref[...]) + bdn_ref[...]))


def _experts(tables, xs, w_gu, b_gu, w_dn, b_dn, tile, n_work):
    rows = xs.shape[0]
    row_block = pl.BlockSpec((tile * ROW_SUB, LANES), lambda w, blk, ex, ne, nw: (blk[w], 0))
    grid_spec = pltpu.PrefetchScalarGridSpec(
        num_scalar_prefetch=4,
        grid=(n_work,),
        in_specs=[
            row_block,
            pl.BlockSpec((None, D_MODEL, 2 * D_FF), lambda w, blk, ex, ne, nw: (ex[w], 0, 0)),
            pl.BlockSpec((None, 1, 2 * D_FF), lambda w, blk, ex, ne, nw: (ex[w], 0, 0)),
            pl.BlockSpec((None, D_FF, D_MODEL), lambda w, blk, ex, ne, nw: (ex[w], 0, 0)),
            pl.BlockSpec((None, 1, D_MODEL), lambda w, blk, ex, ne, nw: (ex[w], 0, 0)),
        ],
        out_specs=row_block,
        scratch_shapes=[pltpu.VMEM((D_MODEL, 2 * D_FF), BF16), pltpu.VMEM((D_FF, D_MODEL), BF16)],
    )
    return pl.pallas_call(
        functools.partial(_expert_body, tile=tile),
        grid_spec=grid_spec,
        out_shape=jax.ShapeDtypeStruct(xs.shape, U32),
        compiler_params=pltpu.CompilerParams(
            dimension_semantics=("arbitrary",), vmem_limit_bytes=VMEM_LIMIT),
        name="experts",
    )(*tables, xs, w_gu, b_gu, w_dn, b_dn)


def _combine_body(dest_ref, dest_next_ref, gate_ref, h2_ref, nfw_ref, ys_hbm, o_ref, buf, sems,
                  *, tile, n_steps):
    i = pl.program_id(0)
    slot = i % 2
    other = 1 - slot

    def row_copy(d_ref, s, t, k):
        return pltpu.make_async_copy(ys_hbm.at[_tokens(d_ref[t * TOP_K + k], 1)], buf.at[s, k, _tokens(t, 1)], sems.at[s])

    def wait_slot(s):
        for k in range(TOP_K):
            pltpu.make_async_copy(ys_hbm.at[_tokens(0, tile)], buf.at[s, k], sems.at[s]).wait()

    @pl.when(i == 0)
    def _():
        def start(t, carry):
            for k in range(TOP_K):
                row_copy(dest_ref, 0, t, k).start(priority=k % 2)
            return carry

        lax.fori_loop(0, tile, start, 0)

    wait_slot(slot)
    nfw = nfw_ref[...]
    for c in range(tile // COMBINE_CHUNK):
        rows = pl.ds(c * COMBINE_CHUNK, COMBINE_CHUNK)
        gate = gate_ref[rows, :]
        h3 = h2_ref[rows, :]
        packed = [_load_rows(buf.at[slot, k, _tokens(c * COMBINE_CHUNK, COMBINE_CHUNK)]) for k in range(TOP_K)]
        for t in range(c * COMBINE_CHUNK, (c + 1) * COMBINE_CHUNK):
            for k in range(TOP_K):
                row_copy(dest_next_ref, other, t, k).start(priority=k % 2)
        for k in range(TOP_K):
            h3 = h3 + gate[:, k:k + 1] * _unpack_bf16_pairs(packed[k])
        o_ref[rows, :] = _rms(h3, nfw)

    @pl.when(i == n_steps - 1)
    def _():
        wait_slot(other)


def _combine(dest_flat, gates, h2, norm_f_w, ys, tile):
    rows = h2.shape[0]
    n_steps = rows // tile
    return pl.pallas_call(
        functools.partial(_combine_body, tile=tile, n_steps=n_steps),
        grid=(n_steps,),
        in_specs=[
            pl.BlockSpec((tile * TOP_K,), lambda i: (i,), memory_space=pltpu.SMEM),
            pl.BlockSpec((tile * TOP_K,), lambda i: (jnp.minimum(i + 1, n_steps - 1),),
                         memory_space=pltpu.SMEM),
            pl.BlockSpec((tile, TOP_K), lambda i: (i, 0)),
            pl.BlockSpec((tile, D_MODEL), lambda i: (i, 0)),
            pl.BlockSpec((1, D_MODEL), lambda i: (0, 0)),
            pl.BlockSpec(memory_space=pl.ANY),
        ],
        out_specs=pl.BlockSpec((tile, D_MODEL), lambda i: (i, 0)),
        out_shape=jax.ShapeDtypeStruct((rows, D_MODEL), F32),
        scratch_shapes=[pltpu.VMEM((2, TOP_K, tile * ROW_SUB, LANES), U32), pltpu.SemaphoreType.DMA((2,))],
        compiler_params=pltpu.CompilerParams(
            dimension_semantics=("arbitrary",), vmem_limit_bytes=VMEM_LIMIT),
        name="combine_final",
    )(dest_flat, dest_flat, gates, h2, norm_f_w, ys)


def _expert_tables(counts, n_assign, tile):
    n_work = n_assign // tile + N_EXPERTS
    nb = (counts + tile - 1) // tile
    bend = jnp.cumsum(nb)
    total = bend[-1]
    pstarts = (bend - nb) * tile
    blk = jnp.minimum(jnp.arange(n_work, dtype=I32), total - 1).astype(I32)
    ex = jnp.minimum(jnp.sum(bend[None, :] <= blk[:, None], axis=1), N_EXPERTS - 1).astype(I32)
    prev_ex = jnp.concatenate([jnp.full((1,), -1, I32), ex[:-1]])
    new_ex = (ex != prev_ex).astype(I32)
    fill_start = (pstarts + counts).astype(I32)
    fill_len = (nb * tile - counts).astype(I32)
    return (blk, ex, new_ex, total.reshape(1).astype(I32)), pstarts, fill_start, fill_len, n_work


def kernel(x, meta, norm1_w, w_in, gla_wa2_f, gla_ba2_f, gla_wa2_b, gla_ba2_b, gla_norm_w, conv_w, conv_b, dt_bias_f, dt_bias_b, a_log_f, a_log_b, ssd_d, ssd_norm_w, w_out, norm2_w, w_router, b_router, w_gu, b_gu, w_dn, b_dn, norm_f_w):
    bsz, seq, d = x.shape
    n_tok = bsz * seq
    l = 0

    wi = w_in[l]
    a_cols = wi[:, 3072:3104]
    dt_cols = wi[:, 5664:5696]
    w_perm = jnp.concatenate(
        [wi[:, :3072], wi[:, 3104:5664], a_cols, dt_cols,
         jnp.zeros((d, N_PROJ - COL_SMALL - 64), F32)], axis=1).astype(BF16)

    def lane_rows(w, lane0):
        return jnp.zeros((LANES, w.shape[1]), F32).at[lane0:lane0 + w.shape[0]].set(w)

    wa_f = lane_rows(gla_wa2_f[l], LANE_AF).astype(BF16)
    wa_b = lane_rows(gla_wa2_b[l], LANE_AB).astype(BF16)
    ba_f = gla_ba2_f[l][None, :]
    ba_b = gla_ba2_b[l][None, :]

    def lane_vec(vf, vb):
        z = jnp.zeros((1, LANES), F32)
        return z.at[0, LANE_DTF:LANE_DTF + SSD_HEADS].set(vf).at[0, LANE_DTB:LANE_DTB + SSD_HEADS].set(vb)

    dtb = lane_vec(dt_bias_f[l], dt_bias_b[l])
    alog = lane_vec(a_log_f[l], a_log_b[l])
    lane_id = (jnp.arange(2 * LANES) % LANES)[None, :, None]
    head_id = (jnp.arange(SSD_GROUP_CH) // SSD_HEAD_DIM)[None, None, :]
    grp = jnp.arange(SSD_GROUPS)[:, None, None] * SSD_GROUP_HEADS
    e_f = (lane_id == LANE_DTF + grp + head_id).astype(BF16)
    e_b = (lane_id == LANE_DTB + grp + head_id).astype(BF16)
    dexp = jnp.repeat(ssd_d[l], SSD_HEAD_DIM)[None, :]

    wr = jnp.zeros((d, LANES), F32).at[:, :N_EXPERTS].set(w_router[l])
    wr_hi = wr.astype(BF16)
    wr_lo = (wr - wr_hi.astype(F32)).astype(BF16)
    b_r = jnp.full((1, LANES), -1e30, F32).at[0, :N_EXPERTS].set(b_router[l])

    x2d = x.reshape(n_tok, d)
    x_meta = jnp.pad(meta.astype(F32), ((NPAD, 0), (0, 0)))
    n1 = norm1_w[l][None, :]
    proj = _inproj(x2d, n1, w_perm, PROJ_TILE).reshape(bsz, seq, N_PROJ)
    proj_meta = _inproj(x_meta, n1, w_perm, CHUNK)
    o_gla = _gla(proj, proj_meta, wa_f, ba_f, wa_b, ba_b, gla_norm_w[l][None, :])
    y_ssd = _ssd(proj, proj_meta, conv_w[l], conv_b[l][None, :], dtb, alog, e_f, e_b, dexp,
                 ssd_norm_w[l][None, :])

    wo = w_out[l].astype(BF16)
    h2, n2p, code, gates, cnt = _outproj(
        x2d, o_gla.reshape(n_tok, GLA_DV), y_ssd.reshape(n_tok, SSD_INNER),
        wo[:GLA_DV], wo[GLA_DV:], norm2_w[l][None, :], wr_hi, wr_lo, b_r, PROJ_TILE)

    counts = cnt[0, :N_EXPERTS].astype(I32)
    tables, starts, fill_start, fill_len, n_work = _expert_tables(counts, n_tok * TOP_K, EXPERT_TILE)
    expert_of = (code % N_EXPERTS)[..., None] == jnp.arange(N_EXPERTS, dtype=I32)
    dest = (code // N_EXPERTS + jnp.sum(jnp.where(expert_of, starts, 0), axis=-1)).astype(I32).reshape(-1)

    xs = _dispatch(fill_start, fill_len, dest, n2p, ROW_TILE, n_work * EXPERT_TILE)
    ys = _experts(tables, xs, w_gu[l], b_gu[l][:, None, :], w_dn[l], b_dn[l][:, None, :],
                  EXPERT_TILE, n_work)
    out = _combine(dest, gates, h2, norm_f_w[None, :], ys, COMBINE_TILE)
    return out.reshape(bsz, seq, d)
```

```python
import functools

import jax
import jax.numpy as jnp
from jax import lax
from jax.experimental import pallas as pl
from jax.experimental.pallas import tpu as pltpu

F32 = jnp.float32
BF16 = jnp.bfloat16
I32 = jnp.int32
U32 = jnp.uint32

D_MODEL = 1024
N_META = 16
CHUNK = 64
NPAD = CHUNK - N_META
GLA_HEADS = 4
GLA_DK = 512
GLA_DV = 1024
GLA_HEAD_K = GLA_DK // GLA_HEADS
GLA_HEAD_V = GLA_DV // GLA_HEADS
GLA_RANK = 16
GLA_GATE_NORM = 16.0
SSD_INNER = 1024
SSD_HEAD_DIM = 64
SSD_HEADS = SSD_INNER // SSD_HEAD_DIM
SSD_GROUPS = 2
SSD_GROUP_HEADS = SSD_HEADS // SSD_GROUPS
SSD_GROUP_CH = SSD_INNER // SSD_GROUPS
SSD_STATE = 128
SSD_CONV = 5
N_EXPERTS = 32
TOP_K = 4
D_FF = 1024
SWIGLU_LIMIT = 7.0
SWIGLU_ALPHA = 1.702
EPS = 1e-6

LANES = 128
ROW_WORDS = D_MODEL // 2
ROW_SUB = ROW_WORDS // LANES
HALO = 16

COL_Q = 0
COL_K = 512
COL_V = 1024
COL_G = 2048
COL_Z = 3072
COL_X = 4096
COL_B = 5120
COL_C = 5376
COL_SMALL = 5632
N_PROJ = 5760
LANE_AF = 0
LANE_AB = 16
LANE_DTF = 32
LANE_DTB = 48

ROW_TILE = 512
PROJ_TILE = 1024
EXPERT_TILE = 512
EXPERT_SUBTILES = 2
ROUTER_SUBTILES = 2
COMBINE_TILE = 256
COMBINE_CHUNK = 16
VMEM_LIMIT = 56 * 1024 * 1024


def _dot(a, b):
    return jnp.dot(a, b, preferred_element_type=F32)


def _dot_nt(a, b):
    return lax.dot_general(a, b, (((1,), (1,)), ((), ())), preferred_element_type=F32)


def _dot_tn(a, b):
    return lax.dot_general(a, b, (((0,), (0,)), ((), ())), preferred_element_type=F32)


def _split(x):
    hi = x.astype(BF16)
    lo = (x - hi.astype(F32)).astype(BF16)
    return hi, lo


def _sel_dot_l(m01, x):
    hi, lo = _split(x)
    return _dot(m01, jnp.concatenate([hi, lo], axis=0))


def _sel_dot_r(x, m01):
    hi, lo = _split(x)
    return _dot(jnp.concatenate([hi, lo], axis=1), m01)


def _twice_cols(m):
    return jnp.concatenate([m, m], axis=1)


def _softplus(x):
    return jnp.maximum(x, 0.0) + jnp.log(1.0 + jnp.exp(-jnp.abs(x)))


def _log_sigmoid(x):
    return jnp.minimum(x, 0.0) - jnp.log(1.0 + jnp.exp(-jnp.abs(x)))


def _silu(x):
    return x * jax.nn.sigmoid(x)


def _rms(x, w):
    return x * lax.rsqrt(jnp.mean(x * x, axis=-1, keepdims=True) + EPS) * w


def _pack_bf16_pairs(x):
    w = x.shape[1] // 2
    return pltpu.pack_elementwise([x[:, :w], x[:, w:]], packed_dtype=BF16)


def _unpack_bf16_pairs(p):
    lo, hi = (pltpu.unpack_elementwise(p, index=i, packed_dtype=BF16, unpacked_dtype=F32) for i in range(2))
    return jnp.concatenate([lo, hi], axis=1)


def _tokens(first, count):
    if not isinstance(first, int):
        first = pl.multiple_of(first * ROW_SUB, ROW_SUB)
    else:
        first = first * ROW_SUB
    return pl.ds(first, count * ROW_SUB)


def _load_rows(ref2):
    r = ref2.shape[0] // ROW_SUB
    return jnp.concatenate([ref2[pl.ds(j, r, stride=ROW_SUB), :] for j in range(ROW_SUB)], axis=1)


def _store_rows(ref2, val):
    r = ref2.shape[0] // ROW_SUB
    for j in range(ROW_SUB):
        ref2[pl.ds(j, r, stride=ROW_SUB), :] = val[:, j * LANES:(j + 1) * LANES]


def _tri(n, *, lower, inclusive):
    r = lax.broadcasted_iota(I32, (n, n), 0)
    c = lax.broadcasted_iota(I32, (n, n), 1)
    if lower:
        return (c <= r) if inclusive else (c < r)
    return (c >= r) if inclusive else (c > r)


def _inproj_body(x_ref, nw_ref, w_ref, o_ref, *, col_chunks):
    xn = _rms(x_ref[...], nw_ref[...]).astype(BF16)
    for lo, hi in col_chunks:
        o_ref[:, lo:hi] = _dot(xn, w_ref[:, lo:hi]).astype(BF16)


def _inproj(x2d, norm_w, w_perm, tile):
    rows = x2d.shape[0]
    tile = min(tile, rows)
    col_chunks = tuple((c, min(c + 1024, N_PROJ)) for c in range(0, N_PROJ, 1024))
    return pl.pallas_call(
        functools.partial(_inproj_body, col_chunks=col_chunks),
        grid=(rows // tile,),
        in_specs=[
            pl.BlockSpec((tile, D_MODEL), lambda i: (i, 0)),
            pl.BlockSpec((1, D_MODEL), lambda i: (0, 0)),
            pl.BlockSpec((D_MODEL, N_PROJ), lambda i: (0, 0), pipeline_mode=pl.Buffered(1)),
        ],
        out_specs=pl.BlockSpec((tile, N_PROJ), lambda i: (i, 0)),
        out_shape=jax.ShapeDtypeStruct((rows, N_PROJ), BF16),
        compiler_params=pltpu.CompilerParams(
            dimension_semantics=("arbitrary",), vmem_limit_bytes=VMEM_LIMIT),
        name="inproj",
    )(x2d, norm_w, w_perm)


GLA_HEADS_PER_STEP = 2
GLA_STEPS_PER_ITER = 8
SSD_STEPS_PER_ITER = 8


def _gla_body(q_ref, k_ref, v_ref, g_ref, sm_ref, qm_ref, km_ref, vm_ref, smm_ref,
              waf_ref, baf_ref, wab_ref, bab_ref, nw_ref, o_ref, st_ref, *, n_chunks):
    heads = GLA_HEADS_PER_STEP
    nprob = 2 * heads
    rows_all = nprob * CHUNK
    wk = heads * GLA_HEAD_K
    dk, dv = GLA_HEAD_K, GLA_HEAD_V

    tri_f = _tri(CHUNK, lower=True, inclusive=True)
    r2 = lax.broadcasted_iota(I32, (2 * CHUNK, 2 * CHUNK), 0)
    c2 = lax.broadcasted_iota(I32, (2 * CHUNK, 2 * CHUNK), 1)
    cum2 = ((r2 < CHUNK) & (c2 <= r2)) | ((r2 >= CHUNK) & (c2 >= r2))
    tri2 = _twice_cols(cum2.astype(BF16))
    ra = lax.broadcasted_iota(I32, (rows_all, rows_all), 0)
    ca = lax.broadcasted_iota(I32, (rows_all, rows_all), 1)
    same = (ra // CHUNK) == (ca // CHUNK)
    att_mask = same & (((ra < heads * CHUNK) & (ca <= ra)) | ((ra >= heads * CHUNK) & (ca > ra)))
    rb_ = lax.broadcasted_iota(I32, (rows_all, nprob * dk), 0) // CHUNK
    cb_ = lax.broadcasted_iota(I32, (rows_all, nprob * dk), 1) // dk
    own = rb_ == cb_

    waf, baf = waf_ref[...], baf_ref[...]
    wab, bab = wab_ref[...], bab_ref[...]
    wa_cat = jnp.concatenate([waf, wab], axis=1)
    nw = nw_ref[...]
    zero16 = jnp.zeros((), BF16)

    st_ref[...] = jnp.zeros_like(st_ref)

    meta_rows = lax.broadcasted_iota(I32, (CHUNK, wk), 0) >= NPAD
    lg = _log_sigmoid(_dot(smm_ref[...], waf) + baf) * (1.0 / GLA_GATE_NORM)
    bm = _sel_dot_l(_twice_cols(tri_f.astype(BF16)), jnp.where(meta_rows, lg, 0.0))
    for h in range(heads):
        lanes = slice(h * dk, (h + 1) * dk)
        bh, toth = bm[:, lanes], bm[CHUNK - 1:CHUNK, lanes]
        kend = (km_ref[:, lanes].astype(F32) * jnp.exp(toth - bh)).astype(BF16)
        st_ref[:, lanes] = _dot_tn(vm_ref[:, h * dv:(h + 1) * dv], kend)

    def stack(f, b, w):
        return jnp.concatenate([f[:, h * w:(h + 1) * w] for h in range(heads)]
                               + [b[:, h * w:(h + 1) * w] for h in range(heads)], axis=0)

    def steps(j, finalize):
        ids = [j * GLA_STEPS_PER_ITER + s for s in range(GLA_STEPS_PER_ITER)]
        rfs = [pl.ds(pl.multiple_of(i * CHUNK, CHUNK), CHUNK) for i in ids]
        rbs = [pl.ds(pl.multiple_of((n_chunks - 1 - i) * CHUNK, CHUNK), CHUNK) for i in ids]
        zs = [_dot(jnp.concatenate([sm_ref[rf, :], sm_ref[rb, :]], axis=0), wa_cat) for rf, rb in zip(rfs, rbs)]
        b2s = []
        for z in zs:
            lg = jnp.concatenate([z[:CHUNK, :wk] + baf, z[CHUNK:, wk:] + bab], axis=0)
            b2s.append(_sel_dot_l(tri2, _log_sigmoid(lg) * (1.0 / GLA_GATE_NORM)))
        vss, qds, kends, tots, gs = [], [], [], [], []
        for b2, rf, rb in zip(b2s, rfs, rbs):
            tot_f, tot_b = b2[CHUNK - 1:CHUNK], b2[CHUNK:CHUNK + 1]
            bst = stack(b2[:CHUNK], b2[CHUNK:], dk)
            tst = stack(jnp.broadcast_to(tot_f, (CHUNK, wk)), jnp.broadcast_to(tot_b, (CHUNK, wk)), dk)
            qs = stack(q_ref[rf, :], q_ref[rb, :], dk).astype(F32)
            ks = stack(k_ref[rf, :], k_ref[rb, :], dk).astype(F32)
            qd = (qs * (GLA_HEAD_K ** -0.5) * jnp.exp(bst)).astype(BF16)
            kd = (ks * jnp.exp(-bst)).astype(BF16)
            vss.append(stack(v_ref[rf, :], v_ref[rb, :], dv))
            qds.append(qd)
            kends.append((ks * jnp.exp(tst - bst)).astype(BF16))
            tots.append(jnp.concatenate([tot_f, tot_b], axis=1))
            gs.append(_dot_nt(qd, kd))
        intras = [_dot(jnp.where(att_mask, g, 0.0).astype(BF16), vs) for g, vs in zip(gs, vss)]
        upds = [_dot_tn(vs, jnp.where(own, jnp.concatenate([kend] * nprob, axis=1), zero16))
                for vs, kend in zip(vss, kends)]
        st = st_ref[...]
        outs = []
        for qd, tot, upd, intra in zip(qds, tots, upds, intras):
            qd_own = jnp.where(own, jnp.concatenate([qd] * nprob, axis=1), zero16)
            outs.append(intra + _dot_nt(qd_own, st.astype(BF16)))
            st = st * jnp.exp(tot) + upd
        st_ref[...] = st
        for out, rf, rb in zip(outs, rfs, rbs):
            for p in range(nprob):
                rows, h = (rf if p < heads else rb), p % heads
                o = out[p * CHUNK:(p + 1) * CHUNK]
                cols = slice(h * dv, (h + 1) * dv)
                if finalize:
                    o = o + o_ref[rows, cols].astype(F32)
                    gate = _silu(g_ref[rows, cols].astype(F32))
                    o_ref[rows, cols] = (_rms(o, nw) * gate).astype(BF16)
                else:
                    o_ref[rows, cols] = o.astype(BF16)

    def first_half(j, carry):
        steps(j, False)
        return carry

    def second_half(j, carry):
        steps(j, True)
        return carry

    half_iters = n_chunks // 2 // GLA_STEPS_PER_ITER
    lax.fori_loop(0, half_iters, first_half, 0)
    lax.fori_loop(half_iters, 2 * half_iters, second_half, 0)


def _gla(proj, proj_meta, wa_f, ba_f, wa_b, ba_b, norm_w):
    bsz, seq, _ = proj.shape
    assert (seq // CHUNK) % (2 * GLA_STEPS_PER_ITER) == 0
    hp = GLA_HEADS_PER_STEP
    wk, wv = hp * GLA_HEAD_K, hp * GLA_HEAD_V
    kb, vb, gb, sb = COL_K // wk, COL_V // wv, COL_G // wv, COL_SMALL // LANES
    real = lambda width, base: pl.BlockSpec((None, seq, width), lambda b, h: (b, 0, base + h))
    meta = lambda width, base: pl.BlockSpec((CHUNK, width), lambda b, h: (0, base + h))
    per_step = lambda rows_: pl.BlockSpec((rows_, wk), lambda b, h: (0, h))
    return pl.pallas_call(
        functools.partial(_gla_body, n_chunks=seq // CHUNK),
        grid=(bsz, GLA_HEADS // hp),
        in_specs=[
            real(wk, 0), real(wk, kb), real(wv, vb), real(wv, gb),
            pl.BlockSpec((None, seq, LANES), lambda b, h: (b, 0, sb)),
            meta(wk, 0), meta(wk, kb), meta(wv, vb),
            pl.BlockSpec((CHUNK, LANES), lambda b, h: (0, sb)),
            per_step(LANES), per_step(1), per_step(LANES), per_step(1),
            pl.BlockSpec((1, GLA_HEAD_V), lambda b, h: (0, 0)),
        ],
        out_specs=pl.BlockSpec((None, seq, wv), lambda b, h: (b, 0, h)),
        out_shape=jax.ShapeDtypeStruct((bsz, seq, GLA_DV), BF16),
        scratch_shapes=[pltpu.VMEM((GLA_HEAD_V, 2 * hp * GLA_HEAD_K), F32)],
        compiler_params=pltpu.CompilerParams(
            dimension_semantics=("arbitrary", "arbitrary"), vmem_limit_bytes=VMEM_LIMIT),
        name="gla",
    )(proj, proj, proj, proj, proj, proj_meta, proj_meta, proj_meta, proj_meta,
      wa_f, ba_f, wa_b, ba_b, norm_w)


def _conv_silu(win, cw, cb):
    half = (SSD_CONV - 1) // 2
    acc = cb
    for j in range(SSD_CONV):
        lo = HALO - half + j
        acc = acc + win[lo:lo + CHUNK, :] * cw[j:j + 1, :]
    return _silu(acc)


def _ssd_body(x_ref, z_ref, b_ref, c_ref, sm_ref, xm_ref, bm_ref, cm_ref, smm_ref,
              cwx_ref, cbx_ref, cwb_ref, cbb_ref, cwc_ref, cbc_ref, dtb_ref, alog_ref,
              ef_ref, eb_ref, dexp_ref, nw_ref, o_ref, st_ref, xc_ref, bc_ref, cc_ref,
              *, n_chunks):
    gh, p, n = SSD_GROUP_HEADS, SSD_HEAD_DIM, SSD_STATE
    width = gh * p
    two = 2 * CHUNK
    lane_s = lax.broadcasted_iota(I32, (two, width), 1) % p
    row2 = lax.broadcasted_iota(I32, (two, width), 0)
    row_t = row2 % CHUNK
    irep2 = lane_s == row_t
    pair_mask = ((row2 < CHUNK) & (lane_s <= row_t)) | ((row2 >= CHUNK) & (lane_s > row_t))
    r2 = lax.broadcasted_iota(I32, (two, two), 0)
    c2 = lax.broadcasted_iota(I32, (two, two), 1)
    same_dir = (r2 < CHUNK) == (c2 < CHUNK)
    ones2 = _twice_cols(same_dir.astype(BF16))
    cum2 = ((r2 < CHUNK) & (c2 <= r2)) | ((r2 >= CHUNK) & (c2 >= r2))
    tri2 = _twice_cols(cum2.astype(BF16))
    bd_r = lax.broadcasted_iota(I32, (width, width), 0) // p
    bd_c = lax.broadcasted_iota(I32, (width, width), 1) // p
    bdmask = bd_r == bd_c
    zero16 = jnp.zeros((), BF16)

    cwx, cbx = cwx_ref[...], cbx_ref[...]
    cwb, cbb = cwb_ref[...], cbb_ref[...]
    cwc, cbc = cwc_ref[...], cbc_ref[...]
    dtb = dtb_ref[...]
    a_row = -jnp.exp(alog_ref[...])
    ef, eb = ef_ref[...], eb_ref[...]
    dexp, nw = dexp_ref[...], nw_ref[...]
    seq = n_chunks * CHUNK

    def window(ref, mref, r):
        off = pl.multiple_of(r * CHUNK, CHUNK)
        poff = pl.multiple_of(jnp.maximum(off - HALO, 0), HALO)
        noff = pl.multiple_of(jnp.minimum(off + CHUNK, seq - HALO), HALO)
        prev = jnp.where(r == 0, mref[CHUNK - HALO:, :], ref[pl.ds(poff, HALO), :])
        nxt = jnp.where(r == n_chunks - 1, zero16, ref[pl.ds(noff, HALO), :])
        return jnp.concatenate([prev, ref[pl.ds(off, CHUNK), :], nxt], axis=0)

    half = (SSD_CONV - 1) // 2
    side_taps = [j for j in range(SSD_CONV) if j != half]
    win_rows = CHUNK + 2 * HALO
    sr = lax.broadcasted_iota(I32, (len(side_taps) * CHUNK, win_rows), 0)
    sc = lax.broadcasted_iota(I32, (len(side_taps) * CHUNK, win_rows), 1)
    tap_of = sr // CHUNK
    tap_shift = jnp.where(tap_of < half, tap_of, tap_of + 1) - half
    shift_mat = (sc == (sr % CHUNK) + HALO + tap_shift).astype(BF16)
    cw_all = jnp.concatenate([cwx, cwb, cwc], axis=1)
    cb_all = jnp.concatenate([cbx, cbb, cbc], axis=1)

    def conv_body(it, carry):
        chunks = [it * SSD_STEPS_PER_ITER + s for s in range(SSD_STEPS_PER_ITER)]
        wins = [jnp.concatenate([window(x_ref, xm_ref, r), window(b_ref, bm_ref, r), window(c_ref, cm_ref, r)],
                                axis=1) for r in chunks]
        shifts = [_dot(shift_mat, win) for win in wins]
        for r, win, shifted in zip(chunks, wins, shifts):
            rows = pl.ds(pl.multiple_of(r * CHUNK, CHUNK), CHUNK)
            acc = cb_all + win[HALO:HALO + CHUNK].astype(F32) * cw_all[half:half + 1]
            for pos, j in enumerate(side_taps):
                acc = acc + shifted[pos * CHUNK:(pos + 1) * CHUNK] * cw_all[j:j + 1]
            y = _silu(acc).astype(BF16)
            xc_ref[rows, :] = y[:, :width]
            bc_ref[rows, :] = y[:, width:width + n]
            cc_ref[rows, :] = y[:, width + n:]
        return carry

    lax.fori_loop(0, n_chunks // SSD_STEPS_PER_ITER, conv_body, 0)

    st_ref[...] = jnp.zeros_like(st_ref)

    def meta_window(mref, ref):
        zeros = jnp.zeros((HALO, mref.shape[1]), F32)
        return jnp.concatenate([zeros, mref[...].astype(F32), ref[0:HALO, :].astype(F32)], axis=0)

    def meta_mask(width_):
        return lax.broadcasted_iota(I32, (CHUNK, width_), 0) >= NPAD

    xc = jnp.where(meta_mask(width), _conv_silu(meta_window(xm_ref, x_ref), cwx, cbx), 0.0)
    bc = jnp.where(meta_mask(n), _conv_silu(meta_window(bm_ref, b_ref), cwb, cbb), 0.0)
    dt = jnp.where(meta_mask(LANES), _softplus(smm_ref[...].astype(F32) + dtb), 0.0)
    cs = _sel_dot_l(_twice_cols(cum2[:CHUNK, :CHUNK].astype(BF16)), dt * a_row)
    both = _sel_dot_r(jnp.concatenate([cs, dt], axis=0), ef)
    cs_e, dt_e = both[:CHUNK], both[CHUNK:]
    xend = (xc * dt_e * jnp.exp(cs_e[CHUNK - 1:CHUNK] - cs_e)).astype(BF16)
    st_ref[0:n, :] = _dot_tn(bc.astype(BF16), xend)

    zeros_n = jnp.zeros((CHUNK, n), BF16)

    def own_dir(a):
        return jnp.concatenate([jnp.concatenate([a[:CHUNK], zeros_n], axis=1),
                                jnp.concatenate([zeros_n, a[CHUNK:]], axis=1)], axis=0)

    def steps(j, finalize):
        ids = [j * SSD_STEPS_PER_ITER + s for s in range(SSD_STEPS_PER_ITER)]
        rfs = [pl.ds(pl.multiple_of(i * CHUNK, CHUNK), CHUNK) for i in ids]
        rbs = [pl.ds(pl.multiple_of((n_chunks - 1 - i) * CHUNK, CHUNK), CHUNK) for i in ids]
        both_rows = lambda ref, rf, rb: jnp.concatenate([ref[rf, :], ref[rb, :]], axis=0)
        dts = [_softplus(both_rows(sm_ref, rf, rb).astype(F32) + dtb) for rf, rb in zip(rfs, rbs)]
        css = [_sel_dot_l(tri2, dt * a_row) for dt in dts]
        exs = [(_sel_dot_r(jnp.concatenate([cs[:CHUNK], dt[:CHUNK]], axis=0), ef),
                _sel_dot_r(jnp.concatenate([cs[CHUNK:], dt[CHUNK:]], axis=0), eb)) for cs, dt in zip(css, dts)]
        xcs, cs_es, tots, xdts, xends, cbs, bcs, ccs = [], [], [], [], [], [], [], []
        for (ex_f, ex_b), rf, rb in zip(exs, rfs, rbs):
            cs_e = jnp.concatenate([ex_f[:CHUNK], ex_b[:CHUNK]], axis=0)
            dt_e = jnp.concatenate([ex_f[CHUNK:], ex_b[CHUNK:]], axis=0)
            tot_f, tot_b = cs_e[CHUNK - 1:CHUNK], cs_e[CHUNK:CHUNK + 1]
            tot = jnp.concatenate([jnp.broadcast_to(tot_f, (CHUNK, width)),
                                   jnp.broadcast_to(tot_b, (CHUNK, width))], axis=0)
            xc = both_rows(xc_ref, rf, rb).astype(F32)
            bc16, cc16 = both_rows(bc_ref, rf, rb), both_rows(cc_ref, rf, rb)
            xdt = xc * dt_e
            xcs.append(xc)
            cs_es.append(cs_e)
            tots.append((tot_f, tot_b))
            xdts.append(xdt.astype(BF16))
            xends.append((xdt * jnp.exp(tot - cs_e)).astype(BF16))
            bcs.append(bc16)
            ccs.append(cc16)
            cbs.append(_dot_nt(cc16, bc16))
        cb_reps = [jnp.concatenate([jnp.where(same_dir, cb, pltpu.roll(cb, CHUNK, 1))] * (width // two), axis=1)
                   for cb in cbs]
        cs_rows = [_sel_dot_l(ones2, jnp.where(irep2, cs_e, 0.0)) for cs_e in cs_es]
        intras = []
        for cb_rep, cs_row, cs_e, xdt16 in zip(cb_reps, cs_rows, cs_es, xdts):
            decay = jnp.where(pair_mask, jnp.exp(jnp.minimum(cs_e - cs_row, 0.0)), 0.0)
            w = (cb_rep * decay).astype(BF16)
            intras.append(jnp.concatenate(
                [_dot(w[d * CHUNK:(d + 1) * CHUNK],
                      jnp.where(bdmask, jnp.concatenate([xdt16[d * CHUNK:(d + 1) * CHUNK]] * gh, axis=0), zero16))
                 for d in range(2)], axis=0))
        upds = [_dot_tn(own_dir(bc16), xend) for bc16, xend in zip(bcs, xends)]
        st = st_ref[...]
        ys = []
        for intra, cc16, cs_e, (tot_f, tot_b), upd in zip(intras, ccs, cs_es, tots, upds):
            ys.append(intra + _dot(own_dir(cc16), st.astype(BF16)) * jnp.exp(cs_e))
            grow = jnp.concatenate([jnp.broadcast_to(jnp.exp(tot_f), (n, width)),
                                    jnp.broadcast_to(jnp.exp(tot_b), (n, width))], axis=0)
            st = st * grow + upd
        st_ref[...] = st
        for y, xc, rf, rb in zip(ys, xcs, rfs, rbs):
            for d, rows in enumerate((rf, rb)):
                yd = y[d * CHUNK:(d + 1) * CHUNK]
                if finalize:
                    yd = yd + o_ref[rows, :].astype(F32) + xc[d * CHUNK:(d + 1) * CHUNK] * dexp
                    yd = yd * _silu(z_ref[rows, :].astype(F32))
                    o_ref[rows, :] = _rms(yd, nw).astype(BF16)
                else:
                    o_ref[rows, :] = yd.astype(BF16)

    def first_half(j, carry):
        steps(j, False)
        return carry

    def second_half(j, carry):
        steps(j, True)
        return carry

    half_iters = n_chunks // 2 // SSD_STEPS_PER_ITER
    lax.fori_loop(0, half_iters, first_half, 0)
    lax.fori_loop(half_iters, 2 * half_iters, second_half, 0)


def _ssd(proj, proj_meta, conv_w, conv_b, dtb, alog, e_f, e_b, dexp, norm_w):
    bsz, seq, _ = proj.shape
    gc = SSD_GROUP_CH
    xb, zb = COL_X // gc, COL_Z // gc
    bb, cb, sb = COL_B // SSD_STATE, COL_C // SSD_STATE, COL_SMALL // LANES
    real = lambda width, base: pl.BlockSpec((None, seq, width), lambda b, g: (b, 0, base + g))
    meta = lambda width, base: pl.BlockSpec((CHUNK, width), lambda b, g: (0, base + g))
    cpar = lambda rows_, width, base: pl.BlockSpec((rows_, width), lambda b, g: (0, base + g))
    cbb_, ccb_ = SSD_INNER // SSD_STATE, SSD_INNER // SSD_STATE + SSD_GROUPS
    const = lambda shape: pl.BlockSpec(shape, lambda b, g: (0, 0))
    return pl.pallas_call(
        functools.partial(_ssd_body, n_chunks=seq // CHUNK),
        grid=(bsz, SSD_GROUPS),
        in_specs=[
            real(gc, xb), real(gc, zb), real(SSD_STATE, bb), real(SSD_STATE, cb),
            pl.BlockSpec((None, seq, LANES), lambda b, g: (b, 0, sb)),
            meta(gc, xb), meta(SSD_STATE, bb), meta(SSD_STATE, cb),
            pl.BlockSpec((CHUNK, LANES), lambda b, g: (0, sb)),
            cpar(SSD_CONV, gc, 0), cpar(1, gc, 0),
            cpar(SSD_CONV, SSD_STATE, cbb_), cpar(1, SSD_STATE, cbb_),
            cpar(SSD_CONV, SSD_STATE, ccb_), cpar(1, SSD_STATE, ccb_),
            const((1, LANES)), const((1, LANES)),
            pl.BlockSpec((None, 2 * LANES, gc), lambda b, g: (g, 0, 0)),
            pl.BlockSpec((None, 2 * LANES, gc), lambda b, g: (g, 0, 0)),
            pl.BlockSpec((1, gc), lambda b, g: (0, g)),
            pl.BlockSpec((1, gc), lambda b, g: (0, g)),
        ],
        out_specs=pl.BlockSpec((None, seq, gc), lambda b, g: (b, 0, g)),
        out_shape=jax.ShapeDtypeStruct((bsz, seq, SSD_INNER), BF16),
        scratch_shapes=[pltpu.VMEM((2 * SSD_STATE, gc), F32),
                        pltpu.VMEM((seq, gc), BF16), pltpu.VMEM((seq, SSD_STATE), BF16),
                        pltpu.VMEM((seq, SSD_STATE), BF16)],
        compiler_params=pltpu.CompilerParams(
            dimension_semantics=("arbitrary", "arbitrary"), vmem_limit_bytes=VMEM_LIMIT),
        name="ssd",
    )(proj, proj, proj, proj, proj, proj_meta, proj_meta, proj_meta, proj_meta,
      conv_w, conv_b, conv_w, conv_b, conv_w, conv_b, dtb, alog, e_f, e_b, dexp, norm_w)


def _outproj_body(x_ref, og_ref, ys_ref, wo1_ref, wo2_ref, n2w_ref, wrh_ref, wrl_ref, br_ref,
                  h2_ref, n2p_ref, code_ref, gate_ref, cnt_ref, carry_ref, *, tile):
    i = pl.program_id(0)

    @pl.when(i == 0)
    def _():
        carry_ref[...] = jnp.zeros_like(carry_ref)

    sub = tile // ROUTER_SUBTILES
    parts = [pl.ds(s * sub, sub) for s in range(ROUTER_SUBTILES)]
    wo1, wo2, n2w = wo1_ref[...], wo2_ref[...], n2w_ref[...]
    wrh, wrl, br = wrh_ref[...], wrl_ref[...], br_ref[...]
    h2s = [x_ref[p, :] + _dot(og_ref[p, :], wo1) + _dot(ys_ref[p, :], wo2) for p in parts]
    n2s = []
    for p, h2 in zip(parts, h2s):
        h2_ref[p, :] = h2
        n2 = _rms(h2, n2w)
        _store_rows(n2p_ref.at[_tokens(p.start, sub)], _pack_bf16_pairs(n2))
        n2s.append(n2)
    logit_parts = []
    for n2 in n2s:
        nh, nl = _split(n2)
        logit_parts.append(_dot(nh, wrh) + _dot(nh, wrl) + _dot(nl, wrh) + br)

    lane = lax.broadcasted_iota(I32, (sub, LANES), 1)
    lane_f = lane.astype(F32)
    lane4 = lax.broadcasted_iota(I32, (sub, TOP_K), 1)
    per_row = LANES // TOP_K
    tok = lax.broadcasted_iota(I32, (sub, LANES), 0)
    here = (lane // TOP_K) == (tok % per_row)
    gather_rows = (lax.broadcasted_iota(I32, (sub // per_row, sub), 1) // per_row
                   == lax.broadcasted_iota(I32, (sub // per_row, sub), 0)).astype(BF16)
    before_me = _tri(sub, lower=True, inclusive=False).astype(BF16)

    routed = []
    for p, logits in zip(parts, logit_parts):
        vals, onehots, picks = [], [], []
        work = logits
        for k in range(TOP_K):
            m = jnp.max(work, axis=-1, keepdims=True)
            first = jnp.min(jnp.where(work == m, lane_f, float(LANES)), axis=-1, keepdims=True)
            oh = lane_f == first
            work = jnp.where(oh, -jnp.inf, work)
            vals.append(m)
            onehots.append(oh)
            picks.append(first)
        exps = [jnp.exp(v - vals[0]) for v in vals]
        inv = 1.0 / (exps[0] + exps[1] + exps[2] + exps[3])
        gate_out = jnp.zeros((sub, TOP_K), F32)
        for k in range(TOP_K):
            gate_out = jnp.where(lane4 == k, exps[k] * inv, gate_out)
        gate_ref[p, :] = gate_out
        any_oh = (onehots[0] | onehots[1] | onehots[2] | onehots[3])
        any16 = jnp.where(any_oh, 1.0, 0.0).astype(BF16)
        routed.append((onehots, picks, any16, _dot(before_me, any16)))

    carry = carry_ref[...]
    for s, (onehots, picks, any16, before) in enumerate(routed):
        before = before + carry
        carry = carry + jnp.sum(any16.astype(F32), axis=0, keepdims=True)
        rest = [jnp.sum(jnp.where(onehots[k], before, 0.0), axis=-1, keepdims=True) * N_EXPERTS + picks[k]
                for k in range(TOP_K)]
        flat = jnp.zeros((sub // per_row, LANES), F32)
        for scale in (65536.0, 256.0, 1.0):
            piece = [jnp.floor(c * (1.0 / scale)) for c in rest]
            rest = [c - q * scale for c, q in zip(rest, piece)]
            by_k = piece[TOP_K - 1]
            for k in range(TOP_K - 2, -1, -1):
                by_k = jnp.where(lane % TOP_K == k, piece[k], by_k)
            flat = flat + scale * _dot(gather_rows, jnp.where(here, by_k, 0.0).astype(BF16))
        code_ref[pl.ds(s * (sub // per_row), sub // per_row), :] = flat.astype(I32)
    carry_ref[...] = carry
    cnt_ref[...] = carry


def _outproj(x2d, o_gla, y_ssd, w_out1, w_out2, norm2_w, wr_hi, wr_lo, b_r, tile):
    rows = x2d.shape[0]
    tile = min(tile, rows)
    row = lambda width: pl.BlockSpec((tile, width), lambda i: (i, 0))
    const = lambda shape: pl.BlockSpec(shape, lambda i: (0, 0))
    return pl.pallas_call(
        functools.partial(_outproj_body, tile=tile),
        grid=(rows // tile,),
        in_specs=[
            row(D_MODEL), row(GLA_DV), row(SSD_INNER),
            const((GLA_DV, D_MODEL)), const((SSD_INNER, D_MODEL)), const((1, D_MODEL)),
            const((D_MODEL, LANES)), const((D_MODEL, LANES)), const((1, LANES)),
        ],
        out_specs=[
            row(D_MODEL), pl.BlockSpec((tile * ROW_SUB, LANES), lambda i: (i, 0)),
            pl.BlockSpec((tile * TOP_K // LANES, LANES), lambda i: (i, 0)), row(TOP_K), const((1, LANES)),
        ],
        out_shape=[
            jax.ShapeDtypeStruct((rows, D_MODEL), F32),
            jax.ShapeDtypeStruct((rows * ROW_SUB, LANES), U32),
            jax.ShapeDtypeStruct((rows * TOP_K // LANES, LANES), I32),
            jax.ShapeDtypeStruct((rows, TOP_K), F32),
            jax.ShapeDtypeStruct((1, LANES), F32),
        ],
        scratch_shapes=[pltpu.VMEM((1, LANES), F32)],
        compiler_params=pltpu.CompilerParams(
            dimension_semantics=("arbitrary",), vmem_limit_bytes=VMEM_LIMIT),
        name="outproj_router",
    )(x2d, o_gla, y_ssd, w_out1, w_out2, norm2_w, wr_hi, wr_lo, b_r)


def _row_copy(src, src_row, dst, dst_row, sem):
    return pltpu.make_async_copy(src.at[_tokens(src_row, 1)], dst.at[_tokens(dst_row, 1)], sem)


def _dispatch_body(fill_start_ref, fill_len_ref, dest_ref, n2p_ref, xs_hbm, zeros_ref, sem, fill_sem, *, tile):
    @pl.when(pl.program_id(0) == 0)
    def _():
        zeros_ref[...] = jnp.zeros_like(zeros_ref)

        def fill(e, size, wait):
            length = fill_len_ref[e]

            @pl.when((length & size) != 0)
            def _():
                pos = fill_start_ref[e] + (length & (size - 1))
                copy = pltpu.make_async_copy(zeros_ref.at[_tokens(0, size)], xs_hbm.at[_tokens(pos, size)], fill_sem)
                if wait:
                    copy.wait()
                else:
                    copy.start()

        for wait in (False, True):
            for e in range(N_EXPERTS):
                size = 1
                while size < tile:
                    fill(e, size, wait)
                    size *= 2

    for t in range(tile):
        for k in range(TOP_K):
            _row_copy(n2p_ref, t, xs_hbm, dest_ref[t * TOP_K + k], sem).start(priority=k % 2)

    for k in range(TOP_K):
        pltpu.make_async_copy(n2p_ref, xs_hbm.at[_tokens(0, tile)], sem).wait()


def _dispatch(fill_start, fill_len, dest_flat, n2p, tile, out_rows):
    rows = n2p.shape[0] // ROW_SUB
    grid_spec = pltpu.PrefetchScalarGridSpec(
        num_scalar_prefetch=2,
        grid=(rows // tile,),
        in_specs=[
            pl.BlockSpec((tile * TOP_K,), lambda i, fs, fl: (i,), memory_space=pltpu.SMEM),
            pl.BlockSpec((tile * ROW_SUB, LANES), lambda i, fs, fl: (i, 0)),
        ],
        out_specs=pl.BlockSpec(memory_space=pl.ANY),
        scratch_shapes=[pltpu.VMEM((tile // 2 * ROW_SUB, LANES), U32),
                        pltpu.SemaphoreType.DMA(()), pltpu.SemaphoreType.DMA(())],
    )
    return pl.pallas_call(
        functools.partial(_dispatch_body, tile=tile),
        grid_spec=grid_spec,
        out_shape=jax.ShapeDtypeStruct((out_rows * ROW_SUB, LANES), U32),
        compiler_params=pltpu.CompilerParams(dimension_semantics=("arbitrary",)),
        name="dispatch",
    )(fill_start, fill_len, dest_flat, n2p)


def _expert_body(blk_ref, exp_ref, newexp_ref, nextexp_ref, nw_ref,
                 x_ref, wgu_hbm, bgu_ref, wdn_hbm, bdn_ref, o_ref,
                 wgu32_ref, wdn32_ref, wgu16_ref, wdn16_ref, sems, *, tile):
    w = pl.program_id(0)
    cast_rows = 64

    def weight_copies(e):
        return (pltpu.make_async_copy(wgu_hbm.at[e], wgu32_ref, sems.at[0]),
                pltpu.make_async_copy(wdn_hbm.at[e], wdn32_ref, sems.at[1]))

    @pl.when((w < nw_ref[0]) & (newexp_ref[w] == 1))
    def _():
        @pl.when(w == 0)
        def _():
            for copy in weight_copies(exp_ref[0]):
                copy.start()

        for copy in weight_copies(exp_ref[w]):
            copy.wait()

        def cast(i, carry):
            rows = pl.ds(pl.multiple_of(i * cast_rows, cast_rows), cast_rows)
            wgu16_ref[rows, :] = wgu32_ref[rows, :].astype(BF16)
            wdn16_ref[rows, :] = wdn32_ref[rows, :].astype(BF16)
            return carry

        lax.fori_loop(0, D_MODEL // cast_rows, cast, 0)

        @pl.when(nextexp_ref[w] >= 0)
        def _():
            for copy in weight_copies(nextexp_ref[w]):
                copy.start()

    @pl.when(w < nw_ref[0])
    def _():
        sub = tile // EXPERT_SUBTILES
        parts = [_tokens(s * sub, sub) for s in range(EXPERT_SUBTILES)]
        xs = [_unpack_bf16_pairs(_load_rows(x_ref.at[p])).astype(BF16) for p in parts]
        gts = [jnp.minimum(_dot(x, wgu16_ref[:, :D_FF]) + bgu_ref[:, :D_FF], SWIGLU_LIMIT) for x in xs]
        ups = [jnp.clip(_dot(x, wgu16_ref[:, D_FF:]) + bgu_ref[:, D_FF:], -SWIGLU_LIMIT, SWIGLU_LIMIT)
               for x in xs]
        acts = [((up + 1.0) * gt * jax.nn.sigmoid(gt * SWIGLU_ALPHA)).astype(BF16) for gt, up in zip(gts, ups)]
        for p, act in zip(parts, acts):
            _store_rows(o_ref.at[p], _pack_bf16_pairs(_dot(act, wdn16_ref[...]) + bdn_ref[...]))


def _experts(tables, xs, w_gu, b_gu, w_dn, b_dn, tile, n_work):
    rows = xs.shape[0]
    row_block = pl.BlockSpec((tile * ROW_SUB, LANES), lambda w, blk, ex, ne, nx, nw: (blk[w], 0))
    grid_spec = pltpu.PrefetchScalarGridSpec(
        num_scalar_prefetch=5,
        grid=(n_work,),
        in_specs=[
            row_block,
            pl.BlockSpec(memory_space=pl.ANY),
            pl.BlockSpec((None, 1, 2 * D_FF), lambda w, blk, ex, ne, nx, nw: (ex[w], 0, 0)),
            pl.BlockSpec(memory_space=pl.ANY),
            pl.BlockSpec((None, 1, D_MODEL), lambda w, blk, ex, ne, nx, nw: (ex[w], 0, 0)),
        ],
        out_specs=row_block,
        scratch_shapes=[pltpu.VMEM((D_MODEL, 2 * D_FF), F32), pltpu.VMEM((D_FF, D_MODEL), F32),
                        pltpu.VMEM((D_MODEL, 2 * D_FF), BF16), pltpu.VMEM((D_FF, D_MODEL), BF16),
                        pltpu.SemaphoreType.DMA((2,))],
    )
    return pl.pallas_call(
        functools.partial(_expert_body, tile=tile),
        grid_spec=grid_spec,
        out_shape=jax.ShapeDtypeStruct(xs.shape, U32),
        compiler_params=pltpu.CompilerParams(
            dimension_semantics=("arbitrary",), vmem_limit_bytes=VMEM_LIMIT),
        name="experts",
    )(*tables, xs, w_gu, b_gu, w_dn, b_dn)


def _combine_body(dest_ref, dest_next_ref, gate_ref, h2_ref, nfw_ref, ys_hbm, o_ref, buf, sems,
                  *, tile, n_steps):
    i = pl.program_id(0)
    slot = i % 2
    other = 1 - slot

    def row_copy(d_ref, s, t, k):
        return pltpu.make_async_copy(ys_hbm.at[_tokens(d_ref[t * TOP_K + k], 1)], buf.at[s, k, _tokens(t, 1)], sems.at[s])

    def wait_slot(s):
        for k in range(TOP_K):
            pltpu.make_async_copy(ys_hbm.at[_tokens(0, tile)], buf.at[s, k], sems.at[s]).wait()

    @pl.when(i == 0)
    def _():
        def start(t, carry):
            for k in range(TOP_K):
                row_copy(dest_ref, 0, t, k).start(priority=k % 2)
            return carry

        lax.fori_loop(0, tile, start, 0)

    wait_slot(slot)
    nfw = nfw_ref[...]
    for c in range(tile // COMBINE_CHUNK):
        rows = pl.ds(c * COMBINE_CHUNK, COMBINE_CHUNK)
        gate = gate_ref[rows, :]
        h3 = h2_ref[rows, :]
        packed = [_load_rows(buf.at[slot, k, _tokens(c * COMBINE_CHUNK, COMBINE_CHUNK)]) for k in range(TOP_K)]
        for t in range(c * COMBINE_CHUNK, (c + 1) * COMBINE_CHUNK):
            for k in range(TOP_K):
                row_copy(dest_next_ref, other, t, k).start(priority=k % 2)
        for k in range(TOP_K):
            h3 = h3 + gate[:, k:k + 1] * _unpack_bf16_pairs(packed[k])
        o_ref[rows, :] = _rms(h3, nfw)

    @pl.when(i == n_steps - 1)
    def _():
        wait_slot(other)


def _combine(dest_flat, gates, h2, norm_f_w, ys, tile):
    rows = h2.shape[0]
    n_steps = rows // tile
    return pl.pallas_call(
        functools.partial(_combine_body, tile=tile, n_steps=n_steps),
        grid=(n_steps,),
        in_specs=[
            pl.BlockSpec((tile * TOP_K,), lambda i: (i,), memory_space=pltpu.SMEM),
            pl.BlockSpec((tile * TOP_K,), lambda i: (jnp.minimum(i + 1, n_steps - 1),),
                         memory_space=pltpu.SMEM),
            pl.BlockSpec((tile, TOP_K), lambda i: (i, 0)),
            pl.BlockSpec((tile, D_MODEL), lambda i: (i, 0)),
            pl.BlockSpec((1, D_MODEL), lambda i: (0, 0)),
            pl.BlockSpec(memory_space=pl.ANY),
        ],
        out_specs=pl.BlockSpec((tile, D_MODEL), lambda i: (i, 0)),
        out_shape=jax.ShapeDtypeStruct((rows, D_MODEL), F32),
        scratch_shapes=[pltpu.VMEM((2, TOP_K, tile * ROW_SUB, LANES), U32), pltpu.SemaphoreType.DMA((2,))],
        compiler_params=pltpu.CompilerParams(
            dimension_semantics=("arbitrary",), vmem_limit_bytes=VMEM_LIMIT),
        name="combine_final",
    )(dest_flat, dest_flat, gates, h2, norm_f_w, ys)


def _expert_tables(counts, n_assign, tile):
    n_work = n_assign // tile + N_EXPERTS
    nb = (counts + tile - 1) // tile
    bend = jnp.cumsum(nb)
    total = bend[-1]
    pstarts = (bend - nb) * tile
    blk = jnp.minimum(jnp.arange(n_work, dtype=I32), total - 1).astype(I32)
    ex = jnp.minimum(jnp.sum(bend[None, :] <= blk[:, None], axis=1), N_EXPERTS - 1).astype(I32)
    prev_ex = jnp.concatenate([jnp.full((1,), -1, I32), ex[:-1]])
    new_ex = (ex != prev_ex).astype(I32)
    after = jnp.sum(jnp.where(ex[:, None] == jnp.arange(N_EXPERTS, dtype=I32)[None, :], bend[None, :], 0), axis=1)
    idx_after = jnp.minimum(after, total - 1)
    pick = idx_after[:, None] == jnp.arange(n_work, dtype=I32)[None, :]
    next_ex = jnp.where(after < total, jnp.sum(jnp.where(pick, ex[None, :], 0), axis=1), -1).astype(I32)
    fill_start = (pstarts + counts).astype(I32)
    fill_len = (nb * tile - counts).astype(I32)
    return (blk, ex, new_ex, next_ex, total.reshape(1).astype(I32)), pstarts, fill_start, fill_len, n_work


def kernel(x, meta, norm1_w, w_in, gla_wa2_f, gla_ba2_f, gla_wa2_b, gla_ba2_b, gla_norm_w, conv_w, conv_b, dt_bias_f, dt_bias_b, a_log_f, a_log_b, ssd_d, ssd_norm_w, w_out, norm2_w, w_router, b_router, w_gu, b_gu, w_dn, b_dn, norm_f_w):
    bsz, seq, d = x.shape
    n_tok = bsz * seq
    l = 0

    wi = w_in[l]
    a_cols = wi[:, 3072:3104]
    dt_cols = wi[:, 5664:5696]
    w_perm = jnp.concatenate(
        [wi[:, :3072], wi[:, 3104:5664], a_cols, dt_cols,
         jnp.zeros((d, N_PROJ - COL_SMALL - 64), F32)], axis=1).astype(BF16)

    def lane_rows(w, lane0):
        return jnp.zeros((LANES, w.shape[1]), F32).at[lane0:lane0 + w.shape[0]].set(w)

    wa_f = lane_rows(gla_wa2_f[l], LANE_AF).astype(BF16)
    wa_b = lane_rows(gla_wa2_b[l], LANE_AB).astype(BF16)
    ba_f = gla_ba2_f[l][None, :]
    ba_b = gla_ba2_b[l][None, :]

    def lane_vec(vf, vb):
        z = jnp.zeros((1, LANES), F32)
        return z.at[0, LANE_DTF:LANE_DTF + SSD_HEADS].set(vf).at[0, LANE_DTB:LANE_DTB + SSD_HEADS].set(vb)

    dtb = lane_vec(dt_bias_f[l], dt_bias_b[l])
    alog = lane_vec(a_log_f[l], a_log_b[l])
    lane_id = (jnp.arange(2 * LANES) % LANES)[None, :, None]
    head_id = (jnp.arange(SSD_GROUP_CH) // SSD_HEAD_DIM)[None, None, :]
    grp = jnp.arange(SSD_GROUPS)[:, None, None] * SSD_GROUP_HEADS
    e_f = (lane_id == LANE_DTF + grp + head_id).astype(BF16)
    e_b = (lane_id == LANE_DTB + grp + head_id).astype(BF16)
    dexp = jnp.repeat(ssd_d[l], SSD_HEAD_DIM)[None, :]

    wr = jnp.zeros((d, LANES), F32).at[:, :N_EXPERTS].set(w_router[l])
    wr_hi = wr.astype(BF16)
    wr_lo = (wr - wr_hi.astype(F32)).astype(BF16)
    b_r = jnp.full((1, LANES), -1e30, F32).at[0, :N_EXPERTS].set(b_router[l])

    x2d = x.reshape(n_tok, d)
    x_meta = jnp.pad(meta.astype(F32), ((NPAD, 0), (0, 0)))
    n1 = norm1_w[l][None, :]
    proj = _inproj(x2d, n1, w_perm, PROJ_TILE).reshape(bsz, seq, N_PROJ)
    proj_meta = _inproj(x_meta, n1, w_perm, CHUNK)
    o_gla = _gla(proj, proj_meta, wa_f, ba_f, wa_b, ba_b, gla_norm_w[l][None, :])
    y_ssd = _ssd(proj, proj_meta, conv_w[l], conv_b[l][None, :], dtb, alog, e_f, e_b, dexp,
                 ssd_norm_w[l][None, :])

    wo = w_out[l].astype(BF16)
    h2, n2p, code, gates, cnt = _outproj(
        x2d, o_gla.reshape(n_tok, GLA_DV), y_ssd.reshape(n_tok, SSD_INNER),
        wo[:GLA_DV], wo[GLA_DV:], norm2_w[l][None, :], wr_hi, wr_lo, b_r, PROJ_TILE)

    counts = cnt[0, :N_EXPERTS].astype(I32)
    tables, starts, fill_start, fill_len, n_work = _expert_tables(counts, n_tok * TOP_K, EXPERT_TILE)
    expert_of = (code % N_EXPERTS)[..., None] == jnp.arange(N_EXPERTS, dtype=I32)
    dest = (code // N_EXPERTS + jnp.sum(jnp.where(expert_of, starts, 0), axis=-1)).astype(I32).reshape(-1)

    xs = _dispatch(fill_start, fill_len, dest, n2p, ROW_TILE, n_work * EXPERT_TILE)
    ys = _experts(tables, xs, w_gu[l], b_gu[l][:, None, :], w_dn[l], b_dn[l][:, None, :],
                  EXPERT_TILE, n_work)
    out = _combine(dest, gates, h2, norm_f_w[None, :], ys, COMBINE_TILE)
    return out.reshape(bsz, seq, d)
```

```python
import functools

import jax
import jax.numpy as jnp
from jax import lax
from jax.experimental import pallas as pl
from jax.experimental.pallas import tpu as pltpu

F32 = jnp.float32
BF16 = jnp.bfloat16
I32 = jnp.int32
U32 = jnp.uint32

D_MODEL = 1024
N_META = 16
CHUNK = 64
NPAD = CHUNK - N_META
GLA_HEADS = 4
GLA_DK = 512
GLA_DV = 1024
GLA_HEAD_K = GLA_DK // GLA_HEADS
GLA_HEAD_V = GLA_DV // GLA_HEADS
GLA_RANK = 16
GLA_GATE_NORM = 16.0
SSD_INNER = 1024
SSD_HEAD_DIM = 64
SSD_HEADS = SSD_INNER // SSD_HEAD_DIM
SSD_GROUPS = 2
SSD_GROUP_HEADS = SSD_HEADS // SSD_GROUPS
SSD_GROUP_CH = SSD_INNER // SSD_GROUPS
SSD_STATE = 128
SSD_CONV = 5
N_EXPERTS = 32
TOP_K = 4
D_FF = 1024
SWIGLU_LIMIT = 7.0
SWIGLU_ALPHA = 1.702
EPS = 1e-6

LANES = 128
ROW_WORDS = D_MODEL // 2
ROW_SUB = ROW_WORDS // LANES
HALO = 16

COL_Q = 0
COL_K = 512
COL_V = 1024
COL_G = 2048
COL_Z = 3072
COL_X = 4096
COL_B = 5120
COL_C = 5376
COL_SMALL = 5632
N_PROJ = 5760
LANE_AF = 0
LANE_AB = 16
LANE_DTF = 32
LANE_DTB = 48

ROW_TILE = 512
PROJ_TILE = 1024
EXPERT_TILE = 512
EXPERT_SUBTILES = 2
ROUTER_SUBTILES = 2
COMBINE_TILE = 256
COMBINE_CHUNK = 16
VMEM_LIMIT = 56 * 1024 * 1024


def _dot(a, b):
    return jnp.dot(a, b, preferred_element_type=F32)


def _dot_nt(a, b):
    return lax.dot_general(a, b, (((1,), (1,)), ((), ())), preferred_element_type=F32)


def _dot_tn(a, b):
    return lax.dot_general(a, b, (((0,), (0,)), ((), ())), preferred_element_type=F32)


def _split(x):
    hi = x.astype(BF16)
    lo = (x - hi.astype(F32)).astype(BF16)
    return hi, lo


def _sel_dot_l(m01, x):
    hi, lo = _split(x)
    return _dot(m01, jnp.concatenate([hi, lo], axis=0))


def _sel_dot_r(x, m01):
    hi, lo = _split(x)
    return _dot(jnp.concatenate([hi, lo], axis=1), m01)


def _twice_cols(m):
    return jnp.concatenate([m, m], axis=1)


def _softplus(x):
    return jnp.maximum(x, 0.0) + jnp.log(1.0 + jnp.exp(-jnp.abs(x)))


def _log_sigmoid(x):
    return jnp.minimum(x, 0.0) - jnp.log(1.0 + jnp.exp(-jnp.abs(x)))


def _silu(x):
    return x * jax.nn.sigmoid(x)


def _rms(x, w):
    return x * lax.rsqrt(jnp.mean(x * x, axis=-1, keepdims=True) + EPS) * w


def _pack_bf16_pairs(x):
    w = x.shape[1] // 2
    return pltpu.pack_elementwise([x[:, :w], x[:, w:]], packed_dtype=BF16)


def _unpack_bf16_pairs(p):
    lo, hi = (pltpu.unpack_elementwise(p, index=i, packed_dtype=BF16, unpacked_dtype=F32) for i in range(2))
    return jnp.concatenate([lo, hi], axis=1)


def _tokens(first, count):
    if not isinstance(first, int):
        first = pl.multiple_of(first * ROW_SUB, ROW_SUB)
    else:
        first = first * ROW_SUB
    return pl.ds(first, count * ROW_SUB)


def _load_rows(ref2):
    r = ref2.shape[0] // ROW_SUB
    return jnp.concatenate([ref2[pl.ds(j, r, stride=ROW_SUB), :] for j in range(ROW_SUB)], axis=1)


def _store_rows(ref2, val):
    r = ref2.shape[0] // ROW_SUB
    for j in range(ROW_SUB):
        ref2[pl.ds(j, r, stride=ROW_SUB), :] = val[:, j * LANES:(j + 1) * LANES]


def _tri(n, *, lower, inclusive):
    r = lax.broadcasted_iota(I32, (n, n), 0)
    c = lax.broadcasted_iota(I32, (n, n), 1)
    if lower:
        return (c <= r) if inclusive else (c < r)
    return (c >= r) if inclusive else (c > r)


def _inproj_body(x_ref, nw_ref, w_ref, o_ref, *, col_chunks):
    xn = _rms(x_ref[...], nw_ref[...]).astype(BF16)
    for lo, hi in col_chunks:
        o_ref[:, lo:hi] = _dot(xn, w_ref[:, lo:hi]).astype(BF16)


def _inproj(x2d, norm_w, w_perm, tile):
    rows = x2d.shape[0]
    tile = min(tile, rows)
    col_chunks = tuple((c, min(c + 1024, N_PROJ)) for c in range(0, N_PROJ, 1024))
    return pl.pallas_call(
        functools.partial(_inproj_body, col_chunks=col_chunks),
        grid=(rows // tile,),
        in_specs=[
            pl.BlockSpec((tile, D_MODEL), lambda i: (i, 0)),
            pl.BlockSpec((1, D_MODEL), lambda i: (0, 0)),
            pl.BlockSpec((D_MODEL, N_PROJ), lambda i: (0, 0), pipeline_mode=pl.Buffered(1)),
        ],
        out_specs=pl.BlockSpec((tile, N_PROJ), lambda i: (i, 0)),
        out_shape=jax.ShapeDtypeStruct((rows, N_PROJ), BF16),
        compiler_params=pltpu.CompilerParams(
            dimension_semantics=("arbitrary",), vmem_limit_bytes=VMEM_LIMIT),
        name="inproj",
    )(x2d, norm_w, w_perm)


GLA_HEADS_PER_STEP = 2
GLA_STEPS_PER_ITER = 8
SSD_STEPS_PER_ITER = 8


def _gla_body(q_ref, k_ref, v_ref, g_ref, sm_ref, qm_ref, km_ref, vm_ref, smm_ref,
              waf_ref, baf_ref, wab_ref, bab_ref, nw_ref, o_ref, st_ref, *, n_chunks):
    heads = GLA_HEADS_PER_STEP
    nprob = 2 * heads
    rows_all = nprob * CHUNK
    wk = heads * GLA_HEAD_K
    dk, dv = GLA_HEAD_K, GLA_HEAD_V

    tri_f = _tri(CHUNK, lower=True, inclusive=True)
    r2 = lax.broadcasted_iota(I32, (2 * CHUNK, 2 * CHUNK), 0)
    c2 = lax.broadcasted_iota(I32, (2 * CHUNK, 2 * CHUNK), 1)
    cum2 = ((r2 < CHUNK) & (c2 <= r2)) | ((r2 >= CHUNK) & (c2 >= r2))
    tri2 = _twice_cols(cum2.astype(BF16))
    ra = lax.broadcasted_iota(I32, (rows_all, rows_all), 0)
    ca = lax.broadcasted_iota(I32, (rows_all, rows_all), 1)
    same = (ra // CHUNK) == (ca // CHUNK)
    att_mask = same & (((ra < heads * CHUNK) & (ca <= ra)) | ((ra >= heads * CHUNK) & (ca > ra)))
    rb_ = lax.broadcasted_iota(I32, (rows_all, nprob * dk), 0) // CHUNK
    cb_ = lax.broadcasted_iota(I32, (rows_all, nprob * dk), 1) // dk
    own = rb_ == cb_

    waf, baf = waf_ref[...], baf_ref[...]
    wab, bab = wab_ref[...], bab_ref[...]
    wa_cat = jnp.concatenate([waf, wab], axis=1)
    nw = nw_ref[...]
    zero16 = jnp.zeros((), BF16)

    st_ref[...] = jnp.zeros_like(st_ref)

    meta_rows = lax.broadcasted_iota(I32, (CHUNK, wk), 0) >= NPAD
    lg = _log_sigmoid(_dot(smm_ref[...], waf) + baf) * (1.0 / GLA_GATE_NORM)
    bm = _sel_dot_l(_twice_cols(tri_f.astype(BF16)), jnp.where(meta_rows, lg, 0.0))
    for h in range(heads):
        lanes = slice(h * dk, (h + 1) * dk)
        bh, toth = bm[:, lanes], bm[CHUNK - 1:CHUNK, lanes]
        kend = (km_ref[:, lanes].astype(F32) * jnp.exp(toth - bh)).astype(BF16)
        st_ref[:, lanes] = _dot_tn(vm_ref[:, h * dv:(h + 1) * dv], kend)

    def stack(f, b, w):
        return jnp.concatenate([f[:, h * w:(h + 1) * w] for h in range(heads)]
                               + [b[:, h * w:(h + 1) * w] for h in range(heads)], axis=0)

    def steps(j, finalize):
        ids = [j * GLA_STEPS_PER_ITER + s for s in range(GLA_STEPS_PER_ITER)]
        rfs = [pl.ds(pl.multiple_of(i * CHUNK, CHUNK), CHUNK) for i in ids]
        rbs = [pl.ds(pl.multiple_of((n_chunks - 1 - i) * CHUNK, CHUNK), CHUNK) for i in ids]
        zs = [_dot(jnp.concatenate([sm_ref[rf, :], sm_ref[rb, :]], axis=0), wa_cat) for rf, rb in zip(rfs, rbs)]
        b2s = []
        for z in zs:
            lg = jnp.concatenate([z[:CHUNK, :wk] + baf, z[CHUNK:, wk:] + bab], axis=0)
            b2s.append(_sel_dot_l(tri2, _log_sigmoid(lg) * (1.0 / GLA_GATE_NORM)))
        vss, qds, kends, tots, gs = [], [], [], [], []
        for b2, rf, rb in zip(b2s, rfs, rbs):
            tot_f, tot_b = b2[CHUNK - 1:CHUNK], b2[CHUNK:CHUNK + 1]
            bst = stack(b2[:CHUNK], b2[CHUNK:], dk)
            tst = stack(jnp.broadcast_to(tot_f, (CHUNK, wk)), jnp.broadcast_to(tot_b, (CHUNK, wk)), dk)
            qs = stack(q_ref[rf, :], q_ref[rb, :], dk).astype(F32)
            ks = stack(k_ref[rf, :], k_ref[rb, :], dk).astype(F32)
            qd = (qs * (GLA_HEAD_K ** -0.5) * jnp.exp(bst)).astype(BF16)
            kd = (ks * jnp.exp(-bst)).astype(BF16)
            vss.append(stack(v_ref[rf, :], v_ref[rb, :], dv))
            qds.append(qd)
            kends.append((ks * jnp.exp(tst - bst)).astype(BF16))
            tots.append(jnp.concatenate([tot_f, tot_b], axis=1))
            gs.append(_dot_nt(qd, kd))
        intras = [_dot(jnp.where(att_mask, g, 0.0).astype(BF16), vs) for g, vs in zip(gs, vss)]
        upds = [_dot_tn(vs, jnp.where(own, jnp.concatenate([kend] * nprob, axis=1), zero16))
                for vs, kend in zip(vss, kends)]
        st = st_ref[...]
        outs = []
        for qd, tot, upd, intra in zip(qds, tots, upds, intras):
            qd_own = jnp.where(own, jnp.concatenate([qd] * nprob, axis=1), zero16)
            outs.append(intra + _dot_nt(qd_own, st.astype(BF16)))
            st = st * jnp.exp(tot) + upd
        st_ref[...] = st
        for out, rf, rb in zip(outs, rfs, rbs):
            for p in range(nprob):
                rows, h = (rf if p < heads else rb), p % heads
                o = out[p * CHUNK:(p + 1) * CHUNK]
                cols = slice(h * dv, (h + 1) * dv)
                if finalize:
                    o = o + o_ref[rows, cols].astype(F32)
                    gate = _silu(g_ref[rows, cols].astype(F32))
                    o_ref[rows, cols] = (_rms(o, nw) * gate).astype(BF16)
                else:
                    o_ref[rows, cols] = o.astype(BF16)

    def first_half(j, carry):
        steps(j, False)
        return carry

    def second_half(j, carry):
        steps(j, True)
        return carry

    half_iters = n_chunks // 2 // GLA_STEPS_PER_ITER
    lax.fori_loop(0, half_iters, first_half, 0)
    lax.fori_loop(half_iters, 2 * half_iters, second_half, 0)


def _gla(proj, proj_meta, wa_f, ba_f, wa_b, ba_b, norm_w):
    bsz, seq, _ = proj.shape
    assert (seq // CHUNK) % (2 * GLA_STEPS_PER_ITER) == 0
    hp = GLA_HEADS_PER_STEP
    wk, wv = hp * GLA_HEAD_K, hp * GLA_HEAD_V
    kb, vb, gb, sb = COL_K // wk, COL_V // wv, COL_G // wv, COL_SMALL // LANES
    real = lambda width, base: pl.BlockSpec((None, seq, width), lambda b, h: (b, 0, base + h))
    meta = lambda width, base: pl.BlockSpec((CHUNK, width), lambda b, h: (0, base + h))
    per_step = lambda rows_: pl.BlockSpec((rows_, wk), lambda b, h: (0, h))
    return pl.pallas_call(
        functools.partial(_gla_body, n_chunks=seq // CHUNK),
        grid=(bsz, GLA_HEADS // hp),
        in_specs=[
            real(wk, 0), real(wk, kb), real(wv, vb), real(wv, gb),
            pl.BlockSpec((None, seq, LANES), lambda b, h: (b, 0, sb)),
            meta(wk, 0), meta(wk, kb), meta(wv, vb),
            pl.BlockSpec((CHUNK, LANES), lambda b, h: (0, sb)),
            per_step(LANES), per_step(1), per_step(LANES), per_step(1),
            pl.BlockSpec((1, GLA_HEAD_V), lambda b, h: (0, 0)),
        ],
        out_specs=pl.BlockSpec((None, seq, wv), lambda b, h: (b, 0, h)),
        out_shape=jax.ShapeDtypeStruct((bsz, seq, GLA_DV), BF16),
        scratch_shapes=[pltpu.VMEM((GLA_HEAD_V, 2 * hp * GLA_HEAD_K), F32)],
        compiler_params=pltpu.CompilerParams(
            dimension_semantics=("arbitrary", "arbitrary"), vmem_limit_bytes=VMEM_LIMIT),
        name="gla",
    )(proj, proj, proj, proj, proj, proj_meta, proj_meta, proj_meta, proj_meta,
      wa_f, ba_f, wa_b, ba_b, norm_w)


def _conv_silu(win, cw, cb):
    half = (SSD_CONV - 1) // 2
    acc = cb
    for j in range(SSD_CONV):
        lo = HALO - half + j
        acc = acc + win[lo:lo + CHUNK, :] * cw[j:j + 1, :]
    return _silu(acc)


def _ssd_body(x_ref, z_ref, b_ref, c_ref, sm_ref, xm_ref, bm_ref, cm_ref, smm_ref,
              cwx_ref, cbx_ref, cwb_ref, cbb_ref, cwc_ref, cbc_ref, dtb_ref, alog_ref,
              ef_ref, eb_ref, dexp_ref, nw_ref, o_ref, st_ref, xc_ref, bc_ref, cc_ref,
              *, n_chunks):
    gh, p, n = SSD_GROUP_HEADS, SSD_HEAD_DIM, SSD_STATE
    width = gh * p
    two = 2 * CHUNK
    lane_s = lax.broadcasted_iota(I32, (two, width), 1) % p
    row2 = lax.broadcasted_iota(I32, (two, width), 0)
    row_t = row2 % CHUNK
    irep2 = lane_s == row_t
    pair_mask = ((row2 < CHUNK) & (lane_s <= row_t)) | ((row2 >= CHUNK) & (lane_s > row_t))
    r2 = lax.broadcasted_iota(I32, (two, two), 0)
    c2 = lax.broadcasted_iota(I32, (two, two), 1)
    same_dir = (r2 < CHUNK) == (c2 < CHUNK)
    ones2 = _twice_cols(same_dir.astype(BF16))
    cum2 = ((r2 < CHUNK) & (c2 <= r2)) | ((r2 >= CHUNK) & (c2 >= r2))
    tri2 = _twice_cols(cum2.astype(BF16))
    bd_r = lax.broadcasted_iota(I32, (width, width), 0) // p
    bd_c = lax.broadcasted_iota(I32, (width, width), 1) // p
    bdmask = bd_r == bd_c
    zero16 = jnp.zeros((), BF16)

    cwx, cbx = cwx_ref[...], cbx_ref[...]
    cwb, cbb = cwb_ref[...], cbb_ref[...]
    cwc, cbc = cwc_ref[...], cbc_ref[...]
    dtb = dtb_ref[...]
    a_row = -jnp.exp(alog_ref[...])
    ef, eb = ef_ref[...], eb_ref[...]
    dexp, nw = dexp_ref[...], nw_ref[...]
    seq = n_chunks * CHUNK

    def window(ref, mref, r):
        off = pl.multiple_of(r * CHUNK, CHUNK)
        poff = pl.multiple_of(jnp.maximum(off - HALO, 0), HALO)
        noff = pl.multiple_of(jnp.minimum(off + CHUNK, seq - HALO), HALO)
        prev = jnp.where(r == 0, mref[CHUNK - HALO:, :], ref[pl.ds(poff, HALO), :])
        nxt = jnp.where(r == n_chunks - 1, zero16, ref[pl.ds(noff, HALO), :])
        return jnp.concatenate([prev, ref[pl.ds(off, CHUNK), :], nxt], axis=0)

    half = (SSD_CONV - 1) // 2
    side_taps = [j for j in range(SSD_CONV) if j != half]
    win_rows = CHUNK + 2 * HALO
    sr = lax.broadcasted_iota(I32, (len(side_taps) * CHUNK, win_rows), 0)
    sc = lax.broadcasted_iota(I32, (len(side_taps) * CHUNK, win_rows), 1)
    tap_of = sr // CHUNK
    tap_shift = jnp.where(tap_of < half, tap_of, tap_of + 1) - half
    shift_mat = (sc == (sr % CHUNK) + HALO + tap_shift).astype(BF16)
    cw_all = jnp.concatenate([cwx, cwb, cwc], axis=1)
    cb_all = jnp.concatenate([cbx, cbb, cbc], axis=1)

    def conv_body(it, carry):
        chunks = [it * SSD_STEPS_PER_ITER + s for s in range(SSD_STEPS_PER_ITER)]
        wins = [jnp.concatenate([window(x_ref, xm_ref, r), window(b_ref, bm_ref, r), window(c_ref, cm_ref, r)],
                                axis=1) for r in chunks]
        shifts = [_dot(shift_mat, win) for win in wins]
        for r, win, shifted in zip(chunks, wins, shifts):
            rows = pl.ds(pl.multiple_of(r * CHUNK, CHUNK), CHUNK)
            acc = cb_all + win[HALO:HALO + CHUNK].astype(F32) * cw_all[half:half + 1]
            for pos, j in enumerate(side_taps):
                acc = acc + shifted[pos * CHUNK:(pos + 1) * CHUNK] * cw_all[j:j + 1]
            y = _silu(acc).astype(BF16)
            xc_ref[rows, :] = y[:, :width]
            bc_ref[rows, :] = y[:, width:width + n]
            cc_ref[rows, :] = y[:, width + n:]
        return carry

    lax.fori_loop(0, n_chunks // SSD_STEPS_PER_ITER, conv_body, 0)

    st_ref[...] = jnp.zeros_like(st_ref)

    def meta_window(mref, ref):
        zeros = jnp.zeros((HALO, mref.shape[1]), F32)
        return jnp.concatenate([zeros, mref[...].astype(F32), ref[0:HALO, :].astype(F32)], axis=0)

    def meta_mask(width_):
        return lax.broadcasted_iota(I32, (CHUNK, width_), 0) >= NPAD

    xc = jnp.where(meta_mask(width), _conv_silu(meta_window(xm_ref, x_ref), cwx, cbx), 0.0)
    bc = jnp.where(meta_mask(n), _conv_silu(meta_window(bm_ref, b_ref), cwb, cbb), 0.0)
    dt = jnp.where(meta_mask(LANES), _softplus(smm_ref[...].astype(F32) + dtb), 0.0)
    cs = _sel_dot_l(_twice_cols(cum2[:CHUNK, :CHUNK].astype(BF16)), dt * a_row)
    both = _sel_dot_r(jnp.concatenate([cs, dt], axis=0), ef)
    cs_e, dt_e = both[:CHUNK], both[CHUNK:]
    xend = (xc * dt_e * jnp.exp(cs_e[CHUNK - 1:CHUNK] - cs_e)).astype(BF16)
    st_ref[0:n, :] = _dot_tn(bc.astype(BF16), xend)

    zeros_n = jnp.zeros((CHUNK, n), BF16)

    def own_dir(a):
        return jnp.concatenate([jnp.concatenate([a[:CHUNK], zeros_n], axis=1),
                                jnp.concatenate([zeros_n, a[CHUNK:]], axis=1)], axis=0)

    def steps(j, finalize):
        ids = [j * SSD_STEPS_PER_ITER + s for s in range(SSD_STEPS_PER_ITER)]
        rfs = [pl.ds(pl.multiple_of(i * CHUNK, CHUNK), CHUNK) for i in ids]
        rbs = [pl.ds(pl.multiple_of((n_chunks - 1 - i) * CHUNK, CHUNK), CHUNK) for i in ids]
        both_rows = lambda ref, rf, rb: jnp.concatenate([ref[rf, :], ref[rb, :]], axis=0)
        dts = [_softplus(both_rows(sm_ref, rf, rb).astype(F32) + dtb) for rf, rb in zip(rfs, rbs)]
        css = [_sel_dot_l(tri2, dt * a_row) for dt in dts]
        exs = [(_sel_dot_r(jnp.concatenate([cs[:CHUNK], dt[:CHUNK]], axis=0), ef),
                _sel_dot_r(jnp.concatenate([cs[CHUNK:], dt[CHUNK:]], axis=0), eb)) for cs, dt in zip(css, dts)]
        xcs, cs_es, tots, xdts, xends, cbs, bcs, ccs = [], [], [], [], [], [], [], []
        for (ex_f, ex_b), rf, rb in zip(exs, rfs, rbs):
            cs_e = jnp.concatenate([ex_f[:CHUNK], ex_b[:CHUNK]], axis=0)
            dt_e = jnp.concatenate([ex_f[CHUNK:], ex_b[CHUNK:]], axis=0)
            tot_f, tot_b = cs_e[CHUNK - 1:CHUNK], cs_e[CHUNK:CHUNK + 1]
            tot = jnp.concatenate([jnp.broadcast_to(tot_f, (CHUNK, width)),
                                   jnp.broadcast_to(tot_b, (CHUNK, width))], axis=0)
            xc = both_rows(xc_ref, rf, rb).astype(F32)
            bc16, cc16 = both_rows(bc_ref, rf, rb), both_rows(cc_ref, rf, rb)
            xdt = xc * dt_e
            xcs.append(xc)
            cs_es.append(cs_e)
            tots.append((tot_f, tot_b))
            xdts.append(xdt.astype(BF16))
            xends.append((xdt * jnp.exp(tot - cs_e)).astype(BF16))
            bcs.append(bc16)
            ccs.append(cc16)
            cbs.append(_dot_nt(cc16, bc16))
        cb_reps = [jnp.concatenate([jnp.where(same_dir, cb, pltpu.roll(cb, CHUNK, 1))] * (width // two), axis=1)
                   for cb in cbs]
        cs_rows = [_sel_dot_l(ones2, jnp.where(irep2, cs_e, 0.0)) for cs_e in cs_es]
        intras = []
        for cb_rep, cs_row, cs_e, xdt16 in zip(cb_reps, cs_rows, cs_es, xdts):
            decay = jnp.where(pair_mask, jnp.exp(jnp.minimum(cs_e - cs_row, 0.0)), 0.0)
            w = (cb_rep * decay).astype(BF16)
            intras.append(jnp.concatenate(
                [_dot(w[d * CHUNK:(d + 1) * CHUNK],
                      jnp.where(bdmask, jnp.concatenate([xdt16[d * CHUNK:(d + 1) * CHUNK]] * gh, axis=0), zero16))
                 for d in range(2)], axis=0))
        upds = [_dot_tn(own_dir(bc16), xend) for bc16, xend in zip(bcs, xends)]
        st = st_ref[...]
        ys = []
        for intra, cc16, cs_e, (tot_f, tot_b), upd in zip(intras, ccs, cs_es, tots, upds):
            ys.append(intra + _dot(own_dir(cc16), st.astype(BF16)) * jnp.exp(cs_e))
            grow = jnp.concatenate([jnp.broadcast_to(jnp.exp(tot_f), (n, width)),
                                    jnp.broadcast_to(jnp.exp(tot_b), (n, width))], axis=0)
            st = st * grow + upd
        st_ref[...] = st
        for y, xc, rf, rb in zip(ys, xcs, rfs, rbs):
            for d, rows in enumerate((rf, rb)):
                yd = y[d * CHUNK:(d + 1) * CHUNK]
                if finalize:
                    yd = yd + o_ref[rows, :].astype(F32) + xc[d * CHUNK:(d + 1) * CHUNK] * dexp
                    yd = yd * _silu(z_ref[rows, :].astype(F32))
                    o_ref[rows, :] = _rms(yd, nw).astype(BF16)
                else:
                    o_ref[rows, :] = yd.astype(BF16)

    def first_half(j, carry):
        steps(j, False)
        return carry

    def second_half(j, carry):
        steps(j, True)
        return carry

    half_iters = n_chunks // 2 // SSD_STEPS_PER_ITER
    lax.fori_loop(0, half_iters, first_half, 0)
    lax.fori_loop(half_iters, 2 * half_iters, second_half, 0)


def _ssd(proj, proj_meta, conv_w, conv_b, dtb, alog, e_f, e_b, dexp, norm_w):
    bsz, seq, _ = proj.shape
    gc = SSD_GROUP_CH
    xb, zb = COL_X // gc, COL_Z // gc
    bb, cb, sb = COL_B // SSD_STATE, COL_C // SSD_STATE, COL_SMALL // LANES
    real = lambda width, base: pl.BlockSpec((None, seq, width), lambda b, g: (b, 0, base + g))
    meta = lambda width, base: pl.BlockSpec((CHUNK, width), lambda b, g: (0, base + g))
    cpar = lambda rows_, width, base: pl.BlockSpec((rows_, width), lambda b, g: (0, base + g))
    cbb_, ccb_ = SSD_INNER // SSD_STATE, SSD_INNER // SSD_STATE + SSD_GROUPS
    const = lambda shape: pl.BlockSpec(shape, lambda b, g: (0, 0))
    return pl.pallas_call(
        functools.partial(_ssd_body, n_chunks=seq // CHUNK),
        grid=(bsz, SSD_GROUPS),
        in_specs=[
            real(gc, xb), real(gc, zb), real(SSD_STATE, bb), real(SSD_STATE, cb),
            pl.BlockSpec((None, seq, LANES), lambda b, g: (b, 0, sb)),
            meta(gc, xb), meta(SSD_STATE, bb), meta(SSD_STATE, cb),
            pl.BlockSpec((CHUNK, LANES), lambda b, g: (0, sb)),
            cpar(SSD_CONV, gc, 0), cpar(1, gc, 0),
            cpar(SSD_CONV, SSD_STATE, cbb_), cpar(1, SSD_STATE, cbb_),
            cpar(SSD_CONV, SSD_STATE, ccb_), cpar(1, SSD_STATE, ccb_),
            const((1, LANES)), const((1, LANES)),
            pl.BlockSpec((None, 2 * LANES, gc), lambda b, g: (g, 0, 0)),
            pl.BlockSpec((None, 2 * LANES, gc), lambda b, g: (g, 0, 0)),
            pl.BlockSpec((1, gc), lambda b, g: (0, g)),
            pl.BlockSpec((1, gc), lambda b, g: (0, g)),
        ],
        out_specs=pl.BlockSpec((None, seq, gc), lambda b, g: (b, 0, g)),
        out_shape=jax.ShapeDtypeStruct((bsz, seq, SSD_INNER), BF16),
        scratch_shapes=[pltpu.VMEM((2 * SSD_STATE, gc), F32),
                        pltpu.VMEM((seq, gc), BF16), pltpu.VMEM((seq, SSD_STATE), BF16),
                        pltpu.VMEM((seq, SSD_STATE), BF16)],
        compiler_params=pltpu.CompilerParams(
            dimension_semantics=("arbitrary", "arbitrary"), vmem_limit_bytes=VMEM_LIMIT),
        name="ssd",
    )(proj, proj, proj, proj, proj, proj_meta, proj_meta, proj_meta, proj_meta,
      conv_w, conv_b, conv_w, conv_b, conv_w, conv_b, dtb, alog, e_f, e_b, dexp, norm_w)


def _outproj_body(x_ref, og_ref, ys_ref, wo1_ref, wo2_ref, n2w_ref, wrh_ref, wrl_ref, br_ref,
                  h2_ref, n2p_ref, code_ref, gate_ref, cnt_ref, carry_ref, *, tile):
    i = pl.program_id(0)

    @pl.when(i == 0)
    def _():
        carry_ref[...] = jnp.zeros_like(carry_ref)

    sub = tile // ROUTER_SUBTILES
    parts = [pl.ds(s * sub, sub) for s in range(ROUTER_SUBTILES)]
    wo1, wo2, n2w = wo1_ref[...], wo2_ref[...], n2w_ref[...]
    wrh, wrl, br = wrh_ref[...], wrl_ref[...], br_ref[...]
    h2s = [x_ref[p, :] + _dot(og_ref[p, :], wo1) + _dot(ys_ref[p, :], wo2) for p in parts]
    n2s = []
    for p, h2 in zip(parts, h2s):
        h2_ref[p, :] = h2
        n2 = _rms(h2, n2w)
        _store_rows(n2p_ref.at[_tokens(p.start, sub)], _pack_bf16_pairs(n2))
        n2s.append(n2)
    logit_parts = []
    for n2 in n2s:
        nh, nl = _split(n2)
        logit_parts.append(_dot(nh, wrh) + _dot(nh, wrl) + _dot(nl, wrh) + br)

    lane = lax.broadcasted_iota(I32, (sub, LANES), 1)
    lane_f = lane.astype(F32)
    lane4 = lax.broadcasted_iota(I32, (sub, TOP_K), 1)
    per_row = LANES // TOP_K
    tok = lax.broadcasted_iota(I32, (sub, LANES), 0)
    here = (lane // TOP_K) == (tok % per_row)
    gather_rows = (lax.broadcasted_iota(I32, (sub // per_row, sub), 1) // per_row
                   == lax.broadcasted_iota(I32, (sub // per_row, sub), 0)).astype(BF16)
    before_me = _tri(sub, lower=True, inclusive=False).astype(BF16)

    routed = []
    for p, logits in zip(parts, logit_parts):
        vals, onehots, picks = [], [], []
        work = logits
        for k in range(TOP_K):
            m = jnp.max(work, axis=-1, keepdims=True)
            first = jnp.min(jnp.where(work == m, lane_f, float(LANES)), axis=-1, keepdims=True)
            oh = lane_f == first
            work = jnp.where(oh, -jnp.inf, work)
            vals.append(m)
            onehots.append(oh)
            picks.append(first)
        exps = [jnp.exp(v - vals[0]) for v in vals]
        inv = 1.0 / (exps[0] + exps[1] + exps[2] + exps[3])
        gate_out = jnp.zeros((sub, TOP_K), F32)
        for k in range(TOP_K):
            gate_out = jnp.where(lane4 == k, exps[k] * inv, gate_out)
        gate_ref[p, :] = gate_out
        any_oh = (onehots[0] | onehots[1] | onehots[2] | onehots[3])
        any16 = jnp.where(any_oh, 1.0, 0.0).astype(BF16)
        routed.append((onehots, picks, any16, _dot(before_me, any16)))

    carry = carry_ref[...]
    for s, (onehots, picks, any16, before) in enumerate(routed):
        before = before + carry
        carry = carry + jnp.sum(any16.astype(F32), axis=0, keepdims=True)
        rest = [jnp.sum(jnp.where(onehots[k], before, 0.0), axis=-1, keepdims=True) * N_EXPERTS + picks[k]
                for k in range(TOP_K)]
        flat = jnp.zeros((sub // per_row, LANES), F32)
        for scale in (65536.0, 256.0, 1.0):
            piece = [jnp.floor(c * (1.0 / scale)) for c in rest]
            rest = [c - q * scale for c, q in zip(rest, piece)]
            by_k = piece[TOP_K - 1]
            for k in range(TOP_K - 2, -1, -1):
                by_k = jnp.where(lane % TOP_K == k, piece[k], by_k)
            flat = flat + scale * _dot(gather_rows, jnp.where(here, by_k, 0.0).astype(BF16))
        code_ref[pl.ds(s * (sub // per_row), sub // per_row), :] = flat.astype(I32)
    carry_ref[...] = carry
    cnt_ref[...] = carry


def _outproj(x2d, o_gla, y_ssd, w_out1, w_out2, norm2_w, wr_hi, wr_lo, b_r, tile):
    rows = x2d.shape[0]
    tile = min(tile, rows)
    row = lambda width: pl.BlockSpec((tile, width), lambda i: (i, 0))
    const = lambda shape: pl.BlockSpec(shape, lambda i: (0, 0))
    return pl.pallas_call(
        functools.partial(_outproj_body, tile=tile),
        grid=(rows // tile,),
        in_specs=[
            row(D_MODEL), row(GLA_DV), row(SSD_INNER),
            const((GLA_DV, D_MODEL)), const((SSD_INNER, D_MODEL)), const((1, D_MODEL)),
            const((D_MODEL, LANES)), const((D_MODEL, LANES)), const((1, LANES)),
        ],
        out_specs=[
            row(D_MODEL), pl.BlockSpec((tile * ROW_SUB, LANES), lambda i: (i, 0)),
            pl.BlockSpec((tile * TOP_K // LANES, LANES), lambda i: (i, 0)), row(TOP_K), const((1, LANES)),
        ],
        out_shape=[
            jax.ShapeDtypeStruct((rows, D_MODEL), F32),
            jax.ShapeDtypeStruct((rows * ROW_SUB, LANES), U32),
            jax.ShapeDtypeStruct((rows * TOP_K // LANES, LANES), I32),
            jax.ShapeDtypeStruct((rows, TOP_K), F32),
            jax.ShapeDtypeStruct((1, LANES), F32),
        ],
        scratch_shapes=[pltpu.VMEM((1, LANES), F32)],
        compiler_params=pltpu.CompilerParams(
            dimension_semantics=("arbitrary",), vmem_limit_bytes=VMEM_LIMIT),
        name="outproj_router",
    )(x2d, o_gla, y_ssd, w_out1, w_out2, norm2_w, wr_hi, wr_lo, b_r)


def _row_copy(src, src_row, dst, dst_row, sem):
    return pltpu.make_async_copy(src.at[_tokens(src_row, 1)], dst.at[_tokens(dst_row, 1)], sem)


def _dispatch_body(fill_start_ref, fill_len_ref, used_ref, dest_ref, n2p_ref, xs_hbm, zeros_ref, sem, fill_sem,
                   *, tile, n_blocks):
    @pl.when(pl.program_id(0) == 0)
    def _():
        zeros_ref[...] = jnp.zeros_like(zeros_ref)
        half = tile // 2

        def fill(cond, pos, size, wait):
            @pl.when(cond)
            def _():
                copy = pltpu.make_async_copy(zeros_ref.at[_tokens(0, size)], xs_hbm.at[_tokens(pos, size)], fill_sem)
                if wait:
                    copy.wait()
                else:
                    copy.start()

        for wait in (False, True):
            for e in range(N_EXPERTS):
                length = fill_len_ref[e]
                size = 1
                while size < tile:
                    fill((length & size) != 0, fill_start_ref[e] + (length & (size - 1)), size, wait)
                    size *= 2
                blk = used_ref[0] + e
                for part in range(2):
                    fill(blk < n_blocks, blk * tile + part * half, half, wait)

    for t in range(tile):
        for k in range(TOP_K):
            _row_copy(n2p_ref, t, xs_hbm, dest_ref[t * TOP_K + k], sem).start(priority=k % 2)

    for k in range(TOP_K):
        pltpu.make_async_copy(n2p_ref, xs_hbm.at[_tokens(0, tile)], sem).wait()


def _dispatch(fill_start, fill_len, used_blocks, dest_flat, n2p, tile, out_rows):
    rows = n2p.shape[0] // ROW_SUB
    grid_spec = pltpu.PrefetchScalarGridSpec(
        num_scalar_prefetch=3,
        grid=(rows // tile,),
        in_specs=[
            pl.BlockSpec((tile * TOP_K,), lambda i, fs, fl, ub: (i,), memory_space=pltpu.SMEM),
            pl.BlockSpec((tile * ROW_SUB, LANES), lambda i, fs, fl, ub: (i, 0)),
        ],
        out_specs=pl.BlockSpec(memory_space=pl.ANY),
        scratch_shapes=[pltpu.VMEM((tile // 2 * ROW_SUB, LANES), U32),
                        pltpu.SemaphoreType.DMA(()), pltpu.SemaphoreType.DMA(())],
    )
    return pl.pallas_call(
        functools.partial(_dispatch_body, tile=tile, n_blocks=out_rows // tile),
        grid_spec=grid_spec,
        out_shape=jax.ShapeDtypeStruct((out_rows * ROW_SUB, LANES), U32),
        compiler_params=pltpu.CompilerParams(dimension_semantics=("arbitrary",)),
        name="dispatch",
    )(fill_start, fill_len, used_blocks, dest_flat, n2p)


def _expert_body(blk_ref, exp_ref, newexp_ref, nextexp_ref, nw_ref,
                 x_ref, wgu_hbm, bgu_ref, wdn_hbm, bdn_ref, o_ref,
                 wgu32_ref, wdn32_ref, wgu16_ref, wdn16_ref, sems, *, tile):
    w = pl.program_id(0)
    cast_rows = 64

    def weight_copies(e):
        return (pltpu.make_async_copy(wgu_hbm.at[e], wgu32_ref, sems.at[0]),
                pltpu.make_async_copy(wdn_hbm.at[e], wdn32_ref, sems.at[1]))

    @pl.when((w < nw_ref[0]) & (newexp_ref[w] == 1))
    def _():
        @pl.when(w == 0)
        def _():
            for copy in weight_copies(exp_ref[0]):
                copy.start()

        for copy in weight_copies(exp_ref[w]):
            copy.wait()

        def cast(i, carry):
            rows = pl.ds(pl.multiple_of(i * cast_rows, cast_rows), cast_rows)
            wgu16_ref[rows, :] = wgu32_ref[rows, :].astype(BF16)
            wdn16_ref[rows, :] = wdn32_ref[rows, :].astype(BF16)
            return carry

        lax.fori_loop(0, D_MODEL // cast_rows, cast, 0)

        @pl.when(nextexp_ref[w] >= 0)
        def _():
            for copy in weight_copies(nextexp_ref[w]):
                copy.start()

    @pl.when(w < nw_ref[0])
    def _():
        sub = tile // EXPERT_SUBTILES
        parts = [_tokens(s * sub, sub) for s in range(EXPERT_SUBTILES)]
        xs = [_unpack_bf16_pairs(_load_rows(x_ref.at[p])).astype(BF16) for p in parts]
        gts = [jnp.minimum(_dot(x, wgu16_ref[:, :D_FF]) + bgu_ref[:, :D_FF], SWIGLU_LIMIT) for x in xs]
        ups = [jnp.clip(_dot(x, wgu16_ref[:, D_FF:]) + bgu_ref[:, D_FF:], -SWIGLU_LIMIT, SWIGLU_LIMIT)
               for x in xs]
        acts = [((up + 1.0) * gt * jax.nn.sigmoid(gt * SWIGLU_ALPHA)).astype(BF16) for gt, up in zip(gts, ups)]
        for p, act in zip(parts, acts):
            _store_rows(o_ref.at[p], _pack_bf16_pairs(_dot(act, wdn16_ref[...]) + bdn_ref[...]))


def _experts(tables, xs, w_gu, b_gu, w_dn, b_dn, tile, n_work):
    rows = xs.shape[0]
    row_block = pl.BlockSpec((tile * ROW_SUB, LANES), lambda w, blk, ex, ne, nx, nw: (blk[w], 0))
    grid_spec = pltpu.PrefetchScalarGridSpec(
        num_scalar_prefetch=5,
        grid=(n_work,),
        in_specs=[
            row_block,
            pl.BlockSpec(memory_space=pl.ANY),
            pl.BlockSpec((None, 1, 2 * D_FF), lambda w, blk, ex, ne, nx, nw: (ex[w], 0, 0)),
            pl.BlockSpec(memory_space=pl.ANY),
            pl.BlockSpec((None, 1, D_MODEL), lambda w, blk, ex, ne, nx, nw: (ex[w], 0, 0)),
        ],
        out_specs=row_block,
        scratch_shapes=[pltpu.VMEM((D_MODEL, 2 * D_FF), F32), pltpu.VMEM((D_FF, D_MODEL), F32),
                        pltpu.VMEM((D_MODEL, 2 * D_FF), BF16), pltpu.VMEM((D_FF, D_MODEL), BF16),
                        pltpu.SemaphoreType.DMA((2,))],
    )
    return pl.pallas_call(
        functools.partial(_expert_body, tile=tile),
        grid_spec=grid_spec,
        out_shape=jax.ShapeDtypeStruct(xs.shape, U32),
        input_output_aliases={len(tables): 0},
        compiler_params=pltpu.CompilerParams(
            dimension_semantics=("arbitrary",), vmem_limit_bytes=VMEM_LIMIT),
        name="experts",
    )(*tables, xs, w_gu, b_gu, w_dn, b_dn)


def _combine_body(dest_ref, dest_next_ref, gate_ref, h2_ref, nfw_ref, ys_hbm, o_ref, buf, sems,
                  *, tile, n_steps):
    i = pl.program_id(0)
    slot = i % 2
    other = 1 - slot

    def row_copy(d_ref, s, t, k):
        return pltpu.make_async_copy(ys_hbm.at[_tokens(d_ref[t * TOP_K + k], 1)], buf.at[s, k, _tokens(t, 1)], sems.at[s])

    def wait_slot(s):
        for k in range(TOP_K):
            pltpu.make_async_copy(ys_hbm.at[_tokens(0, tile)], buf.at[s, k], sems.at[s]).wait()

    @pl.when(i == 0)
    def _():
        def start(t, carry):
            for k in range(TOP_K):
                row_copy(dest_ref, 0, t, k).start(priority=k % 2)
            return carry

        lax.fori_loop(0, tile, start, 0)

    wait_slot(slot)
    nfw = nfw_ref[...]
    for c in range(tile // COMBINE_CHUNK):
        rows = pl.ds(c * COMBINE_CHUNK, COMBINE_CHUNK)
        gate = gate_ref[rows, :]
        h3 = h2_ref[rows, :]
        packed = [_load_rows(buf.at[slot, k, _tokens(c * COMBINE_CHUNK, COMBINE_CHUNK)]) for k in range(TOP_K)]
        for t in range(c * COMBINE_CHUNK, (c + 1) * COMBINE_CHUNK):
            for k in range(TOP_K):
                row_copy(dest_next_ref, other, t, k).start(priority=k % 2)
        for k in range(TOP_K):
            h3 = h3 + gate[:, k:k + 1] * _unpack_bf16_pairs(packed[k])
        o_ref[rows, :] = _rms(h3, nfw)

    @pl.when(i == n_steps - 1)
    def _():
        wait_slot(other)


def _combine(dest_flat, gates, h2, norm_f_w, ys, tile):
    rows = h2.shape[0]
    n_steps = rows // tile
    return pl.pallas_call(
        functools.partial(_combine_body, tile=tile, n_steps=n_steps),
        grid=(n_steps,),
        in_specs=[
            pl.BlockSpec((tile * TOP_K,), lambda i: (i,), memory_space=pltpu.SMEM),
            pl.BlockSpec((tile * TOP_K,), lambda i: (jnp.minimum(i + 1, n_steps - 1),),
                         memory_space=pltpu.SMEM),
            pl.BlockSpec((tile, TOP_K), lambda i: (i, 0)),
            pl.BlockSpec((tile, D_MODEL), lambda i: (i, 0)),
            pl.BlockSpec((1, D_MODEL), lambda i: (0, 0)),
            pl.BlockSpec(memory_space=pl.ANY),
        ],
        out_specs=pl.BlockSpec((tile, D_MODEL), lambda i: (i, 0)),
        out_shape=jax.ShapeDtypeStruct((rows, D_MODEL), F32),
        scratch_shapes=[pltpu.VMEM((2, TOP_K, tile * ROW_SUB, LANES), U32), pltpu.SemaphoreType.DMA((2,))],
        compiler_params=pltpu.CompilerParams(
            dimension_semantics=("arbitrary",), vmem_limit_bytes=VMEM_LIMIT),
        name="combine_final",
    )(dest_flat, dest_flat, gates, h2, norm_f_w, ys)


def _expert_tables(counts, n_assign, tile):
    n_work = n_assign // tile + N_EXPERTS
    nb = (counts + tile - 1) // tile
    bend = jnp.cumsum(nb)
    total = bend[-1]
    pstarts = (bend - nb) * tile
    blk = jnp.minimum(jnp.arange(n_work, dtype=I32), total - 1).astype(I32)
    ex = jnp.minimum(jnp.sum(bend[None, :] <= blk[:, None], axis=1), N_EXPERTS - 1).astype(I32)
    prev_ex = jnp.concatenate([jnp.full((1,), -1, I32), ex[:-1]])
    new_ex = (ex != prev_ex).astype(I32)
    after = jnp.sum(jnp.where(ex[:, None] == jnp.arange(N_EXPERTS, dtype=I32)[None, :], bend[None, :], 0), axis=1)
    idx_after = jnp.minimum(after, total - 1)
    pick = idx_after[:, None] == jnp.arange(n_work, dtype=I32)[None, :]
    next_ex = jnp.where(after < total, jnp.sum(jnp.where(pick, ex[None, :], 0), axis=1), -1).astype(I32)
    fill_start = (pstarts + counts).astype(I32)
    fill_len = (nb * tile - counts).astype(I32)
    return (blk, ex, new_ex, next_ex, total.reshape(1).astype(I32)), pstarts, fill_start, fill_len, n_work


def kernel(x, meta, norm1_w, w_in, gla_wa2_f, gla_ba2_f, gla_wa2_b, gla_ba2_b, gla_norm_w, conv_w, conv_b, dt_bias_f, dt_bias_b, a_log_f, a_log_b, ssd_d, ssd_norm_w, w_out, norm2_w, w_router, b_router, w_gu, b_gu, w_dn, b_dn, norm_f_w):
    bsz, seq, d = x.shape
    n_tok = bsz * seq
    l = 0

    wi = w_in[l]
    a_cols = wi[:, 3072:3104]
    dt_cols = wi[:, 5664:5696]
    w_perm = jnp.concatenate(
        [wi[:, :3072], wi[:, 3104:5664], a_cols, dt_cols,
         jnp.zeros((d, N_PROJ - COL_SMALL - 64), F32)], axis=1).astype(BF16)

    def lane_rows(w, lane0):
        return jnp.zeros((LANES, w.shape[1]), F32).at[lane0:lane0 + w.shape[0]].set(w)

    wa_f = lane_rows(gla_wa2_f[l], LANE_AF).astype(BF16)
    wa_b = lane_rows(gla_wa2_b[l], LANE_AB).astype(BF16)
    ba_f = gla_ba2_f[l][None, :]
    ba_b = gla_ba2_b[l][None, :]

    def lane_vec(vf, vb):
        z = jnp.zeros((1, LANES), F32)
        return z.at[0, LANE_DTF:LANE_DTF + SSD_HEADS].set(vf).at[0, LANE_DTB:LANE_DTB + SSD_HEADS].set(vb)

    dtb = lane_vec(dt_bias_f[l], dt_bias_b[l])
    alog = lane_vec(a_log_f[l], a_log_b[l])
    lane_id = (jnp.arange(2 * LANES) % LANES)[None, :, None]
    head_id = (jnp.arange(SSD_GROUP_CH) // SSD_HEAD_DIM)[None, None, :]
    grp = jnp.arange(SSD_GROUPS)[:, None, None] * SSD_GROUP_HEADS
    e_f = (lane_id == LANE_DTF + grp + head_id).astype(BF16)
    e_b = (lane_id == LANE_DTB + grp + head_id).astype(BF16)
    dexp = jnp.repeat(ssd_d[l], SSD_HEAD_DIM)[None, :]

    wr = jnp.zeros((d, LANES), F32).at[:, :N_EXPERTS].set(w_router[l])
    wr_hi = wr.astype(BF16)
    wr_lo = (wr - wr_hi.astype(F32)).astype(BF16)
    b_r = jnp.full((1, LANES), -1e30, F32).at[0, :N_EXPERTS].set(b_router[l])

    x2d = x.reshape(n_tok, d)
    x_meta = jnp.pad(meta.astype(F32), ((NPAD, 0), (0, 0)))
    n1 = norm1_w[l][None, :]
    proj = _inproj(x2d, n1, w_perm, PROJ_TILE).reshape(bsz, seq, N_PROJ)
    proj_meta = _inproj(x_meta, n1, w_perm, CHUNK)
    o_gla = _gla(proj, proj_meta, wa_f, ba_f, wa_b, ba_b, gla_norm_w[l][None, :])
    y_ssd = _ssd(proj, proj_meta, conv_w[l], conv_b[l][None, :], dtb, alog, e_f, e_b, dexp,
                 ssd_norm_w[l][None, :])

    wo = w_out[l].astype(BF16)
    h2, n2p, code, gates, cnt = _outproj(
        x2d, o_gla.reshape(n_tok, GLA_DV), y_ssd.reshape(n_tok, SSD_INNER),
        wo[:GLA_DV], wo[GLA_DV:], norm2_w[l][None, :], wr_hi, wr_lo, b_r, PROJ_TILE)

    counts = cnt[0, :N_EXPERTS].astype(I32)
    tables, starts, fill_start, fill_len, n_work = _expert_tables(counts, n_tok * TOP_K, EXPERT_TILE)
    expert_of = (code % N_EXPERTS)[..., None] == jnp.arange(N_EXPERTS, dtype=I32)
    dest = (code // N_EXPERTS + jnp.sum(jnp.where(expert_of, starts, 0), axis=-1)).astype(I32).reshape(-1)

    xs = _dispatch(fill_start, fill_len, tables[-1], dest, n2p, ROW_TILE, n_work * EXPERT_TILE)
    ys = _experts(tables, xs, w_gu[l], b_gu[l][:, None, :], w_dn[l], b_dn[l][:, None, :],
                  EXPERT_TILE, n_work)
    out = _combine(dest, gates, h2, norm_f_w[None, :], ys, COMBINE_TILE)
    return out.reshape(bsz, seq, d)
```
